```python
import math
import jax, jax.numpy as jnp
from jax import lax
import numpy as np

D_MODEL = 2048
BATCH = 4
SEQ = 2048
DEPTH = 2

PLE_DIM = 256
N_EVEN = (DEPTH + 1) // 2
N_ODD = DEPTH // 2
EPS = 1e-6

ML_HEADS = 4
ML_QK_DIM = 128
ML_V_DIM = 256
ML_QK = ML_HEADS * ML_QK_DIM
ML_WIDTH = ML_HEADS * ML_V_DIM
ML_CHUNK = 128
CONV_WIDTH = 4

DA_HEADS = 4
DA_QK_DIM = 128
DA_V_DIM = 2 * DA_QK_DIM
DA_QK = DA_HEADS * 2 * DA_QK_DIM
DA_WIDTH = DA_HEADS * DA_V_DIM
Q_BLOCK = 128
ROPE_THETA = 10000.0

AB_SPLITS = (ML_QK, ML_QK, ML_WIDTH, ML_HEADS, ML_HEADS, ML_WIDTH, ML_WIDTH,
             DA_QK, DA_QK, DA_WIDTH, DA_WIDTH)
AB_COLS = sum(AB_SPLITS)
AB_OUT = ML_WIDTH + DA_WIDTH

SGU_WIDTH = D_MODEL
SGU_CHUNK = 128
SGU_GROUPS = 8
SGU_GROUP_DIM = SGU_WIDTH // SGU_GROUPS

kernel_name = "hybrid_mlstm_diffattn_sgu_decoder"


def rms_norm(x, g):
    xf = x.astype(jnp.float32)
    y = xf * lax.rsqrt(jnp.mean(xf * xf, axis=-1, keepdims=True) + EPS)
    return (y * g.astype(jnp.float32)).astype(x.dtype)


def layer_norm(x, g, b):
    xf = x.astype(jnp.float32)
    mu = jnp.mean(xf, axis=-1, keepdims=True)
    var = jnp.mean(jnp.square(xf - mu), axis=-1, keepdims=True)
    y = (xf - mu) * lax.rsqrt(var + EPS)
    return (y * g.astype(jnp.float32) + b.astype(jnp.float32)).astype(x.dtype)


def split_cols(h, sizes):
    idx = [int(c) for c in np.cumsum(sizes)[:-1]]
    return jnp.split(h, idx, axis=-1)


def rope(x, positions):
    dh = x.shape[-1]
    half = dh // 2
    freq = ROPE_THETA ** (-jnp.arange(half, dtype=jnp.float32) / half)
    ang = positions.astype(jnp.float32)[..., None] * freq
    ang = ang.reshape(ang.shape[:2] + (1,) * (x.ndim - 3) + (half,))
    cos, sin = jnp.cos(ang), jnp.sin(ang)
    x1 = x[..., :half].astype(jnp.float32)
    x2 = x[..., half:].astype(jnp.float32)
    return jnp.concatenate([x1 * cos - x2 * sin, x1 * sin + x2 * cos], axis=-1).astype(x.dtype)


def causal_depthwise_conv(x, w, b):
    C = x.shape[-1]
    y = lax.conv_general_dilated(
        x, w[:, None, :].astype(x.dtype), window_strides=(1,),
        padding=[(CONV_WIDTH - 1, 0)], dimension_numbers=("NWC", "WIO", "NWC"),
        feature_group_count=C)
    return y + b.astype(x.dtype)


def mlstm_chunkwise(q, k, v, i_pre, f_pre):
    B, S, H, dk = q.shape
    dv = v.shape[-1]
    L = ML_CHUNK
    nc = S // L
    q = q.reshape(B, nc, L, H, dk).transpose(0, 1, 3, 2, 4)
    k = k.reshape(B, nc, L, H, dk).transpose(0, 1, 3, 2, 4) * (dk ** -0.5)
    v = v.reshape(B, nc, L, H, dv).transpose(0, 1, 3, 2, 4)
    log_i = i_pre.reshape(B, nc, L, H).transpose(0, 1, 3, 2)
    log_f = jax.nn.log_sigmoid(f_pre).reshape(B, nc, L, H).transpose(0, 1, 3, 2)
    b = jnp.cumsum(log_f, axis=-1)
    g = b[..., -1]

    w = g[..., None] - b + log_i
    m_loc = jnp.max(w, axis=-1)
    e = jnp.exp(w - m_loc[..., None])
    C_loc = jnp.einsum('bchl,bchld,bchle->bchde', e, v, k)
    n_loc = jnp.einsum('bchl,bchle->bche', e, k)

    def step(carry, xs):
        C, n, m = carry
        C_l, n_l, m_l, g_c = xs
        m_new = jnp.maximum(g_c + m, m_l)
        a = jnp.exp(g_c + m - m_new)
        c = jnp.exp(m_l - m_new)
        C_new = a[..., None, None] * C + c[..., None, None] * C_l
        n_new = a[..., None] * n + c[..., None] * n_l
        return (C_new, n_new, m_new), (C, n, m)

    init = (jnp.zeros((B, H, dv, dk), jnp.float32),
            jnp.zeros((B, H, dk), jnp.float32),
            jnp.zeros((B, H), jnp.float32))
    xs = (jnp.moveaxis(C_loc, 1, 0), jnp.moveaxis(n_loc, 1, 0),
          jnp.moveaxis(m_loc, 1, 0), jnp.moveaxis(g, 1, 0))
    _, (C_prev, n_prev, m_prev) = lax.scan(step, init, xs)
    C_prev = jnp.moveaxis(C_prev, 0, 1)
    n_prev = jnp.moveaxis(n_prev, 0, 1)
    m_prev = jnp.moveaxis(m_prev, 0, 1)

    causal = jnp.tril(jnp.ones((L, L), dtype=bool))
    logD = b[..., :, None] - b[..., None, :] + log_i[..., None, :]
    logD = jnp.where(causal, logD, -jnp.inf)
    inter_log = b + m_prev[..., None]
    m_t = jnp.maximum(inter_log, jnp.max(logD, axis=-1))
    Dmat = jnp.exp(logD - m_t[..., None])
    inter_w = jnp.exp(inter_log - m_t)

    s = jnp.einsum('bchtd,bchsd->bchts', q, k) * Dmat
    num = (jnp.einsum('bchts,bchsd->bchtd', s, v)
           + inter_w[..., None] * jnp.einsum('bchde,bchte->bchtd', C_prev, q))
    den = jnp.sum(s, axis=-1) + inter_w * jnp.einsum('bche,bchte->bcht', n_prev, q)
    h = num / jnp.maximum(jnp.abs(den), jnp.exp(-m_t))[..., None]
    return h.transpose(0, 1, 3, 2, 4).reshape(B, S, H, dv)


def diff_attention(q, k, v, lam):
    B, S, H, _, dh = q.shape
    dv = v.shape[-1]
    nb = S // Q_BLOCK
    qf = q.astype(jnp.float32) * (dh ** -0.5)
    kf = k.astype(jnp.float32)
    qb = qf.reshape(B, nb, Q_BLOCK, H, 2, dh).transpose(1, 0, 2, 3, 4, 5)
    k_pos = jnp.arange(S)

    def block(args):
        q_blk, blk = args
        s = jnp.einsum('bqhcd,bkhcd->bhcqk', q_blk, kf)
        q_pos = blk * Q_BLOCK + jnp.arange(Q_BLOCK)
        mask = k_pos[None, :] <= q_pos[:, None]
        pmap = jax.nn.softmax(jnp.where(mask, s, -jnp.inf), axis=-1)
        a = pmap[:, :, 0] - lam * pmap[:, :, 1]
        return jnp.einsum('bhqk,bkhe->bqhe', a.astype(v.dtype), v)

    o = lax.map(block, (qb, jnp.arange(nb)))
    return o.transpose(1, 0, 2, 3, 4).reshape(B, S, H, dv)


def mixer_ab(h, positions, w_in, conv_w, conv_b, i_bias, f_bias, ml_norm_g,
             lam_q1, lam_k1, lam_q2, lam_k2, da_norm_g, w_out, lam_init):
    B, S, _ = h.shape
    proj = h @ w_in
    (ml_q, ml_k, ml_v, ml_i, ml_f, ml_o, ml_z,
     da_q, da_k, da_v, da_z) = split_cols(proj, AB_SPLITS)

    qk = jax.nn.silu(causal_depthwise_conv(jnp.concatenate([ml_q, ml_k], axis=-1), conv_w, conv_b))
    ml_q, ml_k = jnp.split(qk, 2, axis=-1)
    f32 = jnp.float32
    hm = mlstm_chunkwise(
        ml_q.reshape(B, S, ML_HEADS, ML_QK_DIM).astype(f32),
        ml_k.reshape(B, S, ML_HEADS, ML_QK_DIM).astype(f32),
        ml_v.reshape(B, S, ML_HEADS, ML_V_DIM).astype(f32),
        (ml_i + i_bias).astype(f32),
        (ml_f + f_bias).astype(f32))
    hm = rms_norm(hm, ml_norm_g).astype(h.dtype)
    hm = jax.nn.sigmoid(ml_o).reshape(B, S, ML_HEADS, ML_V_DIM) * hm
    a_out = hm.reshape(B, S, ML_WIDTH) * jax.nn.silu(ml_z)

    lam = (jnp.exp(jnp.sum(lam_q1.astype(f32) * lam_k1.astype(f32)))
           - jnp.exp(jnp.sum(lam_q2.astype(f32) * lam_k2.astype(f32))) + lam_init)
    dq = rope(da_q.reshape(B, S, DA_HEADS, 2, DA_QK_DIM), positions)
    dk = rope(da_k.reshape(B, S, DA_HEADS, 2, DA_QK_DIM), positions)
    dvv = da_v.reshape(B, S, DA_HEADS, DA_V_DIM)
    hd = diff_attention(dq, dk, dvv, lam)
    hd = rms_norm(hd, da_norm_g) * (1.0 - lam_init)
    b_out = hd.reshape(B, S, DA_WIDTH) * jax.nn.silu(da_z)

    return jnp.concatenate([a_out, b_out], axis=-1) @ w_out


def mixer_c(h, w_in, ln_g, ln_b, sgu_w, sgu_b, w_out):
    B, S, _ = h.shape
    u, v, z = jnp.split(h @ w_in, 3, axis=-1)
    v = layer_norm(v, ln_g, ln_b)
    nc = S // SGU_CHUNK
    vg = v.reshape(B, nc, SGU_CHUNK, SGU_GROUPS, SGU_GROUP_DIM)
    w_causal = jnp.tril(sgu_w)
    sv = jnp.einsum('gts,bcsgd->bctgd', w_causal, vg) + sgu_b.T[:, :, None]
    sv = sv.reshape(B, S, SGU_WIDTH)
    return (u * sv * jax.nn.silu(z)) @ w_out


def setup_inputs(seed: int = 0) -> dict:
    key = jax.random.key(seed)
    ks = jax.random.split(key, 32)
    nrm = jax.random.normal
    f32 = jnp.float32
    D = D_MODEL
    NE, NO = N_EVEN, N_ODD
    return {
        "x": nrm(ks[0], (BATCH, SEQ, D), f32),
        "p": nrm(ks[1], (DEPTH, BATCH, SEQ, PLE_DIM), f32),
        "positions": jnp.broadcast_to(jnp.arange(SEQ, dtype=jnp.int32)[None, :], (BATCH, SEQ)),
        "ab_norm_g": 1.0 + 0.01 * nrm(ks[2], (NE, D), f32),
        "ab_w_in": nrm(ks[3], (NE, D, AB_COLS), f32) * D ** -0.5,
        "ab_conv_w": nrm(ks[4], (NE, CONV_WIDTH, 2 * ML_QK), f32) * CONV_WIDTH ** -0.5,
        "ab_conv_b": 0.01 * nrm(ks[5], (NE, 2 * ML_QK), f32),
        "ab_i_bias": 0.1 * nrm(ks[6], (NE, ML_HEADS), f32) - 1.0,
        "ab_f_bias": 3.0 + 0.5 * nrm(ks[7], (NE, ML_HEADS), f32),
        "ab_ml_norm_g": 1.0 + 0.01 * nrm(ks[8], (NE, ML_HEADS, ML_V_DIM), f32),
        "ab_lam_q1": 0.1 * nrm(ks[9], (NE, DA_QK_DIM), f32),
        "ab_lam_k1": 0.1 * nrm(ks[10], (NE, DA_QK_DIM), f32),
        "ab_lam_q2": 0.1 * nrm(ks[11], (NE, DA_QK_DIM), f32),
        "ab_lam_k2": 0.1 * nrm(ks[12], (NE, DA_QK_DIM), f32),
        "ab_da_norm_g": 1.0 + 0.01 * nrm(ks[13], (NE, DA_HEADS, DA_V_DIM), f32),
        "ab_w_out": nrm(ks[14], (NE, AB_OUT, D), f32) * AB_OUT ** -0.5,
        "c_norm_g": 1.0 + 0.01 * nrm(ks[15], (NO, D), f32),
        "c_w_in": nrm(ks[16], (NO, D, 3 * SGU_WIDTH), f32) * D ** -0.5,
        "c_ln_g": 1.0 + 0.01 * nrm(ks[17], (NO, SGU_WIDTH), f32),
        "c_ln_b": 0.01 * nrm(ks[18], (NO, SGU_WIDTH), f32),
        "c_sgu_w": nrm(ks[19], (NO, SGU_GROUPS, SGU_CHUNK, SGU_CHUNK), f32) * SGU_CHUNK ** -0.5,
        "c_sgu_b": 1.0 + 0.01 * nrm(ks[20], (NO, SGU_GROUPS, SGU_CHUNK), f32),
        "c_w_out": nrm(ks[21], (NO, SGU_WIDTH, D), f32) * SGU_WIDTH ** -0.5,
        "ple_norm_g": 1.0 + 0.01 * nrm(ks[22], (DEPTH, D), f32),
        "ple_gate_w": nrm(ks[23], (DEPTH, D, D), f32) * D ** -0.5,
        "ple_proj_w": nrm(ks[24], (DEPTH, PLE_DIM, D), f32) * PLE_DIM ** -0.5,
        "final_norm_g": 1.0 + 0.01 * nrm(ks[25], (D,), f32),
    }


def reference(x, p, positions, ab_norm_g, ab_w_in, ab_conv_w, ab_conv_b, ab_i_bias,
              ab_f_bias, ab_ml_norm_g, ab_lam_q1, ab_lam_k1, ab_lam_q2, ab_lam_k2,
              ab_da_norm_g, ab_w_out, c_norm_g, c_w_in, c_ln_g, c_ln_b, c_sgu_w,
              c_sgu_b, c_w_out, ple_norm_g, ple_gate_w, ple_proj_w, final_norm_g):
    for i in range(DEPTH):
        j = i // 2
        if i % 2 == 0:
            lam_init = 0.8 - 0.6 * math.exp(-0.3 * i)
            x = x + mixer_ab(rms_norm(x, ab_norm_g[j]), positions, ab_w_in[j],
                             ab_conv_w[j], ab_conv_b[j], ab_i_bias[j], ab_f_bias[j],
                             ab_ml_norm_g[j], ab_lam_q1[j], ab_lam_k1[j], ab_lam_q2[j],
                             ab_lam_k2[j], ab_da_norm_g[j], ab_w_out[j], lam_init)
        else:
            x = x + mixer_c(rms_norm(x, c_norm_g[j]), c_w_in[j], c_ln_g[j], c_ln_b[j],
                            c_sgu_w[j], c_sgu_b[j], c_w_out[j])
        gate = jax.nn.sigmoid(rms_norm(x, ple_norm_g[i]) @ ple_gate_w[i])
        x = x + gate * (p[i] @ ple_proj_w[i])
    return rms_norm(x, final_norm_g)
```

```python
import functools
import math

import jax
import jax.numpy as jnp
from jax import lax
from jax.experimental import pallas as pl
from jax.experimental.pallas import tpu as pltpu

F32 = jnp.float32
BF16 = jnp.bfloat16

EPS = 1e-6
ROPE_THETA = 10000.0
CONV_WIDTH = 4

ML_HEADS = 4
ML_QK_DIM = 128
ML_V_DIM = 256
ML_CHUNK = 128
DA_HEADS = 4
DA_QK_DIM = 128
DA_V_DIM = 256
SGU_CHUNK = 128
SGU_GROUPS = 8
SGU_GROUP_DIM = 256

ML_Q_OFF, ML_K_OFF, ML_V_OFF, ML_O_OFF, ML_Z_OFF = 0, 512, 1024, 2048, 3072
DA_Q_OFF, DA_K_OFF, DA_V_OFF, DA_Z_OFF = 4096, 5120, 6144, 7168
AB_MAIN_COLS = 8192
GATE_COLS = 128

VMEM_LIMIT_BYTES = 56 * 1024 * 1024

PROJ_TM = 1024
PROJ_TN = 1024
NORM_ROWS = 128
TAIL_TM = 512
TAIL_NC = 512
ATT_TQ = 256
ATT_TK = 256


def _params(semantics):
    return pltpu.CompilerParams(dimension_semantics=semantics, vmem_limit_bytes=VMEM_LIMIT_BYTES)


def _resident(shape, index_map):
    return pl.BlockSpec(shape, index_map, pipeline_mode=pl.Buffered(1))


def _sigmoid(x):
    return 1.0 / (1.0 + jnp.exp(-x))


def _silu(x):
    return x * _sigmoid(x)


def _log_sigmoid(x):
    return jnp.minimum(x, 0.0) - jnp.log(1.0 + jnp.exp(-jnp.abs(x)))


def _rope_table_kernel(pos_ref, cos_ref, sin_ref):
    pos = pos_ref[0]
    half_rows = pos.shape[0]
    lane = lax.broadcasted_iota(jnp.int32, pos.shape, 1)
    lo = lane < 64
    j = jnp.where(lo, lane, lane - 64).astype(F32)
    freq = jnp.exp(j * (-math.log(ROPE_THETA) / 64.0))
    ang = pos * freq
    c = jnp.cos(ang)
    s = jnp.sin(ang)
    cr = pltpu.roll(c, 64, 1)
    sr = pltpu.roll(s, 64, 1)
    cos_ref[0, 0:half_rows, :] = jnp.where(lo, c, cr)
    cos_ref[0, half_rows:2 * half_rows, :] = jnp.where(lo, cr, c)
    sin_ref[0, 0:half_rows, :] = jnp.where(lo, -s, sr)
    sin_ref[0, half_rows:2 * half_rows, :] = jnp.where(lo, -sr, s)


def _rope_tables(positions):
    B, S = positions.shape
    posf = positions.astype(F32)
    lo = jnp.broadcast_to(posf[:, :S // 2, None], (B, S // 2, 64))
    hi = jnp.broadcast_to(posf[:, S // 2:, None], (B, S // 2, 64))
    pos2 = jnp.concatenate([lo, hi], axis=-1)
    return pl.pallas_call(
        _rope_table_kernel,
        grid=(B,),
        in_specs=[pl.BlockSpec((1, S // 2, 128), lambda b: (b, 0, 0))],
        out_specs=[pl.BlockSpec((1, S, 128), lambda b: (b, 0, 0)),
                   pl.BlockSpec((1, S, 128), lambda b: (b, 0, 0))],
        out_shape=[jax.ShapeDtypeStruct((B, S, 128), F32)] * 2,
        compiler_params=_params(("arbitrary",)),
        name="rope_tables",
    )(pos2)


def _normalize_rows(x_ref, g_ref, xn_ref):
    g = g_ref[...]

    def body(r, carry):
        rows = pl.ds(pl.multiple_of(r * NORM_ROWS, NORM_ROWS), NORM_ROWS)
        x = x_ref[rows, :]
        ms = jnp.mean(x * x, axis=-1, keepdims=True)
        xn_ref[rows, :] = (x * lax.rsqrt(ms + EPS) * g).astype(BF16)
        return carry

    lax.fori_loop(0, x_ref.shape[0] // NORM_ROWS, body, 0)


def _norm_proj_kernel(x_ref, g_ref, w_ref, o_ref, xn_ref):
    @pl.when(pl.program_id(1) == 0)
    def _():
        _normalize_rows(x_ref, g_ref, xn_ref)

    o_ref[...] = jnp.dot(xn_ref[...], w_ref[...], preferred_element_type=F32).astype(o_ref.dtype)


def _norm_proj_gates_kernel(x_ref, g_ref, w_ref, wg_ref, o_ref, og_ref, xn_ref):
    @pl.when(pl.program_id(1) == 0)
    def _():
        _normalize_rows(x_ref, g_ref, xn_ref)
        og_ref[...] = jnp.dot(xn_ref[...], wg_ref[...], preferred_element_type=F32)

    o_ref[...] = jnp.dot(xn_ref[...], w_ref[...], preferred_element_type=F32).astype(o_ref.dtype)


def _norm_proj(x2d, g, w, wg=None, name="norm_proj"):
    M, D = x2d.shape
    N = w.shape[1]
    grid = (M // PROJ_TM, N // PROJ_TN)
    in_specs = [pl.BlockSpec((PROJ_TM, D), lambda i, j: (i, 0)),
                pl.BlockSpec((1, D), lambda i, j: (0, 0)),
                pl.BlockSpec((D, PROJ_TN), lambda i, j: (0, j))]
    out_specs = [pl.BlockSpec((PROJ_TM, PROJ_TN), lambda i, j: (i, j))]
    out_shape = [jax.ShapeDtypeStruct((M, N), BF16)]
    args = [x2d, g.reshape(1, D), w]
    kern = _norm_proj_kernel
    if wg is not None:
        in_specs.append(pl.BlockSpec((D, GATE_COLS), lambda i, j: (0, 0)))
        out_specs.append(pl.BlockSpec((PROJ_TM, GATE_COLS), lambda i, j: (i, 0)))
        out_shape.append(jax.ShapeDtypeStruct((M, GATE_COLS), F32))
        args.append(wg)
        kern = _norm_proj_gates_kernel
    return pl.pallas_call(
        kern,
        grid=grid,
        in_specs=in_specs,
        out_specs=out_specs,
        out_shape=out_shape,
        scratch_shapes=[pltpu.VMEM((PROJ_TM, D), BF16)],
        compiler_params=_params(("arbitrary", "arbitrary")),
        name=name,
    )(*args)


def _mlstm_kernel(qp_ref, kp_ref, v_ref, gt_ref, og_ref, z_ref, cwq_ref, cwk_ref, cbq_ref, cbk_ref,
                  gb_ref, ng_ref, out_ref, xs_ref, qc_ref, kc_ref):
    h = pl.program_id(1)
    S = qp_ref.shape[0]
    L = ML_CHUNK
    nc = S // L

    for src, cw_ref, cb_ref, dst, scale in ((qp_ref, cwq_ref, cbq_ref, qc_ref, 1.0),
                                            (kp_ref, cwk_ref, cbk_ref, kc_ref, ML_QK_DIM ** -0.5)):
        xs_ref[0:8, :] = jnp.zeros((8, ML_QK_DIM), F32)
        xs_ref[8:8 + S, :] = src[...].astype(F32)
        cw = cw_ref[...]
        cb = cb_ref[...]
        for c in range(nc):
            acc = cb
            for j in range(CONV_WIDTH):
                start = 8 + c * L - (CONV_WIDTH - 1) + j
                acc = acc + cw[j:j + 1, :] * xs_ref[start:start + L, :]
            dst[c * L:(c + 1) * L, :] = (_silu(acc) * scale).astype(BF16)

    sub = lax.broadcasted_iota(jnp.int32, (L, L), 0)
    lane = lax.broadcasted_iota(jnp.int32, (L, L), 1)
    causal = lane <= sub
    tril_f = causal.astype(F32)
    triu_f = (sub <= lane).astype(F32)
    gate_bias = gb_ref[...]
    norm_g = ng_ref[0]

    def sel_col(x, idx):
        return jnp.sum(jnp.where(lane == idx, x, 0.0), axis=1, keepdims=True)

    def sel_row(x, idx):
        return jnp.sum(jnp.where(sub == idx, x, 0.0), axis=0, keepdims=True)

    def chunk(c, carry):
        ct_prev, n_prev, m_prev = carry
        rows = pl.ds(pl.multiple_of(c * L, L), L)
        gates = gt_ref[rows, :] + gate_bias
        gates_t = gates.T
        logf = _log_sigmoid(gates)
        logf_t = _log_sigmoid(gates_t)
        cum = jnp.dot(tril_f, logf, preferred_element_type=F32, precision=lax.Precision.HIGHEST)
        cum_t = jnp.dot(logf_t, triu_f, preferred_element_type=F32, precision=lax.Precision.HIGHEST)
        b_col = sel_col(cum, h + ML_HEADS)
        b_row = sel_row(cum_t, h + ML_HEADS)
        i_row = sel_row(gates_t, h)
        g_tot = b_row[:, L - 1:L]

        q = qc_ref[rows, :]
        k = kc_ref[rows, :]
        v = v_ref[rows, :]
        k_t = k.astype(F32).T

        w_row = g_tot - b_row + i_row
        m_loc = jnp.max(w_row, axis=1, keepdims=True)
        e_row = jnp.exp(w_row - m_loc)
        kte = (k_t * e_row).astype(BF16)
        ct_loc = jnp.dot(kte, v, preferred_element_type=F32)
        n_loc = jnp.dot(jnp.broadcast_to(e_row, (8, L)).astype(BF16), k,
                        preferred_element_type=F32)[0:1, :]

        log_d = jnp.where(causal, b_col - b_row + i_row, -jnp.inf)
        inter_log = b_col + m_prev
        m_t = jnp.maximum(inter_log, jnp.max(log_d, axis=1, keepdims=True))
        d_mat = jnp.exp(log_d - m_t)
        inter_w = jnp.exp(inter_log - m_t)
        s_mat = jnp.dot(q, k_t.astype(BF16), preferred_element_type=F32) * d_mat
        num = (jnp.dot(s_mat.astype(BF16), v, preferred_element_type=F32)
               + inter_w * jnp.dot(q, ct_prev.astype(BF16), preferred_element_type=F32))
        den = (jnp.sum(s_mat, axis=1, keepdims=True)
               + inter_w * jnp.sum(q.astype(F32) * n_prev, axis=1, keepdims=True))
        hm = num / jnp.maximum(jnp.abs(den), jnp.exp(-m_t))

        hn = hm * lax.rsqrt(jnp.mean(hm * hm, axis=-1, keepdims=True) + EPS) * norm_g
        og = og_ref[rows, :].astype(F32)
        z = z_ref[rows, :].astype(F32)
        out_ref[rows, :] = (_sigmoid(og) * hn * _silu(z)).astype(out_ref.dtype)

        m_new = jnp.maximum(g_tot + m_prev, m_loc)
        a = jnp.exp(g_tot + m_prev - m_new)
        cc = jnp.exp(m_loc - m_new)
        return a * ct_prev + cc * ct_loc, a * n_prev + cc * n_loc, m_new

    init = (jnp.zeros((ML_QK_DIM, ML_V_DIM), F32), jnp.zeros((1, ML_QK_DIM), F32), jnp.zeros((1, 1), F32))
    lax.fori_loop(0, nc, chunk, init)


def _mlstm(proj, gates, conv_w, conv_b, i_bias, f_bias, norm_g, B, S):
    M = B * S
    H = ML_HEADS
    gate_bias = jnp.concatenate([i_bias, f_bias, jnp.zeros((GATE_COLS - 2 * H,), F32)]).reshape(1, GATE_COLS)
    conv_b2 = conv_b.reshape(1, -1)
    norm_g3 = norm_g.reshape(H, 1, ML_V_DIM)
    qb, kb = ML_Q_OFF // ML_QK_DIM, ML_K_OFF // ML_QK_DIM
    vb, ob, zb = ML_V_OFF // ML_V_DIM, ML_O_OFF // ML_V_DIM, ML_Z_OFF // ML_V_DIM
    return pl.pallas_call(
        _mlstm_kernel,
        grid=(B, H),
        in_specs=[
            pl.BlockSpec((S, ML_QK_DIM), lambda b, h: (b, qb + h)),
            pl.BlockSpec((S, ML_QK_DIM), lambda b, h: (b, kb + h)),
            pl.BlockSpec((S, ML_V_DIM), lambda b, h: (b, vb + h)),
            pl.BlockSpec((S, GATE_COLS), lambda b, h: (b, 0)),
            pl.BlockSpec((S, ML_V_DIM), lambda b, h: (b, ob + h)),
            pl.BlockSpec((S, ML_V_DIM), lambda b, h: (b, zb + h)),
            pl.BlockSpec((CONV_WIDTH, ML_QK_DIM), lambda b, h: (0, h)),
            pl.BlockSpec((CONV_WIDTH, ML_QK_DIM), lambda b, h: (0, H + h)),
            pl.BlockSpec((1, ML_QK_DIM), lambda b, h: (0, h)),
            pl.BlockSpec((1, ML_QK_DIM), lambda b, h: (0, H + h)),
            pl.BlockSpec((1, GATE_COLS), lambda b, h: (0, 0)),
            pl.BlockSpec((1, 1, ML_V_DIM), lambda b, h: (h, 0, 0)),
        ],
        out_specs=pl.BlockSpec((S, ML_V_DIM), lambda b, h: (b, h)),
        out_shape=jax.ShapeDtypeStruct((M, H * ML_V_DIM), BF16),
        scratch_shapes=[pltpu.VMEM((8 + S, ML_QK_DIM), F32),
                        pltpu.VMEM((S, ML_QK_DIM), BF16),
                        pltpu.VMEM((S, ML_QK_DIM), BF16)],
        compiler_params=_params(("arbitrary", "arbitrary")),
        name="mlstm",
    )(proj, proj, proj, gates, proj, proj, conv_w, conv_w, conv_b2, conv_b2, gate_bias, norm_g3)


def _diff_attn_kernel(lam_init, q_ref, k_ref, v_ref, z_ref, cos_ref, sin_ref, lq1_ref, lk1_ref, lq2_ref,
                      lk2_ref, ng_ref, o_ref, kr_ref, acc_ref):
    qi = pl.program_id(2)
    S = k_ref.shape[0]
    TQ, TK, DH = ATT_TQ, ATT_TK, DA_QK_DIM

    def rope(x, cs, sn):
        return x * cs + pltpu.roll(x, DH // 2, 1) * sn

    @pl.when(qi == 0)
    def _():
        def body(c, carry):
            rows = pl.ds(pl.multiple_of(c * TK, TK), TK)
            cs = cos_ref[0, rows, :]
            sn = sin_ref[0, rows, :]
            for m in range(2):
                x = k_ref[rows, m * DH:(m + 1) * DH].astype(F32)
                kr_ref[rows, m * DH:(m + 1) * DH] = rope(x, cs, sn).astype(BF16)
            return carry

        lax.fori_loop(0, S // TK, body, 0)

    q_rows = pl.ds(pl.multiple_of(qi * TQ, TQ), TQ)
    cs_q = cos_ref[0, q_rows, :]
    sn_q = sin_ref[0, q_rows, :]
    scale = DH ** -0.5
    q_maps = [(rope(q_ref[:, m * DH:(m + 1) * DH].astype(F32), cs_q, sn_q) * scale).astype(BF16)
              for m in range(2)]

    acc_ref[...] = jnp.zeros(acc_ref.shape, F32)
    diag_mask = (lax.broadcasted_iota(jnp.int32, (TQ, TK), 1)
                 <= lax.broadcasted_iota(jnp.int32, (TQ, TK), 0))

    def kv_step(j, carry, masked):
        rows = pl.ds(pl.multiple_of(j * TK, TK), TK)
        v_blk = v_ref[rows, :]
        new = []
        for m in range(2):
            m_prev, l_prev = carry[2 * m], carry[2 * m + 1]
            k_blk = kr_ref[rows, m * DH:(m + 1) * DH]
            s = lax.dot_general(q_maps[m], k_blk, (((1,), (1,)), ((), ())), preferred_element_type=F32)
            if masked:
                s = jnp.where(diag_mask, s, -jnp.inf)
            m_new = jnp.maximum(m_prev, jnp.max(s, axis=1, keepdims=True))
            p = jnp.exp(s - m_new)
            alpha = jnp.exp(m_prev - m_new)
            l_new = alpha * l_prev + jnp.sum(p, axis=1, keepdims=True)
            acc_ref[m] = alpha * acc_ref[m] + jnp.dot(p.astype(BF16), v_blk, preferred_element_type=F32)
            new += [m_new, l_new]
        return tuple(new)

    neg = jnp.full((TQ, 1), -jnp.inf, F32)
    zero = jnp.zeros((TQ, 1), F32)
    carry = lax.fori_loop(0, qi, functools.partial(kv_step, masked=False), (neg, zero, neg, zero))
    _, l1, _, l2 = kv_step(qi, carry, masked=True)

    lam = (jnp.exp(jnp.sum(lq1_ref[...] * lk1_ref[...], axis=1, keepdims=True))
           - jnp.exp(jnp.sum(lq2_ref[...] * lk2_ref[...], axis=1, keepdims=True)) + lam_init)
    o = acc_ref[0] / l1 - lam * (acc_ref[1] / l2)
    hn = o * lax.rsqrt(jnp.mean(o * o, axis=-1, keepdims=True) + EPS) * ng_ref[0] * (1.0 - lam_init)
    o_ref[...] = (hn * _silu(z_ref[...].astype(F32))).astype(o_ref.dtype)


def _diff_attn(proj, cos2, sin2, lq1, lk1, lq2, lk2, norm_g, lam_init, B, S):
    M = B * S
    H = DA_HEADS
    nq = S // ATT_TQ
    qb, kb, vb, zb = (DA_Q_OFF // DA_V_DIM, DA_K_OFF // DA_V_DIM, DA_V_OFF // DA_V_DIM, DA_Z_OFF // DA_V_DIM)
    vec = lambda a: a.reshape(1, DA_QK_DIM)
    vec_spec = pl.BlockSpec((1, DA_QK_DIM), lambda b, h, i: (0, 0))
    return pl.pallas_call(
        functools.partial(_diff_attn_kernel, lam_init),
        grid=(B, H, nq),
        in_specs=[
            pl.BlockSpec((ATT_TQ, DA_V_DIM), lambda b, h, i: (b * nq + i, qb + h)),
            pl.BlockSpec((S, DA_V_DIM), lambda b, h, i: (b, kb + h)),
            pl.BlockSpec((S, DA_V_DIM), lambda b, h, i: (b, vb + h)),
            pl.BlockSpec((ATT_TQ, DA_V_DIM), lambda b, h, i: (b * nq + i, zb + h)),
            pl.BlockSpec((1, S, DA_QK_DIM), lambda b, h, i: (b, 0, 0)),
            pl.BlockSpec((1, S, DA_QK_DIM), lambda b, h, i: (b, 0, 0)),
            vec_spec, vec_spec, vec_spec, vec_spec,
            pl.BlockSpec((1, 1, DA_V_DIM), lambda b, h, i: (h, 0, 0)),
        ],
        out_specs=pl.BlockSpec((ATT_TQ, DA_V_DIM), lambda b, h, i: (b * nq + i, h)),
        out_shape=jax.ShapeDtypeStruct((M, H * DA_V_DIM), BF16),
        scratch_shapes=[pltpu.VMEM((S, DA_V_DIM), BF16),
                        pltpu.VMEM((2, ATT_TQ, DA_V_DIM), F32)],
        compiler_params=_params(("arbitrary", "arbitrary", "arbitrary")),
        name="diff_attn",
    )(proj, proj, proj, proj, cos2, sin2, vec(lq1), vec(lk1), vec(lq2), vec(lk2),
      norm_g.reshape(H, 1, DA_V_DIM))


def _ple_tail(x1_ref, ssq, p_ref, png_ref, gw_ref, pw_ref, xn_ref, emit):
    D = x1_ref.shape[1]
    inv = lax.rsqrt(ssq * (1.0 / D) + EPS)
    for n0 in range(0, D, TAIL_NC):
        cols = slice(n0, n0 + TAIL_NC)
        xn_ref[:, cols] = (x1_ref[:, cols] * inv * png_ref[:, cols]).astype(BF16)
    p_bf = p_ref[...].astype(BF16)
    for n0 in range(0, D, TAIL_NC):
        cols = slice(n0, n0 + TAIL_NC)
        gate = _sigmoid(jnp.dot(xn_ref[...], gw_ref[:, cols], preferred_element_type=F32))
        pp = jnp.dot(p_bf, pw_ref[:, cols], preferred_element_type=F32)
        emit(n0, x1_ref[:, cols] + gate * pp)


def _out_ple_ab_kernel(a_ref, b_ref, x_ref, p_ref, wo_ref, png_ref, gw_ref, pw_ref, o_ref, x1_ref, xn_ref):
    D = x_ref.shape[1]
    ka = a_ref.shape[1]
    ssq = jnp.zeros((x_ref.shape[0], 1), F32)
    for n0 in range(0, D, TAIL_NC):
        cols = slice(n0, n0 + TAIL_NC)
        y = (jnp.dot(a_ref[...], wo_ref[0:ka, cols], preferred_element_type=F32)
             + jnp.dot(b_ref[...], wo_ref[ka:, cols], preferred_element_type=F32)
             + x_ref[:, cols])
        x1_ref[:, cols] = y
        ssq = ssq + jnp.sum(y * y, axis=1, keepdims=True)

    def emit(n0, val):
        o_ref[:, n0:n0 + TAIL_NC] = val

    _ple_tail(x1_ref, ssq, p_ref, png_ref, gw_ref, pw_ref, xn_ref, emit)


def _out_ple_ab(a_out, b_out, x2d, p2d, w_out, ple_g, gate_w, proj_w):
    M, D = x2d.shape
    KA = a_out.shape[1]
    P = p2d.shape[1]
    TM = TAIL_TM
    row = lambda i: (i, 0)
    fixed = lambda i: (0, 0)
    return pl.pallas_call(
        _out_ple_ab_kernel,
        grid=(M // TM,),
        in_specs=[
            pl.BlockSpec((TM, KA), row),
            pl.BlockSpec((TM, b_out.shape[1]), row),
            pl.BlockSpec((TM, D), row),
            pl.BlockSpec((TM, P), row),
            _resident(w_out.shape, fixed),
            _resident((1, D), fixed),
            _resident(gate_w.shape, fixed),
            _resident(proj_w.shape, fixed),
        ],
        out_specs=pl.BlockSpec((TM, D), row),
        out_shape=jax.ShapeDtypeStruct((M, D), F32),
        scratch_shapes=[pltpu.VMEM((TM, D), F32), pltpu.VMEM((TM, D), BF16)],
        compiler_params=_params(("arbitrary",)),
        name="out_ple_ab",
    )(a_out, b_out, x2d, p2d, w_out, ple_g.reshape(1, D), gate_w, proj_w)


def _sgu_out_ple_kernel(u_ref, v_ref, z_ref, x_ref, p_ref, lng_ref, lnb_ref, sw_ref, sbt_ref, wo_ref,
                        png_ref, gw_ref, pw_ref, fng_ref, o_ref, t_ref, x1_ref, xn_ref):
    TM, D = x_ref.shape
    L = SGU_CHUNK
    GD = SGU_GROUP_DIM
    causal = (lax.broadcasted_iota(jnp.int32, (L, L), 1) <= lax.broadcasted_iota(jnp.int32, (L, L), 0))
    w_causal = [jnp.where(causal, sw_ref[g], 0.0).astype(BF16) for g in range(SGU_GROUPS)]
    lng = lng_ref[...]
    lnb = lnb_ref[...]

    for c in range(TM // L):
        rows = slice(c * L, (c + 1) * L)
        vv = v_ref[rows, :].astype(F32)
        mu = jnp.mean(vv, axis=-1, keepdims=True)
        vc = vv - mu
        var = jnp.mean(vc * vc, axis=-1, keepdims=True)
        vn = (vc * lax.rsqrt(var + EPS) * lng + lnb).astype(BF16)
        for g in range(SGU_GROUPS):
            cols = slice(g * GD, (g + 1) * GD)
            sv = jnp.dot(w_causal[g], vn[:, cols], preferred_element_type=F32) + sbt_ref[:, g:g + 1]
            t = u_ref[rows, cols].astype(F32) * sv * _silu(z_ref[rows, cols].astype(F32))
            t_ref[rows, cols] = t.astype(BF16)

    ssq = jnp.zeros((TM, 1), F32)
    for n0 in range(0, D, TAIL_NC):
        cols = slice(n0, n0 + TAIL_NC)
        y = jnp.dot(t_ref[...], wo_ref[:, cols], preferred_element_type=F32) + x_ref[:, cols]
        x1_ref[:, cols] = y
        ssq = ssq + jnp.sum(y * y, axis=1, keepdims=True)

    ssq2 = [jnp.zeros((TM, 1), F32)]

    def emit(n0, val):
        o_ref[:, n0:n0 + TAIL_NC] = val
        ssq2[0] = ssq2[0] + jnp.sum(val * val, axis=1, keepdims=True)

    _ple_tail(x1_ref, ssq, p_ref, png_ref, gw_ref, pw_ref, xn_ref, emit)

    inv = lax.rsqrt(ssq2[0] * (1.0 / D) + EPS)
    for n0 in range(0, D, TAIL_NC):
        cols = slice(n0, n0 + TAIL_NC)
        o_ref[:, cols] = o_ref[:, cols] * inv * fng_ref[:, cols]


def _sgu_out_ple(uvz, x2d, p2d, ln_g, ln_b, sgu_w, sgu_b, w_out, ple_g, gate_w, proj_w, final_g):
    M, D = x2d.shape
    P = p2d.shape[1]
    TM = TAIL_TM
    row = lambda i: (i, 0)
    fixed = lambda i: (0, 0)
    return pl.pallas_call(
        _sgu_out_ple_kernel,
        grid=(M // TM,),
        in_specs=[
            pl.BlockSpec((TM, D), lambda i: (i, 0)),
            pl.BlockSpec((TM, D), lambda i: (i, 1)),
            pl.BlockSpec((TM, D), lambda i: (i, 2)),
            pl.BlockSpec((TM, D), row),
            pl.BlockSpec((TM, P), row),
            _resident((1, D), fixed),
            _resident((1, D), fixed),
            _resident(sgu_w.shape, lambda i: (0, 0, 0)),
            _resident((SGU_CHUNK, SGU_GROUPS), fixed),
            _resident(w_out.shape, fixed),
            _resident((1, D), fixed),
            _resident(gate_w.shape, fixed),
            _resident(proj_w.shape, fixed),
            _resident((1, D), fixed),
        ],
        out_specs=pl.BlockSpec((TM, D), row),
        out_shape=jax.ShapeDtypeStruct((M, D), F32),
        scratch_shapes=[pltpu.VMEM((TM, D), BF16), pltpu.VMEM((TM, D), F32), pltpu.VMEM((TM, D), BF16)],
        compiler_params=_params(("arbitrary",)),
        name="sgu_out_ple",
    )(uvz, uvz, uvz, x2d, p2d, ln_g.reshape(1, D), ln_b.reshape(1, D), sgu_w, sgu_b.T, w_out,
      ple_g.reshape(1, D), gate_w, proj_w, final_g.reshape(1, D))


def kernel(x, p, positions, ab_norm_g, ab_w_in, ab_conv_w, ab_conv_b, ab_i_bias, ab_f_bias, ab_ml_norm_g,
           ab_lam_q1, ab_lam_k1, ab_lam_q2, ab_lam_k2, ab_da_norm_g, ab_w_out, c_norm_g, c_w_in, c_ln_g,
           c_ln_b, c_sgu_w, c_sgu_b, c_w_out, ple_norm_g, ple_gate_w, ple_proj_w, final_norm_g):
    B, S, D = x.shape
    M = B * S
    assert ab_norm_g.shape[0] == 1 and c_norm_g.shape[0] == 1 and p.shape[0] == 2
    x2d = x.reshape(M, D)
    p2d = p.reshape(2, M, p.shape[-1])

    w_in = ab_w_in[0]
    gate_lo = ML_V_OFF + ML_HEADS * ML_V_DIM
    gate_hi = gate_lo + 2 * ML_HEADS
    w_main = jnp.concatenate([w_in[:, :gate_lo], w_in[:, gate_hi:]], axis=1).astype(BF16)
    w_gate = jnp.pad(w_in[:, gate_lo:gate_hi], ((0, 0), (0, GATE_COLS - 2 * ML_HEADS))).astype(BF16)
    proj, gates = _norm_proj(x2d, ab_norm_g[0], w_main, w_gate, name="in_proj_ab")

    a_out = _mlstm(proj, gates, ab_conv_w[0], ab_conv_b[0], ab_i_bias[0], ab_f_bias[0], ab_ml_norm_g[0], B, S)

    cos2, sin2 = _rope_tables(positions)
    lam_init = 0.8 - 0.6 * math.exp(-0.3 * 0)
    b_out = _diff_attn(proj, cos2, sin2, ab_lam_q1[0], ab_lam_k1[0], ab_lam_q2[0], ab_lam_k2[0],
                       ab_da_norm_g[0], lam_init, B, S)

    x2 = _out_ple_ab(a_out, b_out, x2d, p2d[0], ab_w_out[0].astype(BF16), ple_norm_g[0],
                     ple_gate_w[0].astype(BF16), ple_proj_w[0].astype(BF16))

    (uvz,) = _norm_proj(x2, c_norm_g[0], c_w_in[0].astype(BF16), name="in_proj_c")
    out = _sgu_out_ple(uvz, x2, p2d[1], c_ln_g[0], c_ln_b[0], c_sgu_w[0], c_sgu_b[0],
                       c_w_out[0].astype(BF16), ple_norm_g[1], ple_gate_w[1].astype(BF16),
                       ple_proj_w[1].astype(BF16), final_norm_g)
    return out.reshape(B, S, D)
```

```python
import functools
import math

import jax
import jax.numpy as jnp
from jax import lax
from jax.experimental import pallas as pl
from jax.experimental.pallas import tpu as pltpu

F32 = jnp.float32
BF16 = jnp.bfloat16

EPS = 1e-6
ROPE_THETA = 10000.0
CONV_WIDTH = 4

ML_HEADS = 4
ML_QK_DIM = 128
ML_V_DIM = 256
ML_CHUNK = 128
DA_HEADS = 4
DA_QK_DIM = 128
DA_V_DIM = 256
SGU_CHUNK = 128
SGU_GROUPS = 8
SGU_GROUP_DIM = 256

ML_Q_OFF, ML_K_OFF, ML_V_OFF, ML_O_OFF, ML_Z_OFF = 0, 512, 1024, 2048, 3072
DA_Q_OFF, DA_K_OFF, DA_V_OFF, DA_Z_OFF = 4096, 5120, 6144, 7168
AB_MAIN_COLS = 8192
GATE_COLS = 128

VMEM_LIMIT_BYTES = 56 * 1024 * 1024

PROJ_TM = 1024
PROJ_TN = 1024
NORM_ROWS = 128
TAIL_TM = 512
TAIL_NC = 512
ML_UNROLL = 4
ATT_TQ = 512
ATT_TK = 512
ATT_QSUB = 256
ATT_ROPE_ROWS = 256


def _params(semantics):
    return pltpu.CompilerParams(dimension_semantics=semantics, vmem_limit_bytes=VMEM_LIMIT_BYTES)


def _resident(shape, index_map):
    return pl.BlockSpec(shape, index_map, pipeline_mode=pl.Buffered(1))


def _sigmoid(x):
    return 1.0 / (1.0 + jnp.exp(-x))


def _silu(x):
    return x * _sigmoid(x)


def _log_sigmoid(x):
    return jnp.minimum(x, 0.0) - jnp.log(1.0 + jnp.exp(-jnp.abs(x)))


def _rope_table_kernel(pos_ref, cos_ref, sin_ref):
    pos = pos_ref[0]
    half_rows = pos.shape[0]
    lane = lax.broadcasted_iota(jnp.int32, pos.shape, 1)
    lo = lane < 64
    j = jnp.where(lo, lane, lane - 64).astype(F32)
    freq = jnp.exp(j * (-math.log(ROPE_THETA) / 64.0))
    ang = pos * freq
    c = jnp.cos(ang)
    s = jnp.sin(ang)
    cr = pltpu.roll(c, 64, 1)
    sr = pltpu.roll(s, 64, 1)
    cos_ref[0, 0:half_rows, :] = jnp.where(lo, c, cr)
    cos_ref[0, half_rows:2 * half_rows, :] = jnp.where(lo, cr, c)
    sin_ref[0, 0:half_rows, :] = jnp.where(lo, -s, sr)
    sin_ref[0, half_rows:2 * half_rows, :] = jnp.where(lo, -sr, s)


def _rope_tables(positions):
    B, S = positions.shape
    posf = positions.astype(F32)
    lo = jnp.broadcast_to(posf[:, :S // 2, None], (B, S // 2, 64))
    hi = jnp.broadcast_to(posf[:, S // 2:, None], (B, S // 2, 64))
    pos2 = jnp.concatenate([lo, hi], axis=-1)
    return pl.pallas_call(
        _rope_table_kernel,
        grid=(B,),
        in_specs=[pl.BlockSpec((1, S // 2, 128), lambda b: (b, 0, 0))],
        out_specs=[pl.BlockSpec((1, S, 128), lambda b: (b, 0, 0)),
                   pl.BlockSpec((1, S, 128), lambda b: (b, 0, 0))],
        out_shape=[jax.ShapeDtypeStruct((B, S, 128), F32)] * 2,
        compiler_params=_params(("arbitrary",)),
        name="rope_tables",
    )(pos2)


def _normalize_rows(x_ref, g_ref, xn_ref):
    g = g_ref[...]

    def body(r, carry):
        rows = pl.ds(pl.multiple_of(r * NORM_ROWS, NORM_ROWS), NORM_ROWS)
        x = x_ref[rows, :]
        ms = jnp.mean(x * x, axis=-1, keepdims=True)
        xn_ref[rows, :] = (x * lax.rsqrt(ms + EPS) * g).astype(BF16)
        return carry

    lax.fori_loop(0, x_ref.shape[0] // NORM_ROWS, body, 0)


def _norm_proj_kernel(x_ref, g_ref, w_ref, o_ref, xn_ref):
    @pl.when(pl.program_id(1) == 0)
    def _():
        _normalize_rows(x_ref, g_ref, xn_ref)

    o_ref[...] = jnp.dot(xn_ref[...], w_ref[...], preferred_element_type=F32).astype(o_ref.dtype)


def _norm_proj_gates_kernel(x_ref, g_ref, w_ref, wg_ref, o_ref, og_ref, xn_ref):
    @pl.when(pl.program_id(1) == 0)
    def _():
        _normalize_rows(x_ref, g_ref, xn_ref)
        og_ref[...] = jnp.dot(xn_ref[...], wg_ref[...], preferred_element_type=F32)

    o_ref[...] = jnp.dot(xn_ref[...], w_ref[...], preferred_element_type=F32).astype(o_ref.dtype)


def _norm_proj(x2d, g, w, wg=None, name="norm_proj"):
    M, D = x2d.shape
    N = w.shape[1]
    grid = (M // PROJ_TM, N // PROJ_TN)
    in_specs = [pl.BlockSpec((PROJ_TM, D), lambda i, j: (i, 0)),
                pl.BlockSpec((1, D), lambda i, j: (0, 0)),
                pl.BlockSpec((D, PROJ_TN), lambda i, j: (0, j))]
    out_specs = [pl.BlockSpec((PROJ_TM, PROJ_TN), lambda i, j: (i, j))]
    out_shape = [jax.ShapeDtypeStruct((M, N), BF16)]
    args = [x2d, g.reshape(1, D), w]
    kern = _norm_proj_kernel
    if wg is not None:
        in_specs.append(pl.BlockSpec((D, GATE_COLS), lambda i, j: (0, 0)))
        out_specs.append(pl.BlockSpec((PROJ_TM, GATE_COLS), lambda i, j: (i, 0)))
        out_shape.append(jax.ShapeDtypeStruct((M, GATE_COLS), F32))
        args.append(wg)
        kern = _norm_proj_gates_kernel
    return pl.pallas_call(
        kern,
        grid=grid,
        in_specs=in_specs,
        out_specs=out_specs,
        out_shape=out_shape,
        scratch_shapes=[pltpu.VMEM((PROJ_TM, D), BF16)],
        compiler_params=_params(("arbitrary", "arbitrary")),
        name=name,
    )(*args)


def _mlstm_kernel(qp_ref, kp_ref, v_ref, gt_ref, og_ref, z_ref, cwq_ref, cwk_ref, cbq_ref, cbk_ref,
                  gb_ref, ng_ref, out_ref, xs_ref, qc_ref, kc_ref, bcol_ref, rowq_ref, ctloc_ref, ctprev_ref,
                  prev_ref):
    h = pl.program_id(1)
    S = qp_ref.shape[0]
    L = ML_CHUNK
    nc = S // L

    for src, cw_ref, cb_ref, dst, scale in ((qp_ref, cwq_ref, cbq_ref, qc_ref, 1.0),
                                            (kp_ref, cwk_ref, cbk_ref, kc_ref, ML_QK_DIM ** -0.5)):
        xs_ref[0:8, :] = jnp.zeros((8, ML_QK_DIM), F32)
        xs_ref[8:8 + S, :] = src[...].astype(F32)
        cw = cw_ref[...]
        cb = cb_ref[...]
        for c in range(nc):
            acc = cb
            for j in range(CONV_WIDTH):
                start = 8 + c * L - (CONV_WIDTH - 1) + j
                acc = acc + cw[j:j + 1, :] * xs_ref[start:start + L, :]
            dst[c * L:(c + 1) * L, :] = (_silu(acc) * scale).astype(BF16)

    sub = lax.broadcasted_iota(jnp.int32, (L, L), 0)
    lane = lax.broadcasted_iota(jnp.int32, (L, L), 1)
    causal = lane <= sub
    tril_f = causal.astype(F32)
    triu_f = (sub <= lane).astype(F32)
    gate_bias = gb_ref[...]
    norm_g = ng_ref[0]

    def sel_col(x, idx):
        return jnp.sum(jnp.where(lane == idx, x, 0.0), axis=1, keepdims=True)

    def sel_row(x, idx):
        return jnp.sum(jnp.where(sub == idx, x, 0.0), axis=0, keepdims=True)

    sub8 = lax.broadcasted_iota(jnp.int32, (8, L), 0)

    def pack_rows(*rows_1xl):
        tile = jnp.zeros((8, L), F32)
        for r, row in enumerate(rows_1xl):
            tile = jnp.where(sub8 == r, jnp.broadcast_to(row, (8, L)), tile)
        return tile

    def local_state(c, carry):
        rows = pl.ds(pl.multiple_of(c * L, L), L)
        gates = gt_ref[rows, :] + gate_bias
        gates_t = gates.T
        logf = _log_sigmoid(gates)
        logf_t = _log_sigmoid(gates_t)
        cum = jnp.dot(tril_f, logf, preferred_element_type=F32, precision=lax.Precision.HIGHEST)
        cum_t = jnp.dot(logf_t, triu_f, preferred_element_type=F32, precision=lax.Precision.HIGHEST)
        b_col = sel_col(cum, h + ML_HEADS)
        b_row = sel_row(cum_t, h + ML_HEADS)
        i_row = sel_row(gates_t, h)
        g_tot = b_row[:, L - 1:L]
        bcol_ref[rows, :] = jnp.broadcast_to(b_col, (L, L))

        k = kc_ref[rows, :]
        k_t = k.astype(F32).T
        w_row = g_tot - b_row + i_row
        m_loc = jnp.max(w_row, axis=1, keepdims=True)
        e_row = jnp.exp(w_row - m_loc)
        kte = (k_t * e_row).astype(BF16)
        ctloc_ref[c] = jnp.dot(kte, v_ref[rows, :], preferred_element_type=F32)
        n_loc = jnp.dot(jnp.broadcast_to(e_row, (8, L)).astype(BF16), k,
                        preferred_element_type=F32)[0:1, :]
        rowq_ref[c] = pack_rows(b_row, i_row, n_loc, jnp.broadcast_to(m_loc, (1, L)),
                                jnp.broadcast_to(g_tot, (1, L)))
        return carry

    lax.fori_loop(0, nc, local_state, 0, unroll=ML_UNROLL)

    ct = jnp.zeros((ML_QK_DIM, ML_V_DIM), F32)
    n = jnp.zeros((1, ML_QK_DIM), F32)
    m = jnp.zeros((1, L), F32)
    for c in range(nc):
        ctprev_ref[c] = ct.astype(BF16)
        prev_ref[c] = pack_rows(n, m)
        tile = rowq_ref[c]
        n_loc, m_loc, g_tot = tile[2:3, :], tile[3:4, :], tile[4:5, :]
        m_new = jnp.maximum(g_tot + m, m_loc)
        a = jnp.exp(g_tot + m - m_new)
        cc = jnp.exp(m_loc - m_new)
        ct = a[:, 0:1] * ct + cc[:, 0:1] * ctloc_ref[c]
        n = a * n + cc * n_loc
        m = m_new

    def readout(c, carry):
        rows = pl.ds(pl.multiple_of(c * L, L), L)
        tile = rowq_ref[c]
        b_row, i_row = tile[0:1, :], tile[1:2, :]
        prev = prev_ref[c]
        n_prev, m_prev = prev[0:1, :], prev[1:2, 0:1]
        b_colb = bcol_ref[rows, :]
        b_col = b_colb[:, 0:1]
        q = qc_ref[rows, :]
        k = kc_ref[rows, :]
        v = v_ref[rows, :]

        log_d = jnp.where(causal, b_colb - b_row + i_row, -jnp.inf)
        inter_log = b_col + m_prev
        m_t = jnp.maximum(inter_log, jnp.max(log_d, axis=1, keepdims=True))
        d_mat = jnp.exp(log_d - m_t)
        inter_w = jnp.exp(inter_log - m_t)
        qk = lax.dot_general(q, k, (((1,), (1,)), ((), ())), preferred_element_type=F32)
        s_mat = qk * d_mat
        num = (jnp.dot(s_mat.astype(BF16), v, preferred_element_type=F32)
               + inter_w * jnp.dot(q, ctprev_ref[c], preferred_element_type=F32))
        den = (jnp.sum(s_mat, axis=1, keepdims=True)
               + inter_w * jnp.sum(q.astype(F32) * n_prev, axis=1, keepdims=True))
        hm = num / jnp.maximum(jnp.abs(den), jnp.exp(-m_t))

        hn = hm * lax.rsqrt(jnp.mean(hm * hm, axis=-1, keepdims=True) + EPS) * norm_g
        og = og_ref[rows, :].astype(F32)
        z = z_ref[rows, :].astype(F32)
        out_ref[rows, :] = (_sigmoid(og) * hn * _silu(z)).astype(out_ref.dtype)
        return carry

    lax.fori_loop(0, nc, readout, 0, unroll=ML_UNROLL)


def _mlstm(proj, gates, conv_w, conv_b, i_bias, f_bias, norm_g, B, S):
    M = B * S
    H = ML_HEADS
    gate_bias = jnp.concatenate([i_bias, f_bias, jnp.zeros((GATE_COLS - 2 * H,), F32)]).reshape(1, GATE_COLS)
    conv_b2 = conv_b.reshape(1, -1)
    norm_g3 = norm_g.reshape(H, 1, ML_V_DIM)
    qb, kb = ML_Q_OFF // ML_QK_DIM, ML_K_OFF // ML_QK_DIM
    vb, ob, zb = ML_V_OFF // ML_V_DIM, ML_O_OFF // ML_V_DIM, ML_Z_OFF // ML_V_DIM
    return pl.pallas_call(
        _mlstm_kernel,
        grid=(B, H),
        in_specs=[
            pl.BlockSpec((S, ML_QK_DIM), lambda b, h: (b, qb + h)),
            pl.BlockSpec((S, ML_QK_DIM), lambda b, h: (b, kb + h)),
            pl.BlockSpec((S, ML_V_DIM), lambda b, h: (b, vb + h)),
            pl.BlockSpec((S, GATE_COLS), lambda b, h: (b, 0)),
            pl.BlockSpec((S, ML_V_DIM), lambda b, h: (b, ob + h)),
            pl.BlockSpec((S, ML_V_DIM), lambda b, h: (b, zb + h)),
            pl.BlockSpec((CONV_WIDTH, ML_QK_DIM), lambda b, h: (0, h)),
            pl.BlockSpec((CONV_WIDTH, ML_QK_DIM), lambda b, h: (0, H + h)),
            pl.BlockSpec((1, ML_QK_DIM), lambda b, h: (0, h)),
            pl.BlockSpec((1, ML_QK_DIM), lambda b, h: (0, H + h)),
            pl.BlockSpec((1, GATE_COLS), lambda b, h: (0, 0)),
            pl.BlockSpec((1, 1, ML_V_DIM), lambda b, h: (h, 0, 0)),
        ],
        out_specs=pl.BlockSpec((S, ML_V_DIM), lambda b, h: (b, h)),
        out_shape=jax.ShapeDtypeStruct((M, H * ML_V_DIM), BF16),
        scratch_shapes=[pltpu.VMEM((8 + S, ML_QK_DIM), F32),
                        pltpu.VMEM((S, ML_QK_DIM), BF16),
                        pltpu.VMEM((S, ML_QK_DIM), BF16),
                        pltpu.VMEM((S, ML_CHUNK), F32),
                        pltpu.VMEM((S // ML_CHUNK, 8, ML_CHUNK), F32),
                        pltpu.VMEM((S // ML_CHUNK, ML_QK_DIM, ML_V_DIM), F32),
                        pltpu.VMEM((S // ML_CHUNK, ML_QK_DIM, ML_V_DIM), BF16),
                        pltpu.VMEM((S // ML_CHUNK, 8, ML_QK_DIM), F32)],
        compiler_params=_params(("arbitrary", "arbitrary")),
        name="mlstm",
    )(proj, proj, proj, gates, proj, proj, conv_w, conv_w, conv_b2, conv_b2, gate_bias, norm_g3)


def _diff_attn_kernel(lam_init, q_ref, k_ref, v_ref, z_ref, cos_ref, sin_ref, lq1_ref, lk1_ref, lq2_ref,
                      lk2_ref, ng_ref, o_ref, qr_ref, kr_ref, acc_ref):
    S = k_ref.shape[0]
    TQ, TK, QS, DH = ATT_TQ, ATT_TK, ATT_QSUB, DA_QK_DIM
    n_sub = TQ // QS

    def rope(x, cs, sn):
        return x * cs + pltpu.roll(x, DH // 2, 1) * sn

    q_scale = DH ** -0.5 * math.log2(math.e)

    def rope_rows(c, carry):
        rows = pl.ds(pl.multiple_of(c * ATT_ROPE_ROWS, ATT_ROPE_ROWS), ATT_ROPE_ROWS)
        cs = cos_ref[0, rows, :]
        sn = sin_ref[0, rows, :]
        for m in range(2):
            cols = slice(m * DH, (m + 1) * DH)
            kr_ref[rows, cols] = rope(k_ref[rows, cols].astype(F32), cs, sn).astype(BF16)
            qr_ref[rows, cols] = (rope(q_ref[rows, cols].astype(F32), cs, sn) * q_scale).astype(BF16)
        return carry

    lax.fori_loop(0, S // ATT_ROPE_ROWS, rope_rows, 0)

    def chain(q0, m, stats, start, ncols, mask):
        q_rows = slice(q0, q0 + QS)
        kv_rows = slice(start, start + ncols)
        s = lax.dot_general(qr_ref[q_rows, m * DH:(m + 1) * DH], kr_ref[kv_rows, m * DH:(m + 1) * DH],
                            (((1,), (1,)), ((), ())), preferred_element_type=F32)
        if mask is not None:
            s = jnp.where(mask, s, -jnp.inf)
        row_max = jnp.max(s, axis=1, keepdims=True)
        if stats is None:
            m_new = row_max
            p = jnp.exp2(s - m_new)
            l_new = jnp.sum(p, axis=1, keepdims=True)
            acc_ref[m, q_rows, :] = jnp.dot(p.astype(BF16), v_ref[kv_rows, :], preferred_element_type=F32)
        else:
            m_prev, l_prev = stats
            m_new = jnp.maximum(m_prev, row_max)
            p = jnp.exp2(s - m_new)
            alpha = jnp.exp2(m_prev - m_new)
            l_new = alpha * l_prev + jnp.sum(p, axis=1, keepdims=True)
            acc_ref[m, q_rows, :] = (alpha * acc_ref[m, q_rows, :]
                                     + jnp.dot(p.astype(BF16), v_ref[kv_rows, :], preferred_element_type=F32))
        return m_new, l_new

    lam = (jnp.exp(jnp.sum(lq1_ref[...] * lk1_ref[...], axis=1, keepdims=True))
           - jnp.exp(jnp.sum(lq2_ref[...] * lk2_ref[...], axis=1, keepdims=True)) + lam_init)
    norm_g = ng_ref[0] * (1.0 - lam_init)

    for i in range(S // TQ):
        stats = [None] * (2 * n_sub)
        for j in range(i):
            for qs in range(n_sub):
                for m in range(2):
                    stats[qs * 2 + m] = chain(i * TQ + qs * QS, m, stats[qs * 2 + m], j * TK, TK, None)
        for qs in range(n_sub):
            ncols = (qs + 1) * QS
            mask = (lax.broadcasted_iota(jnp.int32, (QS, ncols), 1)
                    <= lax.broadcasted_iota(jnp.int32, (QS, ncols), 0) + qs * QS)
            for m in range(2):
                stats[qs * 2 + m] = chain(i * TQ + qs * QS, m, stats[qs * 2 + m], i * TK, ncols, mask)
        for qs in range(n_sub):
            rows = slice(i * TQ + qs * QS, i * TQ + (qs + 1) * QS)
            l1 = stats[qs * 2][1]
            l2 = stats[qs * 2 + 1][1]
            o = acc_ref[0, rows, :] / l1 - lam * (acc_ref[1, rows, :] / l2)
            hn = o * lax.rsqrt(jnp.mean(o * o, axis=-1, keepdims=True) + EPS) * norm_g
            o_ref[rows, :] = (hn * _silu(z_ref[rows, :].astype(F32))).astype(o_ref.dtype)


def _diff_attn(proj, cos2, sin2, lq1, lk1, lq2, lk2, norm_g, lam_init, B, S):
    M = B * S
    H = DA_HEADS
    qb, kb, vb, zb = (DA_Q_OFF // DA_V_DIM, DA_K_OFF // DA_V_DIM, DA_V_OFF // DA_V_DIM, DA_Z_OFF // DA_V_DIM)
    vec = lambda a: a.reshape(1, DA_QK_DIM)
    vec_spec = pl.BlockSpec((1, DA_QK_DIM), lambda b, h: (0, 0))
    table_spec = pl.BlockSpec((1, S, DA_QK_DIM), lambda b, h: (b, 0, 0))
    return pl.pallas_call(
        functools.partial(_diff_attn_kernel, lam_init),
        grid=(B, H),
        in_specs=[
            pl.BlockSpec((S, DA_V_DIM), lambda b, h: (b, qb + h)),
            pl.BlockSpec((S, DA_V_DIM), lambda b, h: (b, kb + h)),
            pl.BlockSpec((S, DA_V_DIM), lambda b, h: (b, vb + h)),
            pl.BlockSpec((S, DA_V_DIM), lambda b, h: (b, zb + h)),
            table_spec, table_spec,
            vec_spec, vec_spec, vec_spec, vec_spec,
            pl.BlockSpec((1, 1, DA_V_DIM), lambda b, h: (h, 0, 0)),
        ],
        out_specs=pl.BlockSpec((S, DA_V_DIM), lambda b, h: (b, h)),
        out_shape=jax.ShapeDtypeStruct((M, H * DA_V_DIM), BF16),
        scratch_shapes=[pltpu.VMEM((S, DA_V_DIM), BF16),
                        pltpu.VMEM((S, DA_V_DIM), BF16),
                        pltpu.VMEM((2, S, DA_V_DIM), F32)],
        compiler_params=_params(("arbitrary", "arbitrary")),
        name="diff_attn",
    )(proj, proj, proj, proj, cos2, sin2, vec(lq1), vec(lk1), vec(lq2), vec(lk2),
      norm_g.reshape(H, 1, DA_V_DIM))


def _ple_tail(x1_ref, ssq, p_ref, png_ref, gw_ref, pw_ref, xn_ref, emit):
    D = x1_ref.shape[1]
    inv = lax.rsqrt(ssq * (1.0 / D) + EPS)
    for n0 in range(0, D, TAIL_NC):
        cols = slice(n0, n0 + TAIL_NC)
        xn_ref[:, cols] = (x1_ref[:, cols] * inv * png_ref[:, cols]).astype(BF16)
    p_bf = p_ref[...].astype(BF16)
    for n0 in range(0, D, TAIL_NC):
        cols = slice(n0, n0 + TAIL_NC)
        gate = _sigmoid(jnp.dot(xn_ref[...], gw_ref[:, cols], preferred_element_type=F32))
        pp = jnp.dot(p_bf, pw_ref[:, cols], preferred_element_type=F32)
        emit(n0, x1_ref[:, cols] + gate * pp)


def _out_ple_ab_kernel(a_ref, b_ref, x_ref, p_ref, wo_ref, png_ref, gw_ref, pw_ref, o_ref, x1_ref, xn_ref):
    D = x_ref.shape[1]
    ka = a_ref.shape[1]
    ssq = jnp.zeros((x_ref.shape[0], 1), F32)
    for n0 in range(0, D, TAIL_NC):
        cols = slice(n0, n0 + TAIL_NC)
        y = (jnp.dot(a_ref[...], wo_ref[0:ka, cols], preferred_element_type=F32)
             + jnp.dot(b_ref[...], wo_ref[ka:, cols], preferred_element_type=F32)
             + x_ref[:, cols])
        x1_ref[:, cols] = y
        ssq = ssq + jnp.sum(y * y, axis=1, keepdims=True)

    def emit(n0, val):
        o_ref[:, n0:n0 + TAIL_NC] = val

    _ple_tail(x1_ref, ssq, p_ref, png_ref, gw_ref, pw_ref, xn_ref, emit)


def _out_ple_ab(a_out, b_out, x2d, p2d, w_out, ple_g, gate_w, proj_w):
    M, D = x2d.shape
    KA = a_out.shape[1]
    P = p2d.shape[1]
    TM = TAIL_TM
    row = lambda i: (i, 0)
    fixed = lambda i: (0, 0)
    return pl.pallas_call(
        _out_ple_ab_kernel,
        grid=(M // TM,),
        in_specs=[
            pl.BlockSpec((TM, KA), row),
            pl.BlockSpec((TM, b_out.shape[1]), row),
            pl.BlockSpec((TM, D), row),
            pl.BlockSpec((TM, P), row),
            _resident(w_out.shape, fixed),
            _resident((1, D), fixed),
            _resident(gate_w.shape, fixed),
            _resident(proj_w.shape, fixed),
        ],
        out_specs=pl.BlockSpec((TM, D), row),
        out_shape=jax.ShapeDtypeStruct((M, D), F32),
        scratch_shapes=[pltpu.VMEM((TM, D), F32), pltpu.VMEM((TM, D), BF16)],
        compiler_params=_params(("arbitrary",)),
        name="out_ple_ab",
    )(a_out, b_out, x2d, p2d, w_out, ple_g.reshape(1, D), gate_w, proj_w)


def _sgu_out_ple_kernel(u_ref, v_ref, z_ref, x_ref, p_ref, lng_ref, lnb_ref, sw_ref, sbt_ref, wo_ref,
                        png_ref, gw_ref, pw_ref, fng_ref, o_ref, t_ref, x1_ref, xn_ref):
    TM, D = x_ref.shape
    L = SGU_CHUNK
    GD = SGU_GROUP_DIM
    causal = (lax.broadcasted_iota(jnp.int32, (L, L), 1) <= lax.broadcasted_iota(jnp.int32, (L, L), 0))
    w_causal = [jnp.where(causal, sw_ref[g], 0.0).astype(BF16) for g in range(SGU_GROUPS)]
    lng = lng_ref[...]
    lnb = lnb_ref[...]

    for c in range(TM // L):
        rows = slice(c * L, (c + 1) * L)
        vv = v_ref[rows, :].astype(F32)
        mu = jnp.mean(vv, axis=-1, keepdims=True)
        vc = vv - mu
        var = jnp.mean(vc * vc, axis=-1, keepdims=True)
        vn = (vc * lax.rsqrt(var + EPS) * lng + lnb).astype(BF16)
        for g in range(SGU_GROUPS):
            cols = slice(g * GD, (g + 1) * GD)
            sv = jnp.dot(w_causal[g], vn[:, cols], preferred_element_type=F32) + sbt_ref[:, g:g + 1]
            t = u_ref[rows, cols].astype(F32) * sv * _silu(z_ref[rows, cols].astype(F32))
            t_ref[rows, cols] = t.astype(BF16)

    ssq = jnp.zeros((TM, 1), F32)
    for n0 in range(0, D, TAIL_NC):
        cols = slice(n0, n0 + TAIL_NC)
        y = jnp.dot(t_ref[...], wo_ref[:, cols], preferred_element_type=F32) + x_ref[:, cols]
        x1_ref[:, cols] = y
        ssq = ssq + jnp.sum(y * y, axis=1, keepdims=True)

    ssq2 = [jnp.zeros((TM, 1), F32)]

    def emit(n0, val):
        o_ref[:, n0:n0 + TAIL_NC] = val
        ssq2[0] = ssq2[0] + jnp.sum(val * val, axis=1, keepdims=True)

    _ple_tail(x1_ref, ssq, p_ref, png_ref, gw_ref, pw_ref, xn_ref, emit)

    inv = lax.rsqrt(ssq2[0] * (1.0 / D) + EPS)
    for n0 in range(0, D, TAIL_NC):
        cols = slice(n0, n0 + TAIL_NC)
        o_ref[:, cols] = o_ref[:, cols] * inv * fng_ref[:, cols]


def _sgu_out_ple(uvz, x2d, p2d, ln_g, ln_b, sgu_w, sgu_b, w_out, ple_g, gate_w, proj_w, final_g):
    M, D = x2d.shape
    P = p2d.shape[1]
    TM = TAIL_TM
    row = lambda i: (i, 0)
    fixed = lambda i: (0, 0)
    return pl.pallas_call(
        _sgu_out_ple_kernel,
        grid=(M // TM,),
        in_specs=[
            pl.BlockSpec((TM, D), lambda i: (i, 0)),
            pl.BlockSpec((TM, D), lambda i: (i, 1)),
            pl.BlockSpec((TM, D), lambda i: (i, 2)),
            pl.BlockSpec((TM, D), row),
            pl.BlockSpec((TM, P), row),
            _resident((1, D), fixed),
            _resident((1, D), fixed),
            _resident(sgu_w.shape, lambda i: (0, 0, 0)),
            _resident((SGU_CHUNK, SGU_GROUPS), fixed),
            _resident(w_out.shape, fixed),
            _resident((1, D), fixed),
            _resident(gate_w.shape, fixed),
            _resident(proj_w.shape, fixed),
            _resident((1, D), fixed),
        ],
        out_specs=pl.BlockSpec((TM, D), row),
        out_shape=jax.ShapeDtypeStruct((M, D), F32),
        scratch_shapes=[pltpu.VMEM((TM, D), BF16), pltpu.VMEM((TM, D), F32), pltpu.VMEM((TM, D), BF16)],
        compiler_params=_params(("arbitrary",)),
        name="sgu_out_ple",
    )(uvz, uvz, uvz, x2d, p2d, ln_g.reshape(1, D), ln_b.reshape(1, D), sgu_w, sgu_b.T, w_out,
      ple_g.reshape(1, D), gate_w, proj_w, final_g.reshape(1, D))


def kernel(x, p, positions, ab_norm_g, ab_w_in, ab_conv_w, ab_conv_b, ab_i_bias, ab_f_bias, ab_ml_norm_g,
           ab_lam_q1, ab_lam_k1, ab_lam_q2, ab_lam_k2, ab_da_norm_g, ab_w_out, c_norm_g, c_w_in, c_ln_g,
           c_ln_b, c_sgu_w, c_sgu_b, c_w_out, ple_norm_g, ple_gate_w, ple_proj_w, final_norm_g):
    B, S, D = x.shape
    M = B * S
    assert ab_norm_g.shape[0] == 1 and c_norm_g.shape[0] == 1 and p.shape[0] == 2
    x2d = x.reshape(M, D)
    p2d = p.reshape(2, M, p.shape[-1])

    w_in = ab_w_in[0]
    gate_lo = ML_V_OFF + ML_HEADS * ML_V_DIM
    gate_hi = gate_lo + 2 * ML_HEADS
    w_main = jnp.concatenate([w_in[:, :gate_lo], w_in[:, gate_hi:]], axis=1).astype(BF16)
    w_gate = jnp.pad(w_in[:, gate_lo:gate_hi], ((0, 0), (0, GATE_COLS - 2 * ML_HEADS))).astype(BF16)
    proj, gates = _norm_proj(x2d, ab_norm_g[0], w_main, w_gate, name="in_proj_ab")

    a_out = _mlstm(proj, gates, ab_conv_w[0], ab_conv_b[0], ab_i_bias[0], ab_f_bias[0], ab_ml_norm_g[0], B, S)

    cos2, sin2 = _rope_tables(positions)
    lam_init = 0.8 - 0.6 * math.exp(-0.3 * 0)
    b_out = _diff_attn(proj, cos2, sin2, ab_lam_q1[0], ab_lam_k1[0], ab_lam_q2[0], ab_lam_k2[0],
                       ab_da_norm_g[0], lam_init, B, S)

    x2 = _out_ple_ab(a_out, b_out, x2d, p2d[0], ab_w_out[0].astype(BF16), ple_norm_g[0],
                     ple_gate_w[0].astype(BF16), ple_proj_w[0].astype(BF16))

    (uvz,) = _norm_proj(x2, c_norm_g[0], c_w_in[0].astype(BF16), name="in_proj_c")
    out = _sgu_out_ple(uvz, x2, p2d[1], c_ln_g[0], c_ln_b[0], c_sgu_w[0], c_sgu_b[0],
                       c_w_out[0].astype(BF16), ple_norm_g[1], ple_gate_w[1].astype(BF16),
                       ple_proj_w[1].astype(BF16), final_norm_g)
    return out.reshape(B, S, D)
```

```python
import functools
import math

import jax
import jax.numpy as jnp
from jax import lax
from jax.experimental import pallas as pl
from jax.experimental.pallas import tpu as pltpu

F32 = jnp.float32
BF16 = jnp.bfloat16
HIGHEST = lax.Precision.HIGHEST

EPS = 1e-6
ROPE_THETA = 10000.0
CONV_WIDTH = 4

ML_HEADS = 4
ML_QK_DIM = 128
ML_V_DIM = 256
DA_HEADS = 4
DA_QK_DIM = 128
DA_V_DIM = 256
SGU_CHUNK = 128
SGU_GROUPS = 8
SGU_GROUP_DIM = 256

ML_Q_OFF, ML_K_OFF, ML_V_OFF, ML_O_OFF, ML_Z_OFF = 0, 512, 1024, 2048, 3072
DA_Q_OFF, DA_K_OFF, DA_V_OFF, DA_Z_OFF = 4096, 5120, 6144, 7168
GATE_COLS = 128

VMEM_LIMIT_BYTES = 56 * 1024 * 1024

PROJ_TM = 1024
PROJ_TN = 1024
NORM_ROWS = 128
TAIL_TM = 512
TAIL_NC = 512
CONV_ROWS = 128
ML_BLOCK = 256
ATT_TQ = 512
ATT_TK = 512
ATT_QSUB = 256
ATT_ROPE_ROWS = 256


def _params(semantics):
    return pltpu.CompilerParams(dimension_semantics=semantics, vmem_limit_bytes=VMEM_LIMIT_BYTES)


def _resident(shape, index_map):
    return pl.BlockSpec(shape, index_map, pipeline_mode=pl.Buffered(1))


def _sigmoid(x):
    return 1.0 / (1.0 + jnp.exp(-x))


def _silu(x):
    return x * _sigmoid(x)


def _log_sigmoid(x):
    return jnp.minimum(x, 0.0) - jnp.log(1.0 + jnp.exp(-jnp.abs(x)))


def _iota(shape, axis):
    return lax.broadcasted_iota(jnp.int32, shape, axis)


def _rope_table_kernel(pos_ref, cos_ref, sin_ref):
    pos = pos_ref[0]
    half_rows = pos.shape[0]
    lane = _iota(pos.shape, 1)
    lo = lane < 64
    j = jnp.where(lo, lane, lane - 64).astype(F32)
    freq = jnp.exp(j * (-math.log(ROPE_THETA) / 64.0))
    ang = pos * freq
    c = jnp.cos(ang)
    s = jnp.sin(ang)
    cr = pltpu.roll(c, 64, 1)
    sr = pltpu.roll(s, 64, 1)
    cos_ref[0, 0:half_rows, :] = jnp.where(lo, c, cr)
    cos_ref[0, half_rows:2 * half_rows, :] = jnp.where(lo, cr, c)
    sin_ref[0, 0:half_rows, :] = jnp.where(lo, -s, sr)
    sin_ref[0, half_rows:2 * half_rows, :] = jnp.where(lo, -sr, s)


def _rope_tables(positions):
    B, S = positions.shape
    posf = positions.astype(F32)
    lo = jnp.broadcast_to(posf[:, :S // 2, None], (B, S // 2, 64))
    hi = jnp.broadcast_to(posf[:, S // 2:, None], (B, S // 2, 64))
    pos2 = jnp.concatenate([lo, hi], axis=-1)
    return pl.pallas_call(
        _rope_table_kernel,
        grid=(B,),
        in_specs=[pl.BlockSpec((1, S // 2, 128), lambda b: (b, 0, 0))],
        out_specs=[pl.BlockSpec((1, S, 128), lambda b: (b, 0, 0)),
                   pl.BlockSpec((1, S, 128), lambda b: (b, 0, 0))],
        out_shape=[jax.ShapeDtypeStruct((B, S, 128), F32)] * 2,
        compiler_params=_params(("arbitrary",)),
        name="rope_tables",
    )(pos2)


def _normalize_rows(x_ref, g_ref, xn_ref):
    g = g_ref[...]

    def body(r, carry):
        rows = pl.ds(pl.multiple_of(r * NORM_ROWS, NORM_ROWS), NORM_ROWS)
        x = x_ref[rows, :]
        ms = jnp.mean(x * x, axis=-1, keepdims=True)
        xn_ref[rows, :] = (x * lax.rsqrt(ms + EPS) * g).astype(BF16)
        return carry

    lax.fori_loop(0, x_ref.shape[0] // NORM_ROWS, body, 0)


def _norm_proj_kernel(tile_counts, has_gates, x_ref, g_ref, *refs):
    w_refs = refs[:len(tile_counts)]
    refs = refs[len(tile_counts):]
    if has_gates:
        wg_ref, o_ref, og_ref, xn_ref = refs
    else:
        o_ref, xn_ref = refs
    j = pl.program_id(1)

    @pl.when(j == 0)
    def _():
        _normalize_rows(x_ref, g_ref, xn_ref)
        if has_gates:
            og_ref[...] = jnp.dot(xn_ref[...], wg_ref[...], preferred_element_type=F32)

    lo = 0
    for w_ref, count in zip(w_refs, tile_counts):
        @pl.when((j >= lo) & (j < lo + count))
        def _(w_ref=w_ref):
            o_ref[...] = jnp.dot(xn_ref[...], w_ref[...], preferred_element_type=F32).astype(o_ref.dtype)
        lo += count


def _norm_proj(x2d, g, weights, wg=None, name="norm_proj"):
    M, D = x2d.shape
    tile_counts = tuple(w.shape[1] // PROJ_TN for w in weights)
    N = sum(w.shape[1] for w in weights)
    grid = (M // PROJ_TM, N // PROJ_TN)
    in_specs = [pl.BlockSpec((PROJ_TM, D), lambda i, j: (i, 0)),
                pl.BlockSpec((1, D), lambda i, j: (0, 0))]
    lo = 0
    for count in tile_counts:
        in_specs.append(pl.BlockSpec((D, PROJ_TN),
                                     lambda i, j, lo=lo, count=count: (0, jnp.clip(j - lo, 0, count - 1))))
        lo += count
    out_specs = [pl.BlockSpec((PROJ_TM, PROJ_TN), lambda i, j: (i, j))]
    out_shape = [jax.ShapeDtypeStruct((M, N), BF16)]
    args = [x2d, g.reshape(1, D), *weights]
    if wg is not None:
        in_specs.append(pl.BlockSpec((D, GATE_COLS), lambda i, j: (0, 0)))
        out_specs.append(pl.BlockSpec((PROJ_TM, GATE_COLS), lambda i, j: (i, 0)))
        out_shape.append(jax.ShapeDtypeStruct((M, GATE_COLS), F32))
        args.append(wg)
    return pl.pallas_call(
        functools.partial(_norm_proj_kernel, tile_counts, wg is not None),
        grid=grid,
        in_specs=in_specs,
        out_specs=out_specs,
        out_shape=out_shape,
        scratch_shapes=[pltpu.VMEM((PROJ_TM, D), BF16)],
        compiler_params=_params(("arbitrary", "arbitrary")),
        name=name,
    )(*args)


def _mlstm_kernel(qp_ref, kp_ref, v_ref, gt_ref, og_ref, z_ref, cwq_ref, cwk_ref, cbq_ref, cbk_ref,
                  gb_ref, ng_ref, out_ref, xs_ref, qc_ref, kc_ref):
    h = pl.program_id(1)
    S = qp_ref.shape[0]
    LB = ML_BLOCK
    nb = S // LB
    DK, DV = ML_QK_DIM, ML_V_DIM

    for src, cw_ref, cb_ref, dst, scale in ((qp_ref, cwq_ref, cbq_ref, qc_ref, 1.0),
                                            (kp_ref, cwk_ref, cbk_ref, kc_ref, DK ** -0.5)):
        xs_ref[0:8, :] = jnp.zeros((8, DK), F32)
        xs_ref[8:8 + S, :] = src[...].astype(F32)
        cw = cw_ref[...]
        cb = cb_ref[...]
        for c in range(S // CONV_ROWS):
            acc = cb
            for j in range(CONV_WIDTH):
                start = 8 + c * CONV_ROWS - (CONV_WIDTH - 1) + j
                acc = acc + cw[j:j + 1, :] * xs_ref[start:start + CONV_ROWS, :]
            dst[c * CONV_ROWS:(c + 1) * CONV_ROWS, :] = (_silu(acc) * scale).astype(BF16)

    gates = gt_ref[...] + gb_ref[...]
    pick = (_iota((8, GATE_COLS), 1) == _iota((8, GATE_COLS), 0)).astype(F32)
    gate_rows = lax.dot_general(pick, gates, (((1,), (1,)), ((), ())), preferred_element_type=F32,
                                precision=HIGHEST)
    sub8 = _iota((8, S), 0)
    i_all = jnp.sum(jnp.where(sub8 == h, gate_rows, 0.0), axis=0, keepdims=True)
    logf_all = _log_sigmoid(jnp.sum(jnp.where(sub8 == h + ML_HEADS, gate_rows, 0.0), axis=0, keepdims=True))

    blk = _iota((nb, LB), 0)

    def to_blocks(row):
        tile = jnp.zeros((nb, LB), F32)
        for c in range(nb):
            tile = jnp.where(blk == c, jnp.broadcast_to(row[:, c * LB:(c + 1) * LB], (nb, LB)), tile)
        return tile

    sub = _iota((LB, LB), 0)
    lane = _iota((LB, LB), 1)
    causal = lane <= sub
    i_b = to_blocks(i_all)
    b_b = jnp.dot(to_blocks(logf_all), (sub <= lane).astype(F32), preferred_element_type=F32,
                  precision=HIGHEST)
    g_tot = b_b[:, LB - 1:LB]
    w_b = g_tot - b_b + i_b
    m_loc = jnp.max(w_b, axis=1, keepdims=True)
    e_b = jnp.exp(w_b - m_loc)
    b_pad = jnp.concatenate([b_b, jnp.zeros((GATE_COLS - nb, LB), F32)], axis=0)
    b_cols = lax.dot_general((sub == lane).astype(F32), b_pad, (((1,), (1,)), ((), ())),
                             preferred_element_type=F32, precision=HIGHEST)

    norm_g = ng_ref[0]
    ct = jnp.zeros((DK, DV), F32)
    n = jnp.zeros((DK, 1), F32)
    m = jnp.zeros((1, 1), F32)
    for c in range(nb):
        rows = slice(c * LB, (c + 1) * LB)
        b_row, i_row, e_row = b_b[c:c + 1, :], i_b[c:c + 1, :], e_b[c:c + 1, :]
        b_col = b_cols[:, c:c + 1]
        q = qc_ref[rows, :]
        v = v_ref[rows, :]
        k_t = kc_ref[rows, :].astype(F32).T

        rhs = jnp.concatenate([k_t.astype(BF16), ct.astype(BF16),
                               jnp.broadcast_to(n, (DK, 128)).astype(BF16)], axis=1)
        big = jnp.dot(q, rhs, preferred_element_type=F32)
        qk, inter, qn = big[:, :LB], big[:, LB:LB + DV], big[:, LB + DV:LB + DV + 1]

        log_d = jnp.where(causal, b_col - b_row + i_row, -jnp.inf)
        inter_log = b_col + m
        m_t = jnp.maximum(inter_log, jnp.max(log_d, axis=1, keepdims=True))
        d_mat = jnp.exp(log_d - m_t)
        inter_w = jnp.exp(inter_log - m_t)
        s_mat = qk * d_mat
        num = jnp.dot(s_mat.astype(BF16), v, preferred_element_type=F32) + inter_w * inter
        den = jnp.sum(s_mat, axis=1, keepdims=True) + inter_w * qn
        hm = num / jnp.maximum(jnp.abs(den), jnp.exp(-m_t))

        hn = hm * lax.rsqrt(jnp.mean(hm * hm, axis=-1, keepdims=True) + EPS) * norm_g
        og = og_ref[rows, :].astype(F32)
        z = z_ref[rows, :].astype(F32)
        out_ref[rows, :] = (_sigmoid(og) * hn * _silu(z)).astype(out_ref.dtype)

        if c + 1 < nb:
            ke = k_t * e_row
            ct_loc = jnp.dot(ke.astype(BF16), v, preferred_element_type=F32)
            n_loc = jnp.sum(ke, axis=1, keepdims=True)
            g_c, ml_c = g_tot[c:c + 1, :], m_loc[c:c + 1, :]
            m_new = jnp.maximum(g_c + m, ml_c)
            a = jnp.exp(g_c + m - m_new)
            cc = jnp.exp(ml_c - m_new)
            ct = a * ct + cc * ct_loc
            n = a * n + cc * n_loc
            m = m_new


def _mlstm(proj, gates, conv_w, conv_b, i_bias, f_bias, norm_g, B, S):
    M = B * S
    H = ML_HEADS
    assert S // ML_BLOCK == 8, "block rows are packed into one 8-sublane tile"
    gate_bias = jnp.concatenate([i_bias, f_bias, jnp.zeros((GATE_COLS - 2 * H,), F32)]).reshape(1, GATE_COLS)
    conv_b2 = conv_b.reshape(1, -1)
    norm_g3 = norm_g.reshape(H, 1, ML_V_DIM)
    qb, kb = ML_Q_OFF // ML_QK_DIM, ML_K_OFF // ML_QK_DIM
    vb, ob, zb = ML_V_OFF // ML_V_DIM, ML_O_OFF // ML_V_DIM, ML_Z_OFF // ML_V_DIM
    return pl.pallas_call(
        _mlstm_kernel,
        grid=(B, H),
        in_specs=[
            pl.BlockSpec((S, ML_QK_DIM), lambda b, h: (b, qb + h)),
            pl.BlockSpec((S, ML_QK_DIM), lambda b, h: (b, kb + h)),
            pl.BlockSpec((S, ML_V_DIM), lambda b, h: (b, vb + h)),
            pl.BlockSpec((S, GATE_COLS), lambda b, h: (b, 0)),
            pl.BlockSpec((S, ML_V_DIM), lambda b, h: (b, ob + h)),
            pl.BlockSpec((S, ML_V_DIM), lambda b, h: (b, zb + h)),
            pl.BlockSpec((CONV_WIDTH, ML_QK_DIM), lambda b, h: (0, h)),
            pl.BlockSpec((CONV_WIDTH, ML_QK_DIM), lambda b, h: (0, H + h)),
            pl.BlockSpec((1, ML_QK_DIM), lambda b, h: (0, h)),
            pl.BlockSpec((1, ML_QK_DIM), lambda b, h: (0, H + h)),
            pl.BlockSpec((1, GATE_COLS), lambda b, h: (0, 0)),
            pl.BlockSpec((1, 1, ML_V_DIM), lambda b, h: (h, 0, 0)),
        ],
        out_specs=pl.BlockSpec((S, ML_V_DIM), lambda b, h: (b, h)),
        out_shape=jax.ShapeDtypeStruct((M, H * ML_V_DIM), BF16),
        scratch_shapes=[pltpu.VMEM((8 + S, ML_QK_DIM), F32),
                        pltpu.VMEM((S, ML_QK_DIM), BF16),
                        pltpu.VMEM((S, ML_QK_DIM), BF16)],
        compiler_params=_params(("arbitrary", "arbitrary")),
        name="mlstm",
    )(proj, proj, proj, gates, proj, proj, conv_w, conv_w, conv_b2, conv_b2, gate_bias, norm_g3)


def _diff_attn_kernel(lam_init, q_ref, k_ref, v_ref, z_ref, cos_ref, sin_ref, lq1_ref, lk1_ref, lq2_ref,
                      lk2_ref, ng_ref, o_ref, qr_ref, kr_ref, acc_ref):
    S = k_ref.shape[0]
    TQ, TK, QS, DH = ATT_TQ, ATT_TK, ATT_QSUB, DA_QK_DIM
    n_sub = TQ // QS

    def rope(x, cs, sn):
        return x * cs + pltpu.roll(x, DH // 2, 1) * sn

    q_scale = DH ** -0.5 * math.log2(math.e)

    def rope_rows(c, carry):
        rows = pl.ds(pl.multiple_of(c * ATT_ROPE_ROWS, ATT_ROPE_ROWS), ATT_ROPE_ROWS)
        cs = cos_ref[0, rows, :]
        sn = sin_ref[0, rows, :]
        for m in range(2):
            cols = slice(m * DH, (m + 1) * DH)
            kr_ref[rows, cols] = rope(k_ref[rows, cols].astype(F32), cs, sn).astype(BF16)
            qr_ref[rows, cols] = (rope(q_ref[rows, cols].astype(F32), cs, sn) * q_scale).astype(BF16)
        return carry

    lax.fori_loop(0, S // ATT_ROPE_ROWS, rope_rows, 0)

    def chain(q0, m, stats, start, ncols, mask):
        q_rows = slice(q0, q0 + QS)
        kv_rows = slice(start, start + ncols)
        s = lax.dot_general(qr_ref[q_rows, m * DH:(m + 1) * DH], kr_ref[kv_rows, m * DH:(m + 1) * DH],
                            (((1,), (1,)), ((), ())), preferred_element_type=F32)
        if mask is not None:
            s = jnp.where(mask, s, -jnp.inf)
        row_max = jnp.max(s, axis=1, keepdims=True)
        if stats is None:
            m_new = row_max
            p = jnp.exp2(s - m_new)
            l_new = jnp.sum(p, axis=1, keepdims=True)
            acc_ref[m, q_rows, :] = jnp.dot(p.astype(BF16), v_ref[kv_rows, :], preferred_element_type=F32)
        else:
            m_prev, l_prev = stats
            m_new = jnp.maximum(m_prev, row_max)
            p = jnp.exp2(s - m_new)
            alpha = jnp.exp2(m_prev - m_new)
            l_new = alpha * l_prev + jnp.sum(p, axis=1, keepdims=True)
            acc_ref[m, q_rows, :] = (alpha * acc_ref[m, q_rows, :]
                                     + jnp.dot(p.astype(BF16), v_ref[kv_rows, :], preferred_element_type=F32))
        return m_new, l_new

    lam = (jnp.exp(jnp.sum(lq1_ref[...] * lk1_ref[...], axis=1, keepdims=True))
           - jnp.exp(jnp.sum(lq2_ref[...] * lk2_ref[...], axis=1, keepdims=True)) + lam_init)
    norm_g = ng_ref[0] * (1.0 - lam_init)

    for i in range(S // TQ):
        stats = [None] * (2 * n_sub)
        for j in range(i):
            for qs in range(n_sub):
                for m in range(2):
                    stats[qs * 2 + m] = chain(i * TQ + qs * QS, m, stats[qs * 2 + m], j * TK, TK, None)
        for qs in range(n_sub):
            ncols = (qs + 1) * QS
            mask = _iota((QS, ncols), 1) <= _iota((QS, ncols), 0) + qs * QS
            for m in range(2):
                stats[qs * 2 + m] = chain(i * TQ + qs * QS, m, stats[qs * 2 + m], i * TK, ncols, mask)
        for qs in range(n_sub):
            rows = slice(i * TQ + qs * QS, i * TQ + (qs + 1) * QS)
            l1 = stats[qs * 2][1]
            l2 = stats[qs * 2 + 1][1]
            o = acc_ref[0, rows, :] / l1 - lam * (acc_ref[1, rows, :] / l2)
            hn = o * lax.rsqrt(jnp.mean(o * o, axis=-1, keepdims=True) + EPS) * norm_g
            o_ref[rows, :] = (hn * _silu(z_ref[rows, :].astype(F32))).astype(o_ref.dtype)


def _diff_attn(proj, cos2, sin2, lq1, lk1, lq2, lk2, norm_g, lam_init, B, S):
    M = B * S
    H = DA_HEADS
    qb, kb, vb, zb = (DA_Q_OFF // DA_V_DIM, DA_K_OFF // DA_V_DIM, DA_V_OFF // DA_V_DIM, DA_Z_OFF // DA_V_DIM)
    vec = lambda a: a.reshape(1, DA_QK_DIM)
    vec_spec = pl.BlockSpec((1, DA_QK_DIM), lambda b, h: (0, 0))
    table_spec = pl.BlockSpec((1, S, DA_QK_DIM), lambda b, h: (b, 0, 0))
    return pl.pallas_call(
        functools.partial(_diff_attn_kernel, lam_init),
        grid=(B, H),
        in_specs=[
            pl.BlockSpec((S, DA_V_DIM), lambda b, h: (b, qb + h)),
            pl.BlockSpec((S, DA_V_DIM), lambda b, h: (b, kb + h)),
            pl.BlockSpec((S, DA_V_DIM), lambda b, h: (b, vb + h)),
            pl.BlockSpec((S, DA_V_DIM), lambda b, h: (b, zb + h)),
            table_spec, table_spec,
            vec_spec, vec_spec, vec_spec, vec_spec,
            pl.BlockSpec((1, 1, DA_V_DIM), lambda b, h: (h, 0, 0)),
        ],
        out_specs=pl.BlockSpec((S, DA_V_DIM), lambda b, h: (b, h)),
        out_shape=jax.ShapeDtypeStruct((M, H * DA_V_DIM), BF16),
        scratch_shapes=[pltpu.VMEM((S, DA_V_DIM), BF16),
                        pltpu.VMEM((S, DA_V_DIM), BF16),
                        pltpu.VMEM((2, S, DA_V_DIM), F32)],
        compiler_params=_params(("arbitrary", "arbitrary")),
        name="diff_attn",
    )(proj, proj, proj, proj, cos2, sin2, vec(lq1), vec(lk1), vec(lq2), vec(lk2),
      norm_g.reshape(H, 1, DA_V_DIM))


def _ple_tail(x1_ref, ssq, p_ref, png_ref, gw_ref, pw_ref, xn_ref, emit):
    D = x1_ref.shape[1]
    inv = lax.rsqrt(ssq * (1.0 / D) + EPS)
    for n0 in range(0, D, TAIL_NC):
        cols = slice(n0, n0 + TAIL_NC)
        xn_ref[:, cols] = (x1_ref[:, cols] * inv * png_ref[:, cols]).astype(BF16)
    p_bf = p_ref[...].astype(BF16)
    for n0 in range(0, D, TAIL_NC):
        cols = slice(n0, n0 + TAIL_NC)
        gate = _sigmoid(jnp.dot(xn_ref[...], gw_ref[:, cols], preferred_element_type=F32))
        pp = jnp.dot(p_bf, pw_ref[:, cols], preferred_element_type=F32)
        emit(n0, x1_ref[:, cols] + gate * pp)


def _out_ple_ab_kernel(a_ref, b_ref, x_ref, p_ref, wo_ref, png_ref, gw_ref, pw_ref, o_ref, x1_ref, xn_ref):
    D = x_ref.shape[1]
    ka = a_ref.shape[1]
    ssq = jnp.zeros((x_ref.shape[0], 1), F32)
    for n0 in range(0, D, TAIL_NC):
        cols = slice(n0, n0 + TAIL_NC)
        y = (jnp.dot(a_ref[...], wo_ref[0:ka, cols], preferred_element_type=F32)
             + jnp.dot(b_ref[...], wo_ref[ka:, cols], preferred_element_type=F32)
             + x_ref[:, cols])
        x1_ref[:, cols] = y
        ssq = ssq + jnp.sum(y * y, axis=1, keepdims=True)

    def emit(n0, val):
        o_ref[:, n0:n0 + TAIL_NC] = val

    _ple_tail(x1_ref, ssq, p_ref, png_ref, gw_ref, pw_ref, xn_ref, emit)


def _out_ple_ab(a_out, b_out, x2d, p3d, layer, w_out, ple_g, gate_w, proj_w):
    M, D = x2d.shape
    KA = a_out.shape[1]
    P = p3d.shape[2]
    TM = TAIL_TM
    row = lambda i: (i, 0)
    fixed = lambda i: (0, 0)
    return pl.pallas_call(
        _out_ple_ab_kernel,
        grid=(M // TM,),
        in_specs=[
            pl.BlockSpec((TM, KA), row),
            pl.BlockSpec((TM, b_out.shape[1]), row),
            pl.BlockSpec((TM, D), row),
            pl.BlockSpec((None, TM, P), lambda i: (layer, i, 0)),
            _resident(w_out.shape, fixed),
            _resident((1, D), fixed),
            _resident(gate_w.shape, fixed),
            _resident(proj_w.shape, fixed),
        ],
        out_specs=pl.BlockSpec((TM, D), row),
        out_shape=jax.ShapeDtypeStruct((M, D), F32),
        scratch_shapes=[pltpu.VMEM((TM, D), F32), pltpu.VMEM((TM, D), BF16)],
        compiler_params=_params(("arbitrary",)),
        name="out_ple_ab",
    )(a_out, b_out, x2d, p3d, w_out, ple_g.reshape(1, D), gate_w, proj_w)


def _sgu_out_ple_kernel(u_ref, v_ref, z_ref, x_ref, p_ref, lng_ref, lnb_ref, sw_ref, sbt_ref, wo_ref,
                        png_ref, gw_ref, pw_ref, fng_ref, o_ref, t_ref, x1_ref, xn_ref):
    TM, D = x_ref.shape
    L = SGU_CHUNK
    GD = SGU_GROUP_DIM
    causal = _iota((L, L), 1) <= _iota((L, L), 0)
    w_causal = [jnp.where(causal, sw_ref[g], 0.0).astype(BF16) for g in range(SGU_GROUPS)]
    lng = lng_ref[...]
    lnb = lnb_ref[...]

    for c in range(TM // L):
        rows = slice(c * L, (c + 1) * L)
        vv = v_ref[rows, :].astype(F32)
        mu = jnp.mean(vv, axis=-1, keepdims=True)
        vc = vv - mu
        var = jnp.mean(vc * vc, axis=-1, keepdims=True)
        vn = (vc * lax.rsqrt(var + EPS) * lng + lnb).astype(BF16)
        for g in range(SGU_GROUPS):
            cols = slice(g * GD, (g + 1) * GD)
            sv = jnp.dot(w_causal[g], vn[:, cols], preferred_element_type=F32) + sbt_ref[:, g:g + 1]
            t = u_ref[rows, cols].astype(F32) * sv * _silu(z_ref[rows, cols].astype(F32))
            t_ref[rows, cols] = t.astype(BF16)

    ssq = jnp.zeros((TM, 1), F32)
    for n0 in range(0, D, TAIL_NC):
        cols = slice(n0, n0 + TAIL_NC)
        y = jnp.dot(t_ref[...], wo_ref[:, cols], preferred_element_type=F32) + x_ref[:, cols]
        x1_ref[:, cols] = y
        ssq = ssq + jnp.sum(y * y, axis=1, keepdims=True)

    ssq2 = [jnp.zeros((TM, 1), F32)]

    def emit(n0, val):
        o_ref[:, n0:n0 + TAIL_NC] = val
        ssq2[0] = ssq2[0] + jnp.sum(val * val, axis=1, keepdims=True)

    _ple_tail(x1_ref, ssq, p_ref, png_ref, gw_ref, pw_ref, xn_ref, emit)

    inv = lax.rsqrt(ssq2[0] * (1.0 / D) + EPS)
    for n0 in range(0, D, TAIL_NC):
        cols = slice(n0, n0 + TAIL_NC)
        o_ref[:, cols] = o_ref[:, cols] * inv * fng_ref[:, cols]


def _sgu_out_ple(uvz, x2d, p3d, layer, ln_g, ln_b, sgu_w, sgu_b, w_out, ple_g, gate_w, proj_w, final_g):
    M, D = x2d.shape
    P = p3d.shape[2]
    TM = TAIL_TM
    row = lambda i: (i, 0)
    fixed = lambda i: (0, 0)
    return pl.pallas_call(
        _sgu_out_ple_kernel,
        grid=(M // TM,),
        in_specs=[
            pl.BlockSpec((TM, D), lambda i: (i, 0)),
            pl.BlockSpec((TM, D), lambda i: (i, 1)),
            pl.BlockSpec((TM, D), lambda i: (i, 2)),
            pl.BlockSpec((TM, D), row),
            pl.BlockSpec((None, TM, P), lambda i: (layer, i, 0)),
            _resident((1, D), fixed),
            _resident((1, D), fixed),
            _resident(sgu_w.shape, lambda i: (0, 0, 0)),
            _resident((SGU_CHUNK, SGU_GROUPS), fixed),
            _resident(w_out.shape, fixed),
            _resident((1, D), fixed),
            _resident(gate_w.shape, fixed),
            _resident(proj_w.shape, fixed),
            _resident((1, D), fixed),
        ],
        out_specs=pl.BlockSpec((TM, D), row),
        out_shape=jax.ShapeDtypeStruct((M, D), F32),
        scratch_shapes=[pltpu.VMEM((TM, D), BF16), pltpu.VMEM((TM, D), F32), pltpu.VMEM((TM, D), BF16)],
        compiler_params=_params(("arbitrary",)),
        name="sgu_out_ple",
    )(uvz, uvz, uvz, x2d, p3d, ln_g.reshape(1, D), ln_b.reshape(1, D), sgu_w, sgu_b.T, w_out,
      ple_g.reshape(1, D), gate_w, proj_w, final_g.reshape(1, D))


def kernel(x, p, positions, ab_norm_g, ab_w_in, ab_conv_w, ab_conv_b, ab_i_bias, ab_f_bias, ab_ml_norm_g,
           ab_lam_q1, ab_lam_k1, ab_lam_q2, ab_lam_k2, ab_da_norm_g, ab_w_out, c_norm_g, c_w_in, c_ln_g,
           c_ln_b, c_sgu_w, c_sgu_b, c_w_out, ple_norm_g, ple_gate_w, ple_proj_w, final_norm_g):
    B, S, D = x.shape
    M = B * S
    assert ab_norm_g.shape[0] == 1 and c_norm_g.shape[0] == 1 and p.shape[0] == 2
    x2d = x.reshape(M, D)
    p3d = p.reshape(2, M, p.shape[-1])

    w_in = ab_w_in[0]
    gate_lo = ML_V_OFF + ML_HEADS * ML_V_DIM
    gate_hi = gate_lo + 2 * ML_HEADS
    w_head = w_in[:, :gate_lo].astype(BF16)
    w_tail = w_in[:, gate_hi:].astype(BF16)
    w_gate = jnp.pad(w_in[:, gate_lo:gate_hi], ((0, 0), (0, GATE_COLS - 2 * ML_HEADS))).astype(BF16)
    proj, gates = _norm_proj(x2d, ab_norm_g[0], [w_head, w_tail], w_gate, name="in_proj_ab")

    a_out = _mlstm(proj, gates, ab_conv_w[0], ab_conv_b[0], ab_i_bias[0], ab_f_bias[0], ab_ml_norm_g[0], B, S)

    cos2, sin2 = _rope_tables(positions)
    lam_init = 0.8 - 0.6 * math.exp(-0.3 * 0)
    b_out = _diff_attn(proj, cos2, sin2, ab_lam_q1[0], ab_lam_k1[0], ab_lam_q2[0], ab_lam_k2[0],
                       ab_da_norm_g[0], lam_init, B, S)

    x2 = _out_ple_ab(a_out, b_out, x2d, p3d, 0, ab_w_out[0].astype(BF16), ple_norm_g[0],
                     ple_gate_w[0].astype(BF16), ple_proj_w[0].astype(BF16))

    (uvz,) = _norm_proj(x2, c_norm_g[0], [c_w_in[0].astype(BF16)], name="in_proj_c")
    out = _sgu_out_ple(uvz, x2, p3d, 1, c_ln_g[0], c_ln_b[0], c_sgu_w[0], c_sgu_b[0],
                       c_w_out[0].astype(BF16), ple_norm_g[1], ple_gate_w[1].astype(BF16),
                       ple_proj_w[1].astype(BF16), final_norm_g)
    return out.reshape(B, S, D)
```

```python
import functools
import math

import jax
import jax.numpy as jnp
from jax import lax
from jax.experimental import pallas as pl
from jax.experimental.pallas import tpu as pltpu

F32 = jnp.float32
BF16 = jnp.bfloat16
HIGHEST = lax.Precision.HIGHEST

EPS = 1e-6
ROPE_THETA = 10000.0
CONV_WIDTH = 4

ML_HEADS = 4
ML_QK_DIM = 128
ML_V_DIM = 256
DA_HEADS = 4
DA_QK_DIM = 128
DA_V_DIM = 256
SGU_CHUNK = 128
SGU_GROUPS = 8
SGU_GROUP_DIM = 256

ML_Q_OFF, ML_K_OFF, ML_V_OFF, ML_O_OFF, ML_Z_OFF = 0, 512, 1024, 2048, 3072
DA_Q_OFF, DA_K_OFF, DA_V_OFF, DA_Z_OFF = 4096, 5120, 6144, 7168
GATE_COLS = 128
GATE_START = ML_V_OFF + ML_HEADS * ML_V_DIM

VMEM_LIMIT_BYTES = 56 * 1024 * 1024

PROJ_TM = 1024
PROJ_TN = 1024
NORM_ROWS = 128
TAIL_TM = 512
TAIL_NC = 512
W_PREP_ROWS = 256
CONV_ROWS = 128
ML_BLOCK = 256
ATT_TQ = 512
ATT_TK = 512
ATT_QSUB = 256
ATT_ROPE_ROWS = 256


def _params(semantics):
    return pltpu.CompilerParams(dimension_semantics=semantics, vmem_limit_bytes=VMEM_LIMIT_BYTES)


def _resident(shape, index_map):
    return pl.BlockSpec(shape, index_map, pipeline_mode=pl.Buffered(1))


def _sigmoid(x):
    return 1.0 / (1.0 + jnp.exp(-x))


def _silu(x):
    return x * _sigmoid(x)


def _log_sigmoid(x):
    return jnp.minimum(x, 0.0) - jnp.log(1.0 + jnp.exp(-jnp.abs(x)))


def _iota(shape, axis):
    return lax.broadcasted_iota(jnp.int32, shape, axis)


def _rope_table_kernel(pos_ref, cos_ref, sin_ref):
    pos = pos_ref[0]
    half_rows = pos.shape[0]
    lane = _iota(pos.shape, 1)
    lo = lane < 64
    j = jnp.where(lo, lane, lane - 64).astype(F32)
    freq = jnp.exp(j * (-math.log(ROPE_THETA) / 64.0))
    ang = pos * freq
    c = jnp.cos(ang)
    s = jnp.sin(ang)
    cr = pltpu.roll(c, 64, 1)
    sr = pltpu.roll(s, 64, 1)
    cos_ref[0, 0:half_rows, :] = jnp.where(lo, c, cr)
    cos_ref[0, half_rows:2 * half_rows, :] = jnp.where(lo, cr, c)
    sin_ref[0, 0:half_rows, :] = jnp.where(lo, -s, sr)
    sin_ref[0, half_rows:2 * half_rows, :] = jnp.where(lo, -sr, s)


def _rope_tables(positions):
    B, S = positions.shape
    posf = positions.astype(F32)
    lo = jnp.broadcast_to(posf[:, :S // 2, None], (B, S // 2, 64))
    hi = jnp.broadcast_to(posf[:, S // 2:, None], (B, S // 2, 64))
    pos2 = jnp.concatenate([lo, hi], axis=-1)
    return pl.pallas_call(
        _rope_table_kernel,
        grid=(B,),
        in_specs=[pl.BlockSpec((1, S // 2, 128), lambda b: (b, 0, 0))],
        out_specs=[pl.BlockSpec((1, S, 128), lambda b: (b, 0, 0)),
                   pl.BlockSpec((1, S, 128), lambda b: (b, 0, 0))],
        out_shape=[jax.ShapeDtypeStruct((B, S, 128), F32)] * 2,
        compiler_params=_params(("arbitrary",)),
        name="rope_tables",
    )(pos2)


def _w_in_prep_kernel(n_gate, w_ref, edge_ref, o_ref, og_ref):
    R, W = w_ref.shape
    lane = _iota((R, 128), 1)
    gate_blk = GATE_START // 128
    for c in range(gate_blk):
        cols = slice(c * 128, (c + 1) * 128)
        o_ref[:, cols] = w_ref[:, cols].astype(BF16)
    gate_cols = slice(GATE_START, GATE_START + 128)
    og_ref[...] = jnp.where(lane < n_gate, w_ref[:, gate_cols], 0.0).astype(BF16)
    keep = 128 - n_gate
    cur = pltpu.roll(w_ref[:, gate_cols], keep, 1)
    for c in range(gate_blk, W // 128):
        nxt_src = w_ref[:, (c + 1) * 128:(c + 2) * 128] if (c + 2) * 128 <= W else edge_ref[...]
        nxt = pltpu.roll(nxt_src, keep, 1)
        o_ref[:, c * 128:(c + 1) * 128] = jnp.where(lane < keep, cur, nxt).astype(BF16)
        cur = nxt


def _w_in_prep(w_in3d):
    _, D, cols = w_in3d.shape
    n_gate = 2 * ML_HEADS
    W = cols - n_gate
    assert W % 128 == 0 and GATE_START % 128 == 0
    R = W_PREP_ROWS
    return pl.pallas_call(
        functools.partial(_w_in_prep_kernel, n_gate),
        grid=(D // R,),
        in_specs=[pl.BlockSpec((None, R, W), lambda i: (0, i, 0)),
                  pl.BlockSpec((None, R, 128), lambda i: (0, i, W // 128))],
        out_specs=[pl.BlockSpec((R, W), lambda i: (i, 0)),
                   pl.BlockSpec((R, GATE_COLS), lambda i: (i, 0))],
        out_shape=[jax.ShapeDtypeStruct((D, W), BF16), jax.ShapeDtypeStruct((D, GATE_COLS), BF16)],
        compiler_params=_params(("arbitrary",)),
        name="w_in_prep",
    )(w_in3d, w_in3d)


def _normalize_rows(x_ref, g_ref, xn_ref):
    g = g_ref[...]

    def body(r, carry):
        rows = pl.ds(pl.multiple_of(r * NORM_ROWS, NORM_ROWS), NORM_ROWS)
        x = x_ref[rows, :]
        ms = jnp.mean(x * x, axis=-1, keepdims=True)
        xn_ref[rows, :] = (x * lax.rsqrt(ms + EPS) * g).astype(BF16)
        return carry

    lax.fori_loop(0, x_ref.shape[0] // NORM_ROWS, body, 0)


def _norm_proj_gates_kernel(x_ref, g_ref, w_ref, wg_ref, o_ref, og_ref, xn_ref):
    @pl.when(pl.program_id(1) == 0)
    def _():
        _normalize_rows(x_ref, g_ref, xn_ref)
        og_ref[...] = jnp.dot(xn_ref[...], wg_ref[...], preferred_element_type=F32)

    o_ref[...] = jnp.dot(xn_ref[...], w_ref[...], preferred_element_type=F32).astype(o_ref.dtype)


def _norm_proj_gates(x2d, g, w, wg):
    M, D = x2d.shape
    N = w.shape[1]
    return pl.pallas_call(
        _norm_proj_gates_kernel,
        grid=(M // PROJ_TM, N // PROJ_TN),
        in_specs=[pl.BlockSpec((PROJ_TM, D), lambda i, j: (i, 0)),
                  pl.BlockSpec((1, D), lambda i, j: (0, 0)),
                  pl.BlockSpec((D, PROJ_TN), lambda i, j: (0, j)),
                  pl.BlockSpec((D, GATE_COLS), lambda i, j: (0, 0))],
        out_specs=[pl.BlockSpec((PROJ_TM, PROJ_TN), lambda i, j: (i, j)),
                   pl.BlockSpec((PROJ_TM, GATE_COLS), lambda i, j: (i, 0))],
        out_shape=[jax.ShapeDtypeStruct((M, N), BF16), jax.ShapeDtypeStruct((M, GATE_COLS), F32)],
        scratch_shapes=[pltpu.VMEM((PROJ_TM, D), BF16)],
        compiler_params=_params(("arbitrary", "arbitrary")),
        name="in_proj_ab",
    )(x2d, g.reshape(1, D), w, wg)


_SGU_STEP_BLOCKS = (2, 3, 0, 4, 1, 5)


def _sgu_step_block(j):
    blk = jnp.int32(_SGU_STEP_BLOCKS[-1])
    for step in range(len(_SGU_STEP_BLOCKS) - 2, -1, -1):
        blk = jnp.where(j == step, _SGU_STEP_BLOCKS[step], blk)
    return blk


def _in_proj_sgu_kernel(x_ref, g_ref, w_ref, lng_ref, lnb_ref, sw_ref, sbt_ref, t_ref, xn_ref, v_ref, u_ref,
                        st_ref):
    j = pl.program_id(1)
    TM = x_ref.shape[0]
    TN = w_ref.shape[1]
    W = v_ref.shape[1]
    L = SGU_CHUNK
    GD = SGU_GROUP_DIM
    groups_per_step = TN // GD

    def proj_cols(cols):
        return jnp.dot(xn_ref[...], w_ref[:, cols], preferred_element_type=F32)

    def project_v(half):
        for gl in range(groups_per_step):
            cols = slice(gl * GD, (gl + 1) * GD)
            slab = proj_cols(cols)
            v_ref[:, half * TN + gl * GD:half * TN + (gl + 1) * GD] = slab.astype(BF16)
            if half == 0 and gl == 0:
                shift = jnp.mean(slab, axis=1, keepdims=True)
                s1 = jnp.zeros((TM, 1), F32)
                s2 = jnp.zeros((TM, 1), F32)
            elif gl == 0:
                shift, s1, s2 = st_ref[0], st_ref[1], st_ref[2]
            d = slab - shift
            s1 = s1 + jnp.sum(d, axis=1, keepdims=True)
            s2 = s2 + jnp.sum(d * d, axis=1, keepdims=True)
        if half == 0:
            st_ref[0], st_ref[1], st_ref[2] = shift, s1, s2
        else:
            m1 = s1 * (1.0 / W)
            st_ref[0] = shift + m1
            st_ref[1] = lax.rsqrt(s2 * (1.0 / W) - m1 * m1 + EPS)

    def gate(half):
        causal = _iota((L, L), 1) <= _iota((L, L), 0)
        for gl in range(groups_per_step):
            g = half * groups_per_step + gl
            cols = slice(gl * GD, (gl + 1) * GD)
            vcols = slice(g * GD, (g + 1) * GD)
            z = proj_cols(cols)
            w_causal = jnp.where(causal, sw_ref[g], 0.0).astype(BF16)
            bias = sbt_ref[:, g:g + 1]
            lng = lng_ref[:, vcols]
            lnb = lnb_ref[:, vcols]
            for c in range(TM // L):
                rows = slice(c * L, (c + 1) * L)
                vn = ((v_ref[rows, vcols].astype(F32) - st_ref[0, rows, :]) * st_ref[1, rows, :] * lng
                      + lnb).astype(BF16)
                sv = jnp.dot(w_causal, vn, preferred_element_type=F32) + bias
                t = u_ref[rows, cols].astype(F32) * sv * _silu(z[rows, :])
                t_ref[rows, cols] = t.astype(BF16)

    @pl.when(j == 0)
    def _():
        _normalize_rows(x_ref, g_ref, xn_ref)
        project_v(0)

    @pl.when(j == 1)
    def _():
        project_v(1)

    @pl.when((j == 2) | (j == 4))
    def _():
        u_ref[...] = jnp.dot(xn_ref[...], w_ref[...], preferred_element_type=F32).astype(BF16)

    @pl.when(j == 3)
    def _():
        gate(0)

    @pl.when(j == 5)
    def _():
        gate(1)


def _in_proj_sgu(x2d, g, w, ln_g, ln_b, sgu_w, sgu_b):
    M, D = x2d.shape
    W = w.shape[1] // 3
    assert W == 2 * PROJ_TN and len(_SGU_STEP_BLOCKS) == 3 * W // PROJ_TN
    fixed = lambda i, j: (0, 0)
    return pl.pallas_call(
        _in_proj_sgu_kernel,
        grid=(M // PROJ_TM, len(_SGU_STEP_BLOCKS)),
        in_specs=[pl.BlockSpec((PROJ_TM, D), lambda i, j: (i, 0)),
                  pl.BlockSpec((1, D), fixed),
                  pl.BlockSpec((D, PROJ_TN), lambda i, j: (0, _sgu_step_block(j))),
                  pl.BlockSpec((1, W), fixed),
                  pl.BlockSpec((1, W), fixed),
                  pl.BlockSpec(sgu_w.shape, lambda i, j: (0, 0, 0)),
                  pl.BlockSpec((SGU_CHUNK, SGU_GROUPS), fixed)],
        out_specs=pl.BlockSpec((PROJ_TM, PROJ_TN), lambda i, j: (i, jnp.where(j <= 3, 0, 1))),
        out_shape=jax.ShapeDtypeStruct((M, W), BF16),
        scratch_shapes=[pltpu.VMEM((PROJ_TM, D), BF16),
                        pltpu.VMEM((PROJ_TM, W), BF16),
                        pltpu.VMEM((PROJ_TM, PROJ_TN), BF16),
                        pltpu.VMEM((3, PROJ_TM, 1), F32)],
        compiler_params=_params(("arbitrary", "arbitrary")),
        name="in_proj_sgu",
    )(x2d, g.reshape(1, D), w, ln_g.reshape(1, W), ln_b.reshape(1, W), sgu_w, sgu_b.T)


def _mlstm_kernel(qp_ref, kp_ref, v_ref, gt_ref, og_ref, z_ref, cwq_ref, cwk_ref, cbq_ref, cbk_ref,
                  gb_ref, ng_ref, out_ref, xs_ref, qc_ref, kc_ref):
    h = pl.program_id(1)
    S = qp_ref.shape[0]
    LB = ML_BLOCK
    nb = S // LB
    DK, DV = ML_QK_DIM, ML_V_DIM

    for src, cw_ref, cb_ref, dst, scale in ((qp_ref, cwq_ref, cbq_ref, qc_ref, 1.0),
                                            (kp_ref, cwk_ref, cbk_ref, kc_ref, DK ** -0.5)):
        xs_ref[0:8, :] = jnp.zeros((8, DK), F32)
        xs_ref[8:8 + S, :] = src[...].astype(F32)
        cw = cw_ref[...]
        cb = cb_ref[...]
        for c in range(S // CONV_ROWS):
            acc = cb
            for j in range(CONV_WIDTH):
                start = 8 + c * CONV_ROWS - (CONV_WIDTH - 1) + j
                acc = acc + cw[j:j + 1, :] * xs_ref[start:start + CONV_ROWS, :]
            dst[c * CONV_ROWS:(c + 1) * CONV_ROWS, :] = (_silu(acc) * scale).astype(BF16)

    gates = gt_ref[...] + gb_ref[...]
    pick = (_iota((8, GATE_COLS), 1) == _iota((8, GATE_COLS), 0)).astype(F32)
    gate_rows = lax.dot_general(pick, gates, (((1,), (1,)), ((), ())), preferred_element_type=F32,
                                precision=HIGHEST)
    sub8 = _iota((8, S), 0)
    i_all = jnp.sum(jnp.where(sub8 == h, gate_rows, 0.0), axis=0, keepdims=True)
    logf_all = _log_sigmoid(jnp.sum(jnp.where(sub8 == h + ML_HEADS, gate_rows, 0.0), axis=0, keepdims=True))

    blk = _iota((nb, LB), 0)

    def to_blocks(row):
        tile = jnp.zeros((nb, LB), F32)
        for c in range(nb):
            tile = jnp.where(blk == c, jnp.broadcast_to(row[:, c * LB:(c + 1) * LB], (nb, LB)), tile)
        return tile

    sub = _iota((LB, LB), 0)
    lane = _iota((LB, LB), 1)
    causal = lane <= sub
    i_b = to_blocks(i_all)
    b_b = jnp.dot(to_blocks(logf_all), (sub <= lane).astype(F32), preferred_element_type=F32,
                  precision=HIGHEST)
    g_tot = b_b[:, LB - 1:LB]
    w_b = g_tot - b_b + i_b
    m_loc = jnp.max(w_b, axis=1, keepdims=True)
    e_b = jnp.exp(w_b - m_loc)
    b_pad = jnp.concatenate([b_b, jnp.zeros((GATE_COLS - nb, LB), F32)], axis=0)
    b_cols = lax.dot_general((sub == lane).astype(F32), b_pad, (((1,), (1,)), ((), ())),
                             preferred_element_type=F32, precision=HIGHEST)

    norm_g = ng_ref[0]
    ct = jnp.zeros((DK, DV), F32)
    n = jnp.zeros((DK, 1), F32)
    m = jnp.zeros((1, 1), F32)
    for c in range(nb):
        rows = slice(c * LB, (c + 1) * LB)
        b_row, i_row, e_row = b_b[c:c + 1, :], i_b[c:c + 1, :], e_b[c:c + 1, :]
        b_col = b_cols[:, c:c + 1]
        q = qc_ref[rows, :]
        v = v_ref[rows, :]
        k_t = kc_ref[rows, :].astype(F32).T

        rhs = jnp.concatenate([k_t.astype(BF16), ct.astype(BF16),
                               jnp.broadcast_to(n, (DK, 128)).astype(BF16)], axis=1)
        big = jnp.dot(q, rhs, preferred_element_type=F32)
        qk, inter, qn = big[:, :LB], big[:, LB:LB + DV], big[:, LB + DV:LB + DV + 1]

        log_d = jnp.where(causal, b_col - b_row + i_row, -jnp.inf)
        inter_log = b_col + m
        m_t = jnp.maximum(inter_log, jnp.max(log_d, axis=1, keepdims=True))
        d_mat = jnp.exp(log_d - m_t)
        inter_w = jnp.exp(inter_log - m_t)
        s_mat = qk * d_mat
        num = jnp.dot(s_mat.astype(BF16), v, preferred_element_type=F32) + inter_w * inter
        den = jnp.sum(s_mat, axis=1, keepdims=True) + inter_w * qn
        hm = num / jnp.maximum(jnp.abs(den), jnp.exp(-m_t))

        hn = hm * lax.rsqrt(jnp.mean(hm * hm, axis=-1, keepdims=True) + EPS) * norm_g
        og = og_ref[rows, :].astype(F32)
        z = z_ref[rows, :].astype(F32)
        out_ref[rows, :] = (_sigmoid(og) * hn * _silu(z)).astype(out_ref.dtype)

        if c + 1 < nb:
            ke = k_t * e_row
            ct_loc = jnp.dot(ke.astype(BF16), v, preferred_element_type=F32)
            n_loc = jnp.sum(ke, axis=1, keepdims=True)
            g_c, ml_c = g_tot[c:c + 1, :], m_loc[c:c + 1, :]
            m_new = jnp.maximum(g_c + m, ml_c)
            a = jnp.exp(g_c + m - m_new)
            cc = jnp.exp(ml_c - m_new)
            ct = a * ct + cc * ct_loc
            n = a * n + cc * n_loc
            m = m_new


def _mlstm(proj, gates, conv_w, conv_b, i_bias, f_bias, norm_g, B, S):
    M = B * S
    H = ML_HEADS
    assert S // ML_BLOCK == 8, "block rows are packed into one 8-sublane tile"
    gate_bias = jnp.concatenate([i_bias, f_bias, jnp.zeros((GATE_COLS - 2 * H,), F32)]).reshape(1, GATE_COLS)
    conv_b2 = conv_b.reshape(1, -1)
    norm_g3 = norm_g.reshape(H, 1, ML_V_DIM)
    qb, kb = ML_Q_OFF // ML_QK_DIM, ML_K_OFF // ML_QK_DIM
    vb, ob, zb = ML_V_OFF // ML_V_DIM, ML_O_OFF // ML_V_DIM, ML_Z_OFF // ML_V_DIM
    return pl.pallas_call(
        _mlstm_kernel,
        grid=(B, H),
        in_specs=[
            pl.BlockSpec((S, ML_QK_DIM), lambda b, h: (b, qb + h)),
            pl.BlockSpec((S, ML_QK_DIM), lambda b, h: (b, kb + h)),
            pl.BlockSpec((S, ML_V_DIM), lambda b, h: (b, vb + h)),
            pl.BlockSpec((S, GATE_COLS), lambda b, h: (b, 0)),
            pl.BlockSpec((S, ML_V_DIM), lambda b, h: (b, ob + h)),
            pl.BlockSpec((S, ML_V_DIM), lambda b, h: (b, zb + h)),
            pl.BlockSpec((CONV_WIDTH, ML_QK_DIM), lambda b, h: (0, h)),
            pl.BlockSpec((CONV_WIDTH, ML_QK_DIM), lambda b, h: (0, H + h)),
            pl.BlockSpec((1, ML_QK_DIM), lambda b, h: (0, h)),
            pl.BlockSpec((1, ML_QK_DIM), lambda b, h: (0, H + h)),
            pl.BlockSpec((1, GATE_COLS), lambda b, h: (0, 0)),
            pl.BlockSpec((1, 1, ML_V_DIM), lambda b, h: (h, 0, 0)),
        ],
        out_specs=pl.BlockSpec((S, ML_V_DIM), lambda b, h: (b, h)),
        out_shape=jax.ShapeDtypeStruct((M, H * ML_V_DIM), BF16),
        scratch_shapes=[pltpu.VMEM((8 + S, ML_QK_DIM), F32),
                        pltpu.VMEM((S, ML_QK_DIM), BF16),
                        pltpu.VMEM((S, ML_QK_DIM), BF16)],
        compiler_params=_params(("arbitrary", "arbitrary")),
        name="mlstm",
    )(proj, proj, proj, gates, proj, proj, conv_w, conv_w, conv_b2, conv_b2, gate_bias, norm_g3)


def _diff_attn_kernel(lam_init, q_ref, k_ref, v_ref, z_ref, cos_ref, sin_ref, lq1_ref, lk1_ref, lq2_ref,
                      lk2_ref, ng_ref, o_ref, qr_ref, kr_ref, acc_ref):
    S = k_ref.shape[0]
    TQ, TK, QS, DH = ATT_TQ, ATT_TK, ATT_QSUB, DA_QK_DIM
    n_sub = TQ // QS

    def rope(x, cs, sn):
        return x * cs + pltpu.roll(x, DH // 2, 1) * sn

    q_scale = DH ** -0.5 * math.log2(math.e)

    def rope_rows(c, carry):
        rows = pl.ds(pl.multiple_of(c * ATT_ROPE_ROWS, ATT_ROPE_ROWS), ATT_ROPE_ROWS)
        cs = cos_ref[0, rows, :]
        sn = sin_ref[0, rows, :]
        for m in range(2):
            cols = slice(m * DH, (m + 1) * DH)
            kr_ref[rows, cols] = rope(k_ref[rows, cols].astype(F32), cs, sn).astype(BF16)
            qr_ref[rows, cols] = (rope(q_ref[rows, cols].astype(F32), cs, sn) * q_scale).astype(BF16)
        return carry

    lax.fori_loop(0, S // ATT_ROPE_ROWS, rope_rows, 0)

    def chain(q0, m, stats, start, ncols, mask):
        q_rows = slice(q0, q0 + QS)
        kv_rows = slice(start, start + ncols)
        s = lax.dot_general(qr_ref[q_rows, m * DH:(m + 1) * DH], kr_ref[kv_rows, m * DH:(m + 1) * DH],
                            (((1,), (1,)), ((), ())), preferred_element_type=F32)
        if mask is not None:
            s = jnp.where(mask, s, -jnp.inf)
        row_max = jnp.max(s, axis=1, keepdims=True)
        if stats is None:
            m_new = row_max
            p = jnp.exp2(s - m_new)
            l_new = jnp.sum(p, axis=1, keepdims=True)
            acc_ref[m, q_rows, :] = jnp.dot(p.astype(BF16), v_ref[kv_rows, :], preferred_element_type=F32)
        else:
            m_prev, l_prev = stats
            m_new = jnp.maximum(m_prev, row_max)
            p = jnp.exp2(s - m_new)
            alpha = jnp.exp2(m_prev - m_new)
            l_new = alpha * l_prev + jnp.sum(p, axis=1, keepdims=True)
            acc_ref[m, q_rows, :] = (alpha * acc_ref[m, q_rows, :]
                                     + jnp.dot(p.astype(BF16), v_ref[kv_rows, :], preferred_element_type=F32))
        return m_new, l_new

    lam = (jnp.exp(jnp.sum(lq1_ref[...] * lk1_ref[...], axis=1, keepdims=True))
           - jnp.exp(jnp.sum(lq2_ref[...] * lk2_ref[...], axis=1, keepdims=True)) + lam_init)
    norm_g = ng_ref[0] * (1.0 - lam_init)

    for i in range(S // TQ):
        stats = [None] * (2 * n_sub)
        for j in range(i):
            for qs in range(n_sub):
                for m in range(2):
                    stats[qs * 2 + m] = chain(i * TQ + qs * QS, m, stats[qs * 2 + m], j * TK, TK, None)
        for qs in range(n_sub):
            ncols = (qs + 1) * QS
            mask = _iota((QS, ncols), 1) <= _iota((QS, ncols), 0) + qs * QS
            for m in range(2):
                stats[qs * 2 + m] = chain(i * TQ + qs * QS, m, stats[qs * 2 + m], i * TK, ncols, mask)
        for qs in range(n_sub):
            rows = slice(i * TQ + qs * QS, i * TQ + (qs + 1) * QS)
            l1 = stats[qs * 2][1]
            l2 = stats[qs * 2 + 1][1]
            o = acc_ref[0, rows, :] / l1 - lam * (acc_ref[1, rows, :] / l2)
            hn = o * lax.rsqrt(jnp.mean(o * o, axis=-1, keepdims=True) + EPS) * norm_g
            o_ref[rows, :] = (hn * _silu(z_ref[rows, :].astype(F32))).astype(o_ref.dtype)


def _diff_attn(proj, cos2, sin2, lq1, lk1, lq2, lk2, norm_g, lam_init, B, S):
    M = B * S
    H = DA_HEADS
    qb, kb, vb, zb = (DA_Q_OFF // DA_V_DIM, DA_K_OFF // DA_V_DIM, DA_V_OFF // DA_V_DIM, DA_Z_OFF // DA_V_DIM)
    vec = lambda a: a.reshape(1, DA_QK_DIM)
    vec_spec = pl.BlockSpec((1, DA_QK_DIM), lambda b, h: (0, 0))
    table_spec = pl.BlockSpec((1, S, DA_QK_DIM), lambda b, h: (b, 0, 0))
    return pl.pallas_call(
        functools.partial(_diff_attn_kernel, lam_init),
        grid=(B, H),
        in_specs=[
            pl.BlockSpec((S, DA_V_DIM), lambda b, h: (b, qb + h)),
            pl.BlockSpec((S, DA_V_DIM), lambda b, h: (b, kb + h)),
            pl.BlockSpec((S, DA_V_DIM), lambda b, h: (b, vb + h)),
            pl.BlockSpec((S, DA_V_DIM), lambda b, h: (b, zb + h)),
            table_spec, table_spec,
            vec_spec, vec_spec, vec_spec, vec_spec,
            pl.BlockSpec((1, 1, DA_V_DIM), lambda b, h: (h, 0, 0)),
        ],
        out_specs=pl.BlockSpec((S, DA_V_DIM), lambda b, h: (b, h)),
        out_shape=jax.ShapeDtypeStruct((M, H * DA_V_DIM), BF16),
        scratch_shapes=[pltpu.VMEM((S, DA_V_DIM), BF16),
                        pltpu.VMEM((S, DA_V_DIM), BF16),
                        pltpu.VMEM((2, S, DA_V_DIM), F32)],
        compiler_params=_params(("arbitrary", "arbitrary")),
        name="diff_attn",
    )(proj, proj, proj, proj, cos2, sin2, vec(lq1), vec(lk1), vec(lq2), vec(lk2),
      norm_g.reshape(H, 1, DA_V_DIM))


def _ple_tail(x1_ref, ssq, p_ref, png_ref, gw_ref, pw_ref, xn_ref, emit):
    D = x1_ref.shape[1]
    inv = lax.rsqrt(ssq * (1.0 / D) + EPS)
    for n0 in range(0, D, TAIL_NC):
        cols = slice(n0, n0 + TAIL_NC)
        xn_ref[:, cols] = (x1_ref[:, cols] * inv * png_ref[:, cols]).astype(BF16)
    p_bf = p_ref[...].astype(BF16)
    for n0 in range(0, D, TAIL_NC):
        cols = slice(n0, n0 + TAIL_NC)
        gate = _sigmoid(jnp.dot(xn_ref[...], gw_ref[:, cols], preferred_element_type=F32))
        pp = jnp.dot(p_bf, pw_ref[:, cols], preferred_element_type=F32)
        emit(n0, x1_ref[:, cols] + gate * pp)


def _out_ple_kernel(n_lhs, has_final_norm, *refs):
    lhs_refs = refs[:n_lhs]
    x_ref, p_ref, wo_ref, png_ref, gw_ref, pw_ref = refs[n_lhs:n_lhs + 6]
    if has_final_norm:
        fng_ref, o_ref, x1_ref, xn_ref = refs[n_lhs + 6:]
    else:
        o_ref, x1_ref, xn_ref = refs[n_lhs + 6:]
    TM, D = x_ref.shape

    ssq = jnp.zeros((TM, 1), F32)
    for n0 in range(0, D, TAIL_NC):
        cols = slice(n0, n0 + TAIL_NC)
        y = x_ref[:, cols]
        k0 = 0
        for lhs_ref in lhs_refs:
            k1 = k0 + lhs_ref.shape[1]
            y = y + jnp.dot(lhs_ref[...], wo_ref[k0:k1, cols], preferred_element_type=F32)
            k0 = k1
        x1_ref[:, cols] = y
        ssq = ssq + jnp.sum(y * y, axis=1, keepdims=True)

    ssq_out = [jnp.zeros((TM, 1), F32)]

    def emit(n0, val):
        o_ref[:, n0:n0 + TAIL_NC] = val
        if has_final_norm:
            ssq_out[0] = ssq_out[0] + jnp.sum(val * val, axis=1, keepdims=True)

    _ple_tail(x1_ref, ssq, p_ref, png_ref, gw_ref, pw_ref, xn_ref, emit)

    if has_final_norm:
        inv = lax.rsqrt(ssq_out[0] * (1.0 / D) + EPS)
        for n0 in range(0, D, TAIL_NC):
            cols = slice(n0, n0 + TAIL_NC)
            o_ref[:, cols] = o_ref[:, cols] * inv * fng_ref[:, cols]


def _out_ple(lhs_list, x2d, p3d, layer, w_out, ple_g, gate_w, proj_w, final_g=None, name="out_ple"):
    M, D = x2d.shape
    P = p3d.shape[2]
    TM = TAIL_TM
    row = lambda i: (i, 0)
    fixed = lambda i: (0, 0)
    in_specs = [pl.BlockSpec((TM, lhs.shape[1]), row) for lhs in lhs_list]
    in_specs += [pl.BlockSpec((TM, D), row),
                 pl.BlockSpec((None, TM, P), lambda i: (layer, i, 0)),
                 _resident(w_out.shape, fixed),
                 _resident((1, D), fixed),
                 _resident(gate_w.shape, fixed),
                 _resident(proj_w.shape, fixed)]
    args = [*lhs_list, x2d, p3d, w_out, ple_g.reshape(1, D), gate_w, proj_w]
    if final_g is not None:
        in_specs.append(_resident((1, D), fixed))
        args.append(final_g.reshape(1, D))
    return pl.pallas_call(
        functools.partial(_out_ple_kernel, len(lhs_list), final_g is not None),
        grid=(M // TM,),
        in_specs=in_specs,
        out_specs=pl.BlockSpec((TM, D), row),
        out_shape=jax.ShapeDtypeStruct((M, D), F32),
        scratch_shapes=[pltpu.VMEM((TM, D), F32), pltpu.VMEM((TM, D), BF16)],
        compiler_params=_params(("arbitrary",)),
        name=name,
    )(*args)


def kernel(x, p, positions, ab_norm_g, ab_w_in, ab_conv_w, ab_conv_b, ab_i_bias, ab_f_bias, ab_ml_norm_g,
           ab_lam_q1, ab_lam_k1, ab_lam_q2, ab_lam_k2, ab_da_norm_g, ab_w_out, c_norm_g, c_w_in, c_ln_g,
           c_ln_b, c_sgu_w, c_sgu_b, c_w_out, ple_norm_g, ple_gate_w, ple_proj_w, final_norm_g):
    B, S, D = x.shape
    M = B * S
    assert ab_norm_g.shape[0] == 1 and c_norm_g.shape[0] == 1 and p.shape[0] == 2
    x2d = x.reshape(M, D)
    p3d = p.reshape(2, M, p.shape[-1])

    w_main, w_gate = _w_in_prep(ab_w_in)
    proj, gates = _norm_proj_gates(x2d, ab_norm_g[0], w_main, w_gate)

    a_out = _mlstm(proj, gates, ab_conv_w[0], ab_conv_b[0], ab_i_bias[0], ab_f_bias[0], ab_ml_norm_g[0], B, S)

    cos2, sin2 = _rope_tables(positions)
    lam_init = 0.8 - 0.6 * math.exp(-0.3 * 0)
    b_out = _diff_attn(proj, cos2, sin2, ab_lam_q1[0], ab_lam_k1[0], ab_lam_q2[0], ab_lam_k2[0],
                       ab_da_norm_g[0], lam_init, B, S)

    x2 = _out_ple([a_out, b_out], x2d, p3d, 0, ab_w_out[0].astype(BF16), ple_norm_g[0],
                  ple_gate_w[0].astype(BF16), ple_proj_w[0].astype(BF16), name="out_ple_ab")

    t = _in_proj_sgu(x2, c_norm_g[0], c_w_in[0].astype(BF16), c_ln_g[0], c_ln_b[0], c_sgu_w[0], c_sgu_b[0])
    out = _out_ple([t], x2, p3d, 1, c_w_out[0].astype(BF16), ple_norm_g[1], ple_gate_w[1].astype(BF16),
                   ple_proj_w[1].astype(BF16), final_g=final_norm_g, name="out_ple_c")
    return out.reshape(B, S, D)
```

```python
import functools
import math

import jax
import jax.numpy as jnp
from jax import lax
from jax.experimental import pallas as pl
from jax.experimental.pallas import tpu as pltpu

F32 = jnp.float32
BF16 = jnp.bfloat16
HIGHEST = lax.Precision.HIGHEST

EPS = 1e-6
ROPE_THETA = 10000.0
CONV_WIDTH = 4

ML_HEADS = 4
ML_QK_DIM = 128
ML_V_DIM = 256
DA_HEADS = 4
DA_QK_DIM = 128
DA_V_DIM = 256
SGU_CHUNK = 128
SGU_GROUPS = 8
SGU_GROUP_DIM = 256

ML_Q_OFF, ML_K_OFF, ML_V_OFF, ML_O_OFF, ML_Z_OFF = 0, 512, 1024, 2048, 3072
DA_Q_OFF, DA_K_OFF, DA_V_OFF, DA_Z_OFF = 4096, 5120, 6144, 7168
GATE_COLS = 128
GATE_START = ML_V_OFF + ML_HEADS * ML_V_DIM

VMEM_LIMIT_BYTES = 56 * 1024 * 1024

PROJ_TM = 1024
PROJ_TN = 1024
NORM_ROWS = 128
TAIL_TM = 512
TAIL_NC = 512
PROJ_SLAB = 256
CONV_ROWS = 128
ML_BLOCK = 256
ATT_TQ = 512
ATT_TK = 512
ATT_QSUB = 256
ATT_ROPE_ROWS = 256


def _params(semantics):
    return pltpu.CompilerParams(dimension_semantics=semantics, vmem_limit_bytes=VMEM_LIMIT_BYTES)


def _resident(shape, index_map):
    return pl.BlockSpec(shape, index_map, pipeline_mode=pl.Buffered(1))


def _sigmoid(x):
    return 1.0 / (1.0 + jnp.exp(-x))


def _silu(x):
    return x * _sigmoid(x)


def _log_sigmoid(x):
    return jnp.minimum(x, 0.0) - jnp.log(1.0 + jnp.exp(-jnp.abs(x)))


def _iota(shape, axis):
    return lax.broadcasted_iota(jnp.int32, shape, axis)


def _rope_table_kernel(pos_ref, cos_ref, sin_ref):
    pos = pos_ref[0]
    half_rows = pos.shape[0]
    lane = _iota(pos.shape, 1)
    lo = lane < 64
    j = jnp.where(lo, lane, lane - 64).astype(F32)
    freq = jnp.exp(j * (-math.log(ROPE_THETA) / 64.0))
    ang = pos * freq
    c = jnp.cos(ang)
    s = jnp.sin(ang)
    cr = pltpu.roll(c, 64, 1)
    sr = pltpu.roll(s, 64, 1)
    cos_ref[0, 0:half_rows, :] = jnp.where(lo, c, cr)
    cos_ref[0, half_rows:2 * half_rows, :] = jnp.where(lo, cr, c)
    sin_ref[0, 0:half_rows, :] = jnp.where(lo, -s, sr)
    sin_ref[0, half_rows:2 * half_rows, :] = jnp.where(lo, -sr, s)


def _rope_tables(positions):
    B, S = positions.shape
    posf = positions.astype(F32)
    lo = jnp.broadcast_to(posf[:, :S // 2, None], (B, S // 2, 64))
    hi = jnp.broadcast_to(posf[:, S // 2:, None], (B, S // 2, 64))
    pos2 = jnp.concatenate([lo, hi], axis=-1)
    return pl.pallas_call(
        _rope_table_kernel,
        grid=(B,),
        in_specs=[pl.BlockSpec((1, S // 2, 128), lambda b: (b, 0, 0))],
        out_specs=[pl.BlockSpec((1, S, 128), lambda b: (b, 0, 0)),
                   pl.BlockSpec((1, S, 128), lambda b: (b, 0, 0))],
        out_shape=[jax.ShapeDtypeStruct((B, S, 128), F32)] * 2,
        compiler_params=_params(("arbitrary",)),
        name="rope_tables",
    )(pos2)


def _normalize_rows(x_ref, g_ref, xn_ref):
    g = g_ref[...]

    def body(r, carry):
        rows = pl.ds(pl.multiple_of(r * NORM_ROWS, NORM_ROWS), NORM_ROWS)
        x = x_ref[rows, :]
        ms = jnp.mean(x * x, axis=-1, keepdims=True)
        xn_ref[rows, :] = (x * lax.rsqrt(ms + EPS) * g).astype(BF16)
        return carry

    lax.fori_loop(0, x_ref.shape[0] // NORM_ROWS, body, 0)


_NT = (((1,), (1,)), ((), ()))


def _norm_proj_gates_kernel(n_gate, x_ref, g_ref, wt_ref, edge_ref, wgt_ref, o_ref, og_ref, xn_ref):
    j = pl.program_id(1)
    TN, D = wt_ref.shape
    first_shifted = GATE_START // TN

    @pl.when(j == 0)
    def _():
        _normalize_rows(x_ref, g_ref, xn_ref)
        wg = jnp.concatenate([wgt_ref[...], jnp.zeros((GATE_COLS - n_gate, D), F32)], axis=0).astype(BF16)
        og_ref[...] = lax.dot_general(xn_ref[...], wg, _NT, preferred_element_type=F32)

    def project(shift):
        for r0 in range(0, TN, PROJ_SLAB):
            lo, hi = r0 + shift, r0 + PROJ_SLAB + shift
            if hi <= TN:
                w = wt_ref[lo:hi, :]
            else:
                w = jnp.concatenate([wt_ref[lo:TN, :], edge_ref[...]], axis=0)
            o_ref[:, r0:r0 + PROJ_SLAB] = lax.dot_general(
                xn_ref[...], w.astype(BF16), _NT, preferred_element_type=F32).astype(o_ref.dtype)

    @pl.when(j < first_shifted)
    def _():
        project(0)

    @pl.when(j >= first_shifted)
    def _():
        project(n_gate)


def _norm_proj_gates(x2d, g, w_t):
    M, D = x2d.shape
    n_gate = 2 * ML_HEADS
    N = w_t.shape[0] - n_gate
    TN = PROJ_TN
    assert n_gate == 8 and GATE_START % TN == 0 and N % TN == 0
    return pl.pallas_call(
        functools.partial(_norm_proj_gates_kernel, n_gate),
        grid=(M // PROJ_TM, N // TN),
        in_specs=[pl.BlockSpec((PROJ_TM, D), lambda i, j: (i, 0)),
                  pl.BlockSpec((1, D), lambda i, j: (0, 0)),
                  pl.BlockSpec((TN, D), lambda i, j: (j, 0)),
                  pl.BlockSpec((n_gate, D), lambda i, j: ((j + 1) * (TN // n_gate), 0)),
                  pl.BlockSpec((n_gate, D), lambda i, j: (GATE_START // n_gate, 0))],
        out_specs=[pl.BlockSpec((PROJ_TM, TN), lambda i, j: (i, j)),
                   pl.BlockSpec((PROJ_TM, GATE_COLS), lambda i, j: (i, 0))],
        out_shape=[jax.ShapeDtypeStruct((M, N), BF16), jax.ShapeDtypeStruct((M, GATE_COLS), F32)],
        scratch_shapes=[pltpu.VMEM((PROJ_TM, D), BF16)],
        compiler_params=_params(("arbitrary", "arbitrary")),
        name="in_proj_ab",
    )(x2d, g.reshape(1, D), w_t, w_t, w_t)


_SGU_STEP_BLOCKS = (2, 3, 0, 4, 1, 5)


def _sgu_step_block(j):
    blk = jnp.int32(_SGU_STEP_BLOCKS[-1])
    for step in range(len(_SGU_STEP_BLOCKS) - 2, -1, -1):
        blk = jnp.where(j == step, _SGU_STEP_BLOCKS[step], blk)
    return blk


def _in_proj_sgu_kernel(x_ref, g_ref, w_ref, lng_ref, lnb_ref, sw_ref, sbt_ref, t_ref, xn_ref, v_ref, u_ref,
                        st_ref):
    j = pl.program_id(1)
    TM = x_ref.shape[0]
    TN = w_ref.shape[1]
    W = v_ref.shape[1]
    L = SGU_CHUNK
    GD = SGU_GROUP_DIM
    groups_per_step = TN // GD

    def proj_cols(cols):
        return jnp.dot(xn_ref[...], w_ref[:, cols].astype(BF16), preferred_element_type=F32)

    def project_v(half):
        for gl in range(groups_per_step):
            cols = slice(gl * GD, (gl + 1) * GD)
            slab = proj_cols(cols)
            v_ref[:, half * TN + gl * GD:half * TN + (gl + 1) * GD] = slab.astype(BF16)
            if half == 0 and gl == 0:
                shift = jnp.mean(slab, axis=1, keepdims=True)
                s1 = jnp.zeros((TM, 1), F32)
                s2 = jnp.zeros((TM, 1), F32)
            elif gl == 0:
                shift, s1, s2 = st_ref[0], st_ref[1], st_ref[2]
            d = slab - shift
            s1 = s1 + jnp.sum(d, axis=1, keepdims=True)
            s2 = s2 + jnp.sum(d * d, axis=1, keepdims=True)
        if half == 0:
            st_ref[0], st_ref[1], st_ref[2] = shift, s1, s2
        else:
            m1 = s1 * (1.0 / W)
            st_ref[0] = shift + m1
            st_ref[1] = lax.rsqrt(s2 * (1.0 / W) - m1 * m1 + EPS)

    def gate(half):
        causal = _iota((L, L), 1) <= _iota((L, L), 0)
        for gl in range(groups_per_step):
            g = half * groups_per_step + gl
            cols = slice(gl * GD, (gl + 1) * GD)
            vcols = slice(g * GD, (g + 1) * GD)
            z = proj_cols(cols)
            w_causal = jnp.where(causal, sw_ref[g], 0.0).astype(BF16)
            bias = sbt_ref[:, g:g + 1]
            lng = lng_ref[:, vcols]
            lnb = lnb_ref[:, vcols]
            for c in range(TM // L):
                rows = slice(c * L, (c + 1) * L)
                vn = ((v_ref[rows, vcols].astype(F32) - st_ref[0, rows, :]) * st_ref[1, rows, :] * lng
                      + lnb).astype(BF16)
                sv = jnp.dot(w_causal, vn, preferred_element_type=F32) + bias
                t = u_ref[rows, cols].astype(F32) * sv * _silu(z[rows, :])
                t_ref[rows, cols] = t.astype(BF16)

    @pl.when(j == 0)
    def _():
        _normalize_rows(x_ref, g_ref, xn_ref)
        project_v(0)

    @pl.when(j == 1)
    def _():
        project_v(1)

    @pl.when((j == 2) | (j == 4))
    def _():
        for gl in range(groups_per_step):
            cols = slice(gl * GD, (gl + 1) * GD)
            u_ref[:, cols] = proj_cols(cols).astype(BF16)

    @pl.when(j == 3)
    def _():
        gate(0)

    @pl.when(j == 5)
    def _():
        gate(1)


def _in_proj_sgu(x2d, g, w, ln_g, ln_b, sgu_w, sgu_b):
    M, D = x2d.shape
    W = w.shape[1] // 3
    assert W == 2 * PROJ_TN and len(_SGU_STEP_BLOCKS) == 3 * W // PROJ_TN
    fixed = lambda i, j: (0, 0)
    return pl.pallas_call(
        _in_proj_sgu_kernel,
        grid=(M // PROJ_TM, len(_SGU_STEP_BLOCKS)),
        in_specs=[pl.BlockSpec((PROJ_TM, D), lambda i, j: (i, 0)),
                  pl.BlockSpec((1, D), fixed),
                  pl.BlockSpec((D, PROJ_TN), lambda i, j: (0, _sgu_step_block(j))),
                  pl.BlockSpec((1, W), fixed),
                  pl.BlockSpec((1, W), fixed),
                  pl.BlockSpec(sgu_w.shape, lambda i, j: (0, 0, 0)),
                  pl.BlockSpec((SGU_CHUNK, SGU_GROUPS), fixed)],
        out_specs=pl.BlockSpec((PROJ_TM, PROJ_TN), lambda i, j: (i, jnp.where(j <= 3, 0, 1))),
        out_shape=jax.ShapeDtypeStruct((M, W), BF16),
        scratch_shapes=[pltpu.VMEM((PROJ_TM, D), BF16),
                        pltpu.VMEM((PROJ_TM, W), BF16),
                        pltpu.VMEM((PROJ_TM, PROJ_TN), BF16),
                        pltpu.VMEM((3, PROJ_TM, 1), F32)],
        compiler_params=_params(("arbitrary", "arbitrary")),
        name="in_proj_sgu",
    )(x2d, g.reshape(1, D), w, ln_g.reshape(1, W), ln_b.reshape(1, W), sgu_w, sgu_b.T)


def _mlstm_kernel(qp_ref, kp_ref, v_ref, gt_ref, og_ref, z_ref, cwq_ref, cwk_ref, cbq_ref, cbk_ref,
                  gb_ref, ng_ref, out_ref, xs_ref, qc_ref, kc_ref):
    h = pl.program_id(1)
    S = qp_ref.shape[0]
    LB = ML_BLOCK
    nb = S // LB
    DK, DV = ML_QK_DIM, ML_V_DIM

    for src, cw_ref, cb_ref, dst, scale in ((qp_ref, cwq_ref, cbq_ref, qc_ref, 1.0),
                                            (kp_ref, cwk_ref, cbk_ref, kc_ref, DK ** -0.5)):
        xs_ref[0:8, :] = jnp.zeros((8, DK), F32)
        xs_ref[8:8 + S, :] = src[...].astype(F32)
        cw = cw_ref[...]
        cb = cb_ref[...]
        for c in range(S // CONV_ROWS):
            acc = cb
            for j in range(CONV_WIDTH):
                start = 8 + c * CONV_ROWS - (CONV_WIDTH - 1) + j
                acc = acc + cw[j:j + 1, :] * xs_ref[start:start + CONV_ROWS, :]
            dst[c * CONV_ROWS:(c + 1) * CONV_ROWS, :] = (_silu(acc) * scale).astype(BF16)

    gates = gt_ref[...] + gb_ref[...]
    pick = (_iota((8, GATE_COLS), 1) == _iota((8, GATE_COLS), 0)).astype(F32)
    gate_rows = lax.dot_general(pick, gates, (((1,), (1,)), ((), ())), preferred_element_type=F32,
                                precision=HIGHEST)
    sub8 = _iota((8, S), 0)
    i_all = jnp.sum(jnp.where(sub8 == h, gate_rows, 0.0), axis=0, keepdims=True)
    logf_all = _log_sigmoid(jnp.sum(jnp.where(sub8 == h + ML_HEADS, gate_rows, 0.0), axis=0, keepdims=True))

    blk = _iota((nb, LB), 0)

    def to_blocks(row):
        tile = jnp.zeros((nb, LB), F32)
        for c in range(nb):
            tile = jnp.where(blk == c, jnp.broadcast_to(row[:, c * LB:(c + 1) * LB], (nb, LB)), tile)
        return tile

    sub = _iota((LB, LB), 0)
    lane = _iota((LB, LB), 1)
    causal = lane <= sub
    i_b = to_blocks(i_all)
    b_b = jnp.dot(to_blocks(logf_all), (sub <= lane).astype(F32), preferred_element_type=F32,
                  precision=HIGHEST)
    g_tot = b_b[:, LB - 1:LB]
    w_b = g_tot - b_b + i_b
    m_loc = jnp.max(w_b, axis=1, keepdims=True)
    e_b = jnp.exp(w_b - m_loc)
    b_pad = jnp.concatenate([b_b, jnp.zeros((GATE_COLS - nb, LB), F32)], axis=0)
    b_cols = lax.dot_general((sub == lane).astype(F32), b_pad, (((1,), (1,)), ((), ())),
                             preferred_element_type=F32, precision=HIGHEST)

    norm_g = ng_ref[0]
    ct = jnp.zeros((DK, DV), F32)
    n = jnp.zeros((DK, 1), F32)
    m = jnp.zeros((1, 1), F32)
    for c in range(nb):
        rows = slice(c * LB, (c + 1) * LB)
        b_row, i_row, e_row = b_b[c:c + 1, :], i_b[c:c + 1, :], e_b[c:c + 1, :]
        b_col = b_cols[:, c:c + 1]
        q = qc_ref[rows, :]
        v = v_ref[rows, :]
        k_t = kc_ref[rows, :].astype(F32).T

        rhs = jnp.concatenate([k_t.astype(BF16), ct.astype(BF16),
                               jnp.broadcast_to(n, (DK, 128)).astype(BF16)], axis=1)
        big = jnp.dot(q, rhs, preferred_element_type=F32)
        qk, inter, qn = big[:, :LB], big[:, LB:LB + DV], big[:, LB + DV:LB + DV + 1]

        log_d = jnp.where(causal, b_col - b_row + i_row, -jnp.inf)
        inter_log = b_col + m
        m_t = jnp.maximum(inter_log, jnp.max(log_d, axis=1, keepdims=True))
        d_mat = jnp.exp(log_d - m_t)
        inter_w = jnp.exp(inter_log - m_t)
        s_mat = qk * d_mat
        num = jnp.dot(s_mat.astype(BF16), v, preferred_element_type=F32) + inter_w * inter
        den = jnp.sum(s_mat, axis=1, keepdims=True) + inter_w * qn
        hm = num / jnp.maximum(jnp.abs(den), jnp.exp(-m_t))

        hn = hm * lax.rsqrt(jnp.mean(hm * hm, axis=-1, keepdims=True) + EPS) * norm_g
        og = og_ref[rows, :].astype(F32)
        z = z_ref[rows, :].astype(F32)
        out_ref[rows, :] = (_sigmoid(og) * hn * _silu(z)).astype(out_ref.dtype)

        if c + 1 < nb:
            ke = k_t * e_row
            ct_loc = jnp.dot(ke.astype(BF16), v, preferred_element_type=F32)
            n_loc = jnp.sum(ke, axis=1, keepdims=True)
            g_c, ml_c = g_tot[c:c + 1, :], m_loc[c:c + 1, :]
            m_new = jnp.maximum(g_c + m, ml_c)
            a = jnp.exp(g_c + m - m_new)
            cc = jnp.exp(ml_c - m_new)
            ct = a * ct + cc * ct_loc
            n = a * n + cc * n_loc
            m = m_new


def _mlstm(proj, gates, conv_w, conv_b, i_bias, f_bias, norm_g, B, S):
    M = B * S
    H = ML_HEADS
    assert S // ML_BLOCK == 8, "block rows are packed into one 8-sublane tile"
    gate_bias = jnp.concatenate([i_bias, f_bias, jnp.zeros((GATE_COLS - 2 * H,), F32)]).reshape(1, GATE_COLS)
    conv_b2 = conv_b.reshape(1, -1)
    norm_g3 = norm_g.reshape(H, 1, ML_V_DIM)
    qb, kb = ML_Q_OFF // ML_QK_DIM, ML_K_OFF // ML_QK_DIM
    vb, ob, zb = ML_V_OFF // ML_V_DIM, ML_O_OFF // ML_V_DIM, ML_Z_OFF // ML_V_DIM
    return pl.pallas_call(
        _mlstm_kernel,
        grid=(B, H),
        in_specs=[
            pl.BlockSpec((S, ML_QK_DIM), lambda b, h: (b, qb + h)),
            pl.BlockSpec((S, ML_QK_DIM), lambda b, h: (b, kb + h)),
            pl.BlockSpec((S, ML_V_DIM), lambda b, h: (b, vb + h)),
            pl.BlockSpec((S, GATE_COLS), lambda b, h: (b, 0)),
            pl.BlockSpec((S, ML_V_DIM), lambda b, h: (b, ob + h)),
            pl.BlockSpec((S, ML_V_DIM), lambda b, h: (b, zb + h)),
            pl.BlockSpec((CONV_WIDTH, ML_QK_DIM), lambda b, h: (0, h)),
            pl.BlockSpec((CONV_WIDTH, ML_QK_DIM), lambda b, h: (0, H + h)),
            pl.BlockSpec((1, ML_QK_DIM), lambda b, h: (0, h)),
            pl.BlockSpec((1, ML_QK_DIM), lambda b, h: (0, H + h)),
            pl.BlockSpec((1, GATE_COLS), lambda b, h: (0, 0)),
            pl.BlockSpec((1, 1, ML_V_DIM), lambda b, h: (h, 0, 0)),
        ],
        out_specs=pl.BlockSpec((S, ML_V_DIM), lambda b, h: (b, h)),
        out_shape=jax.ShapeDtypeStruct((M, H * ML_V_DIM), BF16),
        scratch_shapes=[pltpu.VMEM((8 + S, ML_QK_DIM), F32),
                        pltpu.VMEM((S, ML_QK_DIM), BF16),
                        pltpu.VMEM((S, ML_QK_DIM), BF16)],
        compiler_params=_params(("arbitrary", "arbitrary")),
        name="mlstm",
    )(proj, proj, proj, gates, proj, proj, conv_w, conv_w, conv_b2, conv_b2, gate_bias, norm_g3)


def _diff_attn_kernel(lam_init, q_ref, k_ref, v_ref, z_ref, cos_ref, sin_ref, lq1_ref, lk1_ref, lq2_ref,
                      lk2_ref, ng_ref, o_ref, qr_ref, kr_ref, acc_ref):
    S = k_ref.shape[0]
    TQ, TK, QS, DH = ATT_TQ, ATT_TK, ATT_QSUB, DA_QK_DIM
    n_sub = TQ // QS

    def rope(x, cs, sn):
        return x * cs + pltpu.roll(x, DH // 2, 1) * sn

    q_scale = DH ** -0.5 * math.log2(math.e)

    def rope_rows(c, carry):
        rows = pl.ds(pl.multiple_of(c * ATT_ROPE_ROWS, ATT_ROPE_ROWS), ATT_ROPE_ROWS)
        cs = cos_ref[0, rows, :]
        sn = sin_ref[0, rows, :]
        for m in range(2):
            cols = slice(m * DH, (m + 1) * DH)
            kr_ref[rows, cols] = rope(k_ref[rows, cols].astype(F32), cs, sn).astype(BF16)
            qr_ref[rows, cols] = (rope(q_ref[rows, cols].astype(F32), cs, sn) * q_scale).astype(BF16)
        return carry

    lax.fori_loop(0, S // ATT_ROPE_ROWS, rope_rows, 0)

    def chain(q0, m, stats, start, ncols, mask):
        q_rows = slice(q0, q0 + QS)
        kv_rows = slice(start, start + ncols)
        s = lax.dot_general(qr_ref[q_rows, m * DH:(m + 1) * DH], kr_ref[kv_rows, m * DH:(m + 1) * DH],
                            (((1,), (1,)), ((), ())), preferred_element_type=F32)
        if mask is not None:
            s = jnp.where(mask, s, -jnp.inf)
        row_max = jnp.max(s, axis=1, keepdims=True)
        if stats is None:
            m_new = row_max
            p = jnp.exp2(s - m_new)
            l_new = jnp.sum(p, axis=1, keepdims=True)
            acc_ref[m, q_rows, :] = jnp.dot(p.astype(BF16), v_ref[kv_rows, :], preferred_element_type=F32)
        else:
            m_prev, l_prev = stats
            m_new = jnp.maximum(m_prev, row_max)
            p = jnp.exp2(s - m_new)
            alpha = jnp.exp2(m_prev - m_new)
            l_new = alpha * l_prev + jnp.sum(p, axis=1, keepdims=True)
            acc_ref[m, q_rows, :] = (alpha * acc_ref[m, q_rows, :]
                                     + jnp.dot(p.astype(BF16), v_ref[kv_rows, :], preferred_element_type=F32))
        return m_new, l_new

    lam = (jnp.exp(jnp.sum(lq1_ref[...] * lk1_ref[...], axis=1, keepdims=True))
           - jnp.exp(jnp.sum(lq2_ref[...] * lk2_ref[...], axis=1, keepdims=True)) + lam_init)
    norm_g = ng_ref[0] * (1.0 - lam_init)

    for i in range(S // TQ):
        stats = [None] * (2 * n_sub)
        for j in range(i):
            for qs in range(n_sub):
                for m in range(2):
                    stats[qs * 2 + m] = chain(i * TQ + qs * QS, m, stats[qs * 2 + m], j * TK, TK, None)
        for qs in range(n_sub):
            ncols = (qs + 1) * QS
            mask = _iota((QS, ncols), 1) <= _iota((QS, ncols), 0) + qs * QS
            for m in range(2):
                stats[qs * 2 + m] = chain(i * TQ + qs * QS, m, stats[qs * 2 + m], i * TK, ncols, mask)
        for qs in range(n_sub):
            rows = slice(i * TQ + qs * QS, i * TQ + (qs + 1) * QS)
            l1 = stats[qs * 2][1]
            l2 = stats[qs * 2 + 1][1]
            o = acc_ref[0, rows, :] / l1 - lam * (acc_ref[1, rows, :] / l2)
            hn = o * lax.rsqrt(jnp.mean(o * o, axis=-1, keepdims=True) + EPS) * norm_g
            o_ref[rows, :] = (hn * _silu(z_ref[rows, :].astype(F32))).astype(o_ref.dtype)


def _diff_attn(proj, cos2, sin2, lq1, lk1, lq2, lk2, norm_g, lam_init, B, S):
    M = B * S
    H = DA_HEADS
    qb, kb, vb, zb = (DA_Q_OFF // DA_V_DIM, DA_K_OFF // DA_V_DIM, DA_V_OFF // DA_V_DIM, DA_Z_OFF // DA_V_DIM)
    vec = lambda a: a.reshape(1, DA_QK_DIM)
    vec_spec = pl.BlockSpec((1, DA_QK_DIM), lambda b, h: (0, 0))
    table_spec = pl.BlockSpec((1, S, DA_QK_DIM), lambda b, h: (b, 0, 0))
    return pl.pallas_call(
        functools.partial(_diff_attn_kernel, lam_init),
        grid=(B, H),
        in_specs=[
            pl.BlockSpec((S, DA_V_DIM), lambda b, h: (b, qb + h)),
            pl.BlockSpec((S, DA_V_DIM), lambda b, h: (b, kb + h)),
            pl.BlockSpec((S, DA_V_DIM), lambda b, h: (b, vb + h)),
            pl.BlockSpec((S, DA_V_DIM), lambda b, h: (b, zb + h)),
            table_spec, table_spec,
            vec_spec, vec_spec, vec_spec, vec_spec,
            pl.BlockSpec((1, 1, DA_V_DIM), lambda b, h: (h, 0, 0)),
        ],
        out_specs=pl.BlockSpec((S, DA_V_DIM), lambda b, h: (b, h)),
        out_shape=jax.ShapeDtypeStruct((M, H * DA_V_DIM), BF16),
        scratch_shapes=[pltpu.VMEM((S, DA_V_DIM), BF16),
                        pltpu.VMEM((S, DA_V_DIM), BF16),
                        pltpu.VMEM((2, S, DA_V_DIM), F32)],
        compiler_params=_params(("arbitrary", "arbitrary")),
        name="diff_attn",
    )(proj, proj, proj, proj, cos2, sin2, vec(lq1), vec(lk1), vec(lq2), vec(lk2),
      norm_g.reshape(H, 1, DA_V_DIM))


def _ple_tail(x1_ref, ssq, p_ref, png_ref, gw_ref, pw_ref, xn_ref, emit):
    D = x1_ref.shape[1]
    inv = lax.rsqrt(ssq * (1.0 / D) + EPS)
    for n0 in range(0, D, TAIL_NC):
        cols = slice(n0, n0 + TAIL_NC)
        xn_ref[:, cols] = (x1_ref[:, cols] * inv * png_ref[:, cols]).astype(BF16)
    p_bf = p_ref[...].astype(BF16)
    for n0 in range(0, D, TAIL_NC):
        cols = slice(n0, n0 + TAIL_NC)
        gate = _sigmoid(jnp.dot(xn_ref[...], gw_ref[:, cols], preferred_element_type=F32))
        pp = jnp.dot(p_bf, pw_ref[:, cols], preferred_element_type=F32)
        emit(n0, x1_ref[:, cols] + gate * pp)


def _out_ple_kernel(n_lhs, has_final_norm, *refs):
    lhs_refs = refs[:n_lhs]
    x_ref, p_ref, wo_ref, png_ref, gw_ref, pw_ref = refs[n_lhs:n_lhs + 6]
    if has_final_norm:
        fng_ref, o_ref, x1_ref, xn_ref = refs[n_lhs + 6:]
    else:
        o_ref, x1_ref, xn_ref = refs[n_lhs + 6:]
    TM, D = x_ref.shape

    ssq = jnp.zeros((TM, 1), F32)
    for n0 in range(0, D, TAIL_NC):
        cols = slice(n0, n0 + TAIL_NC)
        y = x_ref[:, cols]
        k0 = 0
        for lhs_ref in lhs_refs:
            k1 = k0 + lhs_ref.shape[1]
            y = y + jnp.dot(lhs_ref[...], wo_ref[k0:k1, cols], preferred_element_type=F32)
            k0 = k1
        x1_ref[:, cols] = y
        ssq = ssq + jnp.sum(y * y, axis=1, keepdims=True)

    ssq_out = [jnp.zeros((TM, 1), F32)]

    def emit(n0, val):
        o_ref[:, n0:n0 + TAIL_NC] = val
        if has_final_norm:
            ssq_out[0] = ssq_out[0] + jnp.sum(val * val, axis=1, keepdims=True)

    _ple_tail(x1_ref, ssq, p_ref, png_ref, gw_ref, pw_ref, xn_ref, emit)

    if has_final_norm:
        inv = lax.rsqrt(ssq_out[0] * (1.0 / D) + EPS)
        for n0 in range(0, D, TAIL_NC):
            cols = slice(n0, n0 + TAIL_NC)
            o_ref[:, cols] = o_ref[:, cols] * inv * fng_ref[:, cols]


def _out_ple(lhs_list, x2d, p3d, layer, w_out, ple_g, gate_w, proj_w, final_g=None, name="out_ple"):
    M, D = x2d.shape
    P = p3d.shape[2]
    TM = TAIL_TM
    row = lambda i: (i, 0)
    fixed = lambda i: (0, 0)
    in_specs = [pl.BlockSpec((TM, lhs.shape[1]), row) for lhs in lhs_list]
    in_specs += [pl.BlockSpec((TM, D), row),
                 pl.BlockSpec((None, TM, P), lambda i: (layer, i, 0)),
                 _resident(w_out.shape, fixed),
                 _resident((1, D), fixed),
                 _resident(gate_w.shape, fixed),
                 _resident(proj_w.shape, fixed)]
    args = [*lhs_list, x2d, p3d, w_out, ple_g.reshape(1, D), gate_w, proj_w]
    if final_g is not None:
        in_specs.append(_resident((1, D), fixed))
        args.append(final_g.reshape(1, D))
    return pl.pallas_call(
        functools.partial(_out_ple_kernel, len(lhs_list), final_g is not None),
        grid=(M // TM,),
        in_specs=in_specs,
        out_specs=pl.BlockSpec((TM, D), row),
        out_shape=jax.ShapeDtypeStruct((M, D), F32),
        scratch_shapes=[pltpu.VMEM((TM, D), F32), pltpu.VMEM((TM, D), BF16)],
        compiler_params=_params(("arbitrary",)),
        name=name,
    )(*args)


def kernel(x, p, positions, ab_norm_g, ab_w_in, ab_conv_w, ab_conv_b, ab_i_bias, ab_f_bias, ab_ml_norm_g,
           ab_lam_q1, ab_lam_k1, ab_lam_q2, ab_lam_k2, ab_da_norm_g, ab_w_out, c_norm_g, c_w_in, c_ln_g,
           c_ln_b, c_sgu_w, c_sgu_b, c_w_out, ple_norm_g, ple_gate_w, ple_proj_w, final_norm_g):
    B, S, D = x.shape
    M = B * S
    assert ab_norm_g.shape[0] == 1 and c_norm_g.shape[0] == 1 and p.shape[0] == 2
    x2d = x.reshape(M, D)
    p3d = p.reshape(2, M, p.shape[-1])

    proj, gates = _norm_proj_gates(x2d, ab_norm_g[0], ab_w_in[0].T)

    a_out = _mlstm(proj, gates, ab_conv_w[0], ab_conv_b[0], ab_i_bias[0], ab_f_bias[0], ab_ml_norm_g[0], B, S)

    cos2, sin2 = _rope_tables(positions)
    lam_init = 0.8 - 0.6 * math.exp(-0.3 * 0)
    b_out = _diff_attn(proj, cos2, sin2, ab_lam_q1[0], ab_lam_k1[0], ab_lam_q2[0], ab_lam_k2[0],
                       ab_da_norm_g[0], lam_init, B, S)

    x2 = _out_ple([a_out, b_out], x2d, p3d, 0, ab_w_out[0].astype(BF16), ple_norm_g[0],
                  ple_gate_w[0].astype(BF16), ple_proj_w[0].astype(BF16), name="out_ple_ab")

    t = _in_proj_sgu(x2, c_norm_g[0], c_w_in[0], c_ln_g[0], c_ln_b[0], c_sgu_w[0], c_sgu_b[0])
    out = _out_ple([t], x2, p3d, 1, c_w_out[0].astype(BF16), ple_norm_g[1], ple_gate_w[1].astype(BF16),
                   ple_proj_w[1].astype(BF16), final_g=final_norm_g, name="out_ple_c")
    return out.reshape(B, S, D)
```

```python
import functools
import math

import jax
import jax.numpy as jnp
from jax import lax
from jax.experimental import pallas as pl
from jax.experimental.pallas import tpu as pltpu

F32 = jnp.float32
BF16 = jnp.bfloat16
HIGHEST = lax.Precision.HIGHEST

EPS = 1e-6
ROPE_THETA = 10000.0
CONV_WIDTH = 4

ML_HEADS = 4
ML_QK_DIM = 128
ML_V_DIM = 256
DA_HEADS = 4
DA_QK_DIM = 128
DA_V_DIM = 256
SGU_CHUNK = 128
SGU_GROUPS = 8
SGU_GROUP_DIM = 256

ML_Q_OFF, ML_K_OFF, ML_V_OFF, ML_O_OFF, ML_Z_OFF = 0, 512, 1024, 2048, 3072
DA_Q_OFF, DA_K_OFF, DA_V_OFF, DA_Z_OFF = 4096, 5120, 6144, 7168
GATE_COLS = 128
GATE_START = ML_V_OFF + ML_HEADS * ML_V_DIM

VMEM_LIMIT_BYTES = 56 * 1024 * 1024

PROJ_TM = 1024
PROJ_TN = 1024
NORM_ROWS = 128
TAIL_TM = 512
TAIL_NC = 512
PROJ_SLAB = 256
CONV_ROWS = 128
ML_BLOCK = 256
ATT_TK = 512
ATT_QSUB = 256


def _params(semantics):
    return pltpu.CompilerParams(dimension_semantics=semantics, vmem_limit_bytes=VMEM_LIMIT_BYTES)


def _resident(shape, index_map):
    return pl.BlockSpec(shape, index_map, pipeline_mode=pl.Buffered(1))


def _sigmoid(x):
    return 1.0 / (1.0 + jnp.exp(-x))


def _silu(x):
    return x * _sigmoid(x)


def _log_sigmoid(x):
    return jnp.minimum(x, 0.0) - jnp.log(1.0 + jnp.exp(-jnp.abs(x)))


def _iota(shape, axis):
    return lax.broadcasted_iota(jnp.int32, shape, axis)


def _rope_table_kernel(pos_ref, cos_ref, sin_ref):
    pos = pos_ref[0]
    half_rows = pos.shape[0]
    lane = _iota(pos.shape, 1)
    lo = lane < 64
    j = jnp.where(lo, lane, lane - 64).astype(F32)
    freq = jnp.exp(j * (-math.log(ROPE_THETA) / 64.0))
    ang = pos * freq
    c = jnp.cos(ang)
    s = jnp.sin(ang)
    cr = pltpu.roll(c, 64, 1)
    sr = pltpu.roll(s, 64, 1)
    cos_ref[0, 0:half_rows, :] = jnp.where(lo, c, cr)
    cos_ref[0, half_rows:2 * half_rows, :] = jnp.where(lo, cr, c)
    sin_ref[0, 0:half_rows, :] = jnp.where(lo, -s, sr)
    sin_ref[0, half_rows:2 * half_rows, :] = jnp.where(lo, -sr, s)


def _rope_tables(positions):
    B, S = positions.shape
    posf = positions.astype(F32)
    lo = jnp.broadcast_to(posf[:, :S // 2, None], (B, S // 2, 64))
    hi = jnp.broadcast_to(posf[:, S // 2:, None], (B, S // 2, 64))
    pos2 = jnp.concatenate([lo, hi], axis=-1)
    return pl.pallas_call(
        _rope_table_kernel,
        grid=(B,),
        in_specs=[pl.BlockSpec((1, S // 2, 128), lambda b: (b, 0, 0))],
        out_specs=[pl.BlockSpec((1, S, 128), lambda b: (b, 0, 0)),
                   pl.BlockSpec((1, S, 128), lambda b: (b, 0, 0))],
        out_shape=[jax.ShapeDtypeStruct((B, S, 128), F32)] * 2,
        compiler_params=_params(("arbitrary",)),
        name="rope_tables",
    )(pos2)


def _normalize_rows(x_ref, g_ref, xn_ref):
    g = g_ref[...]

    def body(r, carry):
        rows = pl.ds(pl.multiple_of(r * NORM_ROWS, NORM_ROWS), NORM_ROWS)
        x = x_ref[rows, :]
        ms = jnp.mean(x * x, axis=-1, keepdims=True)
        xn_ref[rows, :] = (x * lax.rsqrt(ms + EPS) * g).astype(BF16)
        return carry

    lax.fori_loop(0, x_ref.shape[0] // NORM_ROWS, body, 0)


_NT = (((1,), (1,)), ((), ()))


def _in_proj_ab_kernel(n_gate, x_ref, g_ref, wt_ref, edge_ref, wgt_ref, cos_ref, sin_ref, o_ref, og_ref,
                       xn_ref):
    j = pl.program_id(1)
    TN, D = wt_ref.shape
    rope_q_tile, rope_k_tile = DA_Q_OFF // TN, DA_K_OFF // TN
    first_shifted = GATE_START // TN

    @pl.when(j == 0)
    def _():
        _normalize_rows(x_ref, g_ref, xn_ref)
        wg = jnp.concatenate([wgt_ref[...], jnp.zeros((GATE_COLS - n_gate, D), F32)], axis=0).astype(BF16)
        og_ref[...] = lax.dot_general(xn_ref[...], wg, _NT, preferred_element_type=F32)

    def slabs(shift):
        for r0 in range(0, TN, PROJ_SLAB):
            lo, hi = r0 + shift, r0 + PROJ_SLAB + shift
            if hi <= TN:
                w = wt_ref[lo:hi, :]
            else:
                w = jnp.concatenate([wt_ref[lo:TN, :], edge_ref[...]], axis=0)
            yield r0, lax.dot_general(xn_ref[...], w.astype(BF16), _NT, preferred_element_type=F32)

    def plain(shift):
        for r0, y in slabs(shift):
            o_ref[:, r0:r0 + PROJ_SLAB] = y.astype(o_ref.dtype)

    def rope(shift, scale):
        cs = cos_ref[...]
        sn = sin_ref[...]
        for r0, y in slabs(shift):
            for c0 in range(0, PROJ_SLAB, DA_QK_DIM):
                x = y[:, c0:c0 + DA_QK_DIM]
                r = x * cs + pltpu.roll(x, DA_QK_DIM // 2, 1) * sn
                if scale is not None:
                    r = r * scale
                o_ref[:, r0 + c0:r0 + c0 + DA_QK_DIM] = r.astype(o_ref.dtype)

    @pl.when(j < first_shifted)
    def _():
        plain(0)

    @pl.when((j >= first_shifted) & (j != rope_q_tile) & (j != rope_k_tile))
    def _():
        plain(n_gate)

    @pl.when(j == rope_q_tile)
    def _():
        rope(n_gate, DA_QK_DIM ** -0.5 * math.log2(math.e))

    @pl.when(j == rope_k_tile)
    def _():
        rope(n_gate, None)


def _in_proj_ab(x2d, g, w_t, cos2, sin2):
    M, D = x2d.shape
    n_gate = 2 * ML_HEADS
    N = w_t.shape[0] - n_gate
    TM, TN = PROJ_TM, PROJ_TN
    assert n_gate == 8 and GATE_START % TN == 0 and N % TN == 0
    assert DA_Q_OFF % TN == 0 and DA_K_OFF - DA_Q_OFF == TN and DA_V_OFF - DA_K_OFF == TN
    fixed = lambda i, j: (0, 0)
    return pl.pallas_call(
        functools.partial(_in_proj_ab_kernel, n_gate),
        grid=(M // TM, N // TN),
        in_specs=[pl.BlockSpec((TM, D), lambda i, j: (i, 0)),
                  pl.BlockSpec((1, D), fixed),
                  pl.BlockSpec((TN, D), lambda i, j: (j, 0)),
                  pl.BlockSpec((n_gate, D), lambda i, j: ((j + 1) * (TN // n_gate), 0)),
                  pl.BlockSpec((n_gate, D), lambda i, j: (GATE_START // n_gate, 0)),
                  pl.BlockSpec((TM, DA_QK_DIM), lambda i, j: (i, 0)),
                  pl.BlockSpec((TM, DA_QK_DIM), lambda i, j: (i, 0))],
        out_specs=[pl.BlockSpec((TM, TN), lambda i, j: (i, j)),
                   pl.BlockSpec((TM, GATE_COLS), lambda i, j: (i, 0))],
        out_shape=[jax.ShapeDtypeStruct((M, N), BF16), jax.ShapeDtypeStruct((M, GATE_COLS), F32)],
        scratch_shapes=[pltpu.VMEM((TM, D), BF16)],
        compiler_params=_params(("arbitrary", "arbitrary")),
        name="in_proj_ab",
    )(x2d, g.reshape(1, D), w_t, w_t, w_t, cos2.reshape(M, DA_QK_DIM), sin2.reshape(M, DA_QK_DIM))


_SGU_STEP_BLOCKS = (2, 3, 0, 4, 1, 5)


def _sgu_step_block(j):
    blk = jnp.int32(_SGU_STEP_BLOCKS[-1])
    for step in range(len(_SGU_STEP_BLOCKS) - 2, -1, -1):
        blk = jnp.where(j == step, _SGU_STEP_BLOCKS[step], blk)
    return blk


def _in_proj_sgu_kernel(x_ref, g_ref, w_ref, lng_ref, lnb_ref, sw_ref, sbt_ref, t_ref, xn_ref, v_ref, u_ref,
                        st_ref):
    j = pl.program_id(1)
    TM = x_ref.shape[0]
    TN = w_ref.shape[1]
    W = v_ref.shape[1]
    L = SGU_CHUNK
    GD = SGU_GROUP_DIM
    groups_per_step = TN // GD

    def proj_cols(cols):
        return jnp.dot(xn_ref[...], w_ref[:, cols].astype(BF16), preferred_element_type=F32)

    def project_v(half):
        for gl in range(groups_per_step):
            cols = slice(gl * GD, (gl + 1) * GD)
            slab = proj_cols(cols)
            v_ref[:, half * TN + gl * GD:half * TN + (gl + 1) * GD] = slab.astype(BF16)
            if half == 0 and gl == 0:
                shift = jnp.mean(slab, axis=1, keepdims=True)
                s1 = jnp.zeros((TM, 1), F32)
                s2 = jnp.zeros((TM, 1), F32)
            elif gl == 0:
                shift, s1, s2 = st_ref[0], st_ref[1], st_ref[2]
            d = slab - shift
            s1 = s1 + jnp.sum(d, axis=1, keepdims=True)
            s2 = s2 + jnp.sum(d * d, axis=1, keepdims=True)
        if half == 0:
            st_ref[0], st_ref[1], st_ref[2] = shift, s1, s2
        else:
            m1 = s1 * (1.0 / W)
            st_ref[0] = shift + m1
            st_ref[1] = lax.rsqrt(s2 * (1.0 / W) - m1 * m1 + EPS)

    def gate(half):
        causal = _iota((L, L), 1) <= _iota((L, L), 0)
        for gl in range(groups_per_step):
            g = half * groups_per_step + gl
            cols = slice(gl * GD, (gl + 1) * GD)
            vcols = slice(g * GD, (g + 1) * GD)
            z = proj_cols(cols)
            w_causal = jnp.where(causal, sw_ref[g], 0.0).astype(BF16)
            bias = sbt_ref[:, g:g + 1]
            lng = lng_ref[:, vcols]
            lnb = lnb_ref[:, vcols]
            for c in range(TM // L):
                rows = slice(c * L, (c + 1) * L)
                vn = ((v_ref[rows, vcols].astype(F32) - st_ref[0, rows, :]) * st_ref[1, rows, :] * lng
                      + lnb).astype(BF16)
                sv = jnp.dot(w_causal, vn, preferred_element_type=F32) + bias
                t = u_ref[rows, cols].astype(F32) * sv * _silu(z[rows, :])
                t_ref[rows, cols] = t.astype(BF16)

    @pl.when(j == 0)
    def _():
        _normalize_rows(x_ref, g_ref, xn_ref)
        project_v(0)

    @pl.when(j == 1)
    def _():
        project_v(1)

    @pl.when((j == 2) | (j == 4))
    def _():
        for gl in range(groups_per_step):
            cols = slice(gl * GD, (gl + 1) * GD)
            u_ref[:, cols] = proj_cols(cols).astype(BF16)

    @pl.when(j == 3)
    def _():
        gate(0)

    @pl.when(j == 5)
    def _():
        gate(1)


def _in_proj_sgu(x2d, g, w, ln_g, ln_b, sgu_w, sgu_b):
    M, D = x2d.shape
    W = w.shape[1] // 3
    assert W == 2 * PROJ_TN and len(_SGU_STEP_BLOCKS) == 3 * W // PROJ_TN
    fixed = lambda i, j: (0, 0)
    return pl.pallas_call(
        _in_proj_sgu_kernel,
        grid=(M // PROJ_TM, len(_SGU_STEP_BLOCKS)),
        in_specs=[pl.BlockSpec((PROJ_TM, D), lambda i, j: (i, 0)),
                  pl.BlockSpec((1, D), fixed),
                  pl.BlockSpec((D, PROJ_TN), lambda i, j: (0, _sgu_step_block(j))),
                  pl.BlockSpec((1, W), fixed),
                  pl.BlockSpec((1, W), fixed),
                  pl.BlockSpec(sgu_w.shape, lambda i, j: (0, 0, 0)),
                  pl.BlockSpec((SGU_CHUNK, SGU_GROUPS), fixed)],
        out_specs=pl.BlockSpec((PROJ_TM, PROJ_TN), lambda i, j: (i, jnp.where(j <= 3, 0, 1))),
        out_shape=jax.ShapeDtypeStruct((M, W), BF16),
        scratch_shapes=[pltpu.VMEM((PROJ_TM, D), BF16),
                        pltpu.VMEM((PROJ_TM, W), BF16),
                        pltpu.VMEM((PROJ_TM, PROJ_TN), BF16),
                        pltpu.VMEM((3, PROJ_TM, 1), F32)],
        compiler_params=_params(("arbitrary", "arbitrary")),
        name="in_proj_sgu",
    )(x2d, g.reshape(1, D), w, ln_g.reshape(1, W), ln_b.reshape(1, W), sgu_w, sgu_b.T)


def _mlstm_kernel(qp_ref, kp_ref, v_ref, gt_ref, og_ref, z_ref, cwq_ref, cwk_ref, cbq_ref, cbk_ref,
                  gb_ref, ng_ref, out_ref, xs_ref, qc_ref, kc_ref):
    h = pl.program_id(1)
    S = qp_ref.shape[0]
    LB = ML_BLOCK
    nb = S // LB
    DK, DV = ML_QK_DIM, ML_V_DIM

    for src, cw_ref, cb_ref, dst, scale in ((qp_ref, cwq_ref, cbq_ref, qc_ref, 1.0),
                                            (kp_ref, cwk_ref, cbk_ref, kc_ref, DK ** -0.5)):
        xs_ref[0:8, :] = jnp.zeros((8, DK), F32)
        xs_ref[8:8 + S, :] = src[...].astype(F32)
        cw = cw_ref[...]
        cb = cb_ref[...]
        for c in range(S // CONV_ROWS):
            acc = cb
            for j in range(CONV_WIDTH):
                start = 8 + c * CONV_ROWS - (CONV_WIDTH - 1) + j
                acc = acc + cw[j:j + 1, :] * xs_ref[start:start + CONV_ROWS, :]
            dst[c * CONV_ROWS:(c + 1) * CONV_ROWS, :] = (_silu(acc) * scale).astype(BF16)

    gates = gt_ref[...] + gb_ref[...]
    pick = (_iota((8, GATE_COLS), 1) == _iota((8, GATE_COLS), 0)).astype(F32)
    gate_rows = lax.dot_general(pick, gates, (((1,), (1,)), ((), ())), preferred_element_type=F32,
                                precision=HIGHEST)
    sub8 = _iota((8, S), 0)
    i_all = jnp.sum(jnp.where(sub8 == h, gate_rows, 0.0), axis=0, keepdims=True)
    logf_all = _log_sigmoid(jnp.sum(jnp.where(sub8 == h + ML_HEADS, gate_rows, 0.0), axis=0, keepdims=True))

    blk = _iota((nb, LB), 0)

    def to_blocks(row):
        tile = jnp.zeros((nb, LB), F32)
        for c in range(nb):
            tile = jnp.where(blk == c, jnp.broadcast_to(row[:, c * LB:(c + 1) * LB], (nb, LB)), tile)
        return tile

    sub = _iota((LB, LB), 0)
    lane = _iota((LB, LB), 1)
    causal = lane <= sub
    i_b = to_blocks(i_all)
    b_b = jnp.dot(to_blocks(logf_all), (sub <= lane).astype(F32), preferred_element_type=F32,
                  precision=HIGHEST)
    g_tot = b_b[:, LB - 1:LB]
    w_b = g_tot - b_b + i_b
    m_loc = jnp.max(w_b, axis=1, keepdims=True)
    e_b = jnp.exp(w_b - m_loc)
    b_pad = jnp.concatenate([b_b, jnp.zeros((GATE_COLS - nb, LB), F32)], axis=0)
    b_cols = lax.dot_general((sub == lane).astype(F32), b_pad, (((1,), (1,)), ((), ())),
                             preferred_element_type=F32, precision=HIGHEST)

    norm_g = ng_ref[0]
    ct = jnp.zeros((DK, DV), F32)
    n = jnp.zeros((DK, 1), F32)
    m = jnp.zeros((1, 1), F32)
    for c in range(nb):
        rows = slice(c * LB, (c + 1) * LB)
        b_row, i_row, e_row = b_b[c:c + 1, :], i_b[c:c + 1, :], e_b[c:c + 1, :]
        b_col = b_cols[:, c:c + 1]
        q = qc_ref[rows, :]
        v = v_ref[rows, :]
        k_t = kc_ref[rows, :].astype(F32).T

        rhs = jnp.concatenate([k_t.astype(BF16), ct.astype(BF16),
                               jnp.broadcast_to(n, (DK, 128)).astype(BF16)], axis=1)
        big = jnp.dot(q, rhs, preferred_element_type=F32)
        qk, inter, qn = big[:, :LB], big[:, LB:LB + DV], big[:, LB + DV:LB + DV + 1]

        log_d = jnp.where(causal, b_col - b_row + i_row, -jnp.inf)
        inter_log = b_col + m
        m_t = jnp.maximum(inter_log, jnp.max(log_d, axis=1, keepdims=True))
        d_mat = jnp.exp(log_d - m_t)
        inter_w = jnp.exp(inter_log - m_t)
        s_mat = qk * d_mat
        num = jnp.dot(s_mat.astype(BF16), v, preferred_element_type=F32) + inter_w * inter
        den = jnp.sum(s_mat, axis=1, keepdims=True) + inter_w * qn
        hm = num / jnp.maximum(jnp.abs(den), jnp.exp(-m_t))

        hn = hm * lax.rsqrt(jnp.mean(hm * hm, axis=-1, keepdims=True) + EPS) * norm_g
        og = og_ref[rows, :].astype(F32)
        z = z_ref[rows, :].astype(F32)
        out_ref[rows, :] = (_sigmoid(og) * hn * _silu(z)).astype(out_ref.dtype)

        if c + 1 < nb:
            ke = k_t * e_row
            ct_loc = jnp.dot(ke.astype(BF16), v, preferred_element_type=F32)
            n_loc = jnp.sum(ke, axis=1, keepdims=True)
            g_c, ml_c = g_tot[c:c + 1, :], m_loc[c:c + 1, :]
            m_new = jnp.maximum(g_c + m, ml_c)
            a = jnp.exp(g_c + m - m_new)
            cc = jnp.exp(ml_c - m_new)
            ct = a * ct + cc * ct_loc
            n = a * n + cc * n_loc
            m = m_new


def _mlstm(proj, gates, conv_w, conv_b, i_bias, f_bias, norm_g, B, S):
    M = B * S
    H = ML_HEADS
    assert S // ML_BLOCK == 8, "block rows are packed into one 8-sublane tile"
    gate_bias = jnp.concatenate([i_bias, f_bias, jnp.zeros((GATE_COLS - 2 * H,), F32)]).reshape(1, GATE_COLS)
    conv_b2 = conv_b.reshape(1, -1)
    norm_g3 = norm_g.reshape(H, 1, ML_V_DIM)
    qb, kb = ML_Q_OFF // ML_QK_DIM, ML_K_OFF // ML_QK_DIM
    vb, ob, zb = ML_V_OFF // ML_V_DIM, ML_O_OFF // ML_V_DIM, ML_Z_OFF // ML_V_DIM
    return pl.pallas_call(
        _mlstm_kernel,
        grid=(B, H),
        in_specs=[
            pl.BlockSpec((S, ML_QK_DIM), lambda b, h: (b, qb + h)),
            pl.BlockSpec((S, ML_QK_DIM), lambda b, h: (b, kb + h)),
            pl.BlockSpec((S, ML_V_DIM), lambda b, h: (b, vb + h)),
            pl.BlockSpec((S, GATE_COLS), lambda b, h: (b, 0)),
            pl.BlockSpec((S, ML_V_DIM), lambda b, h: (b, ob + h)),
            pl.BlockSpec((S, ML_V_DIM), lambda b, h: (b, zb + h)),
            pl.BlockSpec((CONV_WIDTH, ML_QK_DIM), lambda b, h: (0, h)),
            pl.BlockSpec((CONV_WIDTH, ML_QK_DIM), lambda b, h: (0, H + h)),
            pl.BlockSpec((1, ML_QK_DIM), lambda b, h: (0, h)),
            pl.BlockSpec((1, ML_QK_DIM), lambda b, h: (0, H + h)),
            pl.BlockSpec((1, GATE_COLS), lambda b, h: (0, 0)),
            pl.BlockSpec((1, 1, ML_V_DIM), lambda b, h: (h, 0, 0)),
        ],
        out_specs=pl.BlockSpec((S, ML_V_DIM), lambda b, h: (b, h)),
        out_shape=jax.ShapeDtypeStruct((M, H * ML_V_DIM), BF16),
        scratch_shapes=[pltpu.VMEM((8 + S, ML_QK_DIM), F32),
                        pltpu.VMEM((S, ML_QK_DIM), BF16),
                        pltpu.VMEM((S, ML_QK_DIM), BF16)],
        compiler_params=_params(("arbitrary", "arbitrary")),
        name="mlstm",
    )(proj, proj, proj, gates, proj, proj, conv_w, conv_w, conv_b2, conv_b2, gate_bias, norm_g3)


def _diff_attn_kernel(lam_init, qr_ref, kr_ref, v_ref, z_ref, lq1_ref, lk1_ref, lq2_ref, lk2_ref, ng_ref,
                      o_ref):
    S = kr_ref.shape[0]
    TK, QS, DH = ATT_TK, ATT_QSUB, DA_QK_DIM
    diag_mask = _iota((QS, QS), 1) <= _iota((QS, QS), 0)

    def softmax_pv(q0, m):
        q = qr_ref[q0:q0 + QS, m * DH:(m + 1) * DH]

        def score(start, ncols):
            return lax.dot_general(q, kr_ref[start:start + ncols, m * DH:(m + 1) * DH], _NT,
                                   preferred_element_type=F32)

        scores = [(start, min(TK, q0 - start), score(start, min(TK, q0 - start))) for start in range(0, q0, TK)]
        scores.append((q0, QS, jnp.where(diag_mask, score(q0, QS), -jnp.inf)))
        row_max = functools.reduce(jnp.maximum, [jnp.max(s, axis=1, keepdims=True) for _, _, s in scores])
        acc = None
        l = None
        for start, ncols, s in scores:
            p = jnp.exp2(s - row_max)
            pv = jnp.dot(p.astype(BF16), v_ref[start:start + ncols, :], preferred_element_type=F32)
            ps = jnp.sum(p, axis=1, keepdims=True)
            acc = pv if acc is None else acc + pv
            l = ps if l is None else l + ps
        return acc, l

    lam = (jnp.exp(jnp.sum(lq1_ref[...] * lk1_ref[...], axis=1, keepdims=True))
           - jnp.exp(jnp.sum(lq2_ref[...] * lk2_ref[...], axis=1, keepdims=True)) + lam_init)
    norm_g = ng_ref[0] * (1.0 - lam_init)

    for q0 in range(0, S, QS):
        acc1, l1 = softmax_pv(q0, 0)
        acc2, l2 = softmax_pv(q0, 1)
        o = acc1 / l1 - lam * (acc2 / l2)
        hn = o * lax.rsqrt(jnp.mean(o * o, axis=-1, keepdims=True) + EPS) * norm_g
        o_ref[q0:q0 + QS, :] = (hn * _silu(z_ref[q0:q0 + QS, :].astype(F32))).astype(o_ref.dtype)


def _diff_attn(proj, lq1, lk1, lq2, lk2, norm_g, lam_init, B, S):
    M = B * S
    H = DA_HEADS
    qb, kb, vb, zb = (DA_Q_OFF // DA_V_DIM, DA_K_OFF // DA_V_DIM, DA_V_OFF // DA_V_DIM, DA_Z_OFF // DA_V_DIM)
    vec = lambda a: a.reshape(1, DA_QK_DIM)
    vec_spec = pl.BlockSpec((1, DA_QK_DIM), lambda b, h: (0, 0))
    return pl.pallas_call(
        functools.partial(_diff_attn_kernel, lam_init),
        grid=(B, H),
        in_specs=[
            pl.BlockSpec((S, DA_V_DIM), lambda b, h: (b, qb + h)),
            pl.BlockSpec((S, DA_V_DIM), lambda b, h: (b, kb + h)),
            pl.BlockSpec((S, DA_V_DIM), lambda b, h: (b, vb + h)),
            pl.BlockSpec((S, DA_V_DIM), lambda b, h: (b, zb + h)),
            vec_spec, vec_spec, vec_spec, vec_spec,
            pl.BlockSpec((1, 1, DA_V_DIM), lambda b, h: (h, 0, 0)),
        ],
        out_specs=pl.BlockSpec((S, DA_V_DIM), lambda b, h: (b, h)),
        out_shape=jax.ShapeDtypeStruct((M, H * DA_V_DIM), BF16),
        compiler_params=_params(("arbitrary", "arbitrary")),
        name="diff_attn",
    )(proj, proj, proj, proj, vec(lq1), vec(lk1), vec(lq2), vec(lk2), norm_g.reshape(H, 1, DA_V_DIM))


def _ple_tail(x1_ref, ssq, p_ref, png_ref, gw_ref, pw_ref, xn_ref, emit):
    D = x1_ref.shape[1]
    inv = lax.rsqrt(ssq * (1.0 / D) + EPS)
    for n0 in range(0, D, TAIL_NC):
        cols = slice(n0, n0 + TAIL_NC)
        xn_ref[:, cols] = (x1_ref[:, cols] * inv * png_ref[:, cols]).astype(BF16)
    p_bf = p_ref[...].astype(BF16)
    for n0 in range(0, D, TAIL_NC):
        cols = slice(n0, n0 + TAIL_NC)
        gate = _sigmoid(jnp.dot(xn_ref[...], gw_ref[:, cols], preferred_element_type=F32))
        pp = jnp.dot(p_bf, pw_ref[:, cols], preferred_element_type=F32)
        emit(n0, x1_ref[:, cols] + gate * pp)


def _out_ple_kernel(n_lhs, has_final_norm, *refs):
    lhs_refs = refs[:n_lhs]
    x_ref, p_ref, wo_ref, png_ref, gw_ref, pw_ref = refs[n_lhs:n_lhs + 6]
    if has_final_norm:
        fng_ref, o_ref, x1_ref, xn_ref = refs[n_lhs + 6:]
    else:
        o_ref, x1_ref, xn_ref = refs[n_lhs + 6:]
    TM, D = x_ref.shape

    ssq = jnp.zeros((TM, 1), F32)
    for n0 in range(0, D, TAIL_NC):
        cols = slice(n0, n0 + TAIL_NC)
        y = x_ref[:, cols]
        k0 = 0
        for lhs_ref in lhs_refs:
            k1 = k0 + lhs_ref.shape[1]
            y = y + jnp.dot(lhs_ref[...], wo_ref[k0:k1, cols], preferred_element_type=F32)
            k0 = k1
        x1_ref[:, cols] = y
        ssq = ssq + jnp.sum(y * y, axis=1, keepdims=True)

    ssq_out = [jnp.zeros((TM, 1), F32)]

    def emit(n0, val):
        o_ref[:, n0:n0 + TAIL_NC] = val
        if has_final_norm:
            ssq_out[0] = ssq_out[0] + jnp.sum(val * val, axis=1, keepdims=True)

    _ple_tail(x1_ref, ssq, p_ref, png_ref, gw_ref, pw_ref, xn_ref, emit)

    if has_final_norm:
        inv = lax.rsqrt(ssq_out[0] * (1.0 / D) + EPS)
        for n0 in range(0, D, TAIL_NC):
            cols = slice(n0, n0 + TAIL_NC)
            o_ref[:, cols] = o_ref[:, cols] * inv * fng_ref[:, cols]


def _out_ple(lhs_list, x2d, p3d, layer, w_out, ple_g, gate_w, proj_w, final_g=None, name="out_ple"):
    M, D = x2d.shape
    P = p3d.shape[2]
    TM = TAIL_TM
    row = lambda i: (i, 0)
    fixed = lambda i: (0, 0)
    in_specs = [pl.BlockSpec((TM, lhs.shape[1]), row) for lhs in lhs_list]
    in_specs += [pl.BlockSpec((TM, D), row),
                 pl.BlockSpec((None, TM, P), lambda i: (layer, i, 0)),
                 _resident(w_out.shape, fixed),
                 _resident((1, D), fixed),
                 _resident(gate_w.shape, fixed),
                 _resident(proj_w.shape, fixed)]
    args = [*lhs_list, x2d, p3d, w_out, ple_g.reshape(1, D), gate_w, proj_w]
    if final_g is not None:
        in_specs.append(_resident((1, D), fixed))
        args.append(final_g.reshape(1, D))
    return pl.pallas_call(
        functools.partial(_out_ple_kernel, len(lhs_list), final_g is not None),
        grid=(M // TM,),
        in_specs=in_specs,
        out_specs=pl.BlockSpec((TM, D), row),
        out_shape=jax.ShapeDtypeStruct((M, D), F32),
        scratch_shapes=[pltpu.VMEM((TM, D), F32), pltpu.VMEM((TM, D), BF16)],
        compiler_params=_params(("arbitrary",)),
        name=name,
    )(*args)


def kernel(x, p, positions, ab_norm_g, ab_w_in, ab_conv_w, ab_conv_b, ab_i_bias, ab_f_bias, ab_ml_norm_g,
           ab_lam_q1, ab_lam_k1, ab_lam_q2, ab_lam_k2, ab_da_norm_g, ab_w_out, c_norm_g, c_w_in, c_ln_g,
           c_ln_b, c_sgu_w, c_sgu_b, c_w_out, ple_norm_g, ple_gate_w, ple_proj_w, final_norm_g):
    B, S, D = x.shape
    M = B * S
    assert ab_norm_g.shape[0] == 1 and c_norm_g.shape[0] == 1 and p.shape[0] == 2
    x2d = x.reshape(M, D)
    p3d = p.reshape(2, M, p.shape[-1])

    cos2, sin2 = _rope_tables(positions)
    proj, gates = _in_proj_ab(x2d, ab_norm_g[0], ab_w_in[0].T, cos2, sin2)

    a_out = _mlstm(proj, gates, ab_conv_w[0], ab_conv_b[0], ab_i_bias[0], ab_f_bias[0], ab_ml_norm_g[0], B, S)

    lam_init = 0.8 - 0.6 * math.exp(-0.3 * 0)
    b_out = _diff_attn(proj, ab_lam_q1[0], ab_lam_k1[0], ab_lam_q2[0], ab_lam_k2[0], ab_da_norm_g[0],
                       lam_init, B, S)

    x2 = _out_ple([a_out, b_out], x2d, p3d, 0, ab_w_out[0].astype(BF16), ple_norm_g[0],
                  ple_gate_w[0].astype(BF16), ple_proj_w[0].astype(BF16), name="out_ple_ab")

    t = _in_proj_sgu(x2, c_norm_g[0], c_w_in[0], c_ln_g[0], c_ln_b[0], c_sgu_w[0], c_sgu_b[0])
    out = _out_ple([t], x2, p3d, 1, c_w_out[0].astype(BF16), ple_norm_g[1], ple_gate_w[1].astype(BF16),
                   ple_proj_w[1].astype(BF16), final_g=final_norm_g, name="out_ple_c")
    return out.reshape(B, S, D)
```

```python
import functools
import math

import jax
import jax.numpy as jnp
from jax import lax
from jax.experimental import pallas as pl
from jax.experimental.pallas import tpu as pltpu

F32 = jnp.float32
BF16 = jnp.bfloat16
HIGHEST = lax.Precision.HIGHEST

EPS = 1e-6
ROPE_THETA = 10000.0
CONV_WIDTH = 4

ML_HEADS = 4
ML_QK_DIM = 128
ML_V_DIM = 256
DA_HEADS = 4
DA_QK_DIM = 128
DA_V_DIM = 256
SGU_CHUNK = 128
SGU_GROUPS = 8
SGU_GROUP_DIM = 256

ML_Q_OFF, ML_K_OFF, ML_V_OFF, ML_O_OFF, ML_Z_OFF = 0, 512, 1024, 2048, 3072
DA_Q_OFF, DA_K_OFF, DA_V_OFF, DA_Z_OFF = 4096, 5120, 6144, 7168
GATE_COLS = 128
GATE_START = ML_V_OFF + ML_HEADS * ML_V_DIM

VMEM_LIMIT_BYTES = 56 * 1024 * 1024

PROJ_TM = 1024
PROJ_TN = 1024
NORM_ROWS = 128
TAIL_TM = 512
TAIL_NC = 512
PROJ_SLAB = 256
CONV_ROWS = 128
ML_BLOCK = 256
ATT_QSUB = 256
ATT_IN_FLIGHT = 2


def _params(semantics):
    return pltpu.CompilerParams(dimension_semantics=semantics, vmem_limit_bytes=VMEM_LIMIT_BYTES)


def _resident(shape, index_map):
    return pl.BlockSpec(shape, index_map, pipeline_mode=pl.Buffered(1))


def _sigmoid(x):
    return 1.0 / (1.0 + jnp.exp(-x))


def _silu(x):
    return x * _sigmoid(x)


def _log_sigmoid(x):
    return jnp.minimum(x, 0.0) - jnp.log(1.0 + jnp.exp(-jnp.abs(x)))


def _iota(shape, axis):
    return lax.broadcasted_iota(jnp.int32, shape, axis)


def _rope_table_kernel(pos_ref, cos_ref, sin_ref):
    pos = pos_ref[0]
    half_rows = pos.shape[0]
    lane = _iota(pos.shape, 1)
    lo = lane < 64
    j = jnp.where(lo, lane, lane - 64).astype(F32)
    freq = jnp.exp(j * (-math.log(ROPE_THETA) / 64.0))
    ang = pos * freq
    c = jnp.cos(ang)
    s = jnp.sin(ang)
    cr = pltpu.roll(c, 64, 1)
    sr = pltpu.roll(s, 64, 1)
    cos_ref[0, 0:half_rows, :] = jnp.where(lo, c, cr)
    cos_ref[0, half_rows:2 * half_rows, :] = jnp.where(lo, cr, c)
    sin_ref[0, 0:half_rows, :] = jnp.where(lo, -s, sr)
    sin_ref[0, half_rows:2 * half_rows, :] = jnp.where(lo, -sr, s)


def _rope_tables(positions):
    B, S = positions.shape
    posf = positions.astype(F32)
    lo = jnp.broadcast_to(posf[:, :S // 2, None], (B, S // 2, 64))
    hi = jnp.broadcast_to(posf[:, S // 2:, None], (B, S // 2, 64))
    pos2 = jnp.concatenate([lo, hi], axis=-1)
    return pl.pallas_call(
        _rope_table_kernel,
        grid=(B,),
        in_specs=[pl.BlockSpec((1, S // 2, 128), lambda b: (b, 0, 0))],
        out_specs=[pl.BlockSpec((1, S, 128), lambda b: (b, 0, 0)),
                   pl.BlockSpec((1, S, 128), lambda b: (b, 0, 0))],
        out_shape=[jax.ShapeDtypeStruct((B, S, 128), F32)] * 2,
        compiler_params=_params(("arbitrary",)),
        name="rope_tables",
    )(pos2)


def _normalize_rows(x_ref, g_ref, xn_ref):
    g = g_ref[...]
    for r0 in range(0, x_ref.shape[0], NORM_ROWS):
        rows = slice(r0, r0 + NORM_ROWS)
        x = x_ref[rows, :]
        ms = jnp.mean(x * x, axis=-1, keepdims=True)
        xn_ref[rows, :] = (x * lax.rsqrt(ms + EPS) * g).astype(BF16)


_NT = (((1,), (1,)), ((), ()))


def _in_proj_ab_kernel(n_gate, x_ref, g_ref, wt_ref, edge_ref, wgt_ref, cos_ref, sin_ref, o_ref, og_ref,
                       xn_ref):
    j = pl.program_id(1)
    TN, D = wt_ref.shape
    rope_q_tile, rope_k_tile = DA_Q_OFF // TN, DA_K_OFF // TN
    first_shifted = GATE_START // TN

    @pl.when(j == 0)
    def _():
        _normalize_rows(x_ref, g_ref, xn_ref)
        wg = jnp.concatenate([wgt_ref[...], jnp.zeros((GATE_COLS - n_gate, D), F32)], axis=0).astype(BF16)
        og_ref[...] = lax.dot_general(xn_ref[...], wg, _NT, preferred_element_type=F32)

    def slabs(shift):
        for r0 in range(0, TN, PROJ_SLAB):
            lo, hi = r0 + shift, r0 + PROJ_SLAB + shift
            if hi <= TN:
                w = wt_ref[lo:hi, :]
            else:
                w = jnp.concatenate([wt_ref[lo:TN, :], edge_ref[...]], axis=0)
            yield r0, lax.dot_general(xn_ref[...], w.astype(BF16), _NT, preferred_element_type=F32)

    def plain(shift):
        for r0, y in slabs(shift):
            o_ref[:, r0:r0 + PROJ_SLAB] = y.astype(o_ref.dtype)

    def rope(shift, scale):
        cs = cos_ref[...]
        sn = sin_ref[...]
        for r0, y in slabs(shift):
            for c0 in range(0, PROJ_SLAB, DA_QK_DIM):
                x = y[:, c0:c0 + DA_QK_DIM]
                r = x * cs + pltpu.roll(x, DA_QK_DIM // 2, 1) * sn
                if scale is not None:
                    r = r * scale
                o_ref[:, r0 + c0:r0 + c0 + DA_QK_DIM] = r.astype(o_ref.dtype)

    @pl.when(j < first_shifted)
    def _():
        plain(0)

    @pl.when((j >= first_shifted) & (j != rope_q_tile) & (j != rope_k_tile))
    def _():
        plain(n_gate)

    @pl.when(j == rope_q_tile)
    def _():
        rope(n_gate, DA_QK_DIM ** -0.5 * math.log2(math.e))

    @pl.when(j == rope_k_tile)
    def _():
        rope(n_gate, None)


def _in_proj_ab(x2d, g, w_t, cos2, sin2):
    M, D = x2d.shape
    n_gate = 2 * ML_HEADS
    N = w_t.shape[0] - n_gate
    TM, TN = PROJ_TM, PROJ_TN
    assert n_gate == 8 and GATE_START % TN == 0 and N % TN == 0
    assert DA_Q_OFF % TN == 0 and DA_K_OFF - DA_Q_OFF == TN and DA_V_OFF - DA_K_OFF == TN
    fixed = lambda i, j: (0, 0)
    return pl.pallas_call(
        functools.partial(_in_proj_ab_kernel, n_gate),
        grid=(M // TM, N // TN),
        in_specs=[pl.BlockSpec((TM, D), lambda i, j: (i, 0)),
                  pl.BlockSpec((1, D), fixed),
                  pl.BlockSpec((TN, D), lambda i, j: (j, 0)),
                  pl.BlockSpec((n_gate, D), lambda i, j: ((j + 1) * (TN // n_gate), 0)),
                  pl.BlockSpec((n_gate, D), lambda i, j: (GATE_START // n_gate, 0)),
                  pl.BlockSpec((TM, DA_QK_DIM), lambda i, j: (i, 0)),
                  pl.BlockSpec((TM, DA_QK_DIM), lambda i, j: (i, 0))],
        out_specs=[pl.BlockSpec((TM, TN), lambda i, j: (i, j)),
                   pl.BlockSpec((TM, GATE_COLS), lambda i, j: (i, 0))],
        out_shape=[jax.ShapeDtypeStruct((M, N), BF16), jax.ShapeDtypeStruct((M, GATE_COLS), F32)],
        scratch_shapes=[pltpu.VMEM((TM, D), BF16)],
        compiler_params=_params(("arbitrary", "arbitrary")),
        name="in_proj_ab",
    )(x2d, g.reshape(1, D), w_t, w_t, w_t, cos2.reshape(M, DA_QK_DIM), sin2.reshape(M, DA_QK_DIM))


_SGU_STEP_BLOCKS = (2, 3, 0, 4, 1, 5)


def _sgu_step_block(j):
    blk = jnp.int32(_SGU_STEP_BLOCKS[-1])
    for step in range(len(_SGU_STEP_BLOCKS) - 2, -1, -1):
        blk = jnp.where(j == step, _SGU_STEP_BLOCKS[step], blk)
    return blk


def _in_proj_sgu_kernel(x_ref, g_ref, w_ref, lng_ref, lnb_ref, sw_ref, sbt_ref, t_ref, xn_ref, v_ref, u_ref,
                        st_ref):
    j = pl.program_id(1)
    TM = x_ref.shape[0]
    TN = w_ref.shape[1]
    W = v_ref.shape[1]
    L = SGU_CHUNK
    GD = SGU_GROUP_DIM
    groups_per_step = TN // GD

    def proj_cols(cols):
        return jnp.dot(xn_ref[...], w_ref[:, cols].astype(BF16), preferred_element_type=F32)

    def project_v(half):
        for gl in range(groups_per_step):
            cols = slice(gl * GD, (gl + 1) * GD)
            slab = proj_cols(cols)
            v_ref[:, half * TN + gl * GD:half * TN + (gl + 1) * GD] = slab.astype(BF16)
            if half == 0 and gl == 0:
                shift = jnp.mean(slab, axis=1, keepdims=True)
                s1 = jnp.zeros((TM, 1), F32)
                s2 = jnp.zeros((TM, 1), F32)
            elif gl == 0:
                shift, s1, s2 = st_ref[0], st_ref[1], st_ref[2]
            d = slab - shift
            s1 = s1 + jnp.sum(d, axis=1, keepdims=True)
            s2 = s2 + jnp.sum(d * d, axis=1, keepdims=True)
        if half == 0:
            st_ref[0], st_ref[1], st_ref[2] = shift, s1, s2
        else:
            m1 = s1 * (1.0 / W)
            st_ref[0] = shift + m1
            st_ref[1] = lax.rsqrt(jnp.maximum(s2 * (1.0 / W) - m1 * m1, 0.0) + EPS)

    def gate(half):
        causal = _iota((L, L), 1) <= _iota((L, L), 0)
        for gl in range(groups_per_step):
            g = half * groups_per_step + gl
            cols = slice(gl * GD, (gl + 1) * GD)
            vcols = slice(g * GD, (g + 1) * GD)
            z = proj_cols(cols)
            w_causal = jnp.where(causal, sw_ref[g], 0.0).astype(BF16)
            bias = sbt_ref[:, g:g + 1]
            lng = lng_ref[:, vcols]
            lnb = lnb_ref[:, vcols]
            for c in range(TM // L):
                rows = slice(c * L, (c + 1) * L)
                vn = ((v_ref[rows, vcols].astype(F32) - st_ref[0, rows, :]) * st_ref[1, rows, :] * lng
                      + lnb).astype(BF16)
                sv = jnp.dot(w_causal, vn, preferred_element_type=F32) + bias
                t = u_ref[rows, cols].astype(F32) * sv * _silu(z[rows, :])
                t_ref[rows, cols] = t.astype(BF16)

    @pl.when(j == 0)
    def _():
        _normalize_rows(x_ref, g_ref, xn_ref)
        project_v(0)

    @pl.when(j == 1)
    def _():
        project_v(1)

    @pl.when((j == 2) | (j == 4))
    def _():
        for gl in range(groups_per_step):
            cols = slice(gl * GD, (gl + 1) * GD)
            u_ref[:, cols] = proj_cols(cols).astype(BF16)

    @pl.when(j == 3)
    def _():
        gate(0)

    @pl.when(j == 5)
    def _():
        gate(1)


def _in_proj_sgu(x2d, g, w, ln_g, ln_b, sgu_w, sgu_b):
    M, D = x2d.shape
    W = w.shape[1] // 3
    assert W == 2 * PROJ_TN and len(_SGU_STEP_BLOCKS) == 3 * W // PROJ_TN
    fixed = lambda i, j: (0, 0)
    return pl.pallas_call(
        _in_proj_sgu_kernel,
        grid=(M // PROJ_TM, len(_SGU_STEP_BLOCKS)),
        in_specs=[pl.BlockSpec((PROJ_TM, D), lambda i, j: (i, 0)),
                  pl.BlockSpec((1, D), fixed),
                  pl.BlockSpec((D, PROJ_TN), lambda i, j: (0, _sgu_step_block(j))),
                  pl.BlockSpec((1, W), fixed),
                  pl.BlockSpec((1, W), fixed),
                  pl.BlockSpec(sgu_w.shape, lambda i, j: (0, 0, 0)),
                  pl.BlockSpec((SGU_CHUNK, SGU_GROUPS), fixed)],
        out_specs=pl.BlockSpec((PROJ_TM, PROJ_TN), lambda i, j: (i, jnp.where(j <= 3, 0, 1))),
        out_shape=jax.ShapeDtypeStruct((M, W), BF16),
        scratch_shapes=[pltpu.VMEM((PROJ_TM, D), BF16),
                        pltpu.VMEM((PROJ_TM, W), BF16),
                        pltpu.VMEM((PROJ_TM, PROJ_TN), BF16),
                        pltpu.VMEM((3, PROJ_TM, 1), F32)],
        compiler_params=_params(("arbitrary", "arbitrary")),
        name="in_proj_sgu",
    )(x2d, g.reshape(1, D), w, ln_g.reshape(1, W), ln_b.reshape(1, W), sgu_w, sgu_b.T)


def _mlstm_kernel(qp_ref, kp_ref, v_ref, gt_ref, og_ref, z_ref, cwq_ref, cwk_ref, cbq_ref, cbk_ref,
                  gb_ref, ng_ref, out_ref, xs_ref, qc_ref, kc_ref):
    h = pl.program_id(1)
    S = qp_ref.shape[0]
    LB = ML_BLOCK
    nb = S // LB
    DK, DV = ML_QK_DIM, ML_V_DIM

    for src, cw_ref, cb_ref, dst, scale in ((qp_ref, cwq_ref, cbq_ref, qc_ref, 1.0),
                                            (kp_ref, cwk_ref, cbk_ref, kc_ref, DK ** -0.5)):
        xs_ref[0:8, :] = jnp.zeros((8, DK), F32)
        xs_ref[8:8 + S, :] = src[...].astype(F32)
        cw = cw_ref[...]
        cb = cb_ref[...]
        for c in range(S // CONV_ROWS):
            acc = cb
            for j in range(CONV_WIDTH):
                start = 8 + c * CONV_ROWS - (CONV_WIDTH - 1) + j
                acc = acc + cw[j:j + 1, :] * xs_ref[start:start + CONV_ROWS, :]
            dst[c * CONV_ROWS:(c + 1) * CONV_ROWS, :] = (_silu(acc) * scale).astype(BF16)

    gates = gt_ref[...] + gb_ref[...]
    pick = (_iota((8, GATE_COLS), 1) == _iota((8, GATE_COLS), 0)).astype(F32)
    gate_rows = lax.dot_general(pick, gates, (((1,), (1,)), ((), ())), preferred_element_type=F32,
                                precision=HIGHEST)
    sub8 = _iota((8, S), 0)
    i_all = jnp.sum(jnp.where(sub8 == h, gate_rows, 0.0), axis=0, keepdims=True)
    logf_all = _log_sigmoid(jnp.sum(jnp.where(sub8 == h + ML_HEADS, gate_rows, 0.0), axis=0, keepdims=True))

    blk = _iota((nb, LB), 0)

    def to_blocks(row):
        tile = jnp.zeros((nb, LB), F32)
        for c in range(nb):
            tile = jnp.where(blk == c, jnp.broadcast_to(row[:, c * LB:(c + 1) * LB], (nb, LB)), tile)
        return tile

    sub = _iota((LB, LB), 0)
    lane = _iota((LB, LB), 1)
    causal = lane <= sub
    i_b = to_blocks(i_all)
    b_b = jnp.dot(to_blocks(logf_all), (sub <= lane).astype(F32), preferred_element_type=F32,
                  precision=HIGHEST)
    g_tot = b_b[:, LB - 1:LB]
    w_b = g_tot - b_b + i_b
    m_loc = jnp.max(w_b, axis=1, keepdims=True)
    e_b = jnp.exp(w_b - m_loc)
    b_pad = jnp.concatenate([b_b, jnp.zeros((GATE_COLS - nb, LB), F32)], axis=0)
    b_cols = lax.dot_general((sub == lane).astype(F32), b_pad, (((1,), (1,)), ((), ())),
                             preferred_element_type=F32, precision=HIGHEST)

    norm_g = ng_ref[0]
    ct = jnp.zeros((DK, DV), F32)
    n = jnp.zeros((DK, 1), F32)
    m = jnp.zeros((1, 1), F32)
    for c in range(nb):
        rows = slice(c * LB, (c + 1) * LB)
        b_row, i_row, e_row = b_b[c:c + 1, :], i_b[c:c + 1, :], e_b[c:c + 1, :]
        b_col = b_cols[:, c:c + 1]
        q = qc_ref[rows, :]
        v = v_ref[rows, :]
        k_t = kc_ref[rows, :].astype(F32).T

        rhs = jnp.concatenate([k_t.astype(BF16), ct.astype(BF16),
                               jnp.broadcast_to(n, (DK, 128)).astype(BF16)], axis=1)
        big = jnp.dot(q, rhs, preferred_element_type=F32)
        qk, inter, qn = big[:, :LB], big[:, LB:LB + DV], big[:, LB + DV:LB + DV + 1]

        log_d = jnp.where(causal, b_col - b_row + i_row, -jnp.inf)
        inter_log = b_col + m
        m_t = jnp.maximum(inter_log, jnp.max(log_d, axis=1, keepdims=True))
        d_mat = jnp.exp(log_d - m_t)
        inter_w = jnp.exp(inter_log - m_t)
        s_mat = qk * d_mat
        num = jnp.dot(s_mat.astype(BF16), v, preferred_element_type=F32) + inter_w * inter
        den = jnp.sum(s_mat, axis=1, keepdims=True) + inter_w * qn
        hm = num / jnp.maximum(jnp.abs(den), jnp.exp(-m_t))

        hn = hm * lax.rsqrt(jnp.mean(hm * hm, axis=-1, keepdims=True) + EPS) * norm_g
        og = og_ref[rows, :].astype(F32)
        z = z_ref[rows, :].astype(F32)
        out_ref[rows, :] = (_sigmoid(og) * hn * _silu(z)).astype(out_ref.dtype)

        if c + 1 < nb:
            ke = k_t * e_row
            ct_loc = jnp.dot(ke.astype(BF16), v, preferred_element_type=F32)
            n_loc = jnp.sum(ke, axis=1, keepdims=True)
            g_c, ml_c = g_tot[c:c + 1, :], m_loc[c:c + 1, :]
            m_new = jnp.maximum(g_c + m, ml_c)
            a = jnp.exp(g_c + m - m_new)
            cc = jnp.exp(ml_c - m_new)
            ct = a * ct + cc * ct_loc
            n = a * n + cc * n_loc
            m = m_new


def _mlstm(proj, gates, conv_w, conv_b, i_bias, f_bias, norm_g, B, S):
    M = B * S
    H = ML_HEADS
    assert S // ML_BLOCK == 8, "block rows are packed into one 8-sublane tile"
    gate_bias = jnp.concatenate([i_bias, f_bias, jnp.zeros((GATE_COLS - 2 * H,), F32)]).reshape(1, GATE_COLS)
    conv_b2 = conv_b.reshape(1, -1)
    norm_g3 = norm_g.reshape(H, 1, ML_V_DIM)
    qb, kb = ML_Q_OFF // ML_QK_DIM, ML_K_OFF // ML_QK_DIM
    vb, ob, zb = ML_V_OFF // ML_V_DIM, ML_O_OFF // ML_V_DIM, ML_Z_OFF // ML_V_DIM
    return pl.pallas_call(
        _mlstm_kernel,
        grid=(B, H),
        in_specs=[
            pl.BlockSpec((S, ML_QK_DIM), lambda b, h: (b, qb + h)),
            pl.BlockSpec((S, ML_QK_DIM), lambda b, h: (b, kb + h)),
            pl.BlockSpec((S, ML_V_DIM), lambda b, h: (b, vb + h)),
            pl.BlockSpec((S, GATE_COLS), lambda b, h: (b, 0)),
            pl.BlockSpec((S, ML_V_DIM), lambda b, h: (b, ob + h)),
            pl.BlockSpec((S, ML_V_DIM), lambda b, h: (b, zb + h)),
            pl.BlockSpec((CONV_WIDTH, ML_QK_DIM), lambda b, h: (0, h)),
            pl.BlockSpec((CONV_WIDTH, ML_QK_DIM), lambda b, h: (0, H + h)),
            pl.BlockSpec((1, ML_QK_DIM), lambda b, h: (0, h)),
            pl.BlockSpec((1, ML_QK_DIM), lambda b, h: (0, H + h)),
            pl.BlockSpec((1, GATE_COLS), lambda b, h: (0, 0)),
            pl.BlockSpec((1, 1, ML_V_DIM), lambda b, h: (h, 0, 0)),
        ],
        out_specs=pl.BlockSpec((S, ML_V_DIM), lambda b, h: (b, h)),
        out_shape=jax.ShapeDtypeStruct((M, H * ML_V_DIM), BF16),
        scratch_shapes=[pltpu.VMEM((8 + S, ML_QK_DIM), F32),
                        pltpu.VMEM((S, ML_QK_DIM), BF16),
                        pltpu.VMEM((S, ML_QK_DIM), BF16)],
        compiler_params=_params(("arbitrary", "arbitrary")),
        name="mlstm",
    )(proj, proj, proj, gates, proj, proj, conv_w, conv_w, conv_b2, conv_b2, gate_bias, norm_g3)


def _diff_attn_kernel(lam_init, qr_ref, kr_ref, v_ref, z_ref, lq1_ref, lk1_ref, lq2_ref, lk2_ref, ng_ref,
                      o_ref):
    S = kr_ref.shape[0]
    QS, DH = ATT_QSUB, DA_QK_DIM
    diag_mask = _iota((QS, QS), 1) <= _iota((QS, QS), 0)

    def softmax_pv(q0, m, after):
        q = qr_ref[q0:q0 + QS, m * DH:(m + 1) * DH]
        if after is not None:
            q = jnp.where(after > 0.0, q, jnp.zeros_like(q))

        kv_len = q0 + QS
        s = lax.dot_general(q, kr_ref[0:kv_len, m * DH:(m + 1) * DH], _NT, preferred_element_type=F32)
        s_diag = jnp.where(diag_mask, s[:, q0:], -jnp.inf)
        s = s_diag if q0 == 0 else jnp.concatenate([s[:, :q0], s_diag], axis=1)
        p = jnp.exp2(s - jnp.max(s, axis=1, keepdims=True))
        acc = jnp.dot(p.astype(BF16), v_ref[0:kv_len, :], preferred_element_type=F32)
        return acc, jnp.sum(p, axis=1, keepdims=True)

    lam = (jnp.exp(jnp.sum(lq1_ref[...] * lk1_ref[...], axis=1, keepdims=True))
           - jnp.exp(jnp.sum(lq2_ref[...] * lk2_ref[...], axis=1, keepdims=True)) + lam_init)
    norm_g = ng_ref[0] * (1.0 - lam_init)

    done = [(None, None)] * ATT_IN_FLIGHT
    for q0 in range(0, S, QS):
        acc1, l1 = softmax_pv(q0, 0, done[-ATT_IN_FLIGHT][0])
        acc2, l2 = softmax_pv(q0, 1, done[-ATT_IN_FLIGHT][1])
        done.append((l1, l2))
        o = acc1 / l1 - lam * (acc2 / l2)
        hn = o * lax.rsqrt(jnp.mean(o * o, axis=-1, keepdims=True) + EPS) * norm_g
        o_ref[q0:q0 + QS, :] = (hn * _silu(z_ref[q0:q0 + QS, :].astype(F32))).astype(o_ref.dtype)


def _diff_attn(proj, lq1, lk1, lq2, lk2, norm_g, lam_init, B, S):
    M = B * S
    H = DA_HEADS
    qb, kb, vb, zb = (DA_Q_OFF // DA_V_DIM, DA_K_OFF // DA_V_DIM, DA_V_OFF // DA_V_DIM, DA_Z_OFF // DA_V_DIM)
    vec = lambda a: a.reshape(1, DA_QK_DIM)
    vec_spec = pl.BlockSpec((1, DA_QK_DIM), lambda b, h: (0, 0))
    return pl.pallas_call(
        functools.partial(_diff_attn_kernel, lam_init),
        grid=(B, H),
        in_specs=[
            pl.BlockSpec((S, DA_V_DIM), lambda b, h: (b, qb + h)),
            pl.BlockSpec((S, DA_V_DIM), lambda b, h: (b, kb + h)),
            pl.BlockSpec((S, DA_V_DIM), lambda b, h: (b, vb + h)),
            pl.BlockSpec((S, DA_V_DIM), lambda b, h: (b, zb + h)),
            vec_spec, vec_spec, vec_spec, vec_spec,
            pl.BlockSpec((1, 1, DA_V_DIM), lambda b, h: (h, 0, 0)),
        ],
        out_specs=pl.BlockSpec((S, DA_V_DIM), lambda b, h: (b, h)),
        out_shape=jax.ShapeDtypeStruct((M, H * DA_V_DIM), BF16),
        compiler_params=_params(("arbitrary", "arbitrary")),
        name="diff_attn",
    )(proj, proj, proj, proj, vec(lq1), vec(lk1), vec(lq2), vec(lk2), norm_g.reshape(H, 1, DA_V_DIM))


def _ple_tail(x1_ref, ssq, p_ref, png_ref, gw_ref, pw_ref, xn_ref, emit):
    D = x1_ref.shape[1]
    inv = lax.rsqrt(ssq * (1.0 / D) + EPS)
    for n0 in range(0, D, TAIL_NC):
        cols = slice(n0, n0 + TAIL_NC)
        xn_ref[:, cols] = (x1_ref[:, cols] * inv * png_ref[:, cols]).astype(BF16)
    p_bf = p_ref[...].astype(BF16)
    for n0 in range(0, D, TAIL_NC):
        cols = slice(n0, n0 + TAIL_NC)
        gate = _sigmoid(jnp.dot(xn_ref[...], gw_ref[:, cols], preferred_element_type=F32))
        pp = jnp.dot(p_bf, pw_ref[:, cols], preferred_element_type=F32)
        emit(n0, x1_ref[:, cols] + gate * pp)


def _out_ple_kernel(n_lhs, has_final_norm, *refs):
    lhs_refs = refs[:n_lhs]
    x_ref, p_ref, wo_ref, png_ref, gw_ref, pw_ref = refs[n_lhs:n_lhs + 6]
    if has_final_norm:
        fng_ref, o_ref, x1_ref, xn_ref = refs[n_lhs + 6:]
    else:
        o_ref, x1_ref, xn_ref = refs[n_lhs + 6:]
    TM, D = x_ref.shape

    ssq = jnp.zeros((TM, 1), F32)
    for n0 in range(0, D, TAIL_NC):
        cols = slice(n0, n0 + TAIL_NC)
        y = x_ref[:, cols]
        k0 = 0
        for lhs_ref in lhs_refs:
            k1 = k0 + lhs_ref.shape[1]
            y = y + jnp.dot(lhs_ref[...], wo_ref[k0:k1, cols], preferred_element_type=F32)
            k0 = k1
        x1_ref[:, cols] = y
        ssq = ssq + jnp.sum(y * y, axis=1, keepdims=True)

    ssq_out = [jnp.zeros((TM, 1), F32)]

    def emit(n0, val):
        o_ref[:, n0:n0 + TAIL_NC] = val
        if has_final_norm:
            ssq_out[0] = ssq_out[0] + jnp.sum(val * val, axis=1, keepdims=True)

    _ple_tail(x1_ref, ssq, p_ref, png_ref, gw_ref, pw_ref, xn_ref, emit)

    if has_final_norm:
        inv = lax.rsqrt(ssq_out[0] * (1.0 / D) + EPS)
        for n0 in range(0, D, TAIL_NC):
            cols = slice(n0, n0 + TAIL_NC)
            o_ref[:, cols] = o_ref[:, cols] * inv * fng_ref[:, cols]


def _out_ple(lhs_list, x2d, p3d, layer, w_out, ple_g, gate_w, proj_w, final_g=None, name="out_ple"):
    M, D = x2d.shape
    P = p3d.shape[2]
    TM = TAIL_TM
    row = lambda i: (i, 0)
    fixed = lambda i: (0, 0)
    in_specs = [pl.BlockSpec((TM, lhs.shape[1]), row) for lhs in lhs_list]
    in_specs += [pl.BlockSpec((TM, D), row),
                 pl.BlockSpec((None, TM, P), lambda i: (layer, i, 0)),
                 _resident(w_out.shape, fixed),
                 _resident((1, D), fixed),
                 _resident(gate_w.shape, fixed),
                 _resident(proj_w.shape, fixed)]
    args = [*lhs_list, x2d, p3d, w_out, ple_g.reshape(1, D), gate_w, proj_w]
    if final_g is not None:
        in_specs.append(_resident((1, D), fixed))
        args.append(final_g.reshape(1, D))
    return pl.pallas_call(
        functools.partial(_out_ple_kernel, len(lhs_list), final_g is not None),
        grid=(M // TM,),
        in_specs=in_specs,
        out_specs=pl.BlockSpec((TM, D), row),
        out_shape=jax.ShapeDtypeStruct((M, D), F32),
        scratch_shapes=[pltpu.VMEM((TM, D), F32), pltpu.VMEM((TM, D), BF16)],
        compiler_params=_params(("arbitrary",)),
        name=name,
    )(*args)


def kernel(x, p, positions, ab_norm_g, ab_w_in, ab_conv_w, ab_conv_b, ab_i_bias, ab_f_bias, ab_ml_norm_g,
           ab_lam_q1, ab_lam_k1, ab_lam_q2, ab_lam_k2, ab_da_norm_g, ab_w_out, c_norm_g, c_w_in, c_ln_g,
           c_ln_b, c_sgu_w, c_sgu_b, c_w_out, ple_norm_g, ple_gate_w, ple_proj_w, final_norm_g):
    B, S, D = x.shape
    M = B * S
    assert ab_norm_g.shape[0] == 1 and c_norm_g.shape[0] == 1 and p.shape[0] == 2
    x2d = x.reshape(M, D)
    p3d = p.reshape(2, M, p.shape[-1])

    cos2, sin2 = _rope_tables(positions)
    proj, gates = _in_proj_ab(x2d, ab_norm_g[0], ab_w_in[0].T, cos2, sin2)

    a_out = _mlstm(proj, gates, ab_conv_w[0], ab_conv_b[0], ab_i_bias[0], ab_f_bias[0], ab_ml_norm_g[0], B, S)

    lam_init = 0.8 - 0.6 * math.exp(-0.3 * 0)
    b_out = _diff_attn(proj, ab_lam_q1[0], ab_lam_k1[0], ab_lam_q2[0], ab_lam_k2[0], ab_da_norm_g[0],
                       lam_init, B, S)

    x2 = _out_ple([a_out, b_out], x2d, p3d, 0, ab_w_out[0].astype(BF16), ple_norm_g[0],
                  ple_gate_w[0].astype(BF16), ple_proj_w[0].astype(BF16), name="out_ple_ab")

    t = _in_proj_sgu(x2, c_norm_g[0], c_w_in[0], c_ln_g[0], c_ln_b[0], c_sgu_w[0], c_sgu_b[0])
    out = _out_ple([t], x2, p3d, 1, c_w_out[0].astype(BF16), ple_norm_g[1], ple_gate_w[1].astype(BF16),
                   ple_proj_w[1].astype(BF16), final_g=final_norm_g, name="out_ple_c")
    return out.reshape(B, S, D)
```

```python
import functools
import math

import jax
import jax.numpy as jnp
from jax import lax
from jax.experimental import pallas as pl
from jax.experimental.pallas import tpu as pltpu

F32 = jnp.float32
BF16 = jnp.bfloat16
HIGHEST = lax.Precision.HIGHEST

EPS = 1e-6
ROPE_THETA = 10000.0
CONV_WIDTH = 4

ML_HEADS = 4
ML_QK_DIM = 128
ML_V_DIM = 256
DA_HEADS = 4
DA_QK_DIM = 128
DA_V_DIM = 256
SGU_CHUNK = 128
SGU_GROUPS = 8
SGU_GROUP_DIM = 256

ML_Q_OFF, ML_K_OFF, ML_V_OFF, ML_O_OFF, ML_Z_OFF = 0, 512, 1024, 2048, 3072
DA_Q_OFF, DA_K_OFF, DA_V_OFF, DA_Z_OFF = 4096, 5120, 6144, 7168
GATE_COLS = 128
GATE_START = ML_V_OFF + ML_HEADS * ML_V_DIM

VMEM_LIMIT_BYTES = 56 * 1024 * 1024

PROJ_TM = 1024
PROJ_TN = 1024
NORM_ROWS = 128
TAIL_TM = 256
TAIL_NC = 512
PROJ_SLAB = 256
CONV_ROWS = 128
ML_BLOCK = 256
ATT_QSUB = 256
ATT_IN_FLIGHT = 2


def _params(semantics):
    return pltpu.CompilerParams(dimension_semantics=semantics, vmem_limit_bytes=VMEM_LIMIT_BYTES)


def _resident(shape, index_map):
    return pl.BlockSpec(shape, index_map, pipeline_mode=pl.Buffered(1))


def _sigmoid(x):
    return 1.0 / (1.0 + jnp.exp(-x))


def _silu(x):
    return x * _sigmoid(x)


def _log_sigmoid(x):
    return jnp.minimum(x, 0.0) - jnp.log(1.0 + jnp.exp(-jnp.abs(x)))


def _iota(shape, axis):
    return lax.broadcasted_iota(jnp.int32, shape, axis)


def _rope_table_kernel(pos_ref, cos_ref, sin_ref):
    pos = pos_ref[0]
    half_rows = pos.shape[0]
    lane = _iota(pos.shape, 1)
    lo = lane < 64
    j = jnp.where(lo, lane, lane - 64).astype(F32)
    freq = jnp.exp(j * (-math.log(ROPE_THETA) / 64.0))
    ang = pos * freq
    c = jnp.cos(ang)
    s = jnp.sin(ang)
    cr = pltpu.roll(c, 64, 1)
    sr = pltpu.roll(s, 64, 1)
    cos_ref[0, 0:half_rows, :] = jnp.where(lo, c, cr)
    cos_ref[0, half_rows:2 * half_rows, :] = jnp.where(lo, cr, c)
    sin_ref[0, 0:half_rows, :] = jnp.where(lo, -s, sr)
    sin_ref[0, half_rows:2 * half_rows, :] = jnp.where(lo, -sr, s)


def _rope_tables(positions):
    B, S = positions.shape
    posf = positions.astype(F32)
    lo = jnp.broadcast_to(posf[:, :S // 2, None], (B, S // 2, 64))
    hi = jnp.broadcast_to(posf[:, S // 2:, None], (B, S // 2, 64))
    pos2 = jnp.concatenate([lo, hi], axis=-1)
    return pl.pallas_call(
        _rope_table_kernel,
        grid=(B,),
        in_specs=[pl.BlockSpec((1, S // 2, 128), lambda b: (b, 0, 0))],
        out_specs=[pl.BlockSpec((1, S, 128), lambda b: (b, 0, 0)),
                   pl.BlockSpec((1, S, 128), lambda b: (b, 0, 0))],
        out_shape=[jax.ShapeDtypeStruct((B, S, 128), F32)] * 2,
        compiler_params=_params(("arbitrary",)),
        name="rope_tables",
    )(pos2)


def _normalize_rows(x_ref, g_ref, xn_ref):
    g = g_ref[...]
    for r0 in range(0, x_ref.shape[0], NORM_ROWS):
        rows = slice(r0, r0 + NORM_ROWS)
        x = x_ref[rows, :]
        ms = jnp.mean(x * x, axis=-1, keepdims=True)
        xn_ref[rows, :] = (x * lax.rsqrt(ms + EPS) * g).astype(BF16)


_NT = (((1,), (1,)), ((), ()))


def _in_proj_ab_kernel(n_gate, x_ref, g_ref, wt_ref, edge_ref, wgt_ref, cos_ref, sin_ref, o_ref, og_ref,
                       xn_ref):
    j = pl.program_id(1)
    TN, D = wt_ref.shape
    rope_q_tile, rope_k_tile = DA_Q_OFF // TN, DA_K_OFF // TN
    first_shifted = GATE_START // TN

    @pl.when(j == 0)
    def _():
        _normalize_rows(x_ref, g_ref, xn_ref)
        wg = jnp.concatenate([wgt_ref[...], jnp.zeros((GATE_COLS - n_gate, D), F32)], axis=0).astype(BF16)
        og_ref[...] = lax.dot_general(xn_ref[...], wg, _NT, preferred_element_type=F32)

    def slabs(shift):
        for r0 in range(0, TN, PROJ_SLAB):
            lo, hi = r0 + shift, r0 + PROJ_SLAB + shift
            if hi <= TN:
                w = wt_ref[lo:hi, :]
            else:
                w = jnp.concatenate([wt_ref[lo:TN, :], edge_ref[...]], axis=0)
            yield r0, lax.dot_general(xn_ref[...], w.astype(BF16), _NT, preferred_element_type=F32)

    def plain(shift):
        for r0, y in slabs(shift):
            o_ref[:, r0:r0 + PROJ_SLAB] = y.astype(o_ref.dtype)

    def rope(shift, scale):
        cs = cos_ref[...]
        sn = sin_ref[...]
        for r0, y in slabs(shift):
            for c0 in range(0, PROJ_SLAB, DA_QK_DIM):
                x = y[:, c0:c0 + DA_QK_DIM]
                r = x * cs + pltpu.roll(x, DA_QK_DIM // 2, 1) * sn
                if scale is not None:
                    r = r * scale
                o_ref[:, r0 + c0:r0 + c0 + DA_QK_DIM] = r.astype(o_ref.dtype)

    @pl.when(j < first_shifted)
    def _():
        plain(0)

    @pl.when((j >= first_shifted) & (j != rope_q_tile) & (j != rope_k_tile))
    def _():
        plain(n_gate)

    @pl.when(j == rope_q_tile)
    def _():
        rope(n_gate, DA_QK_DIM ** -0.5 * math.log2(math.e))

    @pl.when(j == rope_k_tile)
    def _():
        rope(n_gate, None)


def _in_proj_ab(x2d, g, w_t, cos2, sin2):
    M, D = x2d.shape
    n_gate = 2 * ML_HEADS
    N = w_t.shape[0] - n_gate
    TM, TN = PROJ_TM, PROJ_TN
    assert n_gate == 8 and GATE_START % TN == 0 and N % TN == 0
    assert DA_Q_OFF % TN == 0 and DA_K_OFF - DA_Q_OFF == TN and DA_V_OFF - DA_K_OFF == TN
    fixed = lambda i, j: (0, 0)
    return pl.pallas_call(
        functools.partial(_in_proj_ab_kernel, n_gate),
        grid=(M // TM, N // TN),
        in_specs=[pl.BlockSpec((TM, D), lambda i, j: (i, 0)),
                  pl.BlockSpec((1, D), fixed),
                  pl.BlockSpec((TN, D), lambda i, j: (j, 0)),
                  pl.BlockSpec((n_gate, D), lambda i, j: ((j + 1) * (TN // n_gate), 0)),
                  pl.BlockSpec((n_gate, D), lambda i, j: (GATE_START // n_gate, 0)),
                  pl.BlockSpec((TM, DA_QK_DIM), lambda i, j: (i, 0)),
                  pl.BlockSpec((TM, DA_QK_DIM), lambda i, j: (i, 0))],
        out_specs=[pl.BlockSpec((TM, TN), lambda i, j: (i, j)),
                   pl.BlockSpec((TM, GATE_COLS), lambda i, j: (i, 0))],
        out_shape=[jax.ShapeDtypeStruct((M, N), BF16), jax.ShapeDtypeStruct((M, GATE_COLS), F32)],
        scratch_shapes=[pltpu.VMEM((TM, D), BF16)],
        compiler_params=_params(("arbitrary", "arbitrary")),
        name="in_proj_ab",
    )(x2d, g.reshape(1, D), w_t, w_t, w_t, cos2.reshape(M, DA_QK_DIM), sin2.reshape(M, DA_QK_DIM))


_SGU_STEP_BLOCKS = (2, 3, 0, 4, 1, 5)


def _sgu_step_block(j):
    blk = jnp.int32(_SGU_STEP_BLOCKS[-1])
    for step in range(len(_SGU_STEP_BLOCKS) - 2, -1, -1):
        blk = jnp.where(j == step, _SGU_STEP_BLOCKS[step], blk)
    return blk


def _in_proj_sgu_kernel(x_ref, g_ref, w_ref, lng_ref, lnb_ref, sw_ref, sbt_ref, t_ref, xn_ref, v_ref, u_ref,
                        st_ref):
    j = pl.program_id(1)
    TM = x_ref.shape[0]
    TN = w_ref.shape[1]
    W = v_ref.shape[1]
    L = SGU_CHUNK
    GD = SGU_GROUP_DIM
    groups_per_step = TN // GD

    def proj_cols(cols):
        return jnp.dot(xn_ref[...], w_ref[:, cols].astype(BF16), preferred_element_type=F32)

    def project_v(half):
        for gl in range(groups_per_step):
            cols = slice(gl * GD, (gl + 1) * GD)
            slab = proj_cols(cols)
            v_ref[:, half * TN + gl * GD:half * TN + (gl + 1) * GD] = slab.astype(BF16)
            if half == 0 and gl == 0:
                shift = jnp.mean(slab, axis=1, keepdims=True)
                s1 = jnp.zeros((TM, 1), F32)
                s2 = jnp.zeros((TM, 1), F32)
            elif gl == 0:
                shift, s1, s2 = st_ref[0], st_ref[1], st_ref[2]
            d = slab - shift
            s1 = s1 + jnp.sum(d, axis=1, keepdims=True)
            s2 = s2 + jnp.sum(d * d, axis=1, keepdims=True)
        if half == 0:
            st_ref[0], st_ref[1], st_ref[2] = shift, s1, s2
        else:
            m1 = s1 * (1.0 / W)
            st_ref[0] = shift + m1
            st_ref[1] = lax.rsqrt(jnp.maximum(s2 * (1.0 / W) - m1 * m1, 0.0) + EPS)

    def gate(half):
        causal = _iota((L, L), 1) <= _iota((L, L), 0)
        for gl in range(groups_per_step):
            g = half * groups_per_step + gl
            cols = slice(gl * GD, (gl + 1) * GD)
            vcols = slice(g * GD, (g + 1) * GD)
            z = proj_cols(cols)
            w_causal = jnp.where(causal, sw_ref[g], 0.0).astype(BF16)
            bias = sbt_ref[:, g:g + 1]
            lng = lng_ref[:, vcols]
            lnb = lnb_ref[:, vcols]
            for c in range(TM // L):
                rows = slice(c * L, (c + 1) * L)
                vn = ((v_ref[rows, vcols].astype(F32) - st_ref[0, rows, :]) * st_ref[1, rows, :] * lng
                      + lnb).astype(BF16)
                sv = jnp.dot(w_causal, vn, preferred_element_type=F32) + bias
                t = u_ref[rows, cols].astype(F32) * sv * _silu(z[rows, :])
                t_ref[rows, cols] = t.astype(BF16)

    @pl.when(j == 0)
    def _():
        _normalize_rows(x_ref, g_ref, xn_ref)
        project_v(0)

    @pl.when(j == 1)
    def _():
        project_v(1)

    @pl.when((j == 2) | (j == 4))
    def _():
        for gl in range(groups_per_step):
            cols = slice(gl * GD, (gl + 1) * GD)
            u_ref[:, cols] = proj_cols(cols).astype(BF16)

    @pl.when(j == 3)
    def _():
        gate(0)

    @pl.when(j == 5)
    def _():
        gate(1)


def _in_proj_sgu(x2d, g, w, ln_g, ln_b, sgu_w, sgu_b):
    M, D = x2d.shape
    W = w.shape[1] // 3
    assert W == 2 * PROJ_TN and len(_SGU_STEP_BLOCKS) == 3 * W // PROJ_TN
    fixed = lambda i, j: (0, 0)
    return pl.pallas_call(
        _in_proj_sgu_kernel,
        grid=(M // PROJ_TM, len(_SGU_STEP_BLOCKS)),
        in_specs=[pl.BlockSpec((PROJ_TM, D), lambda i, j: (i, 0)),
                  pl.BlockSpec((1, D), fixed),
                  pl.BlockSpec((D, PROJ_TN), lambda i, j: (0, _sgu_step_block(j))),
                  pl.BlockSpec((1, W), fixed),
                  pl.BlockSpec((1, W), fixed),
                  pl.BlockSpec(sgu_w.shape, lambda i, j: (0, 0, 0)),
                  pl.BlockSpec((SGU_CHUNK, SGU_GROUPS), fixed)],
        out_specs=pl.BlockSpec((PROJ_TM, PROJ_TN), lambda i, j: (i, jnp.where(j <= 3, 0, 1))),
        out_shape=jax.ShapeDtypeStruct((M, W), BF16),
        scratch_shapes=[pltpu.VMEM((PROJ_TM, D), BF16),
                        pltpu.VMEM((PROJ_TM, W), BF16),
                        pltpu.VMEM((PROJ_TM, PROJ_TN), BF16),
                        pltpu.VMEM((3, PROJ_TM, 1), F32)],
        compiler_params=_params(("arbitrary", "arbitrary")),
        name="in_proj_sgu",
    )(x2d, g.reshape(1, D), w, ln_g.reshape(1, W), ln_b.reshape(1, W), sgu_w, sgu_b.T)


def _mlstm_kernel(qp_ref, kp_ref, v_ref, gt_ref, og_ref, z_ref, cwq_ref, cwk_ref, cbq_ref, cbk_ref,
                  gb_ref, ng_ref, out_ref, xs_ref, qc_ref, kc_ref):
    h = pl.program_id(1)
    S = qp_ref.shape[0]
    LB = ML_BLOCK
    nb = S // LB
    DK, DV = ML_QK_DIM, ML_V_DIM

    for src, cw_ref, cb_ref, dst, scale in ((qp_ref, cwq_ref, cbq_ref, qc_ref, 1.0),
                                            (kp_ref, cwk_ref, cbk_ref, kc_ref, DK ** -0.5)):
        xs_ref[0:8, :] = jnp.zeros((8, DK), F32)
        xs_ref[8:8 + S, :] = src[...].astype(F32)
        cw = cw_ref[...]
        cb = cb_ref[...]
        for c in range(S // CONV_ROWS):
            acc = cb
            for j in range(CONV_WIDTH):
                start = 8 + c * CONV_ROWS - (CONV_WIDTH - 1) + j
                acc = acc + cw[j:j + 1, :] * xs_ref[start:start + CONV_ROWS, :]
            dst[c * CONV_ROWS:(c + 1) * CONV_ROWS, :] = (_silu(acc) * scale).astype(BF16)

    gates = gt_ref[...] + gb_ref[...]
    pick = (_iota((8, GATE_COLS), 1) == _iota((8, GATE_COLS), 0)).astype(F32)
    gate_rows = lax.dot_general(pick, gates, (((1,), (1,)), ((), ())), preferred_element_type=F32,
                                precision=HIGHEST)
    sub8 = _iota((8, S), 0)
    i_all = jnp.sum(jnp.where(sub8 == h, gate_rows, 0.0), axis=0, keepdims=True)
    logf_all = _log_sigmoid(jnp.sum(jnp.where(sub8 == h + ML_HEADS, gate_rows, 0.0), axis=0, keepdims=True))

    blk = _iota((nb, LB), 0)

    def to_blocks(row):
        tile = jnp.zeros((nb, LB), F32)
        for c in range(nb):
            tile = jnp.where(blk == c, jnp.broadcast_to(row[:, c * LB:(c + 1) * LB], (nb, LB)), tile)
        return tile

    sub = _iota((LB, LB), 0)
    lane = _iota((LB, LB), 1)
    causal = lane <= sub
    i_b = to_blocks(i_all)
    b_b = jnp.dot(to_blocks(logf_all), (sub <= lane).astype(F32), preferred_element_type=F32,
                  precision=HIGHEST)
    g_tot = b_b[:, LB - 1:LB]
    w_b = g_tot - b_b + i_b
    m_loc = jnp.max(w_b, axis=1, keepdims=True)
    e_b = jnp.exp(w_b - m_loc)
    b_pad = jnp.concatenate([b_b, jnp.zeros((GATE_COLS - nb, LB), F32)], axis=0)
    b_cols = lax.dot_general((sub == lane).astype(F32), b_pad, (((1,), (1,)), ((), ())),
                             preferred_element_type=F32, precision=HIGHEST)

    norm_g = ng_ref[0]
    ct = jnp.zeros((DK, DV), F32)
    n = jnp.zeros((DK, 1), F32)
    m = jnp.zeros((1, 1), F32)
    for c in range(nb):
        rows = slice(c * LB, (c + 1) * LB)
        b_row, i_row, e_row = b_b[c:c + 1, :], i_b[c:c + 1, :], e_b[c:c + 1, :]
        b_col = b_cols[:, c:c + 1]
        q = qc_ref[rows, :]
        v = v_ref[rows, :]
        k_t = kc_ref[rows, :].astype(F32).T

        rhs = jnp.concatenate([k_t.astype(BF16), ct.astype(BF16),
                               jnp.broadcast_to(n, (DK, 128)).astype(BF16)], axis=1)
        big = jnp.dot(q, rhs, preferred_element_type=F32)
        qk, inter, qn = big[:, :LB], big[:, LB:LB + DV], big[:, LB + DV:LB + DV + 1]

        log_d = jnp.where(causal, b_col - b_row + i_row, -jnp.inf)
        inter_log = b_col + m
        m_t = jnp.maximum(inter_log, jnp.max(log_d, axis=1, keepdims=True))
        d_mat = jnp.exp(log_d - m_t)
        inter_w = jnp.exp(inter_log - m_t)
        s_mat = qk * d_mat
        num = jnp.dot(s_mat.astype(BF16), v, preferred_element_type=F32) + inter_w * inter
        den = jnp.sum(s_mat, axis=1, keepdims=True) + inter_w * qn
        hm = num / jnp.maximum(jnp.abs(den), jnp.exp(-m_t))

        hn = hm * lax.rsqrt(jnp.mean(hm * hm, axis=-1, keepdims=True) + EPS) * norm_g
        og = og_ref[rows, :].astype(F32)
        z = z_ref[rows, :].astype(F32)
        out_ref[rows, :] = (_sigmoid(og) * hn * _silu(z)).astype(out_ref.dtype)

        if c + 1 < nb:
            ke = k_t * e_row
            ct_loc = jnp.dot(ke.astype(BF16), v, preferred_element_type=F32)
            n_loc = jnp.sum(ke, axis=1, keepdims=True)
            g_c, ml_c = g_tot[c:c + 1, :], m_loc[c:c + 1, :]
            m_new = jnp.maximum(g_c + m, ml_c)
            a = jnp.exp(g_c + m - m_new)
            cc = jnp.exp(ml_c - m_new)
            ct = a * ct + cc * ct_loc
            n = a * n + cc * n_loc
            m = m_new


def _mlstm(proj, gates, conv_w, conv_b, i_bias, f_bias, norm_g, B, S):
    M = B * S
    H = ML_HEADS
    assert S // ML_BLOCK == 8, "block rows are packed into one 8-sublane tile"
    gate_bias = jnp.concatenate([i_bias, f_bias, jnp.zeros((GATE_COLS - 2 * H,), F32)]).reshape(1, GATE_COLS)
    conv_b2 = conv_b.reshape(1, -1)
    norm_g3 = norm_g.reshape(H, 1, ML_V_DIM)
    qb, kb = ML_Q_OFF // ML_QK_DIM, ML_K_OFF // ML_QK_DIM
    vb, ob, zb = ML_V_OFF // ML_V_DIM, ML_O_OFF // ML_V_DIM, ML_Z_OFF // ML_V_DIM
    return pl.pallas_call(
        _mlstm_kernel,
        grid=(B, H),
        in_specs=[
            pl.BlockSpec((S, ML_QK_DIM), lambda b, h: (b, qb + h)),
            pl.BlockSpec((S, ML_QK_DIM), lambda b, h: (b, kb + h)),
            pl.BlockSpec((S, ML_V_DIM), lambda b, h: (b, vb + h)),
            pl.BlockSpec((S, GATE_COLS), lambda b, h: (b, 0)),
            pl.BlockSpec((S, ML_V_DIM), lambda b, h: (b, ob + h)),
            pl.BlockSpec((S, ML_V_DIM), lambda b, h: (b, zb + h)),
            pl.BlockSpec((CONV_WIDTH, ML_QK_DIM), lambda b, h: (0, h)),
            pl.BlockSpec((CONV_WIDTH, ML_QK_DIM), lambda b, h: (0, H + h)),
            pl.BlockSpec((1, ML_QK_DIM), lambda b, h: (0, h)),
            pl.BlockSpec((1, ML_QK_DIM), lambda b, h: (0, H + h)),
            pl.BlockSpec((1, GATE_COLS), lambda b, h: (0, 0)),
            pl.BlockSpec((1, 1, ML_V_DIM), lambda b, h: (h, 0, 0)),
        ],
        out_specs=pl.BlockSpec((S, ML_V_DIM), lambda b, h: (b, h)),
        out_shape=jax.ShapeDtypeStruct((M, H * ML_V_DIM), BF16),
        scratch_shapes=[pltpu.VMEM((8 + S, ML_QK_DIM), F32),
                        pltpu.VMEM((S, ML_QK_DIM), BF16),
                        pltpu.VMEM((S, ML_QK_DIM), BF16)],
        compiler_params=_params(("arbitrary", "arbitrary")),
        name="mlstm",
    )(proj, proj, proj, gates, proj, proj, conv_w, conv_w, conv_b2, conv_b2, gate_bias, norm_g3)


def _diff_attn_kernel(lam_init, qr_ref, kr_ref, v_ref, z_ref, lq1_ref, lk1_ref, lq2_ref, lk2_ref, ng_ref,
                      o_ref):
    S = kr_ref.shape[0]
    QS, DH = ATT_QSUB, DA_QK_DIM
    diag_mask = _iota((QS, QS), 1) <= _iota((QS, QS), 0)

    def softmax_pv(q0, m, after):
        q = qr_ref[q0:q0 + QS, m * DH:(m + 1) * DH]
        if after is not None:
            q = jnp.where(after > 0.0, q, jnp.zeros_like(q))

        kv_len = q0 + QS
        s = lax.dot_general(q, kr_ref[0:kv_len, m * DH:(m + 1) * DH], _NT, preferred_element_type=F32)
        s_diag = jnp.where(diag_mask, s[:, q0:], -jnp.inf)
        s = s_diag if q0 == 0 else jnp.concatenate([s[:, :q0], s_diag], axis=1)
        p = jnp.exp2(s - jnp.max(s, axis=1, keepdims=True))
        acc = jnp.dot(p.astype(BF16), v_ref[0:kv_len, :], preferred_element_type=F32)
        return acc, jnp.sum(p, axis=1, keepdims=True)

    lam = (jnp.exp(jnp.sum(lq1_ref[...] * lk1_ref[...], axis=1, keepdims=True))
           - jnp.exp(jnp.sum(lq2_ref[...] * lk2_ref[...], axis=1, keepdims=True)) + lam_init)
    norm_g = ng_ref[0] * (1.0 - lam_init)

    done = [(None, None)] * ATT_IN_FLIGHT
    for q0 in range(0, S, QS):
        acc1, l1 = softmax_pv(q0, 0, done[-ATT_IN_FLIGHT][0])
        acc2, l2 = softmax_pv(q0, 1, done[-ATT_IN_FLIGHT][1])
        done.append((l1, l2))
        o = acc1 / l1 - lam * (acc2 / l2)
        hn = o * lax.rsqrt(jnp.mean(o * o, axis=-1, keepdims=True) + EPS) * norm_g
        o_ref[q0:q0 + QS, :] = (hn * _silu(z_ref[q0:q0 + QS, :].astype(F32))).astype(o_ref.dtype)


def _diff_attn(proj, lq1, lk1, lq2, lk2, norm_g, lam_init, B, S):
    M = B * S
    H = DA_HEADS
    qb, kb, vb, zb = (DA_Q_OFF // DA_V_DIM, DA_K_OFF // DA_V_DIM, DA_V_OFF // DA_V_DIM, DA_Z_OFF // DA_V_DIM)
    vec = lambda a: a.reshape(1, DA_QK_DIM)
    vec_spec = pl.BlockSpec((1, DA_QK_DIM), lambda b, h: (0, 0))
    return pl.pallas_call(
        functools.partial(_diff_attn_kernel, lam_init),
        grid=(B, H),
        in_specs=[
            pl.BlockSpec((S, DA_V_DIM), lambda b, h: (b, qb + h)),
            pl.BlockSpec((S, DA_V_DIM), lambda b, h: (b, kb + h)),
            pl.BlockSpec((S, DA_V_DIM), lambda b, h: (b, vb + h)),
            pl.BlockSpec((S, DA_V_DIM), lambda b, h: (b, zb + h)),
            vec_spec, vec_spec, vec_spec, vec_spec,
            pl.BlockSpec((1, 1, DA_V_DIM), lambda b, h: (h, 0, 0)),
        ],
        out_specs=pl.BlockSpec((S, DA_V_DIM), lambda b, h: (b, h)),
        out_shape=jax.ShapeDtypeStruct((M, H * DA_V_DIM), BF16),
        compiler_params=_params(("arbitrary", "arbitrary")),
        name="diff_attn",
    )(proj, proj, proj, proj, vec(lq1), vec(lk1), vec(lq2), vec(lk2), norm_g.reshape(H, 1, DA_V_DIM))


def _ple_tail(x1_ref, ssq, p_ref, png_ref, gw_ref, pw_ref, xn_ref, emit):
    D = x1_ref.shape[1]
    inv = lax.rsqrt(ssq * (1.0 / D) + EPS)
    for n0 in range(0, D, TAIL_NC):
        cols = slice(n0, n0 + TAIL_NC)
        xn_ref[:, cols] = (x1_ref[:, cols] * inv * png_ref[:, cols]).astype(BF16)
    p_bf = p_ref[...].astype(BF16)
    for n0 in range(0, D, TAIL_NC):
        cols = slice(n0, n0 + TAIL_NC)
        gate = _sigmoid(jnp.dot(xn_ref[...], gw_ref[:, cols].astype(BF16), preferred_element_type=F32))
        pp = jnp.dot(p_bf, pw_ref[:, cols].astype(BF16), preferred_element_type=F32)
        emit(n0, x1_ref[:, cols] + gate * pp)


def _out_ple_kernel(n_lhs, has_final_norm, *refs):
    lhs_refs = refs[:n_lhs]
    x_ref, p_ref, wo_ref, png_ref, gw_ref, pw_ref = refs[n_lhs:n_lhs + 6]
    if has_final_norm:
        fng_ref, o_ref, x1_ref, xn_ref = refs[n_lhs + 6:]
    else:
        o_ref, x1_ref, xn_ref = refs[n_lhs + 6:]
    TM, D = x_ref.shape

    ssq = jnp.zeros((TM, 1), F32)
    for n0 in range(0, D, TAIL_NC):
        cols = slice(n0, n0 + TAIL_NC)
        y = x_ref[:, cols]
        k0 = 0
        for lhs_ref in lhs_refs:
            k1 = k0 + lhs_ref.shape[1]
            y = y + jnp.dot(lhs_ref[...], wo_ref[k0:k1, cols].astype(BF16), preferred_element_type=F32)
            k0 = k1
        x1_ref[:, cols] = y
        ssq = ssq + jnp.sum(y * y, axis=1, keepdims=True)

    ssq_out = [jnp.zeros((TM, 1), F32)]

    def emit(n0, val):
        o_ref[:, n0:n0 + TAIL_NC] = val
        if has_final_norm:
            ssq_out[0] = ssq_out[0] + jnp.sum(val * val, axis=1, keepdims=True)

    _ple_tail(x1_ref, ssq, p_ref, png_ref, gw_ref, pw_ref, xn_ref, emit)

    if has_final_norm:
        inv = lax.rsqrt(ssq_out[0] * (1.0 / D) + EPS)
        for n0 in range(0, D, TAIL_NC):
            cols = slice(n0, n0 + TAIL_NC)
            o_ref[:, cols] = o_ref[:, cols] * inv * fng_ref[:, cols]


def _out_ple(lhs_list, x2d, p3d, layer, w_out, ple_g, gate_w3d, proj_w3d, final_g=None, name="out_ple"):
    M, D = x2d.shape
    P = p3d.shape[2]
    TM = TAIL_TM
    row = lambda i: (i, 0)
    fixed = lambda i: (0, 0)
    layer_block = lambda i: (layer, 0, 0)
    in_specs = [pl.BlockSpec((TM, lhs.shape[1]), row) for lhs in lhs_list]
    in_specs += [pl.BlockSpec((TM, D), row),
                 pl.BlockSpec((None, TM, P), lambda i: (layer, i, 0)),
                 _resident(w_out.shape, fixed),
                 _resident((1, D), fixed),
                 pl.BlockSpec((None, D, D), layer_block, pipeline_mode=pl.Buffered(1)),
                 pl.BlockSpec((None, P, D), layer_block, pipeline_mode=pl.Buffered(1))]
    args = [*lhs_list, x2d, p3d, w_out, ple_g.reshape(1, D), gate_w3d, proj_w3d]
    if final_g is not None:
        in_specs.append(_resident((1, D), fixed))
        args.append(final_g.reshape(1, D))
    return pl.pallas_call(
        functools.partial(_out_ple_kernel, len(lhs_list), final_g is not None),
        grid=(M // TM,),
        in_specs=in_specs,
        out_specs=pl.BlockSpec((TM, D), row),
        out_shape=jax.ShapeDtypeStruct((M, D), F32),
        scratch_shapes=[pltpu.VMEM((TM, D), F32), pltpu.VMEM((TM, D), BF16)],
        compiler_params=_params(("arbitrary",)),
        name=name,
    )(*args)


def kernel(x, p, positions, ab_norm_g, ab_w_in, ab_conv_w, ab_conv_b, ab_i_bias, ab_f_bias, ab_ml_norm_g,
           ab_lam_q1, ab_lam_k1, ab_lam_q2, ab_lam_k2, ab_da_norm_g, ab_w_out, c_norm_g, c_w_in, c_ln_g,
           c_ln_b, c_sgu_w, c_sgu_b, c_w_out, ple_norm_g, ple_gate_w, ple_proj_w, final_norm_g):
    B, S, D = x.shape
    M = B * S
    assert ab_norm_g.shape[0] == 1 and c_norm_g.shape[0] == 1 and p.shape[0] == 2
    x2d = x.reshape(M, D)
    p3d = p.reshape(2, M, p.shape[-1])

    cos2, sin2 = _rope_tables(positions)
    proj, gates = _in_proj_ab(x2d, ab_norm_g[0], ab_w_in[0].T, cos2, sin2)

    a_out = _mlstm(proj, gates, ab_conv_w[0], ab_conv_b[0], ab_i_bias[0], ab_f_bias[0], ab_ml_norm_g[0], B, S)

    lam_init = 0.8 - 0.6 * math.exp(-0.3 * 0)
    b_out = _diff_attn(proj, ab_lam_q1[0], ab_lam_k1[0], ab_lam_q2[0], ab_lam_k2[0], ab_da_norm_g[0],
                       lam_init, B, S)

    x2 = _out_ple([a_out, b_out], x2d, p3d, 0, ab_w_out[0], ple_norm_g[0], ple_gate_w, ple_proj_w,
                  name="out_ple_ab")

    t = _in_proj_sgu(x2, c_norm_g[0], c_w_in[0], c_ln_g[0], c_ln_b[0], c_sgu_w[0], c_sgu_b[0])
    out = _out_ple([t], x2, p3d, 1, c_w_out[0], ple_norm_g[1], ple_gate_w, ple_proj_w,
                   final_g=final_norm_g, name="out_ple_c")
    return out.reshape(B, S, D)
```

```python
import functools
import math

import jax
import jax.numpy as jnp
from jax import lax
from jax.experimental import pallas as pl
from jax.experimental.pallas import tpu as pltpu

F32 = jnp.float32
BF16 = jnp.bfloat16
HIGHEST = lax.Precision.HIGHEST

EPS = 1e-6
ROPE_THETA = 10000.0
CONV_WIDTH = 4

ML_HEADS = 4
ML_QK_DIM = 128
ML_V_DIM = 256
DA_HEADS = 4
DA_QK_DIM = 128
DA_V_DIM = 256
SGU_CHUNK = 128
SGU_GROUPS = 8
SGU_GROUP_DIM = 256

ML_Q_OFF, ML_K_OFF, ML_V_OFF, ML_O_OFF, ML_Z_OFF = 0, 512, 1024, 2048, 3072
DA_Q_OFF, DA_K_OFF, DA_V_OFF, DA_Z_OFF = 4096, 5120, 6144, 7168
GATE_COLS = 128
GATE_START = ML_V_OFF + ML_HEADS * ML_V_DIM

VMEM_LIMIT_BYTES = 56 * 1024 * 1024

PROJ_TM = 1024
PROJ_TN = 1024
NORM_ROWS = 128
TAIL_TM = 256
TAIL_NC = 512
PROJ_SLAB = 256
CONV_ROWS = 128
ML_BLOCK = 256
ATT_QSUB = 256


def _params(semantics):
    return pltpu.CompilerParams(dimension_semantics=semantics, vmem_limit_bytes=VMEM_LIMIT_BYTES)


def _resident(shape, index_map):
    return pl.BlockSpec(shape, index_map, pipeline_mode=pl.Buffered(1))


def _sigmoid(x):
    return 1.0 / (1.0 + jnp.exp(-x))


def _silu(x):
    return x * _sigmoid(x)


def _log_sigmoid(x):
    return jnp.minimum(x, 0.0) - jnp.log(1.0 + jnp.exp(-jnp.abs(x)))


def _iota(shape, axis):
    return lax.broadcasted_iota(jnp.int32, shape, axis)


def _rope_table_kernel(pos_ref, cos_ref, sin_ref):
    pos = pos_ref[0]
    half_rows = pos.shape[0]
    lane = _iota(pos.shape, 1)
    lo = lane < 64
    j = jnp.where(lo, lane, lane - 64).astype(F32)
    freq = jnp.exp(j * (-math.log(ROPE_THETA) / 64.0))
    ang = pos * freq
    c = jnp.cos(ang)
    s = jnp.sin(ang)
    cr = pltpu.roll(c, 64, 1)
    sr = pltpu.roll(s, 64, 1)
    cos_ref[0, 0:half_rows, :] = jnp.where(lo, c, cr)
    cos_ref[0, half_rows:2 * half_rows, :] = jnp.where(lo, cr, c)
    sin_ref[0, 0:half_rows, :] = jnp.where(lo, -s, sr)
    sin_ref[0, half_rows:2 * half_rows, :] = jnp.where(lo, -sr, s)


def _rope_tables(positions):
    B, S = positions.shape
    posf = positions.astype(F32)
    lo = jnp.broadcast_to(posf[:, :S // 2, None], (B, S // 2, 64))
    hi = jnp.broadcast_to(posf[:, S // 2:, None], (B, S // 2, 64))
    pos2 = jnp.concatenate([lo, hi], axis=-1)
    return pl.pallas_call(
        _rope_table_kernel,
        grid=(B,),
        in_specs=[pl.BlockSpec((1, S // 2, 128), lambda b: (b, 0, 0))],
        out_specs=[pl.BlockSpec((1, S, 128), lambda b: (b, 0, 0)),
                   pl.BlockSpec((1, S, 128), lambda b: (b, 0, 0))],
        out_shape=[jax.ShapeDtypeStruct((B, S, 128), F32)] * 2,
        compiler_params=_params(("arbitrary",)),
        name="rope_tables",
    )(pos2)


def _normalize_rows(x_ref, g_ref, xn_ref):
    g = g_ref[...]
    for r0 in range(0, x_ref.shape[0], NORM_ROWS):
        rows = slice(r0, r0 + NORM_ROWS)
        x = x_ref[rows, :]
        ms = jnp.mean(x * x, axis=-1, keepdims=True)
        xn_ref[rows, :] = (x * lax.rsqrt(ms + EPS) * g).astype(BF16)


_NT = (((1,), (1,)), ((), ()))


def _in_proj_ab_kernel(n_gate, x_ref, g_ref, wt_ref, edge_ref, wgt_ref, cos_ref, sin_ref, o_ref, og_ref,
                       xn_ref):
    j = pl.program_id(1)
    TN, D = wt_ref.shape
    rope_q_tile, rope_k_tile = DA_Q_OFF // TN, DA_K_OFF // TN
    first_shifted = GATE_START // TN

    @pl.when(j == 0)
    def _():
        _normalize_rows(x_ref, g_ref, xn_ref)
        wg = jnp.concatenate([wgt_ref[...], jnp.zeros((GATE_COLS - n_gate, D), F32)], axis=0).astype(BF16)
        og_ref[...] = lax.dot_general(xn_ref[...], wg, _NT, preferred_element_type=F32)

    def slabs(shift):
        for r0 in range(0, TN, PROJ_SLAB):
            lo, hi = r0 + shift, r0 + PROJ_SLAB + shift
            if hi <= TN:
                w = wt_ref[lo:hi, :]
            else:
                w = jnp.concatenate([wt_ref[lo:TN, :], edge_ref[...]], axis=0)
            yield r0, lax.dot_general(xn_ref[...], w.astype(BF16), _NT, preferred_element_type=F32)

    def plain(shift):
        for r0, y in slabs(shift):
            o_ref[:, r0:r0 + PROJ_SLAB] = y.astype(o_ref.dtype)

    def rope(shift, scale):
        cs = cos_ref[...]
        sn = sin_ref[...]
        for r0, y in slabs(shift):
            for c0 in range(0, PROJ_SLAB, DA_QK_DIM):
                x = y[:, c0:c0 + DA_QK_DIM]
                r = x * cs + pltpu.roll(x, DA_QK_DIM // 2, 1) * sn
                if scale is not None:
                    r = r * scale
                o_ref[:, r0 + c0:r0 + c0 + DA_QK_DIM] = r.astype(o_ref.dtype)

    @pl.when(j < first_shifted)
    def _():
        plain(0)

    @pl.when((j >= first_shifted) & (j != rope_q_tile) & (j != rope_k_tile))
    def _():
        plain(n_gate)

    @pl.when(j == rope_q_tile)
    def _():
        rope(n_gate, DA_QK_DIM ** -0.5 * math.log2(math.e))

    @pl.when(j == rope_k_tile)
    def _():
        rope(n_gate, None)


def _in_proj_ab(x2d, g, w_t, cos2, sin2):
    M, D = x2d.shape
    n_gate = 2 * ML_HEADS
    N = w_t.shape[0] - n_gate
    TM, TN = PROJ_TM, PROJ_TN
    assert n_gate == 8 and GATE_START % TN == 0 and N % TN == 0
    assert DA_Q_OFF % TN == 0 and DA_K_OFF - DA_Q_OFF == TN and DA_V_OFF - DA_K_OFF == TN
    fixed = lambda i, j: (0, 0)
    return pl.pallas_call(
        functools.partial(_in_proj_ab_kernel, n_gate),
        grid=(M // TM, N // TN),
        in_specs=[pl.BlockSpec((TM, D), lambda i, j: (i, 0)),
                  pl.BlockSpec((1, D), fixed),
                  pl.BlockSpec((TN, D), lambda i, j: (j, 0)),
                  pl.BlockSpec((n_gate, D), lambda i, j: ((j + 1) * (TN // n_gate), 0)),
                  pl.BlockSpec((n_gate, D), lambda i, j: (GATE_START // n_gate, 0)),
                  pl.BlockSpec((TM, DA_QK_DIM), lambda i, j: (i, 0)),
                  pl.BlockSpec((TM, DA_QK_DIM), lambda i, j: (i, 0))],
        out_specs=[pl.BlockSpec((TM, TN), lambda i, j: (i, j)),
                   pl.BlockSpec((TM, GATE_COLS), lambda i, j: (i, 0))],
        out_shape=[jax.ShapeDtypeStruct((M, N), BF16), jax.ShapeDtypeStruct((M, GATE_COLS), F32)],
        scratch_shapes=[pltpu.VMEM((TM, D), BF16)],
        compiler_params=_params(("arbitrary", "arbitrary")),
        name="in_proj_ab",
    )(x2d, g.reshape(1, D), w_t, w_t, w_t, cos2.reshape(M, DA_QK_DIM), sin2.reshape(M, DA_QK_DIM))


_SGU_STEP_BLOCKS = (2, 3, 0, 4, 1, 5)


def _sgu_step_block(j):
    blk = jnp.int32(_SGU_STEP_BLOCKS[-1])
    for step in range(len(_SGU_STEP_BLOCKS) - 2, -1, -1):
        blk = jnp.where(j == step, _SGU_STEP_BLOCKS[step], blk)
    return blk


def _in_proj_sgu_kernel(x_ref, g_ref, w_ref, lng_ref, lnb_ref, sw_ref, sbt_ref, t_ref, xn_ref, v_ref, u_ref,
                        st_ref):
    j = pl.program_id(1)
    TM = x_ref.shape[0]
    TN = w_ref.shape[1]
    W = v_ref.shape[1]
    L = SGU_CHUNK
    GD = SGU_GROUP_DIM
    groups_per_step = TN // GD

    def proj_cols(cols):
        return jnp.dot(xn_ref[...], w_ref[:, cols].astype(BF16), preferred_element_type=F32)

    def project_v(half):
        for gl in range(groups_per_step):
            cols = slice(gl * GD, (gl + 1) * GD)
            slab = proj_cols(cols)
            v_ref[:, half * TN + gl * GD:half * TN + (gl + 1) * GD] = slab.astype(BF16)
            if half == 0 and gl == 0:
                shift = jnp.mean(slab, axis=1, keepdims=True)
                s1 = jnp.zeros((TM, 1), F32)
                s2 = jnp.zeros((TM, 1), F32)
            elif gl == 0:
                shift, s1, s2 = st_ref[0], st_ref[1], st_ref[2]
            d = slab - shift
            s1 = s1 + jnp.sum(d, axis=1, keepdims=True)
            s2 = s2 + jnp.sum(d * d, axis=1, keepdims=True)
        if half == 0:
            st_ref[0], st_ref[1], st_ref[2] = shift, s1, s2
        else:
            m1 = s1 * (1.0 / W)
            st_ref[0] = shift + m1
            st_ref[1] = lax.rsqrt(jnp.maximum(s2 * (1.0 / W) - m1 * m1, 0.0) + EPS)

    def gate(half):
        causal = _iota((L, L), 1) <= _iota((L, L), 0)
        for gl in range(groups_per_step):
            g = half * groups_per_step + gl
            cols = slice(gl * GD, (gl + 1) * GD)
            vcols = slice(g * GD, (g + 1) * GD)
            z = proj_cols(cols)
            w_causal = jnp.where(causal, sw_ref[g], 0.0).astype(BF16)
            bias = sbt_ref[:, g:g + 1]
            lng = lng_ref[:, vcols]
            lnb = lnb_ref[:, vcols]
            for c in range(TM // L):
                rows = slice(c * L, (c + 1) * L)
                vn = ((v_ref[rows, vcols].astype(F32) - st_ref[0, rows, :]) * st_ref[1, rows, :] * lng
                      + lnb).astype(BF16)
                sv = jnp.dot(w_causal, vn, preferred_element_type=F32) + bias
                t = u_ref[rows, cols].astype(F32) * sv * _silu(z[rows, :])
                t_ref[rows, cols] = t.astype(BF16)

    @pl.when(j == 0)
    def _():
        _normalize_rows(x_ref, g_ref, xn_ref)
        project_v(0)

    @pl.when(j == 1)
    def _():
        project_v(1)

    @pl.when((j == 2) | (j == 4))
    def _():
        for gl in range(groups_per_step):
            cols = slice(gl * GD, (gl + 1) * GD)
            u_ref[:, cols] = proj_cols(cols).astype(BF16)

    @pl.when(j == 3)
    def _():
        gate(0)

    @pl.when(j == 5)
    def _():
        gate(1)


def _in_proj_sgu(x2d, g, w, ln_g, ln_b, sgu_w, sgu_b):
    M, D = x2d.shape
    W = w.shape[1] // 3
    assert W == 2 * PROJ_TN and len(_SGU_STEP_BLOCKS) == 3 * W // PROJ_TN
    fixed = lambda i, j: (0, 0)
    return pl.pallas_call(
        _in_proj_sgu_kernel,
        grid=(M // PROJ_TM, len(_SGU_STEP_BLOCKS)),
        in_specs=[pl.BlockSpec((PROJ_TM, D), lambda i, j: (i, 0)),
                  pl.BlockSpec((1, D), fixed),
                  pl.BlockSpec((D, PROJ_TN), lambda i, j: (0, _sgu_step_block(j))),
                  pl.BlockSpec((1, W), fixed),
                  pl.BlockSpec((1, W), fixed),
                  pl.BlockSpec(sgu_w.shape, lambda i, j: (0, 0, 0)),
                  pl.BlockSpec((SGU_CHUNK, SGU_GROUPS), fixed)],
        out_specs=pl.BlockSpec((PROJ_TM, PROJ_TN), lambda i, j: (i, jnp.where(j <= 3, 0, 1))),
        out_shape=jax.ShapeDtypeStruct((M, W), BF16),
        scratch_shapes=[pltpu.VMEM((PROJ_TM, D), BF16),
                        pltpu.VMEM((PROJ_TM, W), BF16),
                        pltpu.VMEM((PROJ_TM, PROJ_TN), BF16),
                        pltpu.VMEM((3, PROJ_TM, 1), F32)],
        compiler_params=_params(("arbitrary", "arbitrary")),
        name="in_proj_sgu",
    )(x2d, g.reshape(1, D), w, ln_g.reshape(1, W), ln_b.reshape(1, W), sgu_w, sgu_b.T)


def _mlstm_kernel(qp_ref, kp_ref, v_ref, gt_ref, og_ref, z_ref, cwq_ref, cwk_ref, cbq_ref, cbk_ref,
                  gb_ref, ng_ref, out_ref, xs_ref, qc_ref, kc_ref):
    h = pl.program_id(1)
    S = qp_ref.shape[0]
    LB = ML_BLOCK
    nb = S // LB
    DK, DV = ML_QK_DIM, ML_V_DIM

    for src, cw_ref, cb_ref, dst, scale in ((qp_ref, cwq_ref, cbq_ref, qc_ref, 1.0),
                                            (kp_ref, cwk_ref, cbk_ref, kc_ref, DK ** -0.5)):
        xs_ref[0:8, :] = jnp.zeros((8, DK), F32)
        xs_ref[8:8 + S, :] = src[...].astype(F32)
        cw = cw_ref[...]
        cb = cb_ref[...]
        for c in range(S // CONV_ROWS):
            acc = cb
            for j in range(CONV_WIDTH):
                start = 8 + c * CONV_ROWS - (CONV_WIDTH - 1) + j
                acc = acc + cw[j:j + 1, :] * xs_ref[start:start + CONV_ROWS, :]
            dst[c * CONV_ROWS:(c + 1) * CONV_ROWS, :] = (_silu(acc) * scale).astype(BF16)

    gates = gt_ref[...] + gb_ref[...]
    pick = (_iota((8, GATE_COLS), 1) == _iota((8, GATE_COLS), 0)).astype(F32)
    gate_rows = lax.dot_general(pick, gates, (((1,), (1,)), ((), ())), preferred_element_type=F32,
                                precision=HIGHEST)
    sub8 = _iota((8, S), 0)
    i_all = jnp.sum(jnp.where(sub8 == h, gate_rows, 0.0), axis=0, keepdims=True)
    logf_all = _log_sigmoid(jnp.sum(jnp.where(sub8 == h + ML_HEADS, gate_rows, 0.0), axis=0, keepdims=True))

    blk = _iota((nb, LB), 0)

    def to_blocks(row):
        tile = jnp.zeros((nb, LB), F32)
        for c in range(nb):
            tile = jnp.where(blk == c, jnp.broadcast_to(row[:, c * LB:(c + 1) * LB], (nb, LB)), tile)
        return tile

    sub = _iota((LB, LB), 0)
    lane = _iota((LB, LB), 1)
    causal = lane <= sub
    i_b = to_blocks(i_all)
    b_b = jnp.dot(to_blocks(logf_all), (sub <= lane).astype(F32), preferred_element_type=F32,
                  precision=HIGHEST)
    g_tot = b_b[:, LB - 1:LB]
    w_b = g_tot - b_b + i_b
    m_loc = jnp.max(w_b, axis=1, keepdims=True)
    e_b = jnp.exp(w_b - m_loc)
    b_pad = jnp.concatenate([b_b, jnp.zeros((GATE_COLS - nb, LB), F32)], axis=0)
    b_cols = lax.dot_general((sub == lane).astype(F32), b_pad, (((1,), (1,)), ((), ())),
                             preferred_element_type=F32, precision=HIGHEST)

    norm_g = ng_ref[0]
    ct = jnp.zeros((DK, DV), F32)
    n = jnp.zeros((DK, 1), F32)
    m = jnp.zeros((1, 1), F32)
    for c in range(nb):
        rows = slice(c * LB, (c + 1) * LB)
        b_row, i_row, e_row = b_b[c:c + 1, :], i_b[c:c + 1, :], e_b[c:c + 1, :]
        b_col = b_cols[:, c:c + 1]
        q = qc_ref[rows, :]
        v = v_ref[rows, :]
        k_t = kc_ref[rows, :].astype(F32).T

        rhs = jnp.concatenate([k_t.astype(BF16), ct.astype(BF16),
                               jnp.broadcast_to(n, (DK, 128)).astype(BF16)], axis=1)
        big = jnp.dot(q, rhs, preferred_element_type=F32)
        qk, inter, qn = big[:, :LB], big[:, LB:LB + DV], big[:, LB + DV:LB + DV + 1]

        log_d = jnp.where(causal, b_col - b_row + i_row, -jnp.inf)
        inter_log = b_col + m
        m_t = jnp.maximum(inter_log, jnp.max(log_d, axis=1, keepdims=True))
        d_mat = jnp.exp(log_d - m_t)
        inter_w = jnp.exp(inter_log - m_t)
        s_mat = qk * d_mat
        num = jnp.dot(s_mat.astype(BF16), v, preferred_element_type=F32) + inter_w * inter
        den = jnp.sum(s_mat, axis=1, keepdims=True) + inter_w * qn
        hm = num / jnp.maximum(jnp.abs(den), jnp.exp(-m_t))

        hn = hm * lax.rsqrt(jnp.mean(hm * hm, axis=-1, keepdims=True) + EPS) * norm_g
        og = og_ref[rows, :].astype(F32)
        z = z_ref[rows, :].astype(F32)
        out_ref[rows, :] = (_sigmoid(og) * hn * _silu(z)).astype(out_ref.dtype)

        if c + 1 < nb:
            ke = k_t * e_row
            ct_loc = jnp.dot(ke.astype(BF16), v, preferred_element_type=F32)
            n_loc = jnp.sum(ke, axis=1, keepdims=True)
            g_c, ml_c = g_tot[c:c + 1, :], m_loc[c:c + 1, :]
            m_new = jnp.maximum(g_c + m, ml_c)
            a = jnp.exp(g_c + m - m_new)
            cc = jnp.exp(ml_c - m_new)
            ct = a * ct + cc * ct_loc
            n = a * n + cc * n_loc
            m = m_new


def _mlstm(proj, gates, conv_w, conv_b, i_bias, f_bias, norm_g, B, S):
    M = B * S
    H = ML_HEADS
    assert S // ML_BLOCK == 8, "block rows are packed into one 8-sublane tile"
    gate_bias = jnp.concatenate([i_bias, f_bias, jnp.zeros((GATE_COLS - 2 * H,), F32)]).reshape(1, GATE_COLS)
    conv_b2 = conv_b.reshape(1, -1)
    norm_g3 = norm_g.reshape(H, 1, ML_V_DIM)
    qb, kb = ML_Q_OFF // ML_QK_DIM, ML_K_OFF // ML_QK_DIM
    vb, ob, zb = ML_V_OFF // ML_V_DIM, ML_O_OFF // ML_V_DIM, ML_Z_OFF // ML_V_DIM
    return pl.pallas_call(
        _mlstm_kernel,
        grid=(B, H),
        in_specs=[
            pl.BlockSpec((S, ML_QK_DIM), lambda b, h: (b, qb + h)),
            pl.BlockSpec((S, ML_QK_DIM), lambda b, h: (b, kb + h)),
            pl.BlockSpec((S, ML_V_DIM), lambda b, h: (b, vb + h)),
            pl.BlockSpec((S, GATE_COLS), lambda b, h: (b, 0)),
            pl.BlockSpec((S, ML_V_DIM), lambda b, h: (b, ob + h)),
            pl.BlockSpec((S, ML_V_DIM), lambda b, h: (b, zb + h)),
            pl.BlockSpec((CONV_WIDTH, ML_QK_DIM), lambda b, h: (0, h)),
            pl.BlockSpec((CONV_WIDTH, ML_QK_DIM), lambda b, h: (0, H + h)),
            pl.BlockSpec((1, ML_QK_DIM), lambda b, h: (0, h)),
            pl.BlockSpec((1, ML_QK_DIM), lambda b, h: (0, H + h)),
            pl.BlockSpec((1, GATE_COLS), lambda b, h: (0, 0)),
            pl.BlockSpec((1, 1, ML_V_DIM), lambda b, h: (h, 0, 0)),
        ],
        out_specs=pl.BlockSpec((S, ML_V_DIM), lambda b, h: (b, h)),
        out_shape=jax.ShapeDtypeStruct((M, H * ML_V_DIM), BF16),
        scratch_shapes=[pltpu.VMEM((8 + S, ML_QK_DIM), F32),
                        pltpu.VMEM((S, ML_QK_DIM), BF16),
                        pltpu.VMEM((S, ML_QK_DIM), BF16)],
        compiler_params=_params(("arbitrary", "arbitrary")),
        name="mlstm",
    )(proj, proj, proj, gates, proj, proj, conv_w, conv_w, conv_b2, conv_b2, gate_bias, norm_g3)


def _diff_attn_kernel(lam_init, qr_ref, kr_ref, v_ref, z_ref, lq1_ref, lk1_ref, lq2_ref, lk2_ref, ng_ref,
                      o_ref):
    S = kr_ref.shape[0]
    QS, DH = ATT_QSUB, DA_QK_DIM
    diag_mask = _iota((QS, QS), 1) <= _iota((QS, QS), 0)

    def softmax_pv(q0, m):
        q = qr_ref[q0:q0 + QS, m * DH:(m + 1) * DH]
        kv_len = q0 + QS
        s = lax.dot_general(q, kr_ref[0:kv_len, m * DH:(m + 1) * DH], _NT, preferred_element_type=F32)
        s_diag = jnp.where(diag_mask, s[:, q0:], -jnp.inf)
        s = s_diag if q0 == 0 else jnp.concatenate([s[:, :q0], s_diag], axis=1)
        p = jnp.exp2(s - jnp.max(s, axis=1, keepdims=True))
        acc = jnp.dot(p.astype(BF16), v_ref[0:kv_len, :], preferred_element_type=F32)
        return acc, jnp.sum(p, axis=1, keepdims=True)

    lam = (jnp.exp(jnp.sum(lq1_ref[...] * lk1_ref[...], axis=1, keepdims=True))
           - jnp.exp(jnp.sum(lq2_ref[...] * lk2_ref[...], axis=1, keepdims=True)) + lam_init)
    norm_g = ng_ref[0] * (1.0 - lam_init)

    for q0 in range(0, S, QS):
        acc1, l1 = softmax_pv(q0, 0)
        acc2, l2 = softmax_pv(q0, 1)
        o = acc1 / l1 - lam * (acc2 / l2)
        hn = o * lax.rsqrt(jnp.mean(o * o, axis=-1, keepdims=True) + EPS) * norm_g
        o_ref[q0:q0 + QS, :] = (hn * _silu(z_ref[q0:q0 + QS, :].astype(F32))).astype(o_ref.dtype)


def _diff_attn(proj, lq1, lk1, lq2, lk2, norm_g, lam_init, B, S):
    M = B * S
    H = DA_HEADS
    qb, kb, vb, zb = (DA_Q_OFF // DA_V_DIM, DA_K_OFF // DA_V_DIM, DA_V_OFF // DA_V_DIM, DA_Z_OFF // DA_V_DIM)
    vec = lambda a: a.reshape(1, DA_QK_DIM)
    vec_spec = pl.BlockSpec((1, DA_QK_DIM), lambda b, h: (0, 0))
    return pl.pallas_call(
        functools.partial(_diff_attn_kernel, lam_init),
        grid=(B, H),
        in_specs=[
            pl.BlockSpec((S, DA_V_DIM), lambda b, h: (b, qb + h)),
            pl.BlockSpec((S, DA_V_DIM), lambda b, h: (b, kb + h)),
            pl.BlockSpec((S, DA_V_DIM), lambda b, h: (b, vb + h)),
            pl.BlockSpec((S, DA_V_DIM), lambda b, h: (b, zb + h)),
            vec_spec, vec_spec, vec_spec, vec_spec,
            pl.BlockSpec((1, 1, DA_V_DIM), lambda b, h: (h, 0, 0)),
        ],
        out_specs=pl.BlockSpec((S, DA_V_DIM), lambda b, h: (b, h)),
        out_shape=jax.ShapeDtypeStruct((M, H * DA_V_DIM), BF16),
        compiler_params=_params(("arbitrary", "arbitrary")),
        name="diff_attn",
    )(proj, proj, proj, proj, vec(lq1), vec(lk1), vec(lq2), vec(lk2), norm_g.reshape(H, 1, DA_V_DIM))


def _ple_tail(x1_ref, ssq, p_ref, png_ref, gw_ref, pw_ref, xn_ref, emit):
    D = x1_ref.shape[1]
    inv = lax.rsqrt(ssq * (1.0 / D) + EPS)
    for n0 in range(0, D, TAIL_NC):
        cols = slice(n0, n0 + TAIL_NC)
        xn_ref[:, cols] = (x1_ref[:, cols] * inv * png_ref[:, cols]).astype(BF16)
    p_bf = p_ref[...].astype(BF16)
    for n0 in range(0, D, TAIL_NC):
        cols = slice(n0, n0 + TAIL_NC)
        gate = _sigmoid(jnp.dot(xn_ref[...], gw_ref[:, cols].astype(BF16), preferred_element_type=F32))
        pp = jnp.dot(p_bf, pw_ref[:, cols].astype(BF16), preferred_element_type=F32)
        emit(n0, x1_ref[:, cols] + gate * pp)


def _out_ple_kernel(n_lhs, has_final_norm, *refs):
    lhs_refs = refs[:n_lhs]
    x_ref, p_ref, wo_ref, png_ref, gw_ref, pw_ref = refs[n_lhs:n_lhs + 6]
    if has_final_norm:
        fng_ref, o_ref, x1_ref, xn_ref = refs[n_lhs + 6:]
    else:
        o_ref, x1_ref, xn_ref = refs[n_lhs + 6:]
    TM, D = x_ref.shape

    ssq = jnp.zeros((TM, 1), F32)
    for n0 in range(0, D, TAIL_NC):
        cols = slice(n0, n0 + TAIL_NC)
        y = x_ref[:, cols]
        k0 = 0
        for lhs_ref in lhs_refs:
            k1 = k0 + lhs_ref.shape[1]
            y = y + jnp.dot(lhs_ref[...], wo_ref[k0:k1, cols].astype(BF16), preferred_element_type=F32)
            k0 = k1
        x1_ref[:, cols] = y
        ssq = ssq + jnp.sum(y * y, axis=1, keepdims=True)

    ssq_out = [jnp.zeros((TM, 1), F32)]

    def emit(n0, val):
        o_ref[:, n0:n0 + TAIL_NC] = val
        if has_final_norm:
            ssq_out[0] = ssq_out[0] + jnp.sum(val * val, axis=1, keepdims=True)

    _ple_tail(x1_ref, ssq, p_ref, png_ref, gw_ref, pw_ref, xn_ref, emit)

    if has_final_norm:
        inv = lax.rsqrt(ssq_out[0] * (1.0 / D) + EPS)
        for n0 in range(0, D, TAIL_NC):
            cols = slice(n0, n0 + TAIL_NC)
            o_ref[:, cols] = o_ref[:, cols] * inv * fng_ref[:, cols]


def _out_ple(lhs_list, x2d, p3d, layer, w_out, ple_g, gate_w3d, proj_w3d, final_g=None, name="out_ple"):
    M, D = x2d.shape
    P = p3d.shape[2]
    TM = TAIL_TM
    row = lambda i: (i, 0)
    fixed = lambda i: (0, 0)
    layer_block = lambda i: (layer, 0, 0)
    in_specs = [pl.BlockSpec((TM, lhs.shape[1]), row) for lhs in lhs_list]
    in_specs += [pl.BlockSpec((TM, D), row),
                 pl.BlockSpec((None, TM, P), lambda i: (layer, i, 0)),
                 _resident(w_out.shape, fixed),
                 _resident((1, D), fixed),
                 pl.BlockSpec((None, D, D), layer_block, pipeline_mode=pl.Buffered(1)),
                 pl.BlockSpec((None, P, D), layer_block, pipeline_mode=pl.Buffered(1))]
    args = [*lhs_list, x2d, p3d, w_out, ple_g.reshape(1, D), gate_w3d, proj_w3d]
    if final_g is not None:
        in_specs.append(_resident((1, D), fixed))
        args.append(final_g.reshape(1, D))
    return pl.pallas_call(
        functools.partial(_out_ple_kernel, len(lhs_list), final_g is not None),
        grid=(M // TM,),
        in_specs=in_specs,
        out_specs=pl.BlockSpec((TM, D), row),
        out_shape=jax.ShapeDtypeStruct((M, D), F32),
        scratch_shapes=[pltpu.VMEM((TM, D), F32), pltpu.VMEM((TM, D), BF16)],
        compiler_params=_params(("arbitrary",)),
        name=name,
    )(*args)


def kernel(x, p, positions, ab_norm_g, ab_w_in, ab_conv_w, ab_conv_b, ab_i_bias, ab_f_bias, ab_ml_norm_g,
           ab_lam_q1, ab_lam_k1, ab_lam_q2, ab_lam_k2, ab_da_norm_g, ab_w_out, c_norm_g, c_w_in, c_ln_g,
           c_ln_b, c_sgu_w, c_sgu_b, c_w_out, ple_norm_g, ple_gate_w, ple_proj_w, final_norm_g):
    B, S, D = x.shape
    M = B * S
    assert ab_norm_g.shape[0] == 1 and c_norm_g.shape[0] == 1 and p.shape[0] == 2
    x2d = x.reshape(M, D)
    p3d = p.reshape(2, M, p.shape[-1])

    cos2, sin2 = _rope_tables(positions)
    proj, gates = _in_proj_ab(x2d, ab_norm_g[0], ab_w_in[0].T, cos2, sin2)

    a_out = _mlstm(proj, gates, ab_conv_w[0], ab_conv_b[0], ab_i_bias[0], ab_f_bias[0], ab_ml_norm_g[0], B, S)

    lam_init = 0.8 - 0.6 * math.exp(-0.3 * 0)
    b_out = _diff_attn(proj, ab_lam_q1[0], ab_lam_k1[0], ab_lam_q2[0], ab_lam_k2[0], ab_da_norm_g[0],
                       lam_init, B, S)

    x2 = _out_ple([a_out, b_out], x2d, p3d, 0, ab_w_out[0], ple_norm_g[0], ple_gate_w, ple_proj_w,
                  name="out_ple_ab")

    t = _in_proj_sgu(x2, c_norm_g[0], c_w_in[0], c_ln_g[0], c_ln_b[0], c_sgu_w[0], c_sgu_b[0])
    out = _out_ple([t], x2, p3d, 1, c_w_out[0], ple_norm_g[1], ple_gate_w, ple_proj_w,
                   final_g=final_norm_g, name="out_ple_c")
    return out.reshape(B, S, D)
```

```python
import functools
import math

import jax
import jax.numpy as jnp
from jax import lax
from jax.experimental import pallas as pl
from jax.experimental.pallas import tpu as pltpu

F32 = jnp.float32
BF16 = jnp.bfloat16
HIGHEST = lax.Precision.HIGHEST

EPS = 1e-6
ROPE_THETA = 10000.0
CONV_WIDTH = 4

ML_HEADS = 4
ML_QK_DIM = 128
ML_V_DIM = 256
DA_HEADS = 4
DA_QK_DIM = 128
DA_V_DIM = 256
SGU_CHUNK = 128
SGU_GROUPS = 8
SGU_GROUP_DIM = 256

ML_Q_OFF, ML_K_OFF, ML_V_OFF, ML_O_OFF, ML_Z_OFF = 0, 512, 1024, 2048, 3072
DA_Q_OFF, DA_K_OFF, DA_V_OFF, DA_Z_OFF = 4096, 5120, 6144, 7168
GATE_COLS = 128
GATE_START = ML_V_OFF + ML_HEADS * ML_V_DIM

VMEM_LIMIT_BYTES = 56 * 1024 * 1024

PROJ_TM = 1024
PROJ_TN = 1024
NORM_ROWS = 128
TAIL_TM = 256
TAIL_NC = 512
PROJ_SLAB = 256
CONV_ROWS = 128
ML_BLOCK = 256
ATT_QSUB = 256


def _params(semantics):
    return pltpu.CompilerParams(dimension_semantics=semantics, vmem_limit_bytes=VMEM_LIMIT_BYTES)


def _resident(shape, index_map):
    return pl.BlockSpec(shape, index_map, pipeline_mode=pl.Buffered(1))


def _sigmoid(x):
    return 1.0 / (1.0 + jnp.exp(-x))


def _silu(x):
    return x * _sigmoid(x)


def _log_sigmoid(x):
    return jnp.minimum(x, 0.0) - jnp.log(1.0 + jnp.exp(-jnp.abs(x)))


def _iota(shape, axis):
    return lax.broadcasted_iota(jnp.int32, shape, axis)


def _rope_table_kernel(pos_ref, cos_ref, sin_ref):
    pos = pos_ref[0]
    half_rows = pos.shape[0]
    lane = _iota(pos.shape, 1)
    lo = lane < 64
    j = jnp.where(lo, lane, lane - 64).astype(F32)
    freq = jnp.exp(j * (-math.log(ROPE_THETA) / 64.0))
    ang = pos * freq
    c = jnp.cos(ang)
    s = jnp.sin(ang)
    cr = pltpu.roll(c, 64, 1)
    sr = pltpu.roll(s, 64, 1)
    cos_ref[0, 0:half_rows, :] = jnp.where(lo, c, cr)
    cos_ref[0, half_rows:2 * half_rows, :] = jnp.where(lo, cr, c)
    sin_ref[0, 0:half_rows, :] = jnp.where(lo, -s, sr)
    sin_ref[0, half_rows:2 * half_rows, :] = jnp.where(lo, -sr, s)


def _rope_tables(positions):
    B, S = positions.shape
    posf = positions.astype(F32)
    lo = jnp.broadcast_to(posf[:, :S // 2, None], (B, S // 2, 64))
    hi = jnp.broadcast_to(posf[:, S // 2:, None], (B, S // 2, 64))
    pos2 = jnp.concatenate([lo, hi], axis=-1)
    return pl.pallas_call(
        _rope_table_kernel,
        grid=(B,),
        in_specs=[pl.BlockSpec((1, S // 2, 128), lambda b: (b, 0, 0))],
        out_specs=[pl.BlockSpec((1, S, 128), lambda b: (b, 0, 0)),
                   pl.BlockSpec((1, S, 128), lambda b: (b, 0, 0))],
        out_shape=[jax.ShapeDtypeStruct((B, S, 128), F32)] * 2,
        compiler_params=_params(("arbitrary",)),
        name="rope_tables",
    )(pos2)


def _normalize_rows(x_ref, g_ref, xn_ref):
    g = g_ref[...]
    for r0 in range(0, x_ref.shape[0], NORM_ROWS):
        rows = slice(r0, r0 + NORM_ROWS)
        x = x_ref[rows, :]
        ms = jnp.mean(x * x, axis=-1, keepdims=True)
        xn_ref[rows, :] = (x * lax.rsqrt(ms + EPS) * g).astype(BF16)


_NT = (((1,), (1,)), ((), ()))


def _in_proj_ab_kernel(n_gate, x_ref, g_ref, wt_ref, edge_ref, wgt_ref, cos_ref, sin_ref, o_ref, og_ref,
                       xn_ref):
    j = pl.program_id(1)
    TN, D = wt_ref.shape
    rope_q_tile, rope_k_tile = DA_Q_OFF // TN, DA_K_OFF // TN
    first_shifted = GATE_START // TN

    @pl.when(j == 0)
    def _():
        _normalize_rows(x_ref, g_ref, xn_ref)
        wg = jnp.concatenate([wgt_ref[...], jnp.zeros((GATE_COLS - n_gate, D), F32)], axis=0).astype(BF16)
        og_ref[...] = lax.dot_general(xn_ref[...], wg, _NT, preferred_element_type=F32)

    def slabs(shift):
        for r0 in range(0, TN, PROJ_SLAB):
            lo, hi = r0 + shift, r0 + PROJ_SLAB + shift
            if hi <= TN:
                w = wt_ref[lo:hi, :]
            else:
                w = jnp.concatenate([wt_ref[lo:TN, :], edge_ref[...]], axis=0)
            yield r0, lax.dot_general(xn_ref[...], w.astype(BF16), _NT, preferred_element_type=F32)

    def plain(shift):
        for r0, y in slabs(shift):
            o_ref[:, r0:r0 + PROJ_SLAB] = y.astype(o_ref.dtype)

    def rope(shift, scale):
        cs = cos_ref[...]
        sn = sin_ref[...]
        for r0, y in slabs(shift):
            for c0 in range(0, PROJ_SLAB, DA_QK_DIM):
                x = y[:, c0:c0 + DA_QK_DIM]
                r = x * cs + pltpu.roll(x, DA_QK_DIM // 2, 1) * sn
                if scale is not None:
                    r = r * scale
                o_ref[:, r0 + c0:r0 + c0 + DA_QK_DIM] = r.astype(o_ref.dtype)

    @pl.when(j < first_shifted)
    def _():
        plain(0)

    @pl.when((j >= first_shifted) & (j != rope_q_tile) & (j != rope_k_tile))
    def _():
        plain(n_gate)

    @pl.when(j == rope_q_tile)
    def _():
        rope(n_gate, DA_QK_DIM ** -0.5 * math.log2(math.e))

    @pl.when(j == rope_k_tile)
    def _():
        rope(n_gate, None)


def _in_proj_ab(x2d, g, w_t, cos2, sin2):
    M, D = x2d.shape
    n_gate = 2 * ML_HEADS
    N = w_t.shape[0] - n_gate
    TM, TN = PROJ_TM, PROJ_TN
    assert n_gate == 8 and GATE_START % TN == 0 and N % TN == 0
    assert DA_Q_OFF % TN == 0 and DA_K_OFF - DA_Q_OFF == TN and DA_V_OFF - DA_K_OFF == TN
    fixed = lambda i, j: (0, 0)
    return pl.pallas_call(
        functools.partial(_in_proj_ab_kernel, n_gate),
        grid=(M // TM, N // TN),
        in_specs=[pl.BlockSpec((TM, D), lambda i, j: (i, 0)),
                  pl.BlockSpec((1, D), fixed),
                  pl.BlockSpec((TN, D), lambda i, j: (j, 0)),
                  pl.BlockSpec((n_gate, D), lambda i, j: ((j + 1) * (TN // n_gate), 0)),
                  pl.BlockSpec((n_gate, D), lambda i, j: (GATE_START // n_gate, 0)),
                  pl.BlockSpec((TM, DA_QK_DIM), lambda i, j: (i, 0)),
                  pl.BlockSpec((TM, DA_QK_DIM), lambda i, j: (i, 0))],
        out_specs=[pl.BlockSpec((TM, TN), lambda i, j: (i, j)),
                   pl.BlockSpec((TM, GATE_COLS), lambda i, j: (i, 0))],
        out_shape=[jax.ShapeDtypeStruct((M, N), BF16), jax.ShapeDtypeStruct((M, GATE_COLS), F32)],
        scratch_shapes=[pltpu.VMEM((TM, D), BF16)],
        compiler_params=_params(("arbitrary", "arbitrary")),
        name="in_proj_ab",
    )(x2d, g.reshape(1, D), w_t, w_t, w_t, cos2.reshape(M, DA_QK_DIM), sin2.reshape(M, DA_QK_DIM))


_SGU_STEP_BLOCKS = (2, 3, 0, 4, 1, 5)


def _sgu_step_block(j):
    blk = jnp.int32(_SGU_STEP_BLOCKS[-1])
    for step in range(len(_SGU_STEP_BLOCKS) - 2, -1, -1):
        blk = jnp.where(j == step, _SGU_STEP_BLOCKS[step], blk)
    return blk


def _in_proj_sgu_kernel(x_ref, g_ref, w_ref, lng_ref, lnb_ref, sw_ref, sbt_ref, t_ref, xn_ref, v_ref, u_ref,
                        st_ref):
    j = pl.program_id(1)
    TM = x_ref.shape[0]
    TN = w_ref.shape[1]
    W = v_ref.shape[1]
    L = SGU_CHUNK
    GD = SGU_GROUP_DIM
    groups_per_step = TN // GD

    def proj_cols(cols):
        return jnp.dot(xn_ref[...], w_ref[:, cols].astype(BF16), preferred_element_type=F32)

    def project_v(half):
        for gl in range(groups_per_step):
            cols = slice(gl * GD, (gl + 1) * GD)
            slab = proj_cols(cols)
            v_ref[:, half * TN + gl * GD:half * TN + (gl + 1) * GD] = slab.astype(BF16)
            if half == 0 and gl == 0:
                shift = jnp.mean(slab, axis=1, keepdims=True)
                s1 = jnp.zeros((TM, 1), F32)
                s2 = jnp.zeros((TM, 1), F32)
            elif gl == 0:
                shift, s1, s2 = st_ref[0], st_ref[1], st_ref[2]
            d = slab - shift
            s1 = s1 + jnp.sum(d, axis=1, keepdims=True)
            s2 = s2 + jnp.sum(d * d, axis=1, keepdims=True)
        if half == 0:
            st_ref[0], st_ref[1], st_ref[2] = shift, s1, s2
        else:
            m1 = s1 * (1.0 / W)
            st_ref[0] = shift + m1
            st_ref[1] = lax.rsqrt(jnp.maximum(s2 * (1.0 / W) - m1 * m1, 0.0) + EPS)

    def gate(half):
        causal = _iota((L, L), 1) <= _iota((L, L), 0)
        for gl in range(groups_per_step):
            g = half * groups_per_step + gl
            cols = slice(gl * GD, (gl + 1) * GD)
            vcols = slice(g * GD, (g + 1) * GD)
            z = proj_cols(cols)
            w_causal = jnp.where(causal, sw_ref[g], 0.0).astype(BF16)
            bias = sbt_ref[:, g:g + 1]
            lng = lng_ref[:, vcols]
            lnb = lnb_ref[:, vcols]
            for c in range(TM // L):
                rows = slice(c * L, (c + 1) * L)
                vn = ((v_ref[rows, vcols].astype(F32) - st_ref[0, rows, :]) * st_ref[1, rows, :] * lng
                      + lnb).astype(BF16)
                sv = jnp.dot(w_causal, vn, preferred_element_type=F32) + bias
                t = u_ref[rows, cols].astype(F32) * sv * _silu(z[rows, :])
                t_ref[rows, cols] = t.astype(BF16)

    @pl.when(j == 0)
    def _():
        _normalize_rows(x_ref, g_ref, xn_ref)
        project_v(0)

    @pl.when(j == 1)
    def _():
        project_v(1)

    @pl.when((j == 2) | (j == 4))
    def _():
        for gl in range(groups_per_step):
            cols = slice(gl * GD, (gl + 1) * GD)
            u_ref[:, cols] = proj_cols(cols).astype(BF16)

    @pl.when(j == 3)
    def _():
        gate(0)

    @pl.when(j == 5)
    def _():
        gate(1)


def _in_proj_sgu(x2d, g, w, ln_g, ln_b, sgu_w, sgu_b):
    M, D = x2d.shape
    W = w.shape[1] // 3
    assert W == 2 * PROJ_TN and len(_SGU_STEP_BLOCKS) == 3 * W // PROJ_TN
    fixed = lambda i, j: (0, 0)
    return pl.pallas_call(
        _in_proj_sgu_kernel,
        grid=(M // PROJ_TM, len(_SGU_STEP_BLOCKS)),
        in_specs=[pl.BlockSpec((PROJ_TM, D), lambda i, j: (i, 0)),
                  pl.BlockSpec((1, D), fixed),
                  pl.BlockSpec((D, PROJ_TN), lambda i, j: (0, _sgu_step_block(j))),
                  pl.BlockSpec((1, W), fixed),
                  pl.BlockSpec((1, W), fixed),
                  pl.BlockSpec(sgu_w.shape, lambda i, j: (0, 0, 0)),
                  pl.BlockSpec((SGU_CHUNK, SGU_GROUPS), fixed)],
        out_specs=pl.BlockSpec((PROJ_TM, PROJ_TN), lambda i, j: (i, jnp.where(j <= 3, 0, 1))),
        out_shape=jax.ShapeDtypeStruct((M, W), BF16),
        scratch_shapes=[pltpu.VMEM((PROJ_TM, D), BF16),
                        pltpu.VMEM((PROJ_TM, W), BF16),
                        pltpu.VMEM((PROJ_TM, PROJ_TN), BF16),
                        pltpu.VMEM((3, PROJ_TM, 1), F32)],
        compiler_params=_params(("arbitrary", "arbitrary")),
        name="in_proj_sgu",
    )(x2d, g.reshape(1, D), w, ln_g.reshape(1, W), ln_b.reshape(1, W), sgu_w, sgu_b.T)


def _mlstm_kernel(qp_ref, kp_ref, v_ref, gt_ref, og_ref, z_ref, cwq_ref, cwk_ref, cbq_ref, cbk_ref,
                  gb_ref, ng_ref, out_ref, xs_ref, qc_ref, kc_ref):
    h = pl.program_id(1)
    S = qp_ref.shape[0]
    LB = ML_BLOCK
    nb = S // LB
    DK, DV = ML_QK_DIM, ML_V_DIM

    for src, cw_ref, cb_ref, dst, scale in ((qp_ref, cwq_ref, cbq_ref, qc_ref, 1.0),
                                            (kp_ref, cwk_ref, cbk_ref, kc_ref, DK ** -0.5)):
        xs_ref[0:8, :] = jnp.zeros((8, DK), F32)
        xs_ref[8:8 + S, :] = src[...].astype(F32)
        cw = cw_ref[...]
        cb = cb_ref[...]
        for c in range(S // CONV_ROWS):
            acc = cb
            for j in range(CONV_WIDTH):
                start = 8 + c * CONV_ROWS - (CONV_WIDTH - 1) + j
                acc = acc + cw[j:j + 1, :] * xs_ref[start:start + CONV_ROWS, :]
            dst[c * CONV_ROWS:(c + 1) * CONV_ROWS, :] = (_silu(acc) * scale).astype(BF16)

    gates = gt_ref[...] + gb_ref[...]
    pick = (_iota((8, GATE_COLS), 1) == _iota((8, GATE_COLS), 0)).astype(F32)
    gate_rows = lax.dot_general(pick, gates, (((1,), (1,)), ((), ())), preferred_element_type=F32,
                                precision=HIGHEST)
    sub8 = _iota((8, S), 0)
    i_all = jnp.sum(jnp.where(sub8 == h, gate_rows, 0.0), axis=0, keepdims=True)
    logf_all = _log_sigmoid(jnp.sum(jnp.where(sub8 == h + ML_HEADS, gate_rows, 0.0), axis=0, keepdims=True))

    blk = _iota((nb, LB), 0)

    def to_blocks(row):
        tile = jnp.zeros((nb, LB), F32)
        for c in range(nb):
            tile = jnp.where(blk == c, jnp.broadcast_to(row[:, c * LB:(c + 1) * LB], (nb, LB)), tile)
        return tile

    sub = _iota((LB, LB), 0)
    lane = _iota((LB, LB), 1)
    causal = lane <= sub
    i_b = to_blocks(i_all)
    b_b = jnp.dot(to_blocks(logf_all), (sub <= lane).astype(F32), preferred_element_type=F32,
                  precision=HIGHEST)
    g_tot = b_b[:, LB - 1:LB]
    w_b = g_tot - b_b + i_b
    m_loc = jnp.max(w_b, axis=1, keepdims=True)
    e_b = jnp.exp(w_b - m_loc)
    b_pad = jnp.concatenate([b_b, jnp.zeros((GATE_COLS - nb, LB), F32)], axis=0)
    b_cols = lax.dot_general((sub == lane).astype(F32), b_pad, (((1,), (1,)), ((), ())),
                             preferred_element_type=F32, precision=HIGHEST)

    norm_g = ng_ref[0]
    ct = jnp.zeros((DK, DV), F32)
    n = jnp.zeros((DK, 1), F32)
    m = jnp.zeros((1, 1), F32)
    for c in range(nb):
        rows = slice(c * LB, (c + 1) * LB)
        b_row, i_row, e_row = b_b[c:c + 1, :], i_b[c:c + 1, :], e_b[c:c + 1, :]
        b_col = b_cols[:, c:c + 1]
        q = qc_ref[rows, :]
        v = v_ref[rows, :]
        k_t = kc_ref[rows, :].astype(F32).T

        rhs = jnp.concatenate([k_t.astype(BF16), ct.astype(BF16),
                               jnp.broadcast_to(n, (DK, 128)).astype(BF16)], axis=1)
        big = jnp.dot(q, rhs, preferred_element_type=F32)
        qk, inter, qn = big[:, :LB], big[:, LB:LB + DV], big[:, LB + DV:LB + DV + 1]

        log_d = jnp.where(causal, b_col - b_row + i_row, -jnp.inf)
        inter_log = b_col + m
        m_t = jnp.maximum(inter_log, jnp.max(log_d, axis=1, keepdims=True))
        d_mat = jnp.exp(log_d - m_t)
        inter_w = jnp.exp(inter_log - m_t)
        s_mat = qk * d_mat
        num = jnp.dot(s_mat.astype(BF16), v, preferred_element_type=F32) + inter_w * inter
        den = jnp.sum(s_mat, axis=1, keepdims=True) + inter_w * qn
        hm = num / jnp.maximum(jnp.abs(den), jnp.exp(-m_t))

        hn = hm * lax.rsqrt(jnp.mean(hm * hm, axis=-1, keepdims=True) + EPS) * norm_g
        og = og_ref[rows, :].astype(F32)
        z = z_ref[rows, :].astype(F32)
        out_ref[rows, :] = (_sigmoid(og) * hn * _silu(z)).astype(out_ref.dtype)

        if c + 1 < nb:
            ke = k_t * e_row
            ct_loc = jnp.dot(ke.astype(BF16), v, preferred_element_type=F32)
            n_loc = jnp.sum(ke, axis=1, keepdims=True)
            g_c, ml_c = g_tot[c:c + 1, :], m_loc[c:c + 1, :]
            m_new = jnp.maximum(g_c + m, ml_c)
            a = jnp.exp(g_c + m - m_new)
            cc = jnp.exp(ml_c - m_new)
            ct = a * ct + cc * ct_loc
            n = a * n + cc * n_loc
            m = m_new


def _mlstm(proj, gates, conv_w, conv_b, i_bias, f_bias, norm_g, B, S):
    M = B * S
    H = ML_HEADS
    assert S // ML_BLOCK == 8, "block rows are packed into one 8-sublane tile"
    gate_bias = jnp.concatenate([i_bias, f_bias, jnp.zeros((GATE_COLS - 2 * H,), F32)]).reshape(1, GATE_COLS)
    conv_b2 = conv_b.reshape(1, -1)
    norm_g3 = norm_g.reshape(H, 1, ML_V_DIM)
    qb, kb = ML_Q_OFF // ML_QK_DIM, ML_K_OFF // ML_QK_DIM
    vb, ob, zb = ML_V_OFF // ML_V_DIM, ML_O_OFF // ML_V_DIM, ML_Z_OFF // ML_V_DIM
    return pl.pallas_call(
        _mlstm_kernel,
        grid=(B, H),
        in_specs=[
            pl.BlockSpec((S, ML_QK_DIM), lambda b, h: (b, qb + h)),
            pl.BlockSpec((S, ML_QK_DIM), lambda b, h: (b, kb + h)),
            pl.BlockSpec((S, ML_V_DIM), lambda b, h: (b, vb + h)),
            pl.BlockSpec((S, GATE_COLS), lambda b, h: (b, 0)),
            pl.BlockSpec((S, ML_V_DIM), lambda b, h: (b, ob + h)),
            pl.BlockSpec((S, ML_V_DIM), lambda b, h: (b, zb + h)),
            pl.BlockSpec((CONV_WIDTH, ML_QK_DIM), lambda b, h: (0, h)),
            pl.BlockSpec((CONV_WIDTH, ML_QK_DIM), lambda b, h: (0, H + h)),
            pl.BlockSpec((1, ML_QK_DIM), lambda b, h: (0, h)),
            pl.BlockSpec((1, ML_QK_DIM), lambda b, h: (0, H + h)),
            pl.BlockSpec((1, GATE_COLS), lambda b, h: (0, 0)),
            pl.BlockSpec((1, 1, ML_V_DIM), lambda b, h: (h, 0, 0)),
        ],
        out_specs=pl.BlockSpec((S, ML_V_DIM), lambda b, h: (b, h)),
        out_shape=jax.ShapeDtypeStruct((M, H * ML_V_DIM), BF16),
        scratch_shapes=[pltpu.VMEM((8 + S, ML_QK_DIM), F32),
                        pltpu.VMEM((S, ML_QK_DIM), BF16),
                        pltpu.VMEM((S, ML_QK_DIM), BF16)],
        compiler_params=_params(("arbitrary", "arbitrary")),
        name="mlstm",
    )(proj, proj, proj, gates, proj, proj, conv_w, conv_w, conv_b2, conv_b2, gate_bias, norm_g3)


def _diff_attn_kernel(lam_init, qr_ref, kr_ref, v_ref, z_ref, lq1_ref, lk1_ref, lq2_ref, lk2_ref, ng_ref,
                      o_ref):
    S = kr_ref.shape[0]
    QS, DH = ATT_QSUB, DA_QK_DIM
    diag_mask = _iota((QS, QS), 1) <= _iota((QS, QS), 0)

    def softmax_pv(q0, m):
        q = qr_ref[q0:q0 + QS, m * DH:(m + 1) * DH]
        kv_len = q0 + QS
        s = lax.dot_general(q, kr_ref[0:kv_len, m * DH:(m + 1) * DH], _NT, preferred_element_type=F32)
        s_diag = jnp.where(diag_mask, s[:, q0:], -jnp.inf)
        s = s_diag if q0 == 0 else jnp.concatenate([s[:, :q0], s_diag], axis=1)
        p = jnp.exp2(s - jnp.max(s, axis=1, keepdims=True))
        acc = jnp.dot(p.astype(BF16), v_ref[0:kv_len, :], preferred_element_type=F32)
        return acc, jnp.sum(p, axis=1, keepdims=True)

    lam = (jnp.exp(jnp.sum(lq1_ref[...] * lk1_ref[...], axis=1, keepdims=True))
           - jnp.exp(jnp.sum(lq2_ref[...] * lk2_ref[...], axis=1, keepdims=True)) + lam_init)
    norm_g = ng_ref[0] * (1.0 - lam_init)

    for q0 in range(0, S, QS):
        acc1, l1 = softmax_pv(q0, 0)
        acc2, l2 = softmax_pv(q0, 1)
        o = acc1 / l1 - lam * (acc2 / l2)
        hn = o * lax.rsqrt(jnp.mean(o * o, axis=-1, keepdims=True) + EPS) * norm_g
        o_ref[q0:q0 + QS, :] = (hn * _silu(z_ref[q0:q0 + QS, :].astype(F32))).astype(o_ref.dtype)


def _diff_attn(proj, lq1, lk1, lq2, lk2, norm_g, lam_init, B, S):
    M = B * S
    H = DA_HEADS
    qb, kb, vb, zb = (DA_Q_OFF // DA_V_DIM, DA_K_OFF // DA_V_DIM, DA_V_OFF // DA_V_DIM, DA_Z_OFF // DA_V_DIM)
    vec = lambda a: a.reshape(1, DA_QK_DIM)
    vec_spec = pl.BlockSpec((1, DA_QK_DIM), lambda b, h: (0, 0))
    return pl.pallas_call(
        functools.partial(_diff_attn_kernel, lam_init),
        grid=(B, H),
        in_specs=[
            pl.BlockSpec((S, DA_V_DIM), lambda b, h: (b, qb + h)),
            pl.BlockSpec((S, DA_V_DIM), lambda b, h: (b, kb + h)),
            pl.BlockSpec((S, DA_V_DIM), lambda b, h: (b, vb + h)),
            pl.BlockSpec((S, DA_V_DIM), lambda b, h: (b, zb + h)),
            vec_spec, vec_spec, vec_spec, vec_spec,
            pl.BlockSpec((1, 1, DA_V_DIM), lambda b, h: (h, 0, 0)),
        ],
        out_specs=pl.BlockSpec((S, DA_V_DIM), lambda b, h: (b, h)),
        out_shape=jax.ShapeDtypeStruct((M, H * DA_V_DIM), BF16),
        compiler_params=_params(("arbitrary", "arbitrary")),
        name="diff_attn",
    )(proj, proj, proj, proj, vec(lq1), vec(lk1), vec(lq2), vec(lk2), norm_g.reshape(H, 1, DA_V_DIM))


def _ple_tail(x1_ref, ssq, p_ref, gw_ref, pw_ref, xg_ref, emit):
    D = x1_ref.shape[1]
    inv = lax.rsqrt(ssq * (1.0 / D) + EPS)
    p_bf = p_ref[...].astype(BF16)
    for n0 in range(0, D, TAIL_NC):
        cols = slice(n0, n0 + TAIL_NC)
        gate = _sigmoid(jnp.dot(xg_ref[...], gw_ref[:, cols].astype(BF16), preferred_element_type=F32) * inv)
        pp = jnp.dot(p_bf, pw_ref[:, cols].astype(BF16), preferred_element_type=F32)
        emit(n0, x1_ref[:, cols] + gate * pp)


def _out_ple_kernel(n_lhs, has_final_norm, *refs):
    lhs_refs = refs[:n_lhs]
    x_ref, p_ref, wo_ref, png_ref, gw_ref, pw_ref = refs[n_lhs:n_lhs + 6]
    if has_final_norm:
        fng_ref, o_ref, x1_ref, xn_ref = refs[n_lhs + 6:]
    else:
        o_ref, x1_ref, xn_ref = refs[n_lhs + 6:]
    TM, D = x_ref.shape

    ssq = jnp.zeros((TM, 1), F32)
    for n0 in range(0, D, TAIL_NC):
        cols = slice(n0, n0 + TAIL_NC)
        y = x_ref[:, cols]
        k0 = 0
        for lhs_ref in lhs_refs:
            k1 = k0 + lhs_ref.shape[1]
            y = y + jnp.dot(lhs_ref[...], wo_ref[k0:k1, cols].astype(BF16), preferred_element_type=F32)
            k0 = k1
        x1_ref[:, cols] = y
        xn_ref[:, cols] = (y * png_ref[:, cols]).astype(BF16)
        ssq = ssq + jnp.sum(y * y, axis=1, keepdims=True)

    ssq_out = [jnp.zeros((TM, 1), F32)]

    def emit(n0, val):
        o_ref[:, n0:n0 + TAIL_NC] = val
        if has_final_norm:
            ssq_out[0] = ssq_out[0] + jnp.sum(val * val, axis=1, keepdims=True)

    _ple_tail(x1_ref, ssq, p_ref, gw_ref, pw_ref, xn_ref, emit)

    if has_final_norm:
        inv = lax.rsqrt(ssq_out[0] * (1.0 / D) + EPS)
        for n0 in range(0, D, TAIL_NC):
            cols = slice(n0, n0 + TAIL_NC)
            o_ref[:, cols] = o_ref[:, cols] * inv * fng_ref[:, cols]


def _out_ple(lhs_list, x2d, p3d, layer, w_out, ple_g, gate_w3d, proj_w3d, final_g=None, name="out_ple"):
    M, D = x2d.shape
    P = p3d.shape[2]
    TM = TAIL_TM
    row = lambda i: (i, 0)
    fixed = lambda i: (0, 0)
    layer_block = lambda i: (layer, 0, 0)
    in_specs = [pl.BlockSpec((TM, lhs.shape[1]), row) for lhs in lhs_list]
    in_specs += [pl.BlockSpec((TM, D), row),
                 pl.BlockSpec((None, TM, P), lambda i: (layer, i, 0)),
                 _resident(w_out.shape, fixed),
                 _resident((1, D), fixed),
                 pl.BlockSpec((None, D, D), layer_block, pipeline_mode=pl.Buffered(1)),
                 pl.BlockSpec((None, P, D), layer_block, pipeline_mode=pl.Buffered(1))]
    args = [*lhs_list, x2d, p3d, w_out, ple_g.reshape(1, D), gate_w3d, proj_w3d]
    if final_g is not None:
        in_specs.append(_resident((1, D), fixed))
        args.append(final_g.reshape(1, D))
    return pl.pallas_call(
        functools.partial(_out_ple_kernel, len(lhs_list), final_g is not None),
        grid=(M // TM,),
        in_specs=in_specs,
        out_specs=pl.BlockSpec((TM, D), row),
        out_shape=jax.ShapeDtypeStruct((M, D), F32),
        scratch_shapes=[pltpu.VMEM((TM, D), F32), pltpu.VMEM((TM, D), BF16)],
        compiler_params=_params(("arbitrary",)),
        name=name,
    )(*args)


def kernel(x, p, positions, ab_norm_g, ab_w_in, ab_conv_w, ab_conv_b, ab_i_bias, ab_f_bias, ab_ml_norm_g,
           ab_lam_q1, ab_lam_k1, ab_lam_q2, ab_lam_k2, ab_da_norm_g, ab_w_out, c_norm_g, c_w_in, c_ln_g,
           c_ln_b, c_sgu_w, c_sgu_b, c_w_out, ple_norm_g, ple_gate_w, ple_proj_w, final_norm_g):
    B, S, D = x.shape
    M = B * S
    assert ab_norm_g.shape[0] == 1 and c_norm_g.shape[0] == 1 and p.shape[0] == 2
    x2d = x.reshape(M, D)
    p3d = p.reshape(2, M, p.shape[-1])

    cos2, sin2 = _rope_tables(positions)
    proj, gates = _in_proj_ab(x2d, ab_norm_g[0], ab_w_in[0].T, cos2, sin2)

    a_out = _mlstm(proj, gates, ab_conv_w[0], ab_conv_b[0], ab_i_bias[0], ab_f_bias[0], ab_ml_norm_g[0], B, S)

    lam_init = 0.8 - 0.6 * math.exp(-0.3 * 0)
    b_out = _diff_attn(proj, ab_lam_q1[0], ab_lam_k1[0], ab_lam_q2[0], ab_lam_k2[0], ab_da_norm_g[0],
                       lam_init, B, S)

    x2 = _out_ple([a_out, b_out], x2d, p3d, 0, ab_w_out[0], ple_norm_g[0], ple_gate_w, ple_proj_w,
                  name="out_ple_ab")

    t = _in_proj_sgu(x2, c_norm_g[0], c_w_in[0], c_ln_g[0], c_ln_b[0], c_sgu_w[0], c_sgu_b[0])
    out = _out_ple([t], x2, p3d, 1, c_w_out[0], ple_norm_g[1], ple_gate_w, ple_proj_w,
                   final_g=final_norm_g, name="out_ple_c")
    return out.reshape(B, S, D)
```

```python
import functools
import math

import jax
import jax.numpy as jnp
from jax import lax
from jax.experimental import pallas as pl
from jax.experimental.pallas import tpu as pltpu

F32 = jnp.float32
BF16 = jnp.bfloat16
HIGHEST = lax.Precision.HIGHEST

EPS = 1e-6
ROPE_THETA = 10000.0
CONV_WIDTH = 4

ML_HEADS = 4
ML_QK_DIM = 128
ML_V_DIM = 256
DA_HEADS = 4
DA_QK_DIM = 128
DA_V_DIM = 256
SGU_CHUNK = 128
SGU_GROUPS = 8
SGU_GROUP_DIM = 256

ML_Q_OFF, ML_K_OFF, ML_V_OFF, ML_O_OFF, ML_Z_OFF = 0, 512, 1024, 2048, 3072
DA_Q_OFF, DA_K_OFF, DA_V_OFF, DA_Z_OFF = 4096, 5120, 6144, 7168
GATE_COLS = 128
GATE_START = ML_V_OFF + ML_HEADS * ML_V_DIM

VMEM_LIMIT_BYTES = 56 * 1024 * 1024

PROJ_TM = 1024
PROJ_TN = 1024
NORM_ROWS = 128
TAIL_TM = 256
TAIL_NC = 512
PROJ_SLAB = 256
CONV_ROWS = 128
ML_BLOCK = 256
ATT_QSUB = 256


def _params(semantics):
    return pltpu.CompilerParams(dimension_semantics=semantics, vmem_limit_bytes=VMEM_LIMIT_BYTES)


def _resident(shape, index_map):
    return pl.BlockSpec(shape, index_map, pipeline_mode=pl.Buffered(1))


def _sigmoid(x):
    return 1.0 / (1.0 + jnp.exp(-x))


def _silu(x):
    return x * _sigmoid(x)


def _log_sigmoid(x):
    return jnp.minimum(x, 0.0) - jnp.log(1.0 + jnp.exp(-jnp.abs(x)))


def _iota(shape, axis):
    return lax.broadcasted_iota(jnp.int32, shape, axis)


def _rope_table_kernel(pos_ref, cos_ref, sin_ref):
    pos = pos_ref[0]
    half_rows = pos.shape[0]
    lane = _iota(pos.shape, 1)
    lo = lane < 64
    j = jnp.where(lo, lane, lane - 64).astype(F32)
    freq = jnp.exp(j * (-math.log(ROPE_THETA) / 64.0))
    ang = pos * freq
    c = jnp.cos(ang)
    s = jnp.sin(ang)
    cr = pltpu.roll(c, 64, 1)
    sr = pltpu.roll(s, 64, 1)
    cos_ref[0, 0:half_rows, :] = jnp.where(lo, c, cr)
    cos_ref[0, half_rows:2 * half_rows, :] = jnp.where(lo, cr, c)
    sin_ref[0, 0:half_rows, :] = jnp.where(lo, -s, sr)
    sin_ref[0, half_rows:2 * half_rows, :] = jnp.where(lo, -sr, s)


def _rope_tables(positions):
    B, S = positions.shape
    posf = positions.astype(F32)
    lo = jnp.broadcast_to(posf[:, :S // 2, None], (B, S // 2, 64))
    hi = jnp.broadcast_to(posf[:, S // 2:, None], (B, S // 2, 64))
    pos2 = jnp.concatenate([lo, hi], axis=-1)
    return pl.pallas_call(
        _rope_table_kernel,
        grid=(B,),
        in_specs=[pl.BlockSpec((1, S // 2, 128), lambda b: (b, 0, 0))],
        out_specs=[pl.BlockSpec((1, S, 128), lambda b: (b, 0, 0)),
                   pl.BlockSpec((1, S, 128), lambda b: (b, 0, 0))],
        out_shape=[jax.ShapeDtypeStruct((B, S, 128), F32)] * 2,
        compiler_params=_params(("arbitrary",)),
        name="rope_tables",
    )(pos2)


def _normalize_rows(x_ref, g_ref, xn_ref):
    g = g_ref[...]
    for r0 in range(0, x_ref.shape[0], NORM_ROWS):
        rows = slice(r0, r0 + NORM_ROWS)
        x = x_ref[rows, :]
        ms = jnp.mean(x * x, axis=-1, keepdims=True)
        xn_ref[rows, :] = (x * lax.rsqrt(ms + EPS) * g).astype(BF16)


def _x_tile_index(n_tiles, i, j):
    return jnp.minimum(i + (j >= 1), n_tiles - 1), 0


_NT = (((1,), (1,)), ((), ()))


def _in_proj_ab_kernel(n_gate, x_ref, g_ref, wt_ref, edge_ref, wgt_ref, cos_ref, sin_ref, o_ref, og_ref,
                       xn_ref):
    j = pl.program_id(1)
    TN, D = wt_ref.shape
    rope_q_tile, rope_k_tile = DA_Q_OFF // TN, DA_K_OFF // TN
    first_shifted = GATE_START // TN

    @pl.when(j == 0)
    def _():
        _normalize_rows(x_ref, g_ref, xn_ref)
        wg = jnp.concatenate([wgt_ref[...], jnp.zeros((GATE_COLS - n_gate, D), F32)], axis=0).astype(BF16)
        og_ref[...] = lax.dot_general(xn_ref[...], wg, _NT, preferred_element_type=F32)

    def slabs(shift):
        for r0 in range(0, TN, PROJ_SLAB):
            lo, hi = r0 + shift, r0 + PROJ_SLAB + shift
            if hi <= TN:
                w = wt_ref[lo:hi, :]
            else:
                w = jnp.concatenate([wt_ref[lo:TN, :], edge_ref[...]], axis=0)
            yield r0, lax.dot_general(xn_ref[...], w.astype(BF16), _NT, preferred_element_type=F32)

    def plain(shift):
        for r0, y in slabs(shift):
            o_ref[:, r0:r0 + PROJ_SLAB] = y.astype(o_ref.dtype)

    def rope(shift, scale):
        cs = cos_ref[...]
        sn = sin_ref[...]
        for r0, y in slabs(shift):
            for c0 in range(0, PROJ_SLAB, DA_QK_DIM):
                x = y[:, c0:c0 + DA_QK_DIM]
                r = x * cs + pltpu.roll(x, DA_QK_DIM // 2, 1) * sn
                if scale is not None:
                    r = r * scale
                o_ref[:, r0 + c0:r0 + c0 + DA_QK_DIM] = r.astype(o_ref.dtype)

    @pl.when(j < first_shifted)
    def _():
        plain(0)

    @pl.when((j >= first_shifted) & (j != rope_q_tile) & (j != rope_k_tile))
    def _():
        plain(n_gate)

    @pl.when(j == rope_q_tile)
    def _():
        rope(n_gate, DA_QK_DIM ** -0.5 * math.log2(math.e))

    @pl.when(j == rope_k_tile)
    def _():
        rope(n_gate, None)


def _in_proj_ab(x2d, g, w_t, cos2, sin2):
    M, D = x2d.shape
    n_gate = 2 * ML_HEADS
    N = w_t.shape[0] - n_gate
    TM, TN = PROJ_TM, PROJ_TN
    assert n_gate == 8 and GATE_START % TN == 0 and N % TN == 0
    assert DA_Q_OFF % TN == 0 and DA_K_OFF - DA_Q_OFF == TN and DA_V_OFF - DA_K_OFF == TN
    fixed = lambda i, j: (0, 0)
    return pl.pallas_call(
        functools.partial(_in_proj_ab_kernel, n_gate),
        grid=(M // TM, N // TN),
        in_specs=[pl.BlockSpec((TM, D), functools.partial(_x_tile_index, M // TM)),
                  pl.BlockSpec((1, D), fixed),
                  pl.BlockSpec((TN, D), lambda i, j: (j, 0)),
                  pl.BlockSpec((n_gate, D), lambda i, j: ((j + 1) * (TN // n_gate), 0)),
                  pl.BlockSpec((n_gate, D), lambda i, j: (GATE_START // n_gate, 0)),
                  pl.BlockSpec((TM, DA_QK_DIM), lambda i, j: (i, 0)),
                  pl.BlockSpec((TM, DA_QK_DIM), lambda i, j: (i, 0))],
        out_specs=[pl.BlockSpec((TM, TN), lambda i, j: (i, j)),
                   pl.BlockSpec((TM, GATE_COLS), lambda i, j: (i, 0))],
        out_shape=[jax.ShapeDtypeStruct((M, N), BF16), jax.ShapeDtypeStruct((M, GATE_COLS), F32)],
        scratch_shapes=[pltpu.VMEM((TM, D), BF16)],
        compiler_params=_params(("arbitrary", "arbitrary")),
        name="in_proj_ab",
    )(x2d, g.reshape(1, D), w_t, w_t, w_t, cos2.reshape(M, DA_QK_DIM), sin2.reshape(M, DA_QK_DIM))


_SGU_STEP_BLOCKS = (2, 3, 0, 4, 1, 5)


def _sgu_step_block(j):
    blk = jnp.int32(_SGU_STEP_BLOCKS[-1])
    for step in range(len(_SGU_STEP_BLOCKS) - 2, -1, -1):
        blk = jnp.where(j == step, _SGU_STEP_BLOCKS[step], blk)
    return blk


def _in_proj_sgu_kernel(x_ref, g_ref, w_ref, lng_ref, lnb_ref, sw_ref, sbt_ref, t_ref, xn_ref, v_ref, u_ref,
                        st_ref):
    j = pl.program_id(1)
    TM = x_ref.shape[0]
    TN = w_ref.shape[1]
    W = v_ref.shape[1]
    L = SGU_CHUNK
    GD = SGU_GROUP_DIM
    groups_per_step = TN // GD

    def proj_cols(cols):
        return jnp.dot(xn_ref[...], w_ref[:, cols].astype(BF16), preferred_element_type=F32)

    def project_v(half):
        for gl in range(groups_per_step):
            cols = slice(gl * GD, (gl + 1) * GD)
            slab = proj_cols(cols)
            v_ref[:, half * TN + gl * GD:half * TN + (gl + 1) * GD] = slab.astype(BF16)
            if half == 0 and gl == 0:
                shift = jnp.mean(slab, axis=1, keepdims=True)
                s1 = jnp.zeros((TM, 1), F32)
                s2 = jnp.zeros((TM, 1), F32)
            elif gl == 0:
                shift, s1, s2 = st_ref[0], st_ref[1], st_ref[2]
            d = slab - shift
            s1 = s1 + jnp.sum(d, axis=1, keepdims=True)
            s2 = s2 + jnp.sum(d * d, axis=1, keepdims=True)
        if half == 0:
            st_ref[0], st_ref[1], st_ref[2] = shift, s1, s2
        else:
            m1 = s1 * (1.0 / W)
            st_ref[0] = shift + m1
            st_ref[1] = lax.rsqrt(jnp.maximum(s2 * (1.0 / W) - m1 * m1, 0.0) + EPS)

    def gate(half):
        causal = _iota((L, L), 1) <= _iota((L, L), 0)
        for gl in range(groups_per_step):
            g = half * groups_per_step + gl
            cols = slice(gl * GD, (gl + 1) * GD)
            vcols = slice(g * GD, (g + 1) * GD)
            z = proj_cols(cols)
            w_causal = jnp.where(causal, sw_ref[g], 0.0).astype(BF16)
            bias = sbt_ref[:, g:g + 1]
            lng = lng_ref[:, vcols]
            lnb = lnb_ref[:, vcols]
            for c in range(TM // L):
                rows = slice(c * L, (c + 1) * L)
                vn = ((v_ref[rows, vcols].astype(F32) - st_ref[0, rows, :]) * st_ref[1, rows, :] * lng
                      + lnb).astype(BF16)
                sv = jnp.dot(w_causal, vn, preferred_element_type=F32) + bias
                t = u_ref[rows, cols].astype(F32) * sv * _silu(z[rows, :])
                t_ref[rows, cols] = t.astype(BF16)

    @pl.when(j == 0)
    def _():
        _normalize_rows(x_ref, g_ref, xn_ref)
        project_v(0)

    @pl.when(j == 1)
    def _():
        project_v(1)

    @pl.when((j == 2) | (j == 4))
    def _():
        for gl in range(groups_per_step):
            cols = slice(gl * GD, (gl + 1) * GD)
            u_ref[:, cols] = proj_cols(cols).astype(BF16)

    @pl.when(j == 3)
    def _():
        gate(0)

    @pl.when(j == 5)
    def _():
        gate(1)


def _in_proj_sgu(x2d, g, w, ln_g, ln_b, sgu_w, sgu_b):
    M, D = x2d.shape
    W = w.shape[1] // 3
    assert W == 2 * PROJ_TN and len(_SGU_STEP_BLOCKS) == 3 * W // PROJ_TN
    fixed = lambda i, j: (0, 0)
    return pl.pallas_call(
        _in_proj_sgu_kernel,
        grid=(M // PROJ_TM, len(_SGU_STEP_BLOCKS)),
        in_specs=[pl.BlockSpec((PROJ_TM, D), functools.partial(_x_tile_index, M // PROJ_TM)),
                  pl.BlockSpec((1, D), fixed),
                  pl.BlockSpec((D, PROJ_TN), lambda i, j: (0, _sgu_step_block(j))),
                  pl.BlockSpec((1, W), fixed),
                  pl.BlockSpec((1, W), fixed),
                  pl.BlockSpec(sgu_w.shape, lambda i, j: (0, 0, 0)),
                  pl.BlockSpec((SGU_CHUNK, SGU_GROUPS), fixed)],
        out_specs=pl.BlockSpec((PROJ_TM, PROJ_TN), lambda i, j: (i, jnp.where(j <= 3, 0, 1))),
        out_shape=jax.ShapeDtypeStruct((M, W), BF16),
        scratch_shapes=[pltpu.VMEM((PROJ_TM, D), BF16),
                        pltpu.VMEM((PROJ_TM, W), BF16),
                        pltpu.VMEM((PROJ_TM, PROJ_TN), BF16),
                        pltpu.VMEM((3, PROJ_TM, 1), F32)],
        compiler_params=_params(("arbitrary", "arbitrary")),
        name="in_proj_sgu",
    )(x2d, g.reshape(1, D), w, ln_g.reshape(1, W), ln_b.reshape(1, W), sgu_w, sgu_b.T)


def _mlstm_kernel(qp_ref, kp_ref, v_ref, gt_ref, og_ref, z_ref, cwq_ref, cwk_ref, cbq_ref, cbk_ref,
                  gb_ref, ng_ref, out_ref, xs_ref, qc_ref, kc_ref):
    h = pl.program_id(1)
    S = qp_ref.shape[0]
    LB = ML_BLOCK
    nb = S // LB
    DK, DV = ML_QK_DIM, ML_V_DIM

    for src, cw_ref, cb_ref, dst, scale in ((qp_ref, cwq_ref, cbq_ref, qc_ref, 1.0),
                                            (kp_ref, cwk_ref, cbk_ref, kc_ref, DK ** -0.5)):
        xs_ref[0:8, :] = jnp.zeros((8, DK), F32)
        xs_ref[8:8 + S, :] = src[...].astype(F32)
        cw = cw_ref[...]
        cb = cb_ref[...]
        for c in range(S // CONV_ROWS):
            acc = cb
            for j in range(CONV_WIDTH):
                start = 8 + c * CONV_ROWS - (CONV_WIDTH - 1) + j
                acc = acc + cw[j:j + 1, :] * xs_ref[start:start + CONV_ROWS, :]
            dst[c * CONV_ROWS:(c + 1) * CONV_ROWS, :] = (_silu(acc) * scale).astype(BF16)

    gates = gt_ref[...] + gb_ref[...]
    pick = (_iota((8, GATE_COLS), 1) == _iota((8, GATE_COLS), 0)).astype(F32)
    gate_rows = lax.dot_general(pick, gates, (((1,), (1,)), ((), ())), preferred_element_type=F32,
                                precision=HIGHEST)
    sub8 = _iota((8, S), 0)
    i_all = jnp.sum(jnp.where(sub8 == h, gate_rows, 0.0), axis=0, keepdims=True)
    logf_all = _log_sigmoid(jnp.sum(jnp.where(sub8 == h + ML_HEADS, gate_rows, 0.0), axis=0, keepdims=True))

    blk = _iota((nb, LB), 0)

    def to_blocks(row):
        tile = jnp.zeros((nb, LB), F32)
        for c in range(nb):
            tile = jnp.where(blk == c, jnp.broadcast_to(row[:, c * LB:(c + 1) * LB], (nb, LB)), tile)
        return tile

    sub = _iota((LB, LB), 0)
    lane = _iota((LB, LB), 1)
    causal = lane <= sub
    i_b = to_blocks(i_all)
    b_b = jnp.dot(to_blocks(logf_all), (sub <= lane).astype(F32), preferred_element_type=F32,
                  precision=HIGHEST)
    g_tot = b_b[:, LB - 1:LB]
    w_b = g_tot - b_b + i_b
    m_loc = jnp.max(w_b, axis=1, keepdims=True)
    e_b = jnp.exp(w_b - m_loc)
    b_pad = jnp.concatenate([b_b, jnp.zeros((GATE_COLS - nb, LB), F32)], axis=0)
    b_cols = lax.dot_general((sub == lane).astype(F32), b_pad, (((1,), (1,)), ((), ())),
                             preferred_element_type=F32, precision=HIGHEST)

    norm_g = ng_ref[0]
    ct = jnp.zeros((DK, DV), F32)
    n = jnp.zeros((DK, 1), F32)
    m = jnp.zeros((1, 1), F32)
    for c in range(nb):
        rows = slice(c * LB, (c + 1) * LB)
        b_row, i_row, e_row = b_b[c:c + 1, :], i_b[c:c + 1, :], e_b[c:c + 1, :]
        b_col = b_cols[:, c:c + 1]
        q = qc_ref[rows, :]
        v = v_ref[rows, :]
        k_t = kc_ref[rows, :].astype(F32).T

        rhs = jnp.concatenate([k_t.astype(BF16), ct.astype(BF16),
                               jnp.broadcast_to(n, (DK, 128)).astype(BF16)], axis=1)
        big = jnp.dot(q, rhs, preferred_element_type=F32)
        qk, inter, qn = big[:, :LB], big[:, LB:LB + DV], big[:, LB + DV:LB + DV + 1]

        log_d = jnp.where(causal, b_col - b_row + i_row, -jnp.inf)
        inter_log = b_col + m
        m_t = jnp.maximum(inter_log, jnp.max(log_d, axis=1, keepdims=True))
        d_mat = jnp.exp(log_d - m_t)
        inter_w = jnp.exp(inter_log - m_t)
        s_mat = qk * d_mat
        num = jnp.dot(s_mat.astype(BF16), v, preferred_element_type=F32) + inter_w * inter
        den = jnp.sum(s_mat, axis=1, keepdims=True) + inter_w * qn
        hm = num / jnp.maximum(jnp.abs(den), jnp.exp(-m_t))

        hn = hm * lax.rsqrt(jnp.mean(hm * hm, axis=-1, keepdims=True) + EPS) * norm_g
        og = og_ref[rows, :].astype(F32)
        z = z_ref[rows, :].astype(F32)
        out_ref[rows, :] = (_sigmoid(og) * hn * _silu(z)).astype(out_ref.dtype)

        if c + 1 < nb:
            ke = k_t * e_row
            ct_loc = jnp.dot(ke.astype(BF16), v, preferred_element_type=F32)
            n_loc = jnp.sum(ke, axis=1, keepdims=True)
            g_c, ml_c = g_tot[c:c + 1, :], m_loc[c:c + 1, :]
            m_new = jnp.maximum(g_c + m, ml_c)
            a = jnp.exp(g_c + m - m_new)
            cc = jnp.exp(ml_c - m_new)
            ct = a * ct + cc * ct_loc
            n = a * n + cc * n_loc
            m = m_new


def _mlstm(proj, gates, conv_w, conv_b, i_bias, f_bias, norm_g, B, S):
    M = B * S
    H = ML_HEADS
    assert S // ML_BLOCK == 8, "block rows are packed into one 8-sublane tile"
    gate_bias = jnp.concatenate([i_bias, f_bias, jnp.zeros((GATE_COLS - 2 * H,), F32)]).reshape(1, GATE_COLS)
    conv_b2 = conv_b.reshape(1, -1)
    norm_g3 = norm_g.reshape(H, 1, ML_V_DIM)
    qb, kb = ML_Q_OFF // ML_QK_DIM, ML_K_OFF // ML_QK_DIM
    vb, ob, zb = ML_V_OFF // ML_V_DIM, ML_O_OFF // ML_V_DIM, ML_Z_OFF // ML_V_DIM
    return pl.pallas_call(
        _mlstm_kernel,
        grid=(B, H),
        in_specs=[
            pl.BlockSpec((S, ML_QK_DIM), lambda b, h: (b, qb + h)),
            pl.BlockSpec((S, ML_QK_DIM), lambda b, h: (b, kb + h)),
            pl.BlockSpec((S, ML_V_DIM), lambda b, h: (b, vb + h)),
            pl.BlockSpec((S, GATE_COLS), lambda b, h: (b, 0)),
            pl.BlockSpec((S, ML_V_DIM), lambda b, h: (b, ob + h)),
            pl.BlockSpec((S, ML_V_DIM), lambda b, h: (b, zb + h)),
            pl.BlockSpec((CONV_WIDTH, ML_QK_DIM), lambda b, h: (0, h)),
            pl.BlockSpec((CONV_WIDTH, ML_QK_DIM), lambda b, h: (0, H + h)),
            pl.BlockSpec((1, ML_QK_DIM), lambda b, h: (0, h)),
            pl.BlockSpec((1, ML_QK_DIM), lambda b, h: (0, H + h)),
            pl.BlockSpec((1, GATE_COLS), lambda b, h: (0, 0)),
            pl.BlockSpec((1, 1, ML_V_DIM), lambda b, h: (h, 0, 0)),
        ],
        out_specs=pl.BlockSpec((S, ML_V_DIM), lambda b, h: (b, h)),
        out_shape=jax.ShapeDtypeStruct((M, H * ML_V_DIM), BF16),
        scratch_shapes=[pltpu.VMEM((8 + S, ML_QK_DIM), F32),
                        pltpu.VMEM((S, ML_QK_DIM), BF16),
                        pltpu.VMEM((S, ML_QK_DIM), BF16)],
        compiler_params=_params(("arbitrary", "arbitrary")),
        name="mlstm",
    )(proj, proj, proj, gates, proj, proj, conv_w, conv_w, conv_b2, conv_b2, gate_bias, norm_g3)


def _diff_attn_kernel(lam_init, qr_ref, kr_ref, v_ref, z_ref, lq1_ref, lk1_ref, lq2_ref, lk2_ref, ng_ref,
                      o_ref):
    S = kr_ref.shape[0]
    QS, DH = ATT_QSUB, DA_QK_DIM
    diag_mask = _iota((QS, QS), 1) <= _iota((QS, QS), 0)

    def softmax_pv(q0, m):
        q = qr_ref[q0:q0 + QS, m * DH:(m + 1) * DH]
        kv_len = q0 + QS
        s = lax.dot_general(q, kr_ref[0:kv_len, m * DH:(m + 1) * DH], _NT, preferred_element_type=F32)
        s_diag = jnp.where(diag_mask, s[:, q0:], -jnp.inf)
        s = s_diag if q0 == 0 else jnp.concatenate([s[:, :q0], s_diag], axis=1)
        p = jnp.exp2(s - jnp.max(s, axis=1, keepdims=True))
        acc = jnp.dot(p.astype(BF16), v_ref[0:kv_len, :], preferred_element_type=F32)
        return acc, jnp.sum(p, axis=1, keepdims=True)

    lam = (jnp.exp(jnp.sum(lq1_ref[...] * lk1_ref[...], axis=1, keepdims=True))
           - jnp.exp(jnp.sum(lq2_ref[...] * lk2_ref[...], axis=1, keepdims=True)) + lam_init)
    norm_g = ng_ref[0] * (1.0 - lam_init)

    for q0 in range(0, S, QS):
        acc1, l1 = softmax_pv(q0, 0)
        acc2, l2 = softmax_pv(q0, 1)
        o = acc1 / l1 - lam * (acc2 / l2)
        hn = o * lax.rsqrt(jnp.mean(o * o, axis=-1, keepdims=True) + EPS) * norm_g
        o_ref[q0:q0 + QS, :] = (hn * _silu(z_ref[q0:q0 + QS, :].astype(F32))).astype(o_ref.dtype)


def _diff_attn(proj, lq1, lk1, lq2, lk2, norm_g, lam_init, B, S):
    M = B * S
    H = DA_HEADS
    qb, kb, vb, zb = (DA_Q_OFF // DA_V_DIM, DA_K_OFF // DA_V_DIM, DA_V_OFF // DA_V_DIM, DA_Z_OFF // DA_V_DIM)
    vec = lambda a: a.reshape(1, DA_QK_DIM)
    vec_spec = pl.BlockSpec((1, DA_QK_DIM), lambda b, h: (0, 0))
    return pl.pallas_call(
        functools.partial(_diff_attn_kernel, lam_init),
        grid=(B, H),
        in_specs=[
            pl.BlockSpec((S, DA_V_DIM), lambda b, h: (b, qb + h)),
            pl.BlockSpec((S, DA_V_DIM), lambda b, h: (b, kb + h)),
            pl.BlockSpec((S, DA_V_DIM), lambda b, h: (b, vb + h)),
            pl.BlockSpec((S, DA_V_DIM), lambda b, h: (b, zb + h)),
            vec_spec, vec_spec, vec_spec, vec_spec,
            pl.BlockSpec((1, 1, DA_V_DIM), lambda b, h: (h, 0, 0)),
        ],
        out_specs=pl.BlockSpec((S, DA_V_DIM), lambda b, h: (b, h)),
        out_shape=jax.ShapeDtypeStruct((M, H * DA_V_DIM), BF16),
        compiler_params=_params(("arbitrary", "arbitrary")),
        name="diff_attn",
    )(proj, proj, proj, proj, vec(lq1), vec(lk1), vec(lq2), vec(lk2), norm_g.reshape(H, 1, DA_V_DIM))


def _ple_tail(x1_ref, ssq, p_ref, gw_ref, pw_ref, xg_ref, emit):
    D = x1_ref.shape[1]
    inv = lax.rsqrt(ssq * (1.0 / D) + EPS)
    p_bf = p_ref[...].astype(BF16)
    for n0 in range(0, D, TAIL_NC):
        cols = slice(n0, n0 + TAIL_NC)
        gate = _sigmoid(jnp.dot(xg_ref[...], gw_ref[:, cols].astype(BF16), preferred_element_type=F32) * inv)
        pp = jnp.dot(p_bf, pw_ref[:, cols].astype(BF16), preferred_element_type=F32)
        emit(n0, x1_ref[:, cols] + gate * pp)


def _out_ple_kernel(n_lhs, has_final_norm, *refs):
    lhs_refs = refs[:n_lhs]
    x_ref, p_ref, wo_ref, png_ref, gw_ref, pw_ref = refs[n_lhs:n_lhs + 6]
    if has_final_norm:
        fng_ref, o_ref, x1_ref, xn_ref = refs[n_lhs + 6:]
    else:
        o_ref, x1_ref, xn_ref = refs[n_lhs + 6:]
    TM, D = x_ref.shape

    ssq = jnp.zeros((TM, 1), F32)
    for n0 in range(0, D, TAIL_NC):
        cols = slice(n0, n0 + TAIL_NC)
        y = x_ref[:, cols]
        k0 = 0
        for lhs_ref in lhs_refs:
            k1 = k0 + lhs_ref.shape[1]
            y = y + jnp.dot(lhs_ref[...], wo_ref[k0:k1, cols].astype(BF16), preferred_element_type=F32)
            k0 = k1
        x1_ref[:, cols] = y
        xn_ref[:, cols] = (y * png_ref[:, cols]).astype(BF16)
        ssq = ssq + jnp.sum(y * y, axis=1, keepdims=True)

    ssq_out = [jnp.zeros((TM, 1), F32)]

    def emit(n0, val):
        o_ref[:, n0:n0 + TAIL_NC] = val
        if has_final_norm:
            ssq_out[0] = ssq_out[0] + jnp.sum(val * val, axis=1, keepdims=True)

    _ple_tail(x1_ref, ssq, p_ref, gw_ref, pw_ref, xn_ref, emit)

    if has_final_norm:
        inv = lax.rsqrt(ssq_out[0] * (1.0 / D) + EPS)
        for n0 in range(0, D, TAIL_NC):
            cols = slice(n0, n0 + TAIL_NC)
            o_ref[:, cols] = o_ref[:, cols] * inv * fng_ref[:, cols]


def _out_ple(lhs_list, x2d, p3d, layer, w_out, ple_g, gate_w3d, proj_w3d, final_g=None, name="out_ple"):
    M, D = x2d.shape
    P = p3d.shape[2]
    TM = TAIL_TM
    row = lambda i: (i, 0)
    fixed = lambda i: (0, 0)
    layer_block = lambda i: (layer, 0, 0)
    in_specs = [pl.BlockSpec((TM, lhs.shape[1]), row) for lhs in lhs_list]
    in_specs += [pl.BlockSpec((TM, D), row),
                 pl.BlockSpec((None, TM, P), lambda i: (layer, i, 0)),
                 _resident(w_out.shape, fixed),
                 _resident((1, D), fixed),
                 pl.BlockSpec((None, D, D), layer_block, pipeline_mode=pl.Buffered(1)),
                 pl.BlockSpec((None, P, D), layer_block, pipeline_mode=pl.Buffered(1))]
    args = [*lhs_list, x2d, p3d, w_out, ple_g.reshape(1, D), gate_w3d, proj_w3d]
    if final_g is not None:
        in_specs.append(_resident((1, D), fixed))
        args.append(final_g.reshape(1, D))
    return pl.pallas_call(
        functools.partial(_out_ple_kernel, len(lhs_list), final_g is not None),
        grid=(M // TM,),
        in_specs=in_specs,
        out_specs=pl.BlockSpec((TM, D), row),
        out_shape=jax.ShapeDtypeStruct((M, D), F32),
        scratch_shapes=[pltpu.VMEM((TM, D), F32), pltpu.VMEM((TM, D), BF16)],
        compiler_params=_params(("arbitrary",)),
        name=name,
    )(*args)


def kernel(x, p, positions, ab_norm_g, ab_w_in, ab_conv_w, ab_conv_b, ab_i_bias, ab_f_bias, ab_ml_norm_g,
           ab_lam_q1, ab_lam_k1, ab_lam_q2, ab_lam_k2, ab_da_norm_g, ab_w_out, c_norm_g, c_w_in, c_ln_g,
           c_ln_b, c_sgu_w, c_sgu_b, c_w_out, ple_norm_g, ple_gate_w, ple_proj_w, final_norm_g):
    B, S, D = x.shape
    M = B * S
    assert ab_norm_g.shape[0] == 1 and c_norm_g.shape[0] == 1 and p.shape[0] == 2
    x2d = x.reshape(M, D)
    p3d = p.reshape(2, M, p.shape[-1])

    cos2, sin2 = _rope_tables(positions)
    proj, gates = _in_proj_ab(x2d, ab_norm_g[0], ab_w_in[0].T, cos2, sin2)

    a_out = _mlstm(proj, gates, ab_conv_w[0], ab_conv_b[0], ab_i_bias[0], ab_f_bias[0], ab_ml_norm_g[0], B, S)

    lam_init = 0.8 - 0.6 * math.exp(-0.3 * 0)
    b_out = _diff_attn(proj, ab_lam_q1[0], ab_lam_k1[0], ab_lam_q2[0], ab_lam_k2[0], ab_da_norm_g[0],
                       lam_init, B, S)

    x2 = _out_ple([a_out, b_out], x2d, p3d, 0, ab_w_out[0], ple_norm_g[0], ple_gate_w, ple_proj_w,
                  name="out_ple_ab")

    t = _in_proj_sgu(x2, c_norm_g[0], c_w_in[0], c_ln_g[0], c_ln_b[0], c_sgu_w[0], c_sgu_b[0])
    out = _out_ple([t], x2, p3d, 1, c_w_out[0], ple_norm_g[1], ple_gate_w, ple_proj_w,
                   final_g=final_norm_g, name="out_ple_c")
    return out.reshape(B, S, D)
```

```python
import functools
import math

import jax
import jax.numpy as jnp
from jax import lax
from jax.experimental import pallas as pl
from jax.experimental.pallas import tpu as pltpu

F32 = jnp.float32
BF16 = jnp.bfloat16
HIGHEST = lax.Precision.HIGHEST

EPS = 1e-6
ROPE_THETA = 10000.0
CONV_WIDTH = 4

ML_HEADS = 4
ML_QK_DIM = 128
ML_V_DIM = 256
DA_HEADS = 4
DA_QK_DIM = 128
DA_V_DIM = 256
SGU_CHUNK = 128
SGU_GROUPS = 8
SGU_GROUP_DIM = 256

ML_Q_OFF, ML_K_OFF, ML_V_OFF, ML_O_OFF, ML_Z_OFF = 0, 512, 1024, 2048, 3072
DA_Q_OFF, DA_K_OFF, DA_V_OFF, DA_Z_OFF = 4096, 5120, 6144, 7168
GATE_COLS = 128
GATE_START = ML_V_OFF + ML_HEADS * ML_V_DIM

VMEM_LIMIT_BYTES = 56 * 1024 * 1024

PROJ_TM = 1024
PROJ_TN = 1024
NORM_ROWS = 128
TAIL_TM = 256
TAIL_NC = 512
PROJ_SLAB = 256
CONV_ROWS = 128
ML_BLOCK = 256
ATT_QSUB = 256


def _params(semantics):
    return pltpu.CompilerParams(dimension_semantics=semantics, vmem_limit_bytes=VMEM_LIMIT_BYTES)


def _resident(shape, index_map):
    return pl.BlockSpec(shape, index_map, pipeline_mode=pl.Buffered(1))


def _sigmoid(x):
    return 1.0 / (1.0 + jnp.exp(-x))


def _silu(x):
    return x * _sigmoid(x)


def _log_sigmoid(x):
    return jnp.minimum(x, 0.0) - jnp.log(1.0 + jnp.exp(-jnp.abs(x)))


def _iota(shape, axis):
    return lax.broadcasted_iota(jnp.int32, shape, axis)


def _rope_table_kernel(pos_ref, cos_ref, sin_ref):
    pos = pos_ref[0]
    half_rows = pos.shape[0]
    lane = _iota(pos.shape, 1)
    lo = lane < 64
    j = jnp.where(lo, lane, lane - 64).astype(F32)
    freq = jnp.exp(j * (-math.log(ROPE_THETA) / 64.0))
    ang = pos * freq
    c = jnp.cos(ang)
    s = jnp.sin(ang)
    cr = pltpu.roll(c, 64, 1)
    sr = pltpu.roll(s, 64, 1)
    cos_ref[0, 0:half_rows, :] = jnp.where(lo, c, cr)
    cos_ref[0, half_rows:2 * half_rows, :] = jnp.where(lo, cr, c)
    sin_ref[0, 0:half_rows, :] = jnp.where(lo, -s, sr)
    sin_ref[0, half_rows:2 * half_rows, :] = jnp.where(lo, -sr, s)


def _rope_tables(positions):
    B, S = positions.shape
    posf = positions.astype(F32)
    lo = jnp.broadcast_to(posf[:, :S // 2, None], (B, S // 2, 64))
    hi = jnp.broadcast_to(posf[:, S // 2:, None], (B, S // 2, 64))
    pos2 = jnp.concatenate([lo, hi], axis=-1)
    return pl.pallas_call(
        _rope_table_kernel,
        grid=(B,),
        in_specs=[pl.BlockSpec((1, S // 2, 128), lambda b: (b, 0, 0))],
        out_specs=[pl.BlockSpec((1, S, 128), lambda b: (b, 0, 0)),
                   pl.BlockSpec((1, S, 128), lambda b: (b, 0, 0))],
        out_shape=[jax.ShapeDtypeStruct((B, S, 128), F32)] * 2,
        compiler_params=_params(("arbitrary",)),
        name="rope_tables",
    )(pos2)


def _normalize_rows(x_ref, g_ref, xn_ref):
    g = g_ref[...]
    for r0 in range(0, x_ref.shape[0], NORM_ROWS):
        rows = slice(r0, r0 + NORM_ROWS)
        x = x_ref[rows, :]
        ms = jnp.mean(x * x, axis=-1, keepdims=True)
        xn_ref[rows, :] = (x * lax.rsqrt(ms + EPS) * g).astype(BF16)


def _x_tile_index(n_tiles, i, j):
    return jnp.minimum(i + (j >= 1), n_tiles - 1), 0


_NT = (((1,), (1,)), ((), ()))


def _in_proj_ab_kernel(n_gate, x_ref, g_ref, wt_ref, edge_ref, wgt_ref, cos_ref, sin_ref, o_ref, og_ref,
                       xn_ref):
    j = pl.program_id(1)
    TN, D = wt_ref.shape
    rope_q_tile, rope_k_tile = DA_Q_OFF // TN, DA_K_OFF // TN
    first_shifted = GATE_START // TN

    @pl.when(j == 0)
    def _():
        _normalize_rows(x_ref, g_ref, xn_ref)
        wg = jnp.concatenate([wgt_ref[...], jnp.zeros((GATE_COLS - n_gate, D), F32)], axis=0).astype(BF16)
        og_ref[...] = lax.dot_general(xn_ref[...], wg, _NT, preferred_element_type=F32)

    def slabs(shift):
        for r0 in range(0, TN, PROJ_SLAB):
            lo, hi = r0 + shift, r0 + PROJ_SLAB + shift
            if hi <= TN:
                w = wt_ref[lo:hi, :]
            else:
                w = jnp.concatenate([wt_ref[lo:TN, :], edge_ref[...]], axis=0)
            yield r0, lax.dot_general(xn_ref[...], w.astype(BF16), _NT, preferred_element_type=F32)

    def plain(shift):
        for r0, y in slabs(shift):
            o_ref[:, r0:r0 + PROJ_SLAB] = y.astype(o_ref.dtype)

    def rope(shift, scale):
        cs = cos_ref[...]
        sn = sin_ref[...]
        for r0, y in slabs(shift):
            for c0 in range(0, PROJ_SLAB, DA_QK_DIM):
                x = y[:, c0:c0 + DA_QK_DIM]
                r = x * cs + pltpu.roll(x, DA_QK_DIM // 2, 1) * sn
                if scale is not None:
                    r = r * scale
                o_ref[:, r0 + c0:r0 + c0 + DA_QK_DIM] = r.astype(o_ref.dtype)

    @pl.when(j < first_shifted)
    def _():
        plain(0)

    @pl.when((j >= first_shifted) & (j != rope_q_tile) & (j != rope_k_tile))
    def _():
        plain(n_gate)

    @pl.when(j == rope_q_tile)
    def _():
        rope(n_gate, DA_QK_DIM ** -0.5 * math.log2(math.e))

    @pl.when(j == rope_k_tile)
    def _():
        rope(n_gate, None)


def _in_proj_ab(x2d, g, w_t, cos2, sin2):
    M, D = x2d.shape
    n_gate = 2 * ML_HEADS
    N = w_t.shape[0] - n_gate
    TM, TN = PROJ_TM, PROJ_TN
    assert n_gate == 8 and GATE_START % TN == 0 and N % TN == 0
    assert DA_Q_OFF % TN == 0 and DA_K_OFF - DA_Q_OFF == TN and DA_V_OFF - DA_K_OFF == TN
    fixed = lambda i, j: (0, 0)
    return pl.pallas_call(
        functools.partial(_in_proj_ab_kernel, n_gate),
        grid=(M // TM, N // TN),
        in_specs=[pl.BlockSpec((TM, D), functools.partial(_x_tile_index, M // TM)),
                  pl.BlockSpec((1, D), fixed),
                  pl.BlockSpec((TN, D), lambda i, j: (j, 0)),
                  pl.BlockSpec((n_gate, D), lambda i, j: ((j + 1) * (TN // n_gate), 0)),
                  pl.BlockSpec((n_gate, D), lambda i, j: (GATE_START // n_gate, 0)),
                  pl.BlockSpec((TM, DA_QK_DIM), lambda i, j: (i, 0)),
                  pl.BlockSpec((TM, DA_QK_DIM), lambda i, j: (i, 0))],
        out_specs=[pl.BlockSpec((TM, TN), lambda i, j: (i, j)),
                   pl.BlockSpec((TM, GATE_COLS), lambda i, j: (i, 0))],
        out_shape=[jax.ShapeDtypeStruct((M, N), BF16), jax.ShapeDtypeStruct((M, GATE_COLS), F32)],
        scratch_shapes=[pltpu.VMEM((TM, D), BF16)],
        compiler_params=_params(("arbitrary", "arbitrary")),
        name="in_proj_ab",
    )(x2d, g.reshape(1, D), w_t, w_t, w_t, cos2.reshape(M, DA_QK_DIM), sin2.reshape(M, DA_QK_DIM))


_SGU_STEP_BLOCKS = (2, 3, 0, 4, 1, 5)


def _sgu_step_block(j):
    blk = jnp.int32(_SGU_STEP_BLOCKS[-1])
    for step in range(len(_SGU_STEP_BLOCKS) - 2, -1, -1):
        blk = jnp.where(j == step, _SGU_STEP_BLOCKS[step], blk)
    return blk


def _in_proj_sgu_kernel(x_ref, g_ref, w_ref, lng_ref, lnb_ref, sw_ref, sbt_ref, t_ref, xn_ref, v_ref, u_ref,
                        st_ref):
    j = pl.program_id(1)
    TM = x_ref.shape[0]
    TN = w_ref.shape[1]
    W = v_ref.shape[1]
    L = SGU_CHUNK
    GD = SGU_GROUP_DIM
    groups_per_step = TN // GD

    def proj_cols(cols):
        return jnp.dot(xn_ref[...], w_ref[:, cols].astype(BF16), preferred_element_type=F32)

    def project_v(half):
        for gl in range(groups_per_step):
            cols = slice(gl * GD, (gl + 1) * GD)
            slab = proj_cols(cols)
            v_ref[:, half * TN + gl * GD:half * TN + (gl + 1) * GD] = slab.astype(BF16)
            if half == 0 and gl == 0:
                shift = jnp.mean(slab, axis=1, keepdims=True)
                s1 = jnp.zeros((TM, 1), F32)
                s2 = jnp.zeros((TM, 1), F32)
            elif gl == 0:
                shift, s1, s2 = st_ref[0], st_ref[1], st_ref[2]
            d = slab - shift
            s1 = s1 + jnp.sum(d, axis=1, keepdims=True)
            s2 = s2 + jnp.sum(d * d, axis=1, keepdims=True)
        if half == 0:
            st_ref[0], st_ref[1], st_ref[2] = shift, s1, s2
        else:
            m1 = s1 * (1.0 / W)
            st_ref[0] = shift + m1
            st_ref[1] = lax.rsqrt(jnp.maximum(s2 * (1.0 / W) - m1 * m1, 0.0) + EPS)

    def gate(half):
        causal = _iota((L, L), 1) <= _iota((L, L), 0)
        for gl in range(groups_per_step):
            g = half * groups_per_step + gl
            cols = slice(gl * GD, (gl + 1) * GD)
            vcols = slice(g * GD, (g + 1) * GD)
            z = proj_cols(cols)
            w_causal = jnp.where(causal, sw_ref[g], 0.0).astype(BF16)
            bias = sbt_ref[:, g:g + 1]
            lng = lng_ref[:, vcols]
            lnb = lnb_ref[:, vcols]
            for c in range(TM // L):
                rows = slice(c * L, (c + 1) * L)
                vn = ((v_ref[rows, vcols].astype(F32) - st_ref[0, rows, :]) * st_ref[1, rows, :] * lng
                      + lnb).astype(BF16)
                sv = jnp.dot(w_causal, vn, preferred_element_type=F32) + bias
                t = u_ref[rows, cols].astype(F32) * sv * _silu(z[rows, :])
                t_ref[rows, cols] = t.astype(BF16)

    @pl.when(j == 0)
    def _():
        _normalize_rows(x_ref, g_ref, xn_ref)
        project_v(0)

    @pl.when(j == 1)
    def _():
        project_v(1)

    @pl.when((j == 2) | (j == 4))
    def _():
        for gl in range(groups_per_step):
            cols = slice(gl * GD, (gl + 1) * GD)
            u_ref[:, cols] = proj_cols(cols).astype(BF16)

    @pl.when(j == 3)
    def _():
        gate(0)

    @pl.when(j == 5)
    def _():
        gate(1)


def _in_proj_sgu(x2d, g, w, ln_g, ln_b, sgu_w, sgu_b):
    M, D = x2d.shape
    W = w.shape[1] // 3
    assert W == 2 * PROJ_TN and len(_SGU_STEP_BLOCKS) == 3 * W // PROJ_TN
    fixed = lambda i, j: (0, 0)
    return pl.pallas_call(
        _in_proj_sgu_kernel,
        grid=(M // PROJ_TM, len(_SGU_STEP_BLOCKS)),
        in_specs=[pl.BlockSpec((PROJ_TM, D), lambda i, j: (i, 0)),
                  pl.BlockSpec((1, D), fixed),
                  pl.BlockSpec((D, PROJ_TN), lambda i, j: (0, _sgu_step_block(j))),
                  pl.BlockSpec((1, W), fixed),
                  pl.BlockSpec((1, W), fixed),
                  pl.BlockSpec(sgu_w.shape, lambda i, j: (0, 0, 0)),
                  pl.BlockSpec((SGU_CHUNK, SGU_GROUPS), fixed)],
        out_specs=pl.BlockSpec((PROJ_TM, PROJ_TN), lambda i, j: (i, jnp.where(j <= 3, 0, 1))),
        out_shape=jax.ShapeDtypeStruct((M, W), BF16),
        scratch_shapes=[pltpu.VMEM((PROJ_TM, D), BF16),
                        pltpu.VMEM((PROJ_TM, W), BF16),
                        pltpu.VMEM((PROJ_TM, PROJ_TN), BF16),
                        pltpu.VMEM((3, PROJ_TM, 1), F32)],
        compiler_params=_params(("arbitrary", "arbitrary")),
        name="in_proj_sgu",
    )(x2d, g.reshape(1, D), w, ln_g.reshape(1, W), ln_b.reshape(1, W), sgu_w, sgu_b.T)


def _mlstm_kernel(qp_ref, kp_ref, v_ref, gt_ref, og_ref, z_ref, cwq_ref, cwk_ref, cbq_ref, cbk_ref,
                  gb_ref, ng_ref, out_ref, xs_ref, qc_ref, kc_ref):
    h = pl.program_id(1)
    S = qp_ref.shape[0]
    LB = ML_BLOCK
    nb = S // LB
    DK, DV = ML_QK_DIM, ML_V_DIM

    for src, cw_ref, cb_ref, dst, scale in ((qp_ref, cwq_ref, cbq_ref, qc_ref, 1.0),
                                            (kp_ref, cwk_ref, cbk_ref, kc_ref, DK ** -0.5)):
        xs_ref[0:8, :] = jnp.zeros((8, DK), F32)
        xs_ref[8:8 + S, :] = src[...].astype(F32)
        cw = cw_ref[...]
        cb = cb_ref[...]
        for c in range(S // CONV_ROWS):
            acc = cb
            for j in range(CONV_WIDTH):
                start = 8 + c * CONV_ROWS - (CONV_WIDTH - 1) + j
                acc = acc + cw[j:j + 1, :] * xs_ref[start:start + CONV_ROWS, :]
            dst[c * CONV_ROWS:(c + 1) * CONV_ROWS, :] = (_silu(acc) * scale).astype(BF16)

    gates = gt_ref[...] + gb_ref[...]
    pick = (_iota((8, GATE_COLS), 1) == _iota((8, GATE_COLS), 0)).astype(F32)
    gate_rows = lax.dot_general(pick, gates, (((1,), (1,)), ((), ())), preferred_element_type=F32,
                                precision=HIGHEST)
    sub8 = _iota((8, S), 0)
    i_all = jnp.sum(jnp.where(sub8 == h, gate_rows, 0.0), axis=0, keepdims=True)
    logf_all = _log_sigmoid(jnp.sum(jnp.where(sub8 == h + ML_HEADS, gate_rows, 0.0), axis=0, keepdims=True))

    blk = _iota((nb, LB), 0)

    def to_blocks(row):
        tile = jnp.zeros((nb, LB), F32)
        for c in range(nb):
            tile = jnp.where(blk == c, jnp.broadcast_to(row[:, c * LB:(c + 1) * LB], (nb, LB)), tile)
        return tile

    sub = _iota((LB, LB), 0)
    lane = _iota((LB, LB), 1)
    causal = lane <= sub
    i_b = to_blocks(i_all)
    b_b = jnp.dot(to_blocks(logf_all), (sub <= lane).astype(F32), preferred_element_type=F32,
                  precision=HIGHEST)
    g_tot = b_b[:, LB - 1:LB]
    w_b = g_tot - b_b + i_b
    m_loc = jnp.max(w_b, axis=1, keepdims=True)
    e_b = jnp.exp(w_b - m_loc)
    b_pad = jnp.concatenate([b_b, jnp.zeros((GATE_COLS - nb, LB), F32)], axis=0)
    b_cols = lax.dot_general((sub == lane).astype(F32), b_pad, (((1,), (1,)), ((), ())),
                             preferred_element_type=F32, precision=HIGHEST)

    norm_g = ng_ref[0]
    ct = jnp.zeros((DK, DV), F32)
    n = jnp.zeros((DK, 1), F32)
    m = jnp.zeros((1, 1), F32)
    for c in range(nb):
        rows = slice(c * LB, (c + 1) * LB)
        b_row, i_row, e_row = b_b[c:c + 1, :], i_b[c:c + 1, :], e_b[c:c + 1, :]
        b_col = b_cols[:, c:c + 1]
        q = qc_ref[rows, :]
        v = v_ref[rows, :]
        k_t = kc_ref[rows, :].astype(F32).T

        rhs = jnp.concatenate([k_t.astype(BF16), ct.astype(BF16),
                               jnp.broadcast_to(n, (DK, 128)).astype(BF16)], axis=1)
        big = jnp.dot(q, rhs, preferred_element_type=F32)
        qk, inter, qn = big[:, :LB], big[:, LB:LB + DV], big[:, LB + DV:LB + DV + 1]

        log_d = jnp.where(causal, b_col - b_row + i_row, -jnp.inf)
        inter_log = b_col + m
        m_t = jnp.maximum(inter_log, jnp.max(log_d, axis=1, keepdims=True))
        d_mat = jnp.exp(log_d - m_t)
        inter_w = jnp.exp(inter_log - m_t)
        s_mat = qk * d_mat
        num = jnp.dot(s_mat.astype(BF16), v, preferred_element_type=F32) + inter_w * inter
        den = jnp.sum(s_mat, axis=1, keepdims=True) + inter_w * qn
        hm = num / jnp.maximum(jnp.abs(den), jnp.exp(-m_t))

        hn = hm * lax.rsqrt(jnp.mean(hm * hm, axis=-1, keepdims=True) + EPS) * norm_g
        og = og_ref[rows, :].astype(F32)
        z = z_ref[rows, :].astype(F32)
        out_ref[rows, :] = (_sigmoid(og) * hn * _silu(z)).astype(out_ref.dtype)

        if c + 1 < nb:
            ke = k_t * e_row
            ct_loc = jnp.dot(ke.astype(BF16), v, preferred_element_type=F32)
            n_loc = jnp.sum(ke, axis=1, keepdims=True)
            g_c, ml_c = g_tot[c:c + 1, :], m_loc[c:c + 1, :]
            m_new = jnp.maximum(g_c + m, ml_c)
            a = jnp.exp(g_c + m - m_new)
            cc = jnp.exp(ml_c - m_new)
            ct = a * ct + cc * ct_loc
            n = a * n + cc * n_loc
            m = m_new


def _mlstm(proj, gates, conv_w, conv_b, i_bias, f_bias, norm_g, B, S):
    M = B * S
    H = ML_HEADS
    assert S // ML_BLOCK == 8, "block rows are packed into one 8-sublane tile"
    gate_bias = jnp.concatenate([i_bias, f_bias, jnp.zeros((GATE_COLS - 2 * H,), F32)]).reshape(1, GATE_COLS)
    conv_b2 = conv_b.reshape(1, -1)
    norm_g3 = norm_g.reshape(H, 1, ML_V_DIM)
    qb, kb = ML_Q_OFF // ML_QK_DIM, ML_K_OFF // ML_QK_DIM
    vb, ob, zb = ML_V_OFF // ML_V_DIM, ML_O_OFF // ML_V_DIM, ML_Z_OFF // ML_V_DIM
    return pl.pallas_call(
        _mlstm_kernel,
        grid=(B, H),
        in_specs=[
            pl.BlockSpec((S, ML_QK_DIM), lambda b, h: (b, qb + h)),
            pl.BlockSpec((S, ML_QK_DIM), lambda b, h: (b, kb + h)),
            pl.BlockSpec((S, ML_V_DIM), lambda b, h: (b, vb + h)),
            pl.BlockSpec((S, GATE_COLS), lambda b, h: (b, 0)),
            pl.BlockSpec((S, ML_V_DIM), lambda b, h: (b, ob + h)),
            pl.BlockSpec((S, ML_V_DIM), lambda b, h: (b, zb + h)),
            pl.BlockSpec((CONV_WIDTH, ML_QK_DIM), lambda b, h: (0, h)),
            pl.BlockSpec((CONV_WIDTH, ML_QK_DIM), lambda b, h: (0, H + h)),
            pl.BlockSpec((1, ML_QK_DIM), lambda b, h: (0, h)),
            pl.BlockSpec((1, ML_QK_DIM), lambda b, h: (0, H + h)),
            pl.BlockSpec((1, GATE_COLS), lambda b, h: (0, 0)),
            pl.BlockSpec((1, 1, ML_V_DIM), lambda b, h: (h, 0, 0)),
        ],
        out_specs=pl.BlockSpec((S, ML_V_DIM), lambda b, h: (b, h)),
        out_shape=jax.ShapeDtypeStruct((M, H * ML_V_DIM), BF16),
        scratch_shapes=[pltpu.VMEM((8 + S, ML_QK_DIM), F32),
                        pltpu.VMEM((S, ML_QK_DIM), BF16),
                        pltpu.VMEM((S, ML_QK_DIM), BF16)],
        compiler_params=_params(("arbitrary", "arbitrary")),
        name="mlstm",
    )(proj, proj, proj, gates, proj, proj, conv_w, conv_w, conv_b2, conv_b2, gate_bias, norm_g3)


def _diff_attn_kernel(lam_init, qr_ref, kr_ref, v_ref, z_ref, lq1_ref, lk1_ref, lq2_ref, lk2_ref, ng_ref,
                      o_ref):
    S = kr_ref.shape[0]
    QS, DH = ATT_QSUB, DA_QK_DIM
    diag_mask = _iota((QS, QS), 1) <= _iota((QS, QS), 0)

    def softmax_pv(q0, m):
        q = qr_ref[q0:q0 + QS, m * DH:(m + 1) * DH]
        kv_len = q0 + QS
        s = lax.dot_general(q, kr_ref[0:kv_len, m * DH:(m + 1) * DH], _NT, preferred_element_type=F32)
        s_diag = jnp.where(diag_mask, s[:, q0:], -jnp.inf)
        s = s_diag if q0 == 0 else jnp.concatenate([s[:, :q0], s_diag], axis=1)
        p = jnp.exp2(s - jnp.max(s, axis=1, keepdims=True))
        acc = jnp.dot(p.astype(BF16), v_ref[0:kv_len, :], preferred_element_type=F32)
        return acc, jnp.sum(p, axis=1, keepdims=True)

    lam = (jnp.exp(jnp.sum(lq1_ref[...] * lk1_ref[...], axis=1, keepdims=True))
           - jnp.exp(jnp.sum(lq2_ref[...] * lk2_ref[...], axis=1, keepdims=True)) + lam_init)
    norm_g = ng_ref[0] * (1.0 - lam_init)

    for q0 in range(0, S, QS):
        acc1, l1 = softmax_pv(q0, 0)
        acc2, l2 = softmax_pv(q0, 1)
        o = acc1 / l1 - lam * (acc2 / l2)
        hn = o * lax.rsqrt(jnp.mean(o * o, axis=-1, keepdims=True) + EPS) * norm_g
        o_ref[q0:q0 + QS, :] = (hn * _silu(z_ref[q0:q0 + QS, :].astype(F32))).astype(o_ref.dtype)


def _diff_attn(proj, lq1, lk1, lq2, lk2, norm_g, lam_init, B, S):
    M = B * S
    H = DA_HEADS
    qb, kb, vb, zb = (DA_Q_OFF // DA_V_DIM, DA_K_OFF // DA_V_DIM, DA_V_OFF // DA_V_DIM, DA_Z_OFF // DA_V_DIM)
    vec = lambda a: a.reshape(1, DA_QK_DIM)
    vec_spec = pl.BlockSpec((1, DA_QK_DIM), lambda b, h: (0, 0))
    return pl.pallas_call(
        functools.partial(_diff_attn_kernel, lam_init),
        grid=(B, H),
        in_specs=[
            pl.BlockSpec((S, DA_V_DIM), lambda b, h: (b, qb + h)),
            pl.BlockSpec((S, DA_V_DIM), lambda b, h: (b, kb + h)),
            pl.BlockSpec((S, DA_V_DIM), lambda b, h: (b, vb + h)),
            pl.BlockSpec((S, DA_V_DIM), lambda b, h: (b, zb + h)),
            vec_spec, vec_spec, vec_spec, vec_spec,
            pl.BlockSpec((1, 1, DA_V_DIM), lambda b, h: (h, 0, 0)),
        ],
        out_specs=pl.BlockSpec((S, DA_V_DIM), lambda b, h: (b, h)),
        out_shape=jax.ShapeDtypeStruct((M, H * DA_V_DIM), BF16),
        compiler_params=_params(("arbitrary", "arbitrary")),
        name="diff_attn",
    )(proj, proj, proj, proj, vec(lq1), vec(lk1), vec(lq2), vec(lk2), norm_g.reshape(H, 1, DA_V_DIM))


def _ple_tail(x1_ref, ssq, p_ref, gw_ref, pw_ref, xg_ref, emit):
    D = x1_ref.shape[1]
    inv = lax.rsqrt(ssq * (1.0 / D) + EPS)
    p_bf = p_ref[...].astype(BF16)
    for n0 in range(0, D, TAIL_NC):
        cols = slice(n0, n0 + TAIL_NC)
        gate = _sigmoid(jnp.dot(xg_ref[...], gw_ref[:, cols].astype(BF16), preferred_element_type=F32) * inv)
        pp = jnp.dot(p_bf, pw_ref[:, cols].astype(BF16), preferred_element_type=F32)
        emit(n0, x1_ref[:, cols] + gate * pp)


def _out_ple_kernel(n_lhs, has_final_norm, *refs):
    lhs_refs = refs[:n_lhs]
    x_ref, p_ref, wo_ref, png_ref, gw_ref, pw_ref = refs[n_lhs:n_lhs + 6]
    if has_final_norm:
        fng_ref, o_ref, x1_ref, xn_ref = refs[n_lhs + 6:]
    else:
        o_ref, x1_ref, xn_ref = refs[n_lhs + 6:]
    TM, D = x_ref.shape

    ssq = jnp.zeros((TM, 1), F32)
    for n0 in range(0, D, TAIL_NC):
        cols = slice(n0, n0 + TAIL_NC)
        y = x_ref[:, cols]
        k0 = 0
        for lhs_ref in lhs_refs:
            k1 = k0 + lhs_ref.shape[1]
            y = y + jnp.dot(lhs_ref[...], wo_ref[k0:k1, cols].astype(BF16), preferred_element_type=F32)
            k0 = k1
        x1_ref[:, cols] = y
        xn_ref[:, cols] = (y * png_ref[:, cols]).astype(BF16)
        ssq = ssq + jnp.sum(y * y, axis=1, keepdims=True)

    ssq_out = [jnp.zeros((TM, 1), F32)]

    def emit(n0, val):
        o_ref[:, n0:n0 + TAIL_NC] = val
        if has_final_norm:
            ssq_out[0] = ssq_out[0] + jnp.sum(val * val, axis=1, keepdims=True)

    _ple_tail(x1_ref, ssq, p_ref, gw_ref, pw_ref, xn_ref, emit)

    if has_final_norm:
        inv = lax.rsqrt(ssq_out[0] * (1.0 / D) + EPS)
        for n0 in range(0, D, TAIL_NC):
            cols = slice(n0, n0 + TAIL_NC)
            o_ref[:, cols] = o_ref[:, cols] * inv * fng_ref[:, cols]


def _out_ple(lhs_list, x2d, p3d, layer, w_out, ple_g, gate_w3d, proj_w3d, final_g=None, name="out_ple"):
    M, D = x2d.shape
    P = p3d.shape[2]
    TM = TAIL_TM
    row = lambda i: (i, 0)
    fixed = lambda i: (0, 0)
    layer_block = lambda i: (layer, 0, 0)
    in_specs = [pl.BlockSpec((TM, lhs.shape[1]), row) for lhs in lhs_list]
    in_specs += [pl.BlockSpec((TM, D), row),
                 pl.BlockSpec((None, TM, P), lambda i: (layer, i, 0)),
                 _resident(w_out.shape, fixed),
                 _resident((1, D), fixed),
                 pl.BlockSpec((None, D, D), layer_block, pipeline_mode=pl.Buffered(1)),
                 pl.BlockSpec((None, P, D), layer_block, pipeline_mode=pl.Buffered(1))]
    args = [*lhs_list, x2d, p3d, w_out, ple_g.reshape(1, D), gate_w3d, proj_w3d]
    if final_g is not None:
        in_specs.append(_resident((1, D), fixed))
        args.append(final_g.reshape(1, D))
    return pl.pallas_call(
        functools.partial(_out_ple_kernel, len(lhs_list), final_g is not None),
        grid=(M // TM,),
        in_specs=in_specs,
        out_specs=pl.BlockSpec((TM, D), row),
        out_shape=jax.ShapeDtypeStruct((M, D), F32),
        scratch_shapes=[pltpu.VMEM((TM, D), F32), pltpu.VMEM((TM, D), BF16)],
        compiler_params=_params(("arbitrary",)),
        name=name,
    )(*args)


def kernel(x, p, positions, ab_norm_g, ab_w_in, ab_conv_w, ab_conv_b, ab_i_bias, ab_f_bias, ab_ml_norm_g,
           ab_lam_q1, ab_lam_k1, ab_lam_q2, ab_lam_k2, ab_da_norm_g, ab_w_out, c_norm_g, c_w_in, c_ln_g,
           c_ln_b, c_sgu_w, c_sgu_b, c_w_out, ple_norm_g, ple_gate_w, ple_proj_w, final_norm_g):
    B, S, D = x.shape
    M = B * S
    assert ab_norm_g.shape[0] == 1 and c_norm_g.shape[0] == 1 and p.shape[0] == 2
    x2d = x.reshape(M, D)
    p3d = p.reshape(2, M, p.shape[-1])

    cos2, sin2 = _rope_tables(positions)
    proj, gates = _in_proj_ab(x2d, ab_norm_g[0], ab_w_in[0].T, cos2, sin2)

    a_out = _mlstm(proj, gates, ab_conv_w[0], ab_conv_b[0], ab_i_bias[0], ab_f_bias[0], ab_ml_norm_g[0], B, S)

    lam_init = 0.8 - 0.6 * math.exp(-0.3 * 0)
    b_out = _diff_attn(proj, ab_lam_q1[0], ab_lam_k1[0], ab_lam_q2[0], ab_lam_k2[0], ab_da_norm_g[0],
                       lam_init, B, S)

    x2 = _out_ple([a_out, b_out], x2d, p3d, 0, ab_w_out[0], ple_norm_g[0], ple_gate_w, ple_proj_w,
                  name="out_ple_ab")

    t = _in_proj_sgu(x2, c_norm_g[0], c_w_in[0], c_ln_g[0], c_ln_b[0], c_sgu_w[0], c_sgu_b[0])
    out = _out_ple([t], x2, p3d, 1, c_w_out[0], ple_norm_g[1], ple_gate_w, ple_proj_w,
                   final_g=final_norm_g, name="out_ple_c")
    return out.reshape(B, S, D)
```

```python
import functools
import math

import jax
import jax.numpy as jnp
from jax import lax
from jax.experimental import pallas as pl
from jax.experimental.pallas import tpu as pltpu

F32 = jnp.float32
BF16 = jnp.bfloat16
HIGHEST = lax.Precision.HIGHEST

EPS = 1e-6
ROPE_THETA = 10000.0
CONV_WIDTH = 4

ML_HEADS = 4
ML_QK_DIM = 128
ML_V_DIM = 256
DA_HEADS = 4
DA_QK_DIM = 128
DA_V_DIM = 256
SGU_CHUNK = 128
SGU_GROUPS = 8
SGU_GROUP_DIM = 256

ML_Q_OFF, ML_K_OFF, ML_V_OFF, ML_O_OFF, ML_Z_OFF = 0, 512, 1024, 2048, 3072
DA_Q_OFF, DA_K_OFF, DA_V_OFF, DA_Z_OFF = 4096, 5120, 6144, 7168
GATE_COLS = 128
GATE_START = ML_V_OFF + ML_HEADS * ML_V_DIM

VMEM_LIMIT_BYTES = 56 * 1024 * 1024

PROJ_TM = 1024
PROJ_TN = 1024
NORM_ROWS = 128
TAIL_TM = 256
TAIL_NC = 512
PROJ_SLAB = 256
CONV_ROWS = 128
ML_BLOCK = 256
ATT_QSUB = 256


def _params(semantics):
    return pltpu.CompilerParams(dimension_semantics=semantics, vmem_limit_bytes=VMEM_LIMIT_BYTES)


def _resident(shape, index_map):
    return pl.BlockSpec(shape, index_map, pipeline_mode=pl.Buffered(1))


def _sigmoid(x):
    return 1.0 / (1.0 + jnp.exp(-x))


def _silu(x):
    return x * _sigmoid(x)


def _log_sigmoid(x):
    return jnp.minimum(x, 0.0) - jnp.log(1.0 + jnp.exp(-jnp.abs(x)))


def _iota(shape, axis):
    return lax.broadcasted_iota(jnp.int32, shape, axis)


def _rope_table_kernel(pos_ref, cos_ref, sin_ref):
    pos = pos_ref[0]
    half_rows = pos.shape[0]
    lane = _iota(pos.shape, 1)
    lo = lane < 64
    j = jnp.where(lo, lane, lane - 64).astype(F32)
    freq = jnp.exp(j * (-math.log(ROPE_THETA) / 64.0))
    ang = pos * freq
    c = jnp.cos(ang)
    s = jnp.sin(ang)
    cr = pltpu.roll(c, 64, 1)
    sr = pltpu.roll(s, 64, 1)
    cos_ref[0, 0:half_rows, :] = jnp.where(lo, c, cr)
    cos_ref[0, half_rows:2 * half_rows, :] = jnp.where(lo, cr, c)
    sin_ref[0, 0:half_rows, :] = jnp.where(lo, -s, sr)
    sin_ref[0, half_rows:2 * half_rows, :] = jnp.where(lo, -sr, s)


def _rope_tables(positions):
    B, S = positions.shape
    posf = positions.astype(F32)
    lo = jnp.broadcast_to(posf[:, :S // 2, None], (B, S // 2, 64))
    hi = jnp.broadcast_to(posf[:, S // 2:, None], (B, S // 2, 64))
    pos2 = jnp.concatenate([lo, hi], axis=-1)
    return pl.pallas_call(
        _rope_table_kernel,
        grid=(B,),
        in_specs=[pl.BlockSpec((1, S // 2, 128), lambda b: (b, 0, 0))],
        out_specs=[pl.BlockSpec((1, S, 128), lambda b: (b, 0, 0)),
                   pl.BlockSpec((1, S, 128), lambda b: (b, 0, 0))],
        out_shape=[jax.ShapeDtypeStruct((B, S, 128), F32)] * 2,
        compiler_params=_params(("arbitrary",)),
        name="rope_tables",
    )(pos2)


def _normalize_rows(x_ref, g_ref, xn_ref):
    g = g_ref[...]
    for r0 in range(0, x_ref.shape[0], NORM_ROWS):
        rows = slice(r0, r0 + NORM_ROWS)
        x = x_ref[rows, :]
        ms = jnp.mean(x * x, axis=-1, keepdims=True)
        xn_ref[rows, :] = (x * lax.rsqrt(ms + EPS) * g).astype(BF16)


def _x_tile_index(n_tiles, i, j):
    return jnp.minimum(i + (j >= 1), n_tiles - 1), 0


_NT = (((1,), (1,)), ((), ()))


def _in_proj_ab_kernel(n_gate, x_ref, g_ref, wt_ref, edge_ref, wgt_ref, cos_ref, sin_ref, o_ref, og_ref,
                       xn_ref):
    j = pl.program_id(1)
    TN, D = wt_ref.shape
    rope_q_tile, rope_k_tile = DA_Q_OFF // TN, DA_K_OFF // TN
    first_shifted = GATE_START // TN

    @pl.when(j == 0)
    def _():
        _normalize_rows(x_ref, g_ref, xn_ref)
        wg = jnp.concatenate([wgt_ref[...], jnp.zeros((16 - n_gate, D), F32)], axis=0).astype(BF16)
        og_ref[...] = lax.dot_general(wg, xn_ref[...], _NT, preferred_element_type=F32)[0:n_gate, :]

    def slabs(shift):
        for r0 in range(0, TN, PROJ_SLAB):
            lo, hi = r0 + shift, r0 + PROJ_SLAB + shift
            if hi <= TN:
                w = wt_ref[lo:hi, :]
            else:
                w = jnp.concatenate([wt_ref[lo:TN, :], edge_ref[...]], axis=0)
            yield r0, lax.dot_general(xn_ref[...], w.astype(BF16), _NT, preferred_element_type=F32)

    def plain(shift):
        for r0, y in slabs(shift):
            o_ref[:, r0:r0 + PROJ_SLAB] = y.astype(o_ref.dtype)

    def rope(shift, scale):
        cs = cos_ref[...]
        sn = sin_ref[...]
        for r0, y in slabs(shift):
            for c0 in range(0, PROJ_SLAB, DA_QK_DIM):
                x = y[:, c0:c0 + DA_QK_DIM]
                r = x * cs + pltpu.roll(x, DA_QK_DIM // 2, 1) * sn
                if scale is not None:
                    r = r * scale
                o_ref[:, r0 + c0:r0 + c0 + DA_QK_DIM] = r.astype(o_ref.dtype)

    @pl.when(j < first_shifted)
    def _():
        plain(0)

    @pl.when((j >= first_shifted) & (j != rope_q_tile) & (j != rope_k_tile))
    def _():
        plain(n_gate)

    @pl.when(j == rope_q_tile)
    def _():
        rope(n_gate, DA_QK_DIM ** -0.5 * math.log2(math.e))

    @pl.when(j == rope_k_tile)
    def _():
        rope(n_gate, None)


def _in_proj_ab(x2d, g, w_t, cos2, sin2):
    M, D = x2d.shape
    n_gate = 2 * ML_HEADS
    N = w_t.shape[0] - n_gate
    TM, TN = PROJ_TM, PROJ_TN
    assert n_gate == 8 and GATE_START % TN == 0 and N % TN == 0
    assert DA_Q_OFF % TN == 0 and DA_K_OFF - DA_Q_OFF == TN and DA_V_OFF - DA_K_OFF == TN
    fixed = lambda i, j: (0, 0)
    return pl.pallas_call(
        functools.partial(_in_proj_ab_kernel, n_gate),
        grid=(M // TM, N // TN),
        in_specs=[pl.BlockSpec((TM, D), functools.partial(_x_tile_index, M // TM)),
                  pl.BlockSpec((1, D), fixed),
                  pl.BlockSpec((TN, D), lambda i, j: (j, 0)),
                  pl.BlockSpec((n_gate, D), lambda i, j: ((j + 1) * (TN // n_gate), 0)),
                  pl.BlockSpec((n_gate, D), lambda i, j: (GATE_START // n_gate, 0)),
                  pl.BlockSpec((TM, DA_QK_DIM), lambda i, j: (i, 0)),
                  pl.BlockSpec((TM, DA_QK_DIM), lambda i, j: (i, 0))],
        out_specs=[pl.BlockSpec((TM, TN), lambda i, j: (i, j)),
                   pl.BlockSpec((n_gate, TM), lambda i, j: (0, i))],
        out_shape=[jax.ShapeDtypeStruct((M, N), BF16), jax.ShapeDtypeStruct((n_gate, M), F32)],
        scratch_shapes=[pltpu.VMEM((TM, D), BF16)],
        compiler_params=_params(("arbitrary", "arbitrary")),
        name="in_proj_ab",
    )(x2d, g.reshape(1, D), w_t, w_t, w_t, cos2.reshape(M, DA_QK_DIM), sin2.reshape(M, DA_QK_DIM))


_SGU_STEP_BLOCKS = (2, 3, 0, 4, 1, 5)


def _sgu_step_block(j):
    blk = jnp.int32(_SGU_STEP_BLOCKS[-1])
    for step in range(len(_SGU_STEP_BLOCKS) - 2, -1, -1):
        blk = jnp.where(j == step, _SGU_STEP_BLOCKS[step], blk)
    return blk


def _in_proj_sgu_kernel(x_ref, g_ref, w_ref, lng_ref, lnb_ref, sw_ref, sbt_ref, t_ref, xn_ref, v_ref, u_ref,
                        st_ref):
    j = pl.program_id(1)
    TM = x_ref.shape[0]
    TN = w_ref.shape[1]
    W = v_ref.shape[1]
    L = SGU_CHUNK
    GD = SGU_GROUP_DIM
    groups_per_step = TN // GD

    def proj_cols(cols):
        return jnp.dot(xn_ref[...], w_ref[:, cols].astype(BF16), preferred_element_type=F32)

    def project_v(half):
        for gl in range(groups_per_step):
            cols = slice(gl * GD, (gl + 1) * GD)
            slab = proj_cols(cols)
            v_ref[:, half * TN + gl * GD:half * TN + (gl + 1) * GD] = slab.astype(BF16)
            if half == 0 and gl == 0:
                shift = jnp.mean(slab, axis=1, keepdims=True)
                s1 = jnp.zeros((TM, 1), F32)
                s2 = jnp.zeros((TM, 1), F32)
            elif gl == 0:
                shift, s1, s2 = st_ref[0], st_ref[1], st_ref[2]
            d = slab - shift
            s1 = s1 + jnp.sum(d, axis=1, keepdims=True)
            s2 = s2 + jnp.sum(d * d, axis=1, keepdims=True)
        if half == 0:
            st_ref[0], st_ref[1], st_ref[2] = shift, s1, s2
        else:
            m1 = s1 * (1.0 / W)
            st_ref[0] = shift + m1
            st_ref[1] = lax.rsqrt(jnp.maximum(s2 * (1.0 / W) - m1 * m1, 0.0) + EPS)

    def gate(half):
        causal = _iota((L, L), 1) <= _iota((L, L), 0)
        for gl in range(groups_per_step):
            g = half * groups_per_step + gl
            cols = slice(gl * GD, (gl + 1) * GD)
            vcols = slice(g * GD, (g + 1) * GD)
            z = proj_cols(cols)
            w_causal = jnp.where(causal, sw_ref[g], 0.0).astype(BF16)
            bias = sbt_ref[:, g:g + 1]
            lng = lng_ref[:, vcols]
            lnb = lnb_ref[:, vcols]
            for c in range(TM // L):
                rows = slice(c * L, (c + 1) * L)
                vn = ((v_ref[rows, vcols].astype(F32) - st_ref[0, rows, :]) * st_ref[1, rows, :] * lng
                      + lnb).astype(BF16)
                sv = jnp.dot(w_causal, vn, preferred_element_type=F32) + bias
                t = u_ref[rows, cols].astype(F32) * sv * _silu(z[rows, :])
                t_ref[rows, cols] = t.astype(BF16)

    @pl.when(j == 0)
    def _():
        _normalize_rows(x_ref, g_ref, xn_ref)
        project_v(0)

    @pl.when(j == 1)
    def _():
        project_v(1)

    @pl.when((j == 2) | (j == 4))
    def _():
        for gl in range(groups_per_step):
            cols = slice(gl * GD, (gl + 1) * GD)
            u_ref[:, cols] = proj_cols(cols).astype(BF16)

    @pl.when(j == 3)
    def _():
        gate(0)

    @pl.when(j == 5)
    def _():
        gate(1)


def _in_proj_sgu(x2d, g, w, ln_g, ln_b, sgu_w, sgu_b):
    M, D = x2d.shape
    W = w.shape[1] // 3
    assert W == 2 * PROJ_TN and len(_SGU_STEP_BLOCKS) == 3 * W // PROJ_TN
    fixed = lambda i, j: (0, 0)
    return pl.pallas_call(
        _in_proj_sgu_kernel,
        grid=(M // PROJ_TM, len(_SGU_STEP_BLOCKS)),
        in_specs=[pl.BlockSpec((PROJ_TM, D), lambda i, j: (i, 0)),
                  pl.BlockSpec((1, D), fixed),
                  pl.BlockSpec((D, PROJ_TN), lambda i, j: (0, _sgu_step_block(j))),
                  pl.BlockSpec((1, W), fixed),
                  pl.BlockSpec((1, W), fixed),
                  pl.BlockSpec(sgu_w.shape, lambda i, j: (0, 0, 0)),
                  pl.BlockSpec((SGU_CHUNK, SGU_GROUPS), fixed)],
        out_specs=pl.BlockSpec((PROJ_TM, PROJ_TN), lambda i, j: (i, jnp.where(j <= 3, 0, 1))),
        out_shape=jax.ShapeDtypeStruct((M, W), BF16),
        scratch_shapes=[pltpu.VMEM((PROJ_TM, D), BF16),
                        pltpu.VMEM((PROJ_TM, W), BF16),
                        pltpu.VMEM((PROJ_TM, PROJ_TN), BF16),
                        pltpu.VMEM((3, PROJ_TM, 1), F32)],
        compiler_params=_params(("arbitrary", "arbitrary")),
        name="in_proj_sgu",
    )(x2d, g.reshape(1, D), w, ln_g.reshape(1, W), ln_b.reshape(1, W), sgu_w, sgu_b.T)


def _mlstm_kernel(qp_ref, kp_ref, v_ref, gt_ref, og_ref, z_ref, cwq_ref, cwk_ref, cbq_ref, cbk_ref,
                  gb_ref, ng_ref, out_ref, xs_ref, qc_ref, kc_ref):
    h = pl.program_id(1)
    S = qp_ref.shape[0]
    LB = ML_BLOCK
    nb = S // LB
    DK, DV = ML_QK_DIM, ML_V_DIM

    for src, cw_ref, cb_ref, dst, scale in ((qp_ref, cwq_ref, cbq_ref, qc_ref, 1.0),
                                            (kp_ref, cwk_ref, cbk_ref, kc_ref, DK ** -0.5)):
        xs_ref[0:8, :] = jnp.zeros((8, DK), F32)
        xs_ref[8:8 + S, :] = src[...].astype(F32)
        cw = cw_ref[...]
        cb = cb_ref[...]
        for c in range(S // CONV_ROWS):
            acc = cb
            for j in range(CONV_WIDTH):
                start = 8 + c * CONV_ROWS - (CONV_WIDTH - 1) + j
                acc = acc + cw[j:j + 1, :] * xs_ref[start:start + CONV_ROWS, :]
            dst[c * CONV_ROWS:(c + 1) * CONV_ROWS, :] = (_silu(acc) * scale).astype(BF16)

    gate_rows = gt_ref[...] + jnp.concatenate([gb_ref[...]] * (S // GATE_COLS), axis=1)
    sub8 = _iota((8, S), 0)
    i_all = jnp.sum(jnp.where(sub8 == h, gate_rows, 0.0), axis=0, keepdims=True)
    logf_all = _log_sigmoid(jnp.sum(jnp.where(sub8 == h + ML_HEADS, gate_rows, 0.0), axis=0, keepdims=True))

    blk = _iota((nb, LB), 0)

    def to_blocks(row):
        tile = jnp.zeros((nb, LB), F32)
        for c in range(nb):
            tile = jnp.where(blk == c, jnp.broadcast_to(row[:, c * LB:(c + 1) * LB], (nb, LB)), tile)
        return tile

    sub = _iota((LB, LB), 0)
    lane = _iota((LB, LB), 1)
    causal = lane <= sub
    i_b = to_blocks(i_all)
    b_b = jnp.dot(to_blocks(logf_all), (sub <= lane).astype(F32), preferred_element_type=F32,
                  precision=HIGHEST)
    g_tot = b_b[:, LB - 1:LB]
    w_b = g_tot - b_b + i_b
    m_loc = jnp.max(w_b, axis=1, keepdims=True)
    e_b = jnp.exp(w_b - m_loc)
    b_pad = jnp.concatenate([b_b, jnp.zeros((GATE_COLS - nb, LB), F32)], axis=0)
    b_cols = lax.dot_general((sub == lane).astype(F32), b_pad, (((1,), (1,)), ((), ())),
                             preferred_element_type=F32, precision=HIGHEST)

    norm_g = ng_ref[0]
    ct = jnp.zeros((DK, DV), F32)
    n = jnp.zeros((DK, 1), F32)
    m = jnp.zeros((1, 1), F32)
    for c in range(nb):
        rows = slice(c * LB, (c + 1) * LB)
        b_row, i_row, e_row = b_b[c:c + 1, :], i_b[c:c + 1, :], e_b[c:c + 1, :]
        b_col = b_cols[:, c:c + 1]
        q = qc_ref[rows, :]
        v = v_ref[rows, :]
        k_t = kc_ref[rows, :].astype(F32).T

        rhs = jnp.concatenate([k_t.astype(BF16), ct.astype(BF16),
                               jnp.broadcast_to(n, (DK, 128)).astype(BF16)], axis=1)
        big = jnp.dot(q, rhs, preferred_element_type=F32)
        qk, inter, qn = big[:, :LB], big[:, LB:LB + DV], big[:, LB + DV:LB + DV + 1]

        log_d = jnp.where(causal, b_col - b_row + i_row, -jnp.inf)
        inter_log = b_col + m
        m_t = jnp.maximum(inter_log, jnp.max(log_d, axis=1, keepdims=True))
        d_mat = jnp.exp(log_d - m_t)
        inter_w = jnp.exp(inter_log - m_t)
        s_mat = qk * d_mat
        num = jnp.dot(s_mat.astype(BF16), v, preferred_element_type=F32) + inter_w * inter
        den = jnp.sum(s_mat, axis=1, keepdims=True) + inter_w * qn
        hm = num / jnp.maximum(jnp.abs(den), jnp.exp(-m_t))

        hn = hm * lax.rsqrt(jnp.mean(hm * hm, axis=-1, keepdims=True) + EPS) * norm_g
        og = og_ref[rows, :].astype(F32)
        z = z_ref[rows, :].astype(F32)
        out_ref[rows, :] = (_sigmoid(og) * hn * _silu(z)).astype(out_ref.dtype)

        if c + 1 < nb:
            ke = k_t * e_row
            ct_loc = jnp.dot(ke.astype(BF16), v, preferred_element_type=F32)
            n_loc = jnp.sum(ke, axis=1, keepdims=True)
            g_c, ml_c = g_tot[c:c + 1, :], m_loc[c:c + 1, :]
            m_new = jnp.maximum(g_c + m, ml_c)
            a = jnp.exp(g_c + m - m_new)
            cc = jnp.exp(ml_c - m_new)
            ct = a * ct + cc * ct_loc
            n = a * n + cc * n_loc
            m = m_new


def _mlstm(proj, gates, conv_w, conv_b, i_bias, f_bias, norm_g, B, S):
    M = B * S
    H = ML_HEADS
    assert S // ML_BLOCK == 8, "block rows are packed into one 8-sublane tile"
    gate_bias = jnp.broadcast_to(jnp.concatenate([i_bias, f_bias])[:, None], (2 * H, GATE_COLS))
    conv_b2 = conv_b.reshape(1, -1)
    norm_g3 = norm_g.reshape(H, 1, ML_V_DIM)
    qb, kb = ML_Q_OFF // ML_QK_DIM, ML_K_OFF // ML_QK_DIM
    vb, ob, zb = ML_V_OFF // ML_V_DIM, ML_O_OFF // ML_V_DIM, ML_Z_OFF // ML_V_DIM
    return pl.pallas_call(
        _mlstm_kernel,
        grid=(B, H),
        in_specs=[
            pl.BlockSpec((S, ML_QK_DIM), lambda b, h: (b, qb + h)),
            pl.BlockSpec((S, ML_QK_DIM), lambda b, h: (b, kb + h)),
            pl.BlockSpec((S, ML_V_DIM), lambda b, h: (b, vb + h)),
            pl.BlockSpec((2 * H, S), lambda b, h: (0, b)),
            pl.BlockSpec((S, ML_V_DIM), lambda b, h: (b, ob + h)),
            pl.BlockSpec((S, ML_V_DIM), lambda b, h: (b, zb + h)),
            pl.BlockSpec((CONV_WIDTH, ML_QK_DIM), lambda b, h: (0, h)),
            pl.BlockSpec((CONV_WIDTH, ML_QK_DIM), lambda b, h: (0, H + h)),
            pl.BlockSpec((1, ML_QK_DIM), lambda b, h: (0, h)),
            pl.BlockSpec((1, ML_QK_DIM), lambda b, h: (0, H + h)),
            pl.BlockSpec((2 * H, GATE_COLS), lambda b, h: (0, 0)),
            pl.BlockSpec((1, 1, ML_V_DIM), lambda b, h: (h, 0, 0)),
        ],
        out_specs=pl.BlockSpec((S, ML_V_DIM), lambda b, h: (b, h)),
        out_shape=jax.ShapeDtypeStruct((M, H * ML_V_DIM), BF16),
        scratch_shapes=[pltpu.VMEM((8 + S, ML_QK_DIM), F32),
                        pltpu.VMEM((S, ML_QK_DIM), BF16),
                        pltpu.VMEM((S, ML_QK_DIM), BF16)],
        compiler_params=_params(("arbitrary", "arbitrary")),
        name="mlstm",
    )(proj, proj, proj, gates, proj, proj, conv_w, conv_w, conv_b2, conv_b2, gate_bias, norm_g3)


def _diff_attn_kernel(lam_init, qr_ref, kr_ref, v_ref, z_ref, lq1_ref, lk1_ref, lq2_ref, lk2_ref, ng_ref,
                      o_ref):
    S = kr_ref.shape[0]
    QS, DH = ATT_QSUB, DA_QK_DIM
    diag_mask = _iota((QS, QS), 1) <= _iota((QS, QS), 0)

    def softmax_pv(q0, m):
        q = qr_ref[q0:q0 + QS, m * DH:(m + 1) * DH]
        kv_len = q0 + QS
        s = lax.dot_general(q, kr_ref[0:kv_len, m * DH:(m + 1) * DH], _NT, preferred_element_type=F32)
        s_diag = jnp.where(diag_mask, s[:, q0:], -jnp.inf)
        s = s_diag if q0 == 0 else jnp.concatenate([s[:, :q0], s_diag], axis=1)
        p = jnp.exp2(s - jnp.max(s, axis=1, keepdims=True))
        acc = jnp.dot(p.astype(BF16), v_ref[0:kv_len, :], preferred_element_type=F32)
        return acc, jnp.sum(p, axis=1, keepdims=True)

    lam = (jnp.exp(jnp.sum(lq1_ref[...] * lk1_ref[...], axis=1, keepdims=True))
           - jnp.exp(jnp.sum(lq2_ref[...] * lk2_ref[...], axis=1, keepdims=True)) + lam_init)
    norm_g = ng_ref[0] * (1.0 - lam_init)

    for q0 in range(0, S, QS):
        acc1, l1 = softmax_pv(q0, 0)
        acc2, l2 = softmax_pv(q0, 1)
        o = acc1 / l1 - lam * (acc2 / l2)
        hn = o * lax.rsqrt(jnp.mean(o * o, axis=-1, keepdims=True) + EPS) * norm_g
        o_ref[q0:q0 + QS, :] = (hn * _silu(z_ref[q0:q0 + QS, :].astype(F32))).astype(o_ref.dtype)


def _diff_attn(proj, lq1, lk1, lq2, lk2, norm_g, lam_init, B, S):
    M = B * S
    H = DA_HEADS
    qb, kb, vb, zb = (DA_Q_OFF // DA_V_DIM, DA_K_OFF // DA_V_DIM, DA_V_OFF // DA_V_DIM, DA_Z_OFF // DA_V_DIM)
    vec = lambda a: a.reshape(1, DA_QK_DIM)
    vec_spec = pl.BlockSpec((1, DA_QK_DIM), lambda b, h: (0, 0))
    return pl.pallas_call(
        functools.partial(_diff_attn_kernel, lam_init),
        grid=(B, H),
        in_specs=[
            pl.BlockSpec((S, DA_V_DIM), lambda b, h: (b, qb + h)),
            pl.BlockSpec((S, DA_V_DIM), lambda b, h: (b, kb + h)),
            pl.BlockSpec((S, DA_V_DIM), lambda b, h: (b, vb + h)),
            pl.BlockSpec((S, DA_V_DIM), lambda b, h: (b, zb + h)),
            vec_spec, vec_spec, vec_spec, vec_spec,
            pl.BlockSpec((1, 1, DA_V_DIM), lambda b, h: (h, 0, 0)),
        ],
        out_specs=pl.BlockSpec((S, DA_V_DIM), lambda b, h: (b, h)),
        out_shape=jax.ShapeDtypeStruct((M, H * DA_V_DIM), BF16),
        compiler_params=_params(("arbitrary", "arbitrary")),
        name="diff_attn",
    )(proj, proj, proj, proj, vec(lq1), vec(lk1), vec(lq2), vec(lk2), norm_g.reshape(H, 1, DA_V_DIM))


def _ple_tail(x1_ref, ssq, p_ref, gw_ref, pw_ref, xg_ref, emit):
    D = x1_ref.shape[1]
    inv = lax.rsqrt(ssq * (1.0 / D) + EPS)
    p_bf = p_ref[...].astype(BF16)
    for n0 in range(0, D, TAIL_NC):
        cols = slice(n0, n0 + TAIL_NC)
        gate = _sigmoid(jnp.dot(xg_ref[...], gw_ref[:, cols].astype(BF16), preferred_element_type=F32) * inv)
        pp = jnp.dot(p_bf, pw_ref[:, cols].astype(BF16), preferred_element_type=F32)
        emit(n0, x1_ref[:, cols] + gate * pp)


def _out_ple_kernel(n_lhs, has_final_norm, *refs):
    lhs_refs = refs[:n_lhs]
    x_ref, p_ref, wo_ref, png_ref, gw_ref, pw_ref = refs[n_lhs:n_lhs + 6]
    if has_final_norm:
        fng_ref, o_ref, x1_ref, xn_ref = refs[n_lhs + 6:]
    else:
        o_ref, x1_ref, xn_ref = refs[n_lhs + 6:]
    TM, D = x_ref.shape

    ssq = jnp.zeros((TM, 1), F32)
    for n0 in range(0, D, TAIL_NC):
        cols = slice(n0, n0 + TAIL_NC)
        y = x_ref[:, cols]
        k0 = 0
        for lhs_ref in lhs_refs:
            k1 = k0 + lhs_ref.shape[1]
            y = y + jnp.dot(lhs_ref[...], wo_ref[k0:k1, cols].astype(BF16), preferred_element_type=F32)
            k0 = k1
        x1_ref[:, cols] = y
        xn_ref[:, cols] = (y * png_ref[:, cols]).astype(BF16)
        ssq = ssq + jnp.sum(y * y, axis=1, keepdims=True)

    ssq_out = [jnp.zeros((TM, 1), F32)]

    def emit(n0, val):
        o_ref[:, n0:n0 + TAIL_NC] = val
        if has_final_norm:
            ssq_out[0] = ssq_out[0] + jnp.sum(val * val, axis=1, keepdims=True)

    _ple_tail(x1_ref, ssq, p_ref, gw_ref, pw_ref, xn_ref, emit)

    if has_final_norm:
        inv = lax.rsqrt(ssq_out[0] * (1.0 / D) + EPS)
        for n0 in range(0, D, TAIL_NC):
            cols = slice(n0, n0 + TAIL_NC)
            o_ref[:, cols] = o_ref[:, cols] * inv * fng_ref[:, cols]


def _out_ple(lhs_list, x2d, p3d, layer, w_out, ple_g, gate_w3d, proj_w3d, final_g=None, name="out_ple"):
    M, D = x2d.shape
    P = p3d.shape[2]
    TM = TAIL_TM
    row = lambda i: (i, 0)
    fixed = lambda i: (0, 0)
    layer_block = lambda i: (layer, 0, 0)
    in_specs = [pl.BlockSpec((TM, lhs.shape[1]), row) for lhs in lhs_list]
    in_specs += [pl.BlockSpec((TM, D), row),
                 pl.BlockSpec((None, TM, P), lambda i: (layer, i, 0)),
                 _resident(w_out.shape, fixed),
                 _resident((1, D), fixed),
                 pl.BlockSpec((None, D, D), layer_block, pipeline_mode=pl.Buffered(1)),
                 pl.BlockSpec((None, P, D), layer_block, pipeline_mode=pl.Buffered(1))]
    args = [*lhs_list, x2d, p3d, w_out, ple_g.reshape(1, D), gate_w3d, proj_w3d]
    if final_g is not None:
        in_specs.append(_resident((1, D), fixed))
        args.append(final_g.reshape(1, D))
    return pl.pallas_call(
        functools.partial(_out_ple_kernel, len(lhs_list), final_g is not None),
        grid=(M // TM,),
        in_specs=in_specs,
        out_specs=pl.BlockSpec((TM, D), row),
        out_shape=jax.ShapeDtypeStruct((M, D), F32),
        scratch_shapes=[pltpu.VMEM((TM, D), F32), pltpu.VMEM((TM, D), BF16)],
        compiler_params=_params(("arbitrary",)),
        name=name,
    )(*args)


def kernel(x, p, positions, ab_norm_g, ab_w_in, ab_conv_w, ab_conv_b, ab_i_bias, ab_f_bias, ab_ml_norm_g,
           ab_lam_q1, ab_lam_k1, ab_lam_q2, ab_lam_k2, ab_da_norm_g, ab_w_out, c_norm_g, c_w_in, c_ln_g,
           c_ln_b, c_sgu_w, c_sgu_b, c_w_out, ple_norm_g, ple_gate_w, ple_proj_w, final_norm_g):
    B, S, D = x.shape
    M = B * S
    assert ab_norm_g.shape[0] == 1 and c_norm_g.shape[0] == 1 and p.shape[0] == 2
    x2d = x.reshape(M, D)
    p3d = p.reshape(2, M, p.shape[-1])

    cos2, sin2 = _rope_tables(positions)
    proj, gates = _in_proj_ab(x2d, ab_norm_g[0], ab_w_in[0].T, cos2, sin2)

    a_out = _mlstm(proj, gates, ab_conv_w[0], ab_conv_b[0], ab_i_bias[0], ab_f_bias[0], ab_ml_norm_g[0], B, S)

    lam_init = 0.8 - 0.6 * math.exp(-0.3 * 0)
    b_out = _diff_attn(proj, ab_lam_q1[0], ab_lam_k1[0], ab_lam_q2[0], ab_lam_k2[0], ab_da_norm_g[0],
                       lam_init, B, S)

    x2 = _out_ple([a_out, b_out], x2d, p3d, 0, ab_w_out[0], ple_norm_g[0], ple_gate_w, ple_proj_w,
                  name="out_ple_ab")

    t = _in_proj_sgu(x2, c_norm_g[0], c_w_in[0], c_ln_g[0], c_ln_b[0], c_sgu_w[0], c_sgu_b[0])
    out = _out_ple([t], x2, p3d, 1, c_w_out[0], ple_norm_g[1], ple_gate_w, ple_proj_w,
                   final_g=final_norm_g, name="out_ple_c")
    return out.reshape(B, S, D)
```

```python
import functools
import math

import jax
import jax.numpy as jnp
from jax import lax
from jax.experimental import pallas as pl
from jax.experimental.pallas import tpu as pltpu

F32 = jnp.float32
BF16 = jnp.bfloat16
HIGHEST = lax.Precision.HIGHEST

EPS = 1e-6
ROPE_THETA = 10000.0
CONV_WIDTH = 4

ML_HEADS = 4
ML_QK_DIM = 128
ML_V_DIM = 256
DA_HEADS = 4
DA_QK_DIM = 128
DA_V_DIM = 256
SGU_CHUNK = 128
SGU_GROUPS = 8
SGU_GROUP_DIM = 256

ML_Q_OFF, ML_K_OFF, ML_V_OFF, ML_O_OFF, ML_Z_OFF = 0, 512, 1024, 2048, 3072
DA_Q_OFF, DA_K_OFF, DA_V_OFF, DA_Z_OFF = 4096, 5120, 6144, 7168
GATE_COLS = 128
GATE_START = ML_V_OFF + ML_HEADS * ML_V_DIM

VMEM_LIMIT_BYTES = 56 * 1024 * 1024

PROJ_TM = 1024
PROJ_TN = 1024
NORM_ROWS = 128
TAIL_TM = 256
TAIL_NC = 512
PROJ_SLAB = 256
CONV_ROWS = 128
ML_BLOCK = 256
ATT_QSUB = 256


def _params(semantics):
    return pltpu.CompilerParams(dimension_semantics=semantics, vmem_limit_bytes=VMEM_LIMIT_BYTES)


def _resident(shape, index_map):
    return pl.BlockSpec(shape, index_map, pipeline_mode=pl.Buffered(1))


def _sigmoid(x):
    return 1.0 / (1.0 + jnp.exp(-x))


def _silu(x):
    return x * _sigmoid(x)


def _log_sigmoid(x):
    return jnp.minimum(x, 0.0) - jnp.log(1.0 + jnp.exp(-jnp.abs(x)))


def _iota(shape, axis):
    return lax.broadcasted_iota(jnp.int32, shape, axis)


def _rope_table_kernel(pos_ref, cos_ref, sin_ref):
    pos = pos_ref[0]
    half_rows = pos.shape[0]
    lane = _iota(pos.shape, 1)
    lo = lane < 64
    j = jnp.where(lo, lane, lane - 64).astype(F32)
    freq = jnp.exp(j * (-math.log(ROPE_THETA) / 64.0))
    ang = pos * freq
    c = jnp.cos(ang)
    s = jnp.sin(ang)
    cr = pltpu.roll(c, 64, 1)
    sr = pltpu.roll(s, 64, 1)
    cos_ref[0, 0:half_rows, :] = jnp.where(lo, c, cr)
    cos_ref[0, half_rows:2 * half_rows, :] = jnp.where(lo, cr, c)
    sin_ref[0, 0:half_rows, :] = jnp.where(lo, -s, sr)
    sin_ref[0, half_rows:2 * half_rows, :] = jnp.where(lo, -sr, s)


def _rope_tables(positions):
    B, S = positions.shape
    posf = positions.astype(F32)
    lo = jnp.broadcast_to(posf[:, :S // 2, None], (B, S // 2, 64))
    hi = jnp.broadcast_to(posf[:, S // 2:, None], (B, S // 2, 64))
    pos2 = jnp.concatenate([lo, hi], axis=-1)
    return pl.pallas_call(
        _rope_table_kernel,
        grid=(B,),
        in_specs=[pl.BlockSpec((1, S // 2, 128), lambda b: (b, 0, 0))],
        out_specs=[pl.BlockSpec((1, S, 128), lambda b: (b, 0, 0)),
                   pl.BlockSpec((1, S, 128), lambda b: (b, 0, 0))],
        out_shape=[jax.ShapeDtypeStruct((B, S, 128), F32)] * 2,
        compiler_params=_params(("arbitrary",)),
        name="rope_tables",
    )(pos2)


def _normalize_rows(x_ref, g_ref, xn_ref):
    g = g_ref[...]
    for r0 in range(0, x_ref.shape[0], NORM_ROWS):
        rows = slice(r0, r0 + NORM_ROWS)
        x = x_ref[rows, :]
        ms = jnp.mean(x * x, axis=-1, keepdims=True)
        xn_ref[rows, :] = (x * lax.rsqrt(ms + EPS) * g).astype(BF16)


def _x_tile_index(n_tiles, i, j):
    return jnp.minimum(i + (j >= 1), n_tiles - 1), 0


_NT = (((1,), (1,)), ((), ()))


def _in_proj_ab_kernel(n_gate, x_ref, g_ref, wt_ref, edge_ref, wgt_ref, cos_ref, sin_ref, o_ref, og_ref,
                       xn_ref):
    j = pl.program_id(1)
    TN, D = wt_ref.shape
    rope_q_tile, rope_k_tile = DA_Q_OFF // TN, DA_K_OFF // TN
    first_shifted = GATE_START // TN

    @pl.when(j == 0)
    def _():
        _normalize_rows(x_ref, g_ref, xn_ref)
        wg = jnp.concatenate([wgt_ref[...], jnp.zeros((16 - n_gate, D), F32)], axis=0).astype(BF16)
        og_ref[...] = lax.dot_general(wg, xn_ref[...], _NT, preferred_element_type=F32)[0:n_gate, :]

    def slabs(shift):
        for r0 in range(0, TN, PROJ_SLAB):
            lo, hi = r0 + shift, r0 + PROJ_SLAB + shift
            if hi <= TN:
                w = wt_ref[lo:hi, :]
            else:
                w = jnp.concatenate([wt_ref[lo:TN, :], edge_ref[...]], axis=0)
            yield r0, lax.dot_general(xn_ref[...], w.astype(BF16), _NT, preferred_element_type=F32)

    def plain(shift):
        for r0, y in slabs(shift):
            o_ref[:, r0:r0 + PROJ_SLAB] = y.astype(o_ref.dtype)

    def rope(shift, scale):
        cs = cos_ref[...]
        sn = sin_ref[...]
        for r0, y in slabs(shift):
            for c0 in range(0, PROJ_SLAB, DA_QK_DIM):
                x = y[:, c0:c0 + DA_QK_DIM]
                r = x * cs + pltpu.roll(x, DA_QK_DIM // 2, 1) * sn
                if scale is not None:
                    r = r * scale
                o_ref[:, r0 + c0:r0 + c0 + DA_QK_DIM] = r.astype(o_ref.dtype)

    @pl.when(j < first_shifted)
    def _():
        plain(0)

    @pl.when((j >= first_shifted) & (j != rope_q_tile) & (j != rope_k_tile))
    def _():
        plain(n_gate)

    @pl.when(j == rope_q_tile)
    def _():
        rope(n_gate, DA_QK_DIM ** -0.5 * math.log2(math.e))

    @pl.when(j == rope_k_tile)
    def _():
        rope(n_gate, None)


def _in_proj_ab(x2d, g, w_t, cos2, sin2):
    M, D = x2d.shape
    n_gate = 2 * ML_HEADS
    N = w_t.shape[0] - n_gate
    TM, TN = PROJ_TM, PROJ_TN
    assert n_gate == 8 and GATE_START % TN == 0 and N % TN == 0
    assert DA_Q_OFF % TN == 0 and DA_K_OFF - DA_Q_OFF == TN and DA_V_OFF - DA_K_OFF == TN
    fixed = lambda i, j: (0, 0)
    return pl.pallas_call(
        functools.partial(_in_proj_ab_kernel, n_gate),
        grid=(M // TM, N // TN),
        in_specs=[pl.BlockSpec((TM, D), functools.partial(_x_tile_index, M // TM)),
                  pl.BlockSpec((1, D), fixed),
                  pl.BlockSpec((TN, D), lambda i, j: (j, 0)),
                  pl.BlockSpec((n_gate, D), lambda i, j: ((j + 1) * (TN // n_gate), 0)),
                  pl.BlockSpec((n_gate, D), lambda i, j: (GATE_START // n_gate, 0)),
                  pl.BlockSpec((TM, DA_QK_DIM), lambda i, j: (i, 0)),
                  pl.BlockSpec((TM, DA_QK_DIM), lambda i, j: (i, 0))],
        out_specs=[pl.BlockSpec((TM, TN), lambda i, j: (i, j)),
                   pl.BlockSpec((n_gate, TM), lambda i, j: (0, i))],
        out_shape=[jax.ShapeDtypeStruct((M, N), BF16), jax.ShapeDtypeStruct((n_gate, M), F32)],
        scratch_shapes=[pltpu.VMEM((TM, D), BF16)],
        compiler_params=_params(("arbitrary", "arbitrary")),
        name="in_proj_ab",
    )(x2d, g.reshape(1, D), w_t, w_t, w_t, cos2.reshape(M, DA_QK_DIM), sin2.reshape(M, DA_QK_DIM))


_SGU_STEP_BLOCKS = (2, 3, 0, 4, 1, 5)


def _sgu_step_block(j):
    blk = jnp.int32(_SGU_STEP_BLOCKS[-1])
    for step in range(len(_SGU_STEP_BLOCKS) - 2, -1, -1):
        blk = jnp.where(j == step, _SGU_STEP_BLOCKS[step], blk)
    return blk


def _in_proj_sgu_kernel(x_ref, g_ref, w_ref, lng_ref, lnb_ref, sw_ref, sbt_ref, t_ref, xn_ref, v_ref, u_ref,
                        st_ref):
    j = pl.program_id(1)
    TM = x_ref.shape[0]
    TN = w_ref.shape[1]
    W = v_ref.shape[1]
    L = SGU_CHUNK
    GD = SGU_GROUP_DIM
    groups_per_step = TN // GD

    def proj_cols(cols):
        return jnp.dot(xn_ref[...], w_ref[:, cols].astype(BF16), preferred_element_type=F32)

    def project_v(half):
        for gl in range(groups_per_step):
            cols = slice(gl * GD, (gl + 1) * GD)
            slab = proj_cols(cols)
            v_ref[:, half * TN + gl * GD:half * TN + (gl + 1) * GD] = slab.astype(BF16)
            if half == 0 and gl == 0:
                shift = jnp.mean(slab, axis=1, keepdims=True)
                s1 = jnp.zeros((TM, 1), F32)
                s2 = jnp.zeros((TM, 1), F32)
            elif gl == 0:
                shift, s1, s2 = st_ref[0], st_ref[1], st_ref[2]
            d = slab - shift
            s1 = s1 + jnp.sum(d, axis=1, keepdims=True)
            s2 = s2 + jnp.sum(d * d, axis=1, keepdims=True)
        if half == 0:
            st_ref[0], st_ref[1], st_ref[2] = shift, s1, s2
        else:
            m1 = s1 * (1.0 / W)
            st_ref[0] = shift + m1
            st_ref[1] = lax.rsqrt(jnp.maximum(s2 * (1.0 / W) - m1 * m1, 0.0) + EPS)

    def gate(half):
        causal = _iota((L, L), 1) <= _iota((L, L), 0)
        for gl in range(groups_per_step):
            g = half * groups_per_step + gl
            cols = slice(gl * GD, (gl + 1) * GD)
            vcols = slice(g * GD, (g + 1) * GD)
            z = proj_cols(cols)
            w_causal = jnp.where(causal, sw_ref[g], 0.0).astype(BF16)
            bias = sbt_ref[:, g:g + 1]
            lng = lng_ref[:, vcols]
            lnb = lnb_ref[:, vcols]
            for c in range(TM // L):
                rows = slice(c * L, (c + 1) * L)
                vn = ((v_ref[rows, vcols].astype(F32) - st_ref[0, rows, :]) * st_ref[1, rows, :] * lng
                      + lnb).astype(BF16)
                sv = jnp.dot(w_causal, vn, preferred_element_type=F32) + bias
                t = u_ref[rows, cols].astype(F32) * sv * _silu(z[rows, :])
                t_ref[rows, cols] = t.astype(BF16)

    @pl.when(j == 0)
    def _():
        _normalize_rows(x_ref, g_ref, xn_ref)
        project_v(0)

    @pl.when(j == 1)
    def _():
        project_v(1)

    @pl.when((j == 2) | (j == 4))
    def _():
        for gl in range(groups_per_step):
            cols = slice(gl * GD, (gl + 1) * GD)
            u_ref[:, cols] = proj_cols(cols).astype(BF16)

    @pl.when(j == 3)
    def _():
        gate(0)

    @pl.when(j == 5)
    def _():
        gate(1)


def _in_proj_sgu(x2d, g, w, ln_g, ln_b, sgu_w, sgu_b):
    M, D = x2d.shape
    W = w.shape[1] // 3
    assert W == 2 * PROJ_TN and len(_SGU_STEP_BLOCKS) == 3 * W // PROJ_TN
    fixed = lambda i, j: (0, 0)
    return pl.pallas_call(
        _in_proj_sgu_kernel,
        grid=(M // PROJ_TM, len(_SGU_STEP_BLOCKS)),
        in_specs=[pl.BlockSpec((PROJ_TM, D), lambda i, j: (i, 0)),
                  pl.BlockSpec((1, D), fixed),
                  pl.BlockSpec((D, PROJ_TN), lambda i, j: (0, _sgu_step_block(j))),
                  pl.BlockSpec((1, W), fixed),
                  pl.BlockSpec((1, W), fixed),
                  pl.BlockSpec(sgu_w.shape, lambda i, j: (0, 0, 0)),
                  pl.BlockSpec((SGU_CHUNK, SGU_GROUPS), fixed)],
        out_specs=pl.BlockSpec((PROJ_TM, PROJ_TN), lambda i, j: (i, jnp.where(j <= 3, 0, 1))),
        out_shape=jax.ShapeDtypeStruct((M, W), BF16),
        scratch_shapes=[pltpu.VMEM((PROJ_TM, D), BF16),
                        pltpu.VMEM((PROJ_TM, W), BF16),
                        pltpu.VMEM((PROJ_TM, PROJ_TN), BF16),
                        pltpu.VMEM((3, PROJ_TM, 1), F32)],
        compiler_params=_params(("arbitrary", "arbitrary")),
        name="in_proj_sgu",
    )(x2d, g.reshape(1, D), w, ln_g.reshape(1, W), ln_b.reshape(1, W), sgu_w, sgu_b.T)


def _mlstm_kernel(qp_ref, kp_ref, v_ref, gt_ref, og_ref, z_ref, cwq_ref, cwk_ref, cbq_ref, cbk_ref,
                  gb_ref, ng_ref, out_ref, xs_ref, qc_ref, kc_ref):
    h = pl.program_id(1)
    S = qp_ref.shape[0]
    LB = ML_BLOCK
    nb = S // LB
    DK, DV = ML_QK_DIM, ML_V_DIM

    for src, cw_ref, cb_ref, dst, scale in ((qp_ref, cwq_ref, cbq_ref, qc_ref, 1.0),
                                            (kp_ref, cwk_ref, cbk_ref, kc_ref, DK ** -0.5)):
        xs_ref[0:8, :] = jnp.zeros((8, DK), F32)
        xs_ref[8:8 + S, :] = src[...].astype(F32)
        cw = cw_ref[...]
        cb = cb_ref[...]
        for c in range(S // CONV_ROWS):
            acc = cb
            for j in range(CONV_WIDTH):
                start = 8 + c * CONV_ROWS - (CONV_WIDTH - 1) + j
                acc = acc + cw[j:j + 1, :] * xs_ref[start:start + CONV_ROWS, :]
            dst[c * CONV_ROWS:(c + 1) * CONV_ROWS, :] = (_silu(acc) * scale).astype(BF16)

    gate_rows = gt_ref[...] + jnp.concatenate([gb_ref[...]] * (S // GATE_COLS), axis=1)
    sub8 = _iota((8, S), 0)
    i_all = jnp.sum(jnp.where(sub8 == h, gate_rows, 0.0), axis=0, keepdims=True)
    logf_all = _log_sigmoid(jnp.sum(jnp.where(sub8 == h + ML_HEADS, gate_rows, 0.0), axis=0, keepdims=True))

    blk = _iota((nb, LB), 0)

    def to_blocks(row):
        tile = jnp.zeros((nb, LB), F32)
        for c in range(nb):
            tile = jnp.where(blk == c, jnp.broadcast_to(row[:, c * LB:(c + 1) * LB], (nb, LB)), tile)
        return tile

    sub = _iota((LB, LB), 0)
    lane = _iota((LB, LB), 1)
    causal = lane <= sub
    i_b = to_blocks(i_all)
    b_b = jnp.dot(to_blocks(logf_all), (sub <= lane).astype(F32), preferred_element_type=F32,
                  precision=HIGHEST)
    g_tot = b_b[:, LB - 1:LB]
    w_b = g_tot - b_b + i_b
    m_loc = jnp.max(w_b, axis=1, keepdims=True)
    e_b = jnp.exp(w_b - m_loc)
    b_cols = jnp.concatenate([b_b, jnp.zeros((GATE_COLS - nb, LB), F32)], axis=0).T

    norm_g = ng_ref[0]
    ct = jnp.zeros((DK, DV), F32)
    n = jnp.zeros((DK, 1), F32)
    m = jnp.zeros((1, 1), F32)
    for c in range(nb):
        rows = slice(c * LB, (c + 1) * LB)
        b_row, i_row, e_row = b_b[c:c + 1, :], i_b[c:c + 1, :], e_b[c:c + 1, :]
        b_col = b_cols[:, c:c + 1]
        q = qc_ref[rows, :]
        v = v_ref[rows, :]
        k_t = kc_ref[rows, :].astype(F32).T

        rhs = jnp.concatenate([k_t.astype(BF16), ct.astype(BF16),
                               jnp.broadcast_to(n, (DK, 128)).astype(BF16)], axis=1)
        big = jnp.dot(q, rhs, preferred_element_type=F32)
        qk, inter, qn = big[:, :LB], big[:, LB:LB + DV], big[:, LB + DV:LB + DV + 1]

        log_d = jnp.where(causal, b_col - b_row + i_row, -jnp.inf)
        inter_log = b_col + m
        m_t = jnp.maximum(inter_log, jnp.max(log_d, axis=1, keepdims=True))
        d_mat = jnp.exp(log_d - m_t)
        inter_w = jnp.exp(inter_log - m_t)
        s_mat = qk * d_mat
        num = jnp.dot(s_mat.astype(BF16), v, preferred_element_type=F32) + inter_w * inter
        den = jnp.sum(s_mat, axis=1, keepdims=True) + inter_w * qn
        hm = num / jnp.maximum(jnp.abs(den), jnp.exp(-m_t))

        hn = hm * lax.rsqrt(jnp.mean(hm * hm, axis=-1, keepdims=True) + EPS) * norm_g
        og = og_ref[rows, :].astype(F32)
        z = z_ref[rows, :].astype(F32)
        out_ref[rows, :] = (_sigmoid(og) * hn * _silu(z)).astype(out_ref.dtype)

        if c + 1 < nb:
            ke = k_t * e_row
            ct_loc = jnp.dot(ke.astype(BF16), v, preferred_element_type=F32)
            n_loc = jnp.sum(ke, axis=1, keepdims=True)
            g_c, ml_c = g_tot[c:c + 1, :], m_loc[c:c + 1, :]
            m_new = jnp.maximum(g_c + m, ml_c)
            a = jnp.exp(g_c + m - m_new)
            cc = jnp.exp(ml_c - m_new)
            ct = a * ct + cc * ct_loc
            n = a * n + cc * n_loc
            m = m_new


def _mlstm(proj, gates, conv_w, conv_b, i_bias, f_bias, norm_g, B, S):
    M = B * S
    H = ML_HEADS
    assert S // ML_BLOCK == 8, "block rows are packed into one 8-sublane tile"
    gate_bias = jnp.broadcast_to(jnp.concatenate([i_bias, f_bias])[:, None], (2 * H, GATE_COLS))
    conv_b2 = conv_b.reshape(1, -1)
    norm_g3 = norm_g.reshape(H, 1, ML_V_DIM)
    qb, kb = ML_Q_OFF // ML_QK_DIM, ML_K_OFF // ML_QK_DIM
    vb, ob, zb = ML_V_OFF // ML_V_DIM, ML_O_OFF // ML_V_DIM, ML_Z_OFF // ML_V_DIM
    return pl.pallas_call(
        _mlstm_kernel,
        grid=(B, H),
        in_specs=[
            pl.BlockSpec((S, ML_QK_DIM), lambda b, h: (b, qb + h)),
            pl.BlockSpec((S, ML_QK_DIM), lambda b, h: (b, kb + h)),
            pl.BlockSpec((S, ML_V_DIM), lambda b, h: (b, vb + h)),
            pl.BlockSpec((2 * H, S), lambda b, h: (0, b)),
            pl.BlockSpec((S, ML_V_DIM), lambda b, h: (b, ob + h)),
            pl.BlockSpec((S, ML_V_DIM), lambda b, h: (b, zb + h)),
            pl.BlockSpec((CONV_WIDTH, ML_QK_DIM), lambda b, h: (0, h)),
            pl.BlockSpec((CONV_WIDTH, ML_QK_DIM), lambda b, h: (0, H + h)),
            pl.BlockSpec((1, ML_QK_DIM), lambda b, h: (0, h)),
            pl.BlockSpec((1, ML_QK_DIM), lambda b, h: (0, H + h)),
            pl.BlockSpec((2 * H, GATE_COLS), lambda b, h: (0, 0)),
            pl.BlockSpec((1, 1, ML_V_DIM), lambda b, h: (h, 0, 0)),
        ],
        out_specs=pl.BlockSpec((S, ML_V_DIM), lambda b, h: (b, h)),
        out_shape=jax.ShapeDtypeStruct((M, H * ML_V_DIM), BF16),
        scratch_shapes=[pltpu.VMEM((8 + S, ML_QK_DIM), F32),
                        pltpu.VMEM((S, ML_QK_DIM), BF16),
                        pltpu.VMEM((S, ML_QK_DIM), BF16)],
        compiler_params=_params(("arbitrary", "arbitrary")),
        name="mlstm",
    )(proj, proj, proj, gates, proj, proj, conv_w, conv_w, conv_b2, conv_b2, gate_bias, norm_g3)


def _diff_attn_kernel(lam_init, qr_ref, kr_ref, v_ref, z_ref, lq1_ref, lk1_ref, lq2_ref, lk2_ref, ng_ref,
                      o_ref):
    S = kr_ref.shape[0]
    QS, DH = ATT_QSUB, DA_QK_DIM
    diag_mask = _iota((QS, QS), 1) <= _iota((QS, QS), 0)

    def softmax_pv(q0, m):
        q = qr_ref[q0:q0 + QS, m * DH:(m + 1) * DH]
        kv_len = q0 + QS
        s = lax.dot_general(q, kr_ref[0:kv_len, m * DH:(m + 1) * DH], _NT, preferred_element_type=F32)
        s_diag = jnp.where(diag_mask, s[:, q0:], -jnp.inf)
        s = s_diag if q0 == 0 else jnp.concatenate([s[:, :q0], s_diag], axis=1)
        p = jnp.exp2(s - jnp.max(s, axis=1, keepdims=True))
        acc = jnp.dot(p.astype(BF16), v_ref[0:kv_len, :], preferred_element_type=F32)
        return acc, jnp.sum(p, axis=1, keepdims=True)

    lam = (jnp.exp(jnp.sum(lq1_ref[...] * lk1_ref[...], axis=1, keepdims=True))
           - jnp.exp(jnp.sum(lq2_ref[...] * lk2_ref[...], axis=1, keepdims=True)) + lam_init)
    norm_g = ng_ref[0] * (1.0 - lam_init)

    for q0 in range(0, S, QS):
        acc1, l1 = softmax_pv(q0, 0)
        acc2, l2 = softmax_pv(q0, 1)
        o = acc1 / l1 - lam * (acc2 / l2)
        hn = o * lax.rsqrt(jnp.mean(o * o, axis=-1, keepdims=True) + EPS) * norm_g
        o_ref[q0:q0 + QS, :] = (hn * _silu(z_ref[q0:q0 + QS, :].astype(F32))).astype(o_ref.dtype)


def _diff_attn(proj, lq1, lk1, lq2, lk2, norm_g, lam_init, B, S):
    M = B * S
    H = DA_HEADS
    qb, kb, vb, zb = (DA_Q_OFF // DA_V_DIM, DA_K_OFF // DA_V_DIM, DA_V_OFF // DA_V_DIM, DA_Z_OFF // DA_V_DIM)
    vec = lambda a: a.reshape(1, DA_QK_DIM)
    vec_spec = pl.BlockSpec((1, DA_QK_DIM), lambda b, h: (0, 0))
    return pl.pallas_call(
        functools.partial(_diff_attn_kernel, lam_init),
        grid=(B, H),
        in_specs=[
            pl.BlockSpec((S, DA_V_DIM), lambda b, h: (b, qb + h)),
            pl.BlockSpec((S, DA_V_DIM), lambda b, h: (b, kb + h)),
            pl.BlockSpec((S, DA_V_DIM), lambda b, h: (b, vb + h)),
            pl.BlockSpec((S, DA_V_DIM), lambda b, h: (b, zb + h)),
            vec_spec, vec_spec, vec_spec, vec_spec,
            pl.BlockSpec((1, 1, DA_V_DIM), lambda b, h: (h, 0, 0)),
        ],
        out_specs=pl.BlockSpec((S, DA_V_DIM), lambda b, h: (b, h)),
        out_shape=jax.ShapeDtypeStruct((M, H * DA_V_DIM), BF16),
        compiler_params=_params(("arbitrary", "arbitrary")),
        name="diff_attn",
    )(proj, proj, proj, proj, vec(lq1), vec(lk1), vec(lq2), vec(lk2), norm_g.reshape(H, 1, DA_V_DIM))


def _ple_tail(x1_ref, ssq, p_ref, gw_ref, pw_ref, xg_ref, emit):
    D = x1_ref.shape[1]
    inv = lax.rsqrt(ssq * (1.0 / D) + EPS)
    p_bf = p_ref[...].astype(BF16)
    for n0 in range(0, D, TAIL_NC):
        cols = slice(n0, n0 + TAIL_NC)
        gate = _sigmoid(jnp.dot(xg_ref[...], gw_ref[:, cols].astype(BF16), preferred_element_type=F32) * inv)
        pp = jnp.dot(p_bf, pw_ref[:, cols].astype(BF16), preferred_element_type=F32)
        emit(n0, x1_ref[:, cols] + gate * pp)


def _out_ple_kernel(n_lhs, has_final_norm, *refs):
    lhs_refs = refs[:n_lhs]
    x_ref, p_ref, wo_ref, png_ref, gw_ref, pw_ref = refs[n_lhs:n_lhs + 6]
    if has_final_norm:
        fng_ref, o_ref, x1_ref, xn_ref = refs[n_lhs + 6:]
    else:
        o_ref, x1_ref, xn_ref = refs[n_lhs + 6:]
    TM, D = x_ref.shape

    ssq = jnp.zeros((TM, 1), F32)
    for n0 in range(0, D, TAIL_NC):
        cols = slice(n0, n0 + TAIL_NC)
        y = x_ref[:, cols]
        k0 = 0
        for lhs_ref in lhs_refs:
            k1 = k0 + lhs_ref.shape[1]
            y = y + jnp.dot(lhs_ref[...], wo_ref[k0:k1, cols].astype(BF16), preferred_element_type=F32)
            k0 = k1
        x1_ref[:, cols] = y
        xn_ref[:, cols] = (y * png_ref[:, cols]).astype(BF16)
        ssq = ssq + jnp.sum(y * y, axis=1, keepdims=True)

    ssq_out = [jnp.zeros((TM, 1), F32)]

    def emit(n0, val):
        o_ref[:, n0:n0 + TAIL_NC] = val
        if has_final_norm:
            ssq_out[0] = ssq_out[0] + jnp.sum(val * val, axis=1, keepdims=True)

    _ple_tail(x1_ref, ssq, p_ref, gw_ref, pw_ref, xn_ref, emit)

    if has_final_norm:
        inv = lax.rsqrt(ssq_out[0] * (1.0 / D) + EPS)
        for n0 in range(0, D, TAIL_NC):
            cols = slice(n0, n0 + TAIL_NC)
            o_ref[:, cols] = o_ref[:, cols] * inv * fng_ref[:, cols]


def _out_ple(lhs_list, x2d, p3d, layer, w_out, ple_g, gate_w3d, proj_w3d, final_g=None, name="out_ple"):
    M, D = x2d.shape
    P = p3d.shape[2]
    TM = TAIL_TM
    row = lambda i: (i, 0)
    fixed = lambda i: (0, 0)
    layer_block = lambda i: (layer, 0, 0)
    in_specs = [pl.BlockSpec((TM, lhs.shape[1]), row) for lhs in lhs_list]
    in_specs += [pl.BlockSpec((TM, D), row),
                 pl.BlockSpec((None, TM, P), lambda i: (layer, i, 0)),
                 _resident(w_out.shape, fixed),
                 _resident((1, D), fixed),
                 pl.BlockSpec((None, D, D), layer_block, pipeline_mode=pl.Buffered(1)),
                 pl.BlockSpec((None, P, D), layer_block, pipeline_mode=pl.Buffered(1))]
    args = [*lhs_list, x2d, p3d, w_out, ple_g.reshape(1, D), gate_w3d, proj_w3d]
    if final_g is not None:
        in_specs.append(_resident((1, D), fixed))
        args.append(final_g.reshape(1, D))
    return pl.pallas_call(
        functools.partial(_out_ple_kernel, len(lhs_list), final_g is not None),
        grid=(M // TM,),
        in_specs=in_specs,
        out_specs=pl.BlockSpec((TM, D), row),
        out_shape=jax.ShapeDtypeStruct((M, D), F32),
        scratch_shapes=[pltpu.VMEM((TM, D), F32), pltpu.VMEM((TM, D), BF16)],
        compiler_params=_params(("arbitrary",)),
        name=name,
    )(*args)


def kernel(x, p, positions, ab_norm_g, ab_w_in, ab_conv_w, ab_conv_b, ab_i_bias, ab_f_bias, ab_ml_norm_g,
           ab_lam_q1, ab_lam_k1, ab_lam_q2, ab_lam_k2, ab_da_norm_g, ab_w_out, c_norm_g, c_w_in, c_ln_g,
           c_ln_b, c_sgu_w, c_sgu_b, c_w_out, ple_norm_g, ple_gate_w, ple_proj_w, final_norm_g):
    B, S, D = x.shape
    M = B * S
    assert ab_norm_g.shape[0] == 1 and c_norm_g.shape[0] == 1 and p.shape[0] == 2
    x2d = x.reshape(M, D)
    p3d = p.reshape(2, M, p.shape[-1])

    cos2, sin2 = _rope_tables(positions)
    proj, gates = _in_proj_ab(x2d, ab_norm_g[0], ab_w_in[0].T, cos2, sin2)

    a_out = _mlstm(proj, gates, ab_conv_w[0], ab_conv_b[0], ab_i_bias[0], ab_f_bias[0], ab_ml_norm_g[0], B, S)

    lam_init = 0.8 - 0.6 * math.exp(-0.3 * 0)
    b_out = _diff_attn(proj, ab_lam_q1[0], ab_lam_k1[0], ab_lam_q2[0], ab_lam_k2[0], ab_da_norm_g[0],
                       lam_init, B, S)

    x2 = _out_ple([a_out, b_out], x2d, p3d, 0, ab_w_out[0], ple_norm_g[0], ple_gate_w, ple_proj_w,
                  name="out_ple_ab")

    t = _in_proj_sgu(x2, c_norm_g[0], c_w_in[0], c_ln_g[0], c_ln_b[0], c_sgu_w[0], c_sgu_b[0])
    out = _out_ple([t], x2, p3d, 1, c_w_out[0], ple_norm_g[1], ple_gate_w, ple_proj_w,
                   final_g=final_norm_g, name="out_ple_c")
    return out.reshape(B, S, D)
```

```python
import functools
import math

import jax
import jax.numpy as jnp
from jax import lax
from jax.experimental import pallas as pl
from jax.experimental.pallas import tpu as pltpu

F32 = jnp.float32
BF16 = jnp.bfloat16
HIGHEST = lax.Precision.HIGHEST

EPS = 1e-6
ROPE_THETA = 10000.0
CONV_WIDTH = 4

ML_HEADS = 4
ML_QK_DIM = 128
ML_V_DIM = 256
DA_HEADS = 4
DA_QK_DIM = 128
DA_V_DIM = 256
SGU_CHUNK = 128
SGU_GROUPS = 8
SGU_GROUP_DIM = 256

ML_Q_OFF, ML_K_OFF, ML_V_OFF, ML_O_OFF, ML_Z_OFF = 0, 512, 1024, 2048, 3072
DA_Q_OFF, DA_K_OFF, DA_V_OFF, DA_Z_OFF = 4096, 5120, 6144, 7168
GATE_COLS = 128
GATE_START = ML_V_OFF + ML_HEADS * ML_V_DIM

VMEM_LIMIT_BYTES = 56 * 1024 * 1024

PROJ_TM = 1024
PROJ_TN = 1024
NORM_ROWS = 128
TAIL_TM = 256
TAIL_NC = 512
PROJ_SLAB = 256
CONV_ROWS = 128
ML_BLOCK = 256
ATT_QSUB = 256


def _params(semantics):
    return pltpu.CompilerParams(dimension_semantics=semantics, vmem_limit_bytes=VMEM_LIMIT_BYTES)


def _resident(shape, index_map):
    return pl.BlockSpec(shape, index_map, pipeline_mode=pl.Buffered(1))


def _sigmoid(x):
    return 1.0 / (1.0 + jnp.exp(-x))


def _silu(x):
    return x * _sigmoid(x)


def _log_sigmoid(x):
    return jnp.minimum(x, 0.0) - jnp.log(1.0 + jnp.exp(-jnp.abs(x)))


def _iota(shape, axis):
    return lax.broadcasted_iota(jnp.int32, shape, axis)


def _rope_table_kernel(pos_ref, cos_ref, sin_ref):
    pos = pos_ref[0]
    half_rows = pos.shape[0]
    lane = _iota(pos.shape, 1)
    lo = lane < 64
    j = jnp.where(lo, lane, lane - 64).astype(F32)
    freq = jnp.exp(j * (-math.log(ROPE_THETA) / 64.0))
    ang = pos * freq
    c = jnp.cos(ang)
    s = jnp.sin(ang)
    cr = pltpu.roll(c, 64, 1)
    sr = pltpu.roll(s, 64, 1)
    cos_ref[0, 0:half_rows, :] = jnp.where(lo, c, cr)
    cos_ref[0, half_rows:2 * half_rows, :] = jnp.where(lo, cr, c)
    sin_ref[0, 0:half_rows, :] = jnp.where(lo, -s, sr)
    sin_ref[0, half_rows:2 * half_rows, :] = jnp.where(lo, -sr, s)


def _rope_tables(positions):
    B, S = positions.shape
    posf = positions.astype(F32)
    lo = jnp.broadcast_to(posf[:, :S // 2, None], (B, S // 2, 64))
    hi = jnp.broadcast_to(posf[:, S // 2:, None], (B, S // 2, 64))
    pos2 = jnp.concatenate([lo, hi], axis=-1)
    return pl.pallas_call(
        _rope_table_kernel,
        grid=(B,),
        in_specs=[pl.BlockSpec((1, S // 2, 128), lambda b: (b, 0, 0))],
        out_specs=[pl.BlockSpec((1, S, 128), lambda b: (b, 0, 0)),
                   pl.BlockSpec((1, S, 128), lambda b: (b, 0, 0))],
        out_shape=[jax.ShapeDtypeStruct((B, S, 128), F32)] * 2,
        compiler_params=_params(("arbitrary",)),
        name="rope_tables",
    )(pos2)


def _normalize_rows(x_ref, g_ref, xn_ref):
    g = g_ref[...]
    for r0 in range(0, x_ref.shape[0], NORM_ROWS):
        rows = slice(r0, r0 + NORM_ROWS)
        x = x_ref[rows, :]
        ms = jnp.mean(x * x, axis=-1, keepdims=True)
        xn_ref[rows, :] = (x * lax.rsqrt(ms + EPS) * g).astype(BF16)


def _x_tile_index(n_tiles, i, j):
    return jnp.minimum(i + (j >= 1), n_tiles - 1), 0


_NT = (((1,), (1,)), ((), ()))


def _in_proj_ab_kernel(n_gate, x_ref, g_ref, wt_ref, edge_ref, wgt_ref, cos_ref, sin_ref, o_ref, og_ref,
                       xn_ref):
    j = pl.program_id(1)
    TN, D = wt_ref.shape
    rope_q_tile, rope_k_tile = DA_Q_OFF // TN, DA_K_OFF // TN
    first_shifted = GATE_START // TN

    @pl.when(j == 0)
    def _():
        _normalize_rows(x_ref, g_ref, xn_ref)
        wg = jnp.concatenate([wgt_ref[...], jnp.zeros((16 - n_gate, D), F32)], axis=0).astype(BF16)
        og_ref[...] = lax.dot_general(wg, xn_ref[...], _NT, preferred_element_type=F32)[0:n_gate, :]

    def slabs(shift):
        for r0 in range(0, TN, PROJ_SLAB):
            lo, hi = r0 + shift, r0 + PROJ_SLAB + shift
            if hi <= TN:
                w = wt_ref[lo:hi, :]
            else:
                w = jnp.concatenate([wt_ref[lo:TN, :], edge_ref[...]], axis=0)
            yield r0, lax.dot_general(xn_ref[...], w.astype(BF16), _NT, preferred_element_type=F32)

    def plain(shift):
        for r0, y in slabs(shift):
            o_ref[:, r0:r0 + PROJ_SLAB] = y.astype(o_ref.dtype)

    def rope(shift, scale):
        cs = cos_ref[...]
        sn = sin_ref[...]
        for r0, y in slabs(shift):
            for c0 in range(0, PROJ_SLAB, DA_QK_DIM):
                x = y[:, c0:c0 + DA_QK_DIM]
                r = x * cs + pltpu.roll(x, DA_QK_DIM // 2, 1) * sn
                if scale is not None:
                    r = r * scale
                o_ref[:, r0 + c0:r0 + c0 + DA_QK_DIM] = r.astype(o_ref.dtype)

    @pl.when(j < first_shifted)
    def _():
        plain(0)

    @pl.when((j >= first_shifted) & (j != rope_q_tile) & (j != rope_k_tile))
    def _():
        plain(n_gate)

    @pl.when(j == rope_q_tile)
    def _():
        rope(n_gate, DA_QK_DIM ** -0.5 * math.log2(math.e))

    @pl.when(j == rope_k_tile)
    def _():
        rope(n_gate, None)


def _in_proj_ab(x2d, g, w_t, cos2, sin2):
    M, D = x2d.shape
    n_gate = 2 * ML_HEADS
    N = w_t.shape[0] - n_gate
    TM, TN = PROJ_TM, PROJ_TN
    assert n_gate == 8 and GATE_START % TN == 0 and N % TN == 0
    assert DA_Q_OFF % TN == 0 and DA_K_OFF - DA_Q_OFF == TN and DA_V_OFF - DA_K_OFF == TN
    fixed = lambda i, j: (0, 0)
    return pl.pallas_call(
        functools.partial(_in_proj_ab_kernel, n_gate),
        grid=(M // TM, N // TN),
        in_specs=[pl.BlockSpec((TM, D), functools.partial(_x_tile_index, M // TM)),
                  pl.BlockSpec((1, D), fixed),
                  pl.BlockSpec((TN, D), lambda i, j: (j, 0)),
                  pl.BlockSpec((n_gate, D), lambda i, j: ((j + 1) * (TN // n_gate), 0)),
                  pl.BlockSpec((n_gate, D), lambda i, j: (GATE_START // n_gate, 0)),
                  pl.BlockSpec((TM, DA_QK_DIM), lambda i, j: (i, 0)),
                  pl.BlockSpec((TM, DA_QK_DIM), lambda i, j: (i, 0))],
        out_specs=[pl.BlockSpec((TM, TN), lambda i, j: (i, j)),
                   pl.BlockSpec((n_gate, TM), lambda i, j: (0, i))],
        out_shape=[jax.ShapeDtypeStruct((M, N), BF16), jax.ShapeDtypeStruct((n_gate, M), F32)],
        scratch_shapes=[pltpu.VMEM((TM, D), BF16)],
        compiler_params=_params(("arbitrary", "arbitrary")),
        name="in_proj_ab",
    )(x2d, g.reshape(1, D), w_t, w_t, w_t, cos2.reshape(M, DA_QK_DIM), sin2.reshape(M, DA_QK_DIM))


_SGU_STEP_BLOCKS = (2, 3, 0, 4, 1, 5)


def _sgu_step_block(j):
    blk = jnp.int32(_SGU_STEP_BLOCKS[-1])
    for step in range(len(_SGU_STEP_BLOCKS) - 2, -1, -1):
        blk = jnp.where(j == step, _SGU_STEP_BLOCKS[step], blk)
    return blk


def _in_proj_sgu_kernel(x_ref, g_ref, w_ref, lng_ref, lnb_ref, sw_ref, sbt_ref, t_ref, xn_ref, v_ref, u_ref,
                        st_ref):
    j = pl.program_id(1)
    TM = x_ref.shape[0]
    TN = w_ref.shape[1]
    W = v_ref.shape[1]
    L = SGU_CHUNK
    GD = SGU_GROUP_DIM
    groups_per_step = TN // GD

    def proj_cols(cols):
        return jnp.dot(xn_ref[...], w_ref[:, cols].astype(BF16), preferred_element_type=F32)

    def project_v(half):
        for gl in range(groups_per_step):
            cols = slice(gl * GD, (gl + 1) * GD)
            slab = proj_cols(cols)
            v_ref[:, half * TN + gl * GD:half * TN + (gl + 1) * GD] = slab.astype(BF16)
            if half == 0 and gl == 0:
                shift = jnp.mean(slab, axis=1, keepdims=True)
                s1 = jnp.zeros((TM, 1), F32)
                s2 = jnp.zeros((TM, 1), F32)
            elif gl == 0:
                shift, s1, s2 = st_ref[0], st_ref[1], st_ref[2]
            d = slab - shift
            s1 = s1 + jnp.sum(d, axis=1, keepdims=True)
            s2 = s2 + jnp.sum(d * d, axis=1, keepdims=True)
        if half == 0:
            st_ref[0], st_ref[1], st_ref[2] = shift, s1, s2
        else:
            m1 = s1 * (1.0 / W)
            st_ref[0] = shift + m1
            st_ref[1] = lax.rsqrt(jnp.maximum(s2 * (1.0 / W) - m1 * m1, 0.0) + EPS)

    def gate(half):
        causal = _iota((L, L), 1) <= _iota((L, L), 0)
        for gl in range(groups_per_step):
            g = half * groups_per_step + gl
            cols = slice(gl * GD, (gl + 1) * GD)
            vcols = slice(g * GD, (g + 1) * GD)
            z = proj_cols(cols)
            w_causal = jnp.where(causal, sw_ref[g], 0.0).astype(BF16)
            bias = sbt_ref[:, g:g + 1]
            lng = lng_ref[:, vcols]
            lnb = lnb_ref[:, vcols]
            for c in range(TM // L):
                rows = slice(c * L, (c + 1) * L)
                vn = ((v_ref[rows, vcols].astype(F32) - st_ref[0, rows, :]) * st_ref[1, rows, :] * lng
                      + lnb).astype(BF16)
                sv = jnp.dot(w_causal, vn, preferred_element_type=F32) + bias
                t = u_ref[rows, cols].astype(F32) * sv * _silu(z[rows, :])
                t_ref[rows, cols] = t.astype(BF16)

    @pl.when(j == 0)
    def _():
        _normalize_rows(x_ref, g_ref, xn_ref)
        project_v(0)

    @pl.when(j == 1)
    def _():
        project_v(1)

    @pl.when((j == 2) | (j == 4))
    def _():
        for gl in range(groups_per_step):
            cols = slice(gl * GD, (gl + 1) * GD)
            u_ref[:, cols] = proj_cols(cols).astype(BF16)

    @pl.when(j == 3)
    def _():
        gate(0)

    @pl.when(j == 5)
    def _():
        gate(1)


def _in_proj_sgu(x2d, g, w, ln_g, ln_b, sgu_w, sgu_b):
    M, D = x2d.shape
    W = w.shape[1] // 3
    assert W == 2 * PROJ_TN and len(_SGU_STEP_BLOCKS) == 3 * W // PROJ_TN
    fixed = lambda i, j: (0, 0)
    return pl.pallas_call(
        _in_proj_sgu_kernel,
        grid=(M // PROJ_TM, len(_SGU_STEP_BLOCKS)),
        in_specs=[pl.BlockSpec((PROJ_TM, D), lambda i, j: (i, 0)),
                  pl.BlockSpec((1, D), fixed),
                  pl.BlockSpec((D, PROJ_TN), lambda i, j: (0, _sgu_step_block(j))),
                  pl.BlockSpec((1, W), fixed),
                  pl.BlockSpec((1, W), fixed),
                  pl.BlockSpec(sgu_w.shape, lambda i, j: (0, 0, 0)),
                  pl.BlockSpec((SGU_CHUNK, SGU_GROUPS), fixed)],
        out_specs=pl.BlockSpec((PROJ_TM, PROJ_TN), lambda i, j: (i, jnp.where(j <= 3, 0, 1))),
        out_shape=jax.ShapeDtypeStruct((M, W), BF16),
        scratch_shapes=[pltpu.VMEM((PROJ_TM, D), BF16),
                        pltpu.VMEM((PROJ_TM, W), BF16),
                        pltpu.VMEM((PROJ_TM, PROJ_TN), BF16),
                        pltpu.VMEM((3, PROJ_TM, 1), F32)],
        compiler_params=_params(("arbitrary", "arbitrary")),
        name="in_proj_sgu",
    )(x2d, g.reshape(1, D), w, ln_g.reshape(1, W), ln_b.reshape(1, W), sgu_w, sgu_b.T)


def _mlstm_kernel(qp_ref, kp_ref, v_ref, gt_ref, og_ref, z_ref, cwq_ref, cwk_ref, cbq_ref, cbk_ref,
                  gb_ref, ng_ref, out_ref, xs_ref, qc_ref, kc_ref):
    h = pl.program_id(1)
    S = qp_ref.shape[0]
    LB = ML_BLOCK
    nb = S // LB
    DK, DV = ML_QK_DIM, ML_V_DIM

    for src, cw_ref, cb_ref, dst, scale in ((qp_ref, cwq_ref, cbq_ref, qc_ref, 1.0),
                                            (kp_ref, cwk_ref, cbk_ref, kc_ref, DK ** -0.5)):
        xs_ref[0:8, :] = jnp.zeros((8, DK), F32)
        xs_ref[8:8 + S, :] = src[...].astype(F32)
        cw = cw_ref[...]
        cb = cb_ref[...]
        for c in range(S // CONV_ROWS):
            acc = cb
            for j in range(CONV_WIDTH):
                start = 8 + c * CONV_ROWS - (CONV_WIDTH - 1) + j
                acc = acc + cw[j:j + 1, :] * xs_ref[start:start + CONV_ROWS, :]
            dst[c * CONV_ROWS:(c + 1) * CONV_ROWS, :] = (_silu(acc) * scale).astype(BF16)

    gate_rows = gt_ref[...] + jnp.concatenate([gb_ref[...]] * (S // GATE_COLS), axis=1)
    sub8 = _iota((8, S), 0)
    i_all = jnp.sum(jnp.where(sub8 == h, gate_rows, 0.0), axis=0, keepdims=True)
    logf_all = _log_sigmoid(jnp.sum(jnp.where(sub8 == h + ML_HEADS, gate_rows, 0.0), axis=0, keepdims=True))

    blk = _iota((nb, LB), 0)

    def to_blocks(row):
        tile = jnp.zeros((nb, LB), F32)
        for c in range(nb):
            tile = jnp.where(blk == c, jnp.broadcast_to(row[:, c * LB:(c + 1) * LB], (nb, LB)), tile)
        return tile

    sub = _iota((LB, LB), 0)
    lane = _iota((LB, LB), 1)
    causal = lane <= sub
    i_b = to_blocks(i_all)
    b_b = jnp.dot(to_blocks(logf_all), (sub <= lane).astype(F32), preferred_element_type=F32,
                  precision=HIGHEST)
    g_tot = b_b[:, LB - 1:LB]
    w_b = g_tot - b_b + i_b
    m_loc = jnp.max(w_b, axis=1, keepdims=True)
    e_b = jnp.exp(w_b - m_loc)
    b_cols = jnp.concatenate([b_b, jnp.zeros((GATE_COLS - nb, LB), F32)], axis=0).T

    norm_g = ng_ref[0]
    ct = jnp.zeros((DK, DV), F32)
    n = jnp.zeros((DK, 1), F32)
    m = jnp.zeros((1, 1), F32)
    for c in range(nb):
        rows = slice(c * LB, (c + 1) * LB)
        b_row, i_row, e_row = b_b[c:c + 1, :], i_b[c:c + 1, :], e_b[c:c + 1, :]
        b_col = b_cols[:, c:c + 1]
        q = qc_ref[rows, :]
        v = v_ref[rows, :]
        k_t = kc_ref[rows, :].astype(F32).T

        rhs = jnp.concatenate([k_t.astype(BF16), ct.astype(BF16),
                               jnp.broadcast_to(n, (DK, 128)).astype(BF16)], axis=1)
        big = jnp.dot(q, rhs, preferred_element_type=F32)
        qk, inter, qn = big[:, :LB], big[:, LB:LB + DV], big[:, LB + DV:LB + DV + 1]

        log_d = jnp.where(causal, b_col - b_row + i_row, -jnp.inf)
        inter_log = b_col + m
        m_t = jnp.maximum(inter_log, jnp.max(log_d, axis=1, keepdims=True))
        d_mat = jnp.exp(log_d - m_t)
        inter_w = jnp.exp(inter_log - m_t)
        s_mat = qk * d_mat
        num = jnp.dot(s_mat.astype(BF16), v, preferred_element_type=F32) + inter_w * inter
        den = jnp.sum(s_mat, axis=1, keepdims=True) + inter_w * qn
        hm = num / jnp.maximum(jnp.abs(den), jnp.exp(-m_t))

        hn = hm * lax.rsqrt(jnp.mean(hm * hm, axis=-1, keepdims=True) + EPS) * norm_g
        og = og_ref[rows, :].astype(F32)
        z = z_ref[rows, :].astype(F32)
        out_ref[rows, :] = (_sigmoid(og) * hn * _silu(z)).astype(out_ref.dtype)

        if c + 1 < nb:
            ke = k_t * e_row
            ct_loc = jnp.dot(ke.astype(BF16), v, preferred_element_type=F32)
            n_loc = jnp.sum(ke, axis=1, keepdims=True)
            g_c, ml_c = g_tot[c:c + 1, :], m_loc[c:c + 1, :]
            m_new = jnp.maximum(g_c + m, ml_c)
            a = jnp.exp(g_c + m - m_new)
            cc = jnp.exp(ml_c - m_new)
            ct = a * ct + cc * ct_loc
            n = a * n + cc * n_loc
            m = m_new


def _mlstm(proj, gates, conv_w, conv_b, i_bias, f_bias, norm_g, B, S):
    M = B * S
    H = ML_HEADS
    assert S // ML_BLOCK == 8, "block rows are packed into one 8-sublane tile"
    gate_bias = jnp.broadcast_to(jnp.concatenate([i_bias, f_bias])[:, None], (2 * H, GATE_COLS))
    conv_b2 = conv_b.reshape(1, -1)
    norm_g3 = norm_g.reshape(H, 1, ML_V_DIM)
    qb, kb = ML_Q_OFF // ML_QK_DIM, ML_K_OFF // ML_QK_DIM
    vb, ob, zb = ML_V_OFF // ML_V_DIM, ML_O_OFF // ML_V_DIM, ML_Z_OFF // ML_V_DIM
    return pl.pallas_call(
        _mlstm_kernel,
        grid=(B, H),
        in_specs=[
            pl.BlockSpec((S, ML_QK_DIM), lambda b, h: (b, qb + h)),
            pl.BlockSpec((S, ML_QK_DIM), lambda b, h: (b, kb + h)),
            pl.BlockSpec((S, ML_V_DIM), lambda b, h: (b, vb + h)),
            pl.BlockSpec((2 * H, S), lambda b, h: (0, b)),
            pl.BlockSpec((S, ML_V_DIM), lambda b, h: (b, ob + h)),
            pl.BlockSpec((S, ML_V_DIM), lambda b, h: (b, zb + h)),
            pl.BlockSpec((CONV_WIDTH, ML_QK_DIM), lambda b, h: (0, h)),
            pl.BlockSpec((CONV_WIDTH, ML_QK_DIM), lambda b, h: (0, H + h)),
            pl.BlockSpec((1, ML_QK_DIM), lambda b, h: (0, h)),
            pl.BlockSpec((1, ML_QK_DIM), lambda b, h: (0, H + h)),
            pl.BlockSpec((2 * H, GATE_COLS), lambda b, h: (0, 0)),
            pl.BlockSpec((1, 1, ML_V_DIM), lambda b, h: (h, 0, 0)),
        ],
        out_specs=pl.BlockSpec((S, ML_V_DIM), lambda b, h: (b, h)),
        out_shape=jax.ShapeDtypeStruct((M, H * ML_V_DIM), BF16),
        scratch_shapes=[pltpu.VMEM((8 + S, ML_QK_DIM), F32),
                        pltpu.VMEM((S, ML_QK_DIM), BF16),
                        pltpu.VMEM((S, ML_QK_DIM), BF16)],
        compiler_params=_params(("arbitrary", "arbitrary")),
        name="mlstm",
    )(proj, proj, proj, gates, proj, proj, conv_w, conv_w, conv_b2, conv_b2, gate_bias, norm_g3)


def _diff_attn_kernel(lam_init, qr_ref, kr_ref, v_ref, z_ref, lq1_ref, lk1_ref, lq2_ref, lk2_ref, ng_ref,
                      o_ref):
    S = kr_ref.shape[0]
    QS, DH = ATT_QSUB, DA_QK_DIM
    diag_mask = _iota((QS, QS), 1) <= _iota((QS, QS), 0)

    def softmax_pv(q0, m):
        q = qr_ref[q0:q0 + QS, m * DH:(m + 1) * DH]
        kv_len = q0 + QS
        s = lax.dot_general(q, kr_ref[0:kv_len, m * DH:(m + 1) * DH], _NT, preferred_element_type=F32)
        s_diag = jnp.where(diag_mask, s[:, q0:], -jnp.inf)
        s = s_diag if q0 == 0 else jnp.concatenate([s[:, :q0], s_diag], axis=1)
        p = jnp.exp2(s - jnp.max(s, axis=1, keepdims=True))
        acc = jnp.dot(p.astype(BF16), v_ref[0:kv_len, :], preferred_element_type=F32)
        return acc, jnp.sum(p, axis=1, keepdims=True)

    lam = (jnp.exp(jnp.sum(lq1_ref[...] * lk1_ref[...], axis=1, keepdims=True))
           - jnp.exp(jnp.sum(lq2_ref[...] * lk2_ref[...], axis=1, keepdims=True)) + lam_init)
    norm_g = ng_ref[0] * (1.0 - lam_init)

    for q0 in reversed(range(0, S, QS)):
        acc1, l1 = softmax_pv(q0, 0)
        acc2, l2 = softmax_pv(q0, 1)
        o = acc1 / l1 - lam * (acc2 / l2)
        hn = o * lax.rsqrt(jnp.mean(o * o, axis=-1, keepdims=True) + EPS) * norm_g
        o_ref[q0:q0 + QS, :] = (hn * _silu(z_ref[q0:q0 + QS, :].astype(F32))).astype(o_ref.dtype)


def _diff_attn(proj, lq1, lk1, lq2, lk2, norm_g, lam_init, B, S):
    M = B * S
    H = DA_HEADS
    qb, kb, vb, zb = (DA_Q_OFF // DA_V_DIM, DA_K_OFF // DA_V_DIM, DA_V_OFF // DA_V_DIM, DA_Z_OFF // DA_V_DIM)
    vec = lambda a: a.reshape(1, DA_QK_DIM)
    vec_spec = pl.BlockSpec((1, DA_QK_DIM), lambda b, h: (0, 0))
    return pl.pallas_call(
        functools.partial(_diff_attn_kernel, lam_init),
        grid=(B, H),
        in_specs=[
            pl.BlockSpec((S, DA_V_DIM), lambda b, h: (b, qb + h)),
            pl.BlockSpec((S, DA_V_DIM), lambda b, h: (b, kb + h)),
            pl.BlockSpec((S, DA_V_DIM), lambda b, h: (b, vb + h)),
            pl.BlockSpec((S, DA_V_DIM), lambda b, h: (b, zb + h)),
            vec_spec, vec_spec, vec_spec, vec_spec,
            pl.BlockSpec((1, 1, DA_V_DIM), lambda b, h: (h, 0, 0)),
        ],
        out_specs=pl.BlockSpec((S, DA_V_DIM), lambda b, h: (b, h)),
        out_shape=jax.ShapeDtypeStruct((M, H * DA_V_DIM), BF16),
        compiler_params=_params(("arbitrary", "arbitrary")),
        name="diff_attn",
    )(proj, proj, proj, proj, vec(lq1), vec(lk1), vec(lq2), vec(lk2), norm_g.reshape(H, 1, DA_V_DIM))


def _ple_tail(x1_ref, ssq, p_ref, gw_ref, pw_ref, xg_ref, emit):
    D = x1_ref.shape[1]
    inv = lax.rsqrt(ssq * (1.0 / D) + EPS)
    p_bf = p_ref[...].astype(BF16)
    for n0 in range(0, D, TAIL_NC):
        cols = slice(n0, n0 + TAIL_NC)
        gate = _sigmoid(jnp.dot(xg_ref[...], gw_ref[:, cols].astype(BF16), preferred_element_type=F32) * inv)
        pp = jnp.dot(p_bf, pw_ref[:, cols].astype(BF16), preferred_element_type=F32)
        emit(n0, x1_ref[:, cols] + gate * pp)


def _out_ple_kernel(n_lhs, has_final_norm, *refs):
    lhs_refs = refs[:n_lhs]
    x_ref, p_ref, wo_ref, png_ref, gw_ref, pw_ref = refs[n_lhs:n_lhs + 6]
    if has_final_norm:
        fng_ref, o_ref, x1_ref, xn_ref = refs[n_lhs + 6:]
    else:
        o_ref, x1_ref, xn_ref = refs[n_lhs + 6:]
    TM, D = x_ref.shape

    ssq = jnp.zeros((TM, 1), F32)
    for n0 in range(0, D, TAIL_NC):
        cols = slice(n0, n0 + TAIL_NC)
        y = x_ref[:, cols]
        k0 = 0
        for lhs_ref in lhs_refs:
            k1 = k0 + lhs_ref.shape[1]
            y = y + jnp.dot(lhs_ref[...], wo_ref[k0:k1, cols].astype(BF16), preferred_element_type=F32)
            k0 = k1
        x1_ref[:, cols] = y
        xn_ref[:, cols] = (y * png_ref[:, cols]).astype(BF16)
        ssq = ssq + jnp.sum(y * y, axis=1, keepdims=True)

    ssq_out = [jnp.zeros((TM, 1), F32)]

    def emit(n0, val):
        o_ref[:, n0:n0 + TAIL_NC] = val
        if has_final_norm:
            ssq_out[0] = ssq_out[0] + jnp.sum(val * val, axis=1, keepdims=True)

    _ple_tail(x1_ref, ssq, p_ref, gw_ref, pw_ref, xn_ref, emit)

    if has_final_norm:
        inv = lax.rsqrt(ssq_out[0] * (1.0 / D) + EPS)
        for n0 in range(0, D, TAIL_NC):
            cols = slice(n0, n0 + TAIL_NC)
            o_ref[:, cols] = o_ref[:, cols] * inv * fng_ref[:, cols]


def _out_ple(lhs_list, x2d, p3d, layer, w_out, ple_g, gate_w3d, proj_w3d, final_g=None, name="out_ple"):
    M, D = x2d.shape
    P = p3d.shape[2]
    TM = TAIL_TM
    row = lambda i: (i, 0)
    fixed = lambda i: (0, 0)
    layer_block = lambda i: (layer, 0, 0)
    in_specs = [pl.BlockSpec((TM, lhs.shape[1]), row) for lhs in lhs_list]
    in_specs += [pl.BlockSpec((TM, D), row),
                 pl.BlockSpec((None, TM, P), lambda i: (layer, i, 0)),
                 _resident(w_out.shape, fixed),
                 _resident((1, D), fixed),
                 pl.BlockSpec((None, D, D), layer_block, pipeline_mode=pl.Buffered(1)),
                 pl.BlockSpec((None, P, D), layer_block, pipeline_mode=pl.Buffered(1))]
    args = [*lhs_list, x2d, p3d, w_out, ple_g.reshape(1, D), gate_w3d, proj_w3d]
    if final_g is not None:
        in_specs.append(_resident((1, D), fixed))
        args.append(final_g.reshape(1, D))
    return pl.pallas_call(
        functools.partial(_out_ple_kernel, len(lhs_list), final_g is not None),
        grid=(M // TM,),
        in_specs=in_specs,
        out_specs=pl.BlockSpec((TM, D), row),
        out_shape=jax.ShapeDtypeStruct((M, D), F32),
        scratch_shapes=[pltpu.VMEM((TM, D), F32), pltpu.VMEM((TM, D), BF16)],
        compiler_params=_params(("arbitrary",)),
        name=name,
    )(*args)


def kernel(x, p, positions, ab_norm_g, ab_w_in, ab_conv_w, ab_conv_b, ab_i_bias, ab_f_bias, ab_ml_norm_g,
           ab_lam_q1, ab_lam_k1, ab_lam_q2, ab_lam_k2, ab_da_norm_g, ab_w_out, c_norm_g, c_w_in, c_ln_g,
           c_ln_b, c_sgu_w, c_sgu_b, c_w_out, ple_norm_g, ple_gate_w, ple_proj_w, final_norm_g):
    B, S, D = x.shape
    M = B * S
    assert ab_norm_g.shape[0] == 1 and c_norm_g.shape[0] == 1 and p.shape[0] == 2
    x2d = x.reshape(M, D)
    p3d = p.reshape(2, M, p.shape[-1])

    cos2, sin2 = _rope_tables(positions)
    proj, gates = _in_proj_ab(x2d, ab_norm_g[0], ab_w_in[0].T, cos2, sin2)

    a_out = _mlstm(proj, gates, ab_conv_w[0], ab_conv_b[0], ab_i_bias[0], ab_f_bias[0], ab_ml_norm_g[0], B, S)

    lam_init = 0.8 - 0.6 * math.exp(-0.3 * 0)
    b_out = _diff_attn(proj, ab_lam_q1[0], ab_lam_k1[0], ab_lam_q2[0], ab_lam_k2[0], ab_da_norm_g[0],
                       lam_init, B, S)

    x2 = _out_ple([a_out, b_out], x2d, p3d, 0, ab_w_out[0], ple_norm_g[0], ple_gate_w, ple_proj_w,
                  name="out_ple_ab")

    t = _in_proj_sgu(x2, c_norm_g[0], c_w_in[0], c_ln_g[0], c_ln_b[0], c_sgu_w[0], c_sgu_b[0])
    out = _out_ple([t], x2, p3d, 1, c_w_out[0], ple_norm_g[1], ple_gate_w, ple_proj_w,
                   final_g=final_norm_g, name="out_ple_c")
    return out.reshape(B, S, D)
```

```python
import functools
import math

import jax
import jax.numpy as jnp
from jax import lax
from jax.experimental import pallas as pl
from jax.experimental.pallas import tpu as pltpu

F32 = jnp.float32
BF16 = jnp.bfloat16
HIGHEST = lax.Precision.HIGHEST

EPS = 1e-6
ROPE_THETA = 10000.0
CONV_WIDTH = 4

ML_HEADS = 4
ML_QK_DIM = 128
ML_V_DIM = 256
DA_HEADS = 4
DA_QK_DIM = 128
DA_V_DIM = 256
SGU_CHUNK = 128
SGU_GROUPS = 8
SGU_GROUP_DIM = 256

ML_Q_OFF, ML_K_OFF, ML_V_OFF, ML_O_OFF, ML_Z_OFF = 0, 512, 1024, 2048, 3072
DA_Q_OFF, DA_K_OFF, DA_V_OFF, DA_Z_OFF = 4096, 5120, 6144, 7168
GATE_COLS = 128
GATE_START = ML_V_OFF + ML_HEADS * ML_V_DIM

VMEM_LIMIT_BYTES = 56 * 1024 * 1024

PROJ_TM = 1024
PROJ_TN = 1024
NORM_ROWS = 128
TAIL_TM = 256
TAIL_NC = 512
PROJ_SLAB = 256
CONV_ROWS = 128
ML_BLOCK = 256
ATT_QSUB = 256
ATT_SCORES_AHEAD = 3


def _params(semantics):
    return pltpu.CompilerParams(dimension_semantics=semantics, vmem_limit_bytes=VMEM_LIMIT_BYTES)


def _resident(shape, index_map):
    return pl.BlockSpec(shape, index_map, pipeline_mode=pl.Buffered(1))


def _sigmoid(x):
    return 1.0 / (1.0 + jnp.exp(-x))


def _silu(x):
    return x * _sigmoid(x)


def _log_sigmoid(x):
    return jnp.minimum(x, 0.0) - jnp.log(1.0 + jnp.exp(-jnp.abs(x)))


def _iota(shape, axis):
    return lax.broadcasted_iota(jnp.int32, shape, axis)


def _rope_table_kernel(pos_ref, cos_ref, sin_ref):
    pos = pos_ref[0]
    half_rows = pos.shape[0]
    lane = _iota(pos.shape, 1)
    lo = lane < 64
    j = jnp.where(lo, lane, lane - 64).astype(F32)
    freq = jnp.exp(j * (-math.log(ROPE_THETA) / 64.0))
    ang = pos * freq
    c = jnp.cos(ang)
    s = jnp.sin(ang)
    cr = pltpu.roll(c, 64, 1)
    sr = pltpu.roll(s, 64, 1)
    cos_ref[0, 0:half_rows, :] = jnp.where(lo, c, cr)
    cos_ref[0, half_rows:2 * half_rows, :] = jnp.where(lo, cr, c)
    sin_ref[0, 0:half_rows, :] = jnp.where(lo, -s, sr)
    sin_ref[0, half_rows:2 * half_rows, :] = jnp.where(lo, -sr, s)


def _rope_tables(positions):
    B, S = positions.shape
    posf = positions.astype(F32)
    lo = jnp.broadcast_to(posf[:, :S // 2, None], (B, S // 2, 64))
    hi = jnp.broadcast_to(posf[:, S // 2:, None], (B, S // 2, 64))
    pos2 = jnp.concatenate([lo, hi], axis=-1)
    return pl.pallas_call(
        _rope_table_kernel,
        grid=(B,),
        in_specs=[pl.BlockSpec((1, S // 2, 128), lambda b: (b, 0, 0))],
        out_specs=[pl.BlockSpec((1, S, 128), lambda b: (b, 0, 0)),
                   pl.BlockSpec((1, S, 128), lambda b: (b, 0, 0))],
        out_shape=[jax.ShapeDtypeStruct((B, S, 128), F32)] * 2,
        compiler_params=_params(("arbitrary",)),
        name="rope_tables",
    )(pos2)


def _normalize_rows(x_ref, g_ref, xn_ref):
    g = g_ref[...]
    for r0 in range(0, x_ref.shape[0], NORM_ROWS):
        rows = slice(r0, r0 + NORM_ROWS)
        x = x_ref[rows, :]
        ms = jnp.mean(x * x, axis=-1, keepdims=True)
        xn_ref[rows, :] = (x * lax.rsqrt(ms + EPS) * g).astype(BF16)


def _x_tile_index(n_tiles, i, j):
    return jnp.minimum(i + (j >= 1), n_tiles - 1), 0


_NT = (((1,), (1,)), ((), ()))


def _in_proj_ab_kernel(n_gate, x_ref, g_ref, wt_ref, edge_ref, wgt_ref, cos_ref, sin_ref, o_ref, og_ref,
                       xn_ref):
    j = pl.program_id(1)
    TN, D = wt_ref.shape
    rope_q_tile, rope_k_tile = DA_Q_OFF // TN, DA_K_OFF // TN
    first_shifted = GATE_START // TN

    @pl.when(j == 0)
    def _():
        _normalize_rows(x_ref, g_ref, xn_ref)
        wg = jnp.concatenate([wgt_ref[...], jnp.zeros((16 - n_gate, D), F32)], axis=0).astype(BF16)
        og_ref[...] = lax.dot_general(wg, xn_ref[...], _NT, preferred_element_type=F32)[0:n_gate, :]

    def slabs(shift):
        for r0 in range(0, TN, PROJ_SLAB):
            lo, hi = r0 + shift, r0 + PROJ_SLAB + shift
            if hi <= TN:
                w = wt_ref[lo:hi, :]
            else:
                w = jnp.concatenate([wt_ref[lo:TN, :], edge_ref[...]], axis=0)
            yield r0, lax.dot_general(xn_ref[...], w.astype(BF16), _NT, preferred_element_type=F32)

    def plain(shift):
        for r0, y in slabs(shift):
            o_ref[:, r0:r0 + PROJ_SLAB] = y.astype(o_ref.dtype)

    def rope(shift, scale):
        cs = cos_ref[...]
        sn = sin_ref[...]
        for r0, y in slabs(shift):
            for c0 in range(0, PROJ_SLAB, DA_QK_DIM):
                x = y[:, c0:c0 + DA_QK_DIM]
                r = x * cs + pltpu.roll(x, DA_QK_DIM // 2, 1) * sn
                if scale is not None:
                    r = r * scale
                o_ref[:, r0 + c0:r0 + c0 + DA_QK_DIM] = r.astype(o_ref.dtype)

    @pl.when(j < first_shifted)
    def _():
        plain(0)

    @pl.when((j >= first_shifted) & (j != rope_q_tile) & (j != rope_k_tile))
    def _():
        plain(n_gate)

    @pl.when(j == rope_q_tile)
    def _():
        rope(n_gate, DA_QK_DIM ** -0.5 * math.log2(math.e))

    @pl.when(j == rope_k_tile)
    def _():
        rope(n_gate, None)


def _in_proj_ab(x2d, g, w_t, cos2, sin2):
    M, D = x2d.shape
    n_gate = 2 * ML_HEADS
    N = w_t.shape[0] - n_gate
    TM, TN = PROJ_TM, PROJ_TN
    assert n_gate == 8 and GATE_START % TN == 0 and N % TN == 0
    assert DA_Q_OFF % TN == 0 and DA_K_OFF - DA_Q_OFF == TN and DA_V_OFF - DA_K_OFF == TN
    fixed = lambda i, j: (0, 0)
    return pl.pallas_call(
        functools.partial(_in_proj_ab_kernel, n_gate),
        grid=(M // TM, N // TN),
        in_specs=[pl.BlockSpec((TM, D), functools.partial(_x_tile_index, M // TM)),
                  pl.BlockSpec((1, D), fixed),
                  pl.BlockSpec((TN, D), lambda i, j: (j, 0)),
                  pl.BlockSpec((n_gate, D), lambda i, j: ((j + 1) * (TN // n_gate), 0)),
                  pl.BlockSpec((n_gate, D), lambda i, j: (GATE_START // n_gate, 0)),
                  pl.BlockSpec((TM, DA_QK_DIM), lambda i, j: (i, 0)),
                  pl.BlockSpec((TM, DA_QK_DIM), lambda i, j: (i, 0))],
        out_specs=[pl.BlockSpec((TM, TN), lambda i, j: (i, j)),
                   pl.BlockSpec((n_gate, TM), lambda i, j: (0, i))],
        out_shape=[jax.ShapeDtypeStruct((M, N), BF16), jax.ShapeDtypeStruct((n_gate, M), F32)],
        scratch_shapes=[pltpu.VMEM((TM, D), BF16)],
        compiler_params=_params(("arbitrary", "arbitrary")),
        name="in_proj_ab",
    )(x2d, g.reshape(1, D), w_t, w_t, w_t, cos2.reshape(M, DA_QK_DIM), sin2.reshape(M, DA_QK_DIM))


_SGU_STEP_BLOCKS = (2, 3, 0, 4, 1, 5)


def _sgu_step_block(j):
    blk = jnp.int32(_SGU_STEP_BLOCKS[-1])
    for step in range(len(_SGU_STEP_BLOCKS) - 2, -1, -1):
        blk = jnp.where(j == step, _SGU_STEP_BLOCKS[step], blk)
    return blk


def _in_proj_sgu_kernel(x_ref, g_ref, w_ref, lng_ref, lnb_ref, sw_ref, sbt_ref, t_ref, xn_ref, v_ref, u_ref,
                        st_ref):
    j = pl.program_id(1)
    TM = x_ref.shape[0]
    TN = w_ref.shape[1]
    W = v_ref.shape[1]
    L = SGU_CHUNK
    GD = SGU_GROUP_DIM
    groups_per_step = TN // GD

    def proj_cols(cols):
        return jnp.dot(xn_ref[...], w_ref[:, cols].astype(BF16), preferred_element_type=F32)

    def project_v(half):
        for gl in range(groups_per_step):
            cols = slice(gl * GD, (gl + 1) * GD)
            slab = proj_cols(cols)
            v_ref[:, half * TN + gl * GD:half * TN + (gl + 1) * GD] = slab.astype(BF16)
            if half == 0 and gl == 0:
                shift = jnp.mean(slab, axis=1, keepdims=True)
                s1 = jnp.zeros((TM, 1), F32)
                s2 = jnp.zeros((TM, 1), F32)
            elif gl == 0:
                shift, s1, s2 = st_ref[0], st_ref[1], st_ref[2]
            d = slab - shift
            s1 = s1 + jnp.sum(d, axis=1, keepdims=True)
            s2 = s2 + jnp.sum(d * d, axis=1, keepdims=True)
        if half == 0:
            st_ref[0], st_ref[1], st_ref[2] = shift, s1, s2
        else:
            m1 = s1 * (1.0 / W)
            st_ref[0] = shift + m1
            st_ref[1] = lax.rsqrt(jnp.maximum(s2 * (1.0 / W) - m1 * m1, 0.0) + EPS)

    def gate(half):
        causal = _iota((L, L), 1) <= _iota((L, L), 0)
        for gl in range(groups_per_step):
            g = half * groups_per_step + gl
            cols = slice(gl * GD, (gl + 1) * GD)
            vcols = slice(g * GD, (g + 1) * GD)
            z = proj_cols(cols)
            w_causal = jnp.where(causal, sw_ref[g], 0.0).astype(BF16)
            bias = sbt_ref[:, g:g + 1]
            lng = lng_ref[:, vcols]
            lnb = lnb_ref[:, vcols]
            for c in range(TM // L):
                rows = slice(c * L, (c + 1) * L)
                vn = ((v_ref[rows, vcols].astype(F32) - st_ref[0, rows, :]) * st_ref[1, rows, :] * lng
                      + lnb).astype(BF16)
                sv = jnp.dot(w_causal, vn, preferred_element_type=F32) + bias
                t = u_ref[rows, cols].astype(F32) * sv * _silu(z[rows, :])
                t_ref[rows, cols] = t.astype(BF16)

    @pl.when(j == 0)
    def _():
        _normalize_rows(x_ref, g_ref, xn_ref)
        project_v(0)

    @pl.when(j == 1)
    def _():
        project_v(1)

    @pl.when((j == 2) | (j == 4))
    def _():
        for gl in range(groups_per_step):
            cols = slice(gl * GD, (gl + 1) * GD)
            u_ref[:, cols] = proj_cols(cols).astype(BF16)

    @pl.when(j == 3)
    def _():
        gate(0)

    @pl.when(j == 5)
    def _():
        gate(1)


def _in_proj_sgu(x2d, g, w, ln_g, ln_b, sgu_w, sgu_b):
    M, D = x2d.shape
    W = w.shape[1] // 3
    assert W == 2 * PROJ_TN and len(_SGU_STEP_BLOCKS) == 3 * W // PROJ_TN
    fixed = lambda i, j: (0, 0)
    return pl.pallas_call(
        _in_proj_sgu_kernel,
        grid=(M // PROJ_TM, len(_SGU_STEP_BLOCKS)),
        in_specs=[pl.BlockSpec((PROJ_TM, D), lambda i, j: (i, 0)),
                  pl.BlockSpec((1, D), fixed),
                  pl.BlockSpec((D, PROJ_TN), lambda i, j: (0, _sgu_step_block(j))),
                  pl.BlockSpec((1, W), fixed),
                  pl.BlockSpec((1, W), fixed),
                  pl.BlockSpec(sgu_w.shape, lambda i, j: (0, 0, 0)),
                  pl.BlockSpec((SGU_CHUNK, SGU_GROUPS), fixed)],
        out_specs=pl.BlockSpec((PROJ_TM, PROJ_TN), lambda i, j: (i, jnp.where(j <= 3, 0, 1))),
        out_shape=jax.ShapeDtypeStruct((M, W), BF16),
        scratch_shapes=[pltpu.VMEM((PROJ_TM, D), BF16),
                        pltpu.VMEM((PROJ_TM, W), BF16),
                        pltpu.VMEM((PROJ_TM, PROJ_TN), BF16),
                        pltpu.VMEM((3, PROJ_TM, 1), F32)],
        compiler_params=_params(("arbitrary", "arbitrary")),
        name="in_proj_sgu",
    )(x2d, g.reshape(1, D), w, ln_g.reshape(1, W), ln_b.reshape(1, W), sgu_w, sgu_b.T)


def _mlstm_kernel(qp_ref, kp_ref, v_ref, gt_ref, og_ref, z_ref, cwq_ref, cwk_ref, cbq_ref, cbk_ref,
                  gb_ref, ng_ref, out_ref, xs_ref, qc_ref, kc_ref):
    h = pl.program_id(1)
    S = qp_ref.shape[0]
    LB = ML_BLOCK
    nb = S // LB
    DK, DV = ML_QK_DIM, ML_V_DIM

    for src, cw_ref, cb_ref, dst, scale in ((kp_ref, cwk_ref, cbk_ref, kc_ref, DK ** -0.5),
                                            (qp_ref, cwq_ref, cbq_ref, qc_ref, 1.0)):
        xs_ref[0:8, :] = jnp.zeros((8, DK), F32)
        xs_ref[8:8 + S, :] = src[...].astype(F32)
        cw = cw_ref[...]
        cb = cb_ref[...]
        for c in range(S // CONV_ROWS):
            acc = cb
            for j in range(CONV_WIDTH):
                start = 8 + c * CONV_ROWS - (CONV_WIDTH - 1) + j
                acc = acc + cw[j:j + 1, :] * xs_ref[start:start + CONV_ROWS, :]
            dst[c * CONV_ROWS:(c + 1) * CONV_ROWS, :] = (_silu(acc) * scale).astype(BF16)

    gate_rows = gt_ref[...] + jnp.concatenate([gb_ref[...]] * (S // GATE_COLS), axis=1)
    sub8 = _iota((8, S), 0)
    i_all = jnp.sum(jnp.where(sub8 == h, gate_rows, 0.0), axis=0, keepdims=True)
    logf_all = _log_sigmoid(jnp.sum(jnp.where(sub8 == h + ML_HEADS, gate_rows, 0.0), axis=0, keepdims=True))

    blk = _iota((nb, LB), 0)

    def to_blocks(row):
        tile = jnp.zeros((nb, LB), F32)
        for c in range(nb):
            tile = jnp.where(blk == c, jnp.broadcast_to(row[:, c * LB:(c + 1) * LB], (nb, LB)), tile)
        return tile

    sub = _iota((LB, LB), 0)
    lane = _iota((LB, LB), 1)
    causal = lane <= sub
    i_b = to_blocks(i_all)
    b_b = jnp.dot(to_blocks(logf_all), (sub <= lane).astype(F32), preferred_element_type=F32,
                  precision=HIGHEST)
    g_tot = b_b[:, LB - 1:LB]
    w_b = g_tot - b_b + i_b
    m_loc = jnp.max(w_b, axis=1, keepdims=True)
    e_b = jnp.exp(w_b - m_loc)
    b_cols = jnp.concatenate([b_b, jnp.zeros((GATE_COLS - nb, LB), F32)], axis=0).T

    norm_g = ng_ref[0]
    ct = jnp.zeros((DK, DV), F32)
    n = jnp.zeros((DK, 1), F32)
    m = jnp.zeros((1, 1), F32)
    for c in range(nb):
        rows = slice(c * LB, (c + 1) * LB)
        b_row, i_row, e_row = b_b[c:c + 1, :], i_b[c:c + 1, :], e_b[c:c + 1, :]
        b_col = b_cols[:, c:c + 1]
        q = qc_ref[rows, :]
        v = v_ref[rows, :]
        k_t = kc_ref[rows, :].astype(F32).T

        ct_in, n_in, m_in = ct, n, m
        if c + 1 < nb:
            ke = k_t * e_row
            ct_loc = jnp.dot(ke.astype(BF16), v, preferred_element_type=F32)
            n_loc = jnp.sum(ke, axis=1, keepdims=True)
            g_c, ml_c = g_tot[c:c + 1, :], m_loc[c:c + 1, :]
            m = jnp.maximum(g_c + m_in, ml_c)
            a = jnp.exp(g_c + m_in - m)
            cc = jnp.exp(ml_c - m)
            ct = a * ct_in + cc * ct_loc
            n = a * n_in + cc * n_loc

        rhs = jnp.concatenate([k_t.astype(BF16), ct_in.astype(BF16),
                               jnp.broadcast_to(n_in, (DK, 128)).astype(BF16)], axis=1)
        big = jnp.dot(q, rhs, preferred_element_type=F32)
        qk, inter, qn = big[:, :LB], big[:, LB:LB + DV], big[:, LB + DV:LB + DV + 1]

        log_d = jnp.where(causal, b_col - b_row + i_row, -jnp.inf)
        inter_log = b_col + m_in
        m_t = jnp.maximum(inter_log, jnp.max(log_d, axis=1, keepdims=True))
        d_mat = jnp.exp(log_d - m_t)
        inter_w = jnp.exp(inter_log - m_t)
        s_mat = qk * d_mat
        num = jnp.dot(s_mat.astype(BF16), v, preferred_element_type=F32) + inter_w * inter
        den = jnp.sum(s_mat, axis=1, keepdims=True) + inter_w * qn
        hm = num / jnp.maximum(jnp.abs(den), jnp.exp(-m_t))

        hn = hm * lax.rsqrt(jnp.mean(hm * hm, axis=-1, keepdims=True) + EPS) * norm_g
        og = og_ref[rows, :].astype(F32)
        z = z_ref[rows, :].astype(F32)
        out_ref[rows, :] = (_sigmoid(og) * hn * _silu(z)).astype(out_ref.dtype)


def _mlstm(proj, gates, conv_w, conv_b, i_bias, f_bias, norm_g, B, S):
    M = B * S
    H = ML_HEADS
    assert S // ML_BLOCK == 8, "block rows are packed into one 8-sublane tile"
    gate_bias = jnp.broadcast_to(jnp.concatenate([i_bias, f_bias])[:, None], (2 * H, GATE_COLS))
    conv_b2 = conv_b.reshape(1, -1)
    norm_g3 = norm_g.reshape(H, 1, ML_V_DIM)
    qb, kb = ML_Q_OFF // ML_QK_DIM, ML_K_OFF // ML_QK_DIM
    vb, ob, zb = ML_V_OFF // ML_V_DIM, ML_O_OFF // ML_V_DIM, ML_Z_OFF // ML_V_DIM
    return pl.pallas_call(
        _mlstm_kernel,
        grid=(B, H),
        in_specs=[
            pl.BlockSpec((S, ML_QK_DIM), lambda b, h: (b, qb + h)),
            pl.BlockSpec((S, ML_QK_DIM), lambda b, h: (b, kb + h)),
            pl.BlockSpec((S, ML_V_DIM), lambda b, h: (b, vb + h)),
            pl.BlockSpec((2 * H, S), lambda b, h: (0, b)),
            pl.BlockSpec((S, ML_V_DIM), lambda b, h: (b, ob + h)),
            pl.BlockSpec((S, ML_V_DIM), lambda b, h: (b, zb + h)),
            pl.BlockSpec((CONV_WIDTH, ML_QK_DIM), lambda b, h: (0, h)),
            pl.BlockSpec((CONV_WIDTH, ML_QK_DIM), lambda b, h: (0, H + h)),
            pl.BlockSpec((1, ML_QK_DIM), lambda b, h: (0, h)),
            pl.BlockSpec((1, ML_QK_DIM), lambda b, h: (0, H + h)),
            pl.BlockSpec((2 * H, GATE_COLS), lambda b, h: (0, 0)),
            pl.BlockSpec((1, 1, ML_V_DIM), lambda b, h: (h, 0, 0)),
        ],
        out_specs=pl.BlockSpec((S, ML_V_DIM), lambda b, h: (b, h)),
        out_shape=jax.ShapeDtypeStruct((M, H * ML_V_DIM), BF16),
        scratch_shapes=[pltpu.VMEM((8 + S, ML_QK_DIM), F32),
                        pltpu.VMEM((S, ML_QK_DIM), BF16),
                        pltpu.VMEM((S, ML_QK_DIM), BF16)],
        compiler_params=_params(("arbitrary", "arbitrary")),
        name="mlstm",
    )(proj, proj, proj, gates, proj, proj, conv_w, conv_w, conv_b2, conv_b2, gate_bias, norm_g3)


def _diff_attn_kernel(lam_init, qr_ref, kr_ref, v_ref, z_ref, lq1_ref, lk1_ref, lq2_ref, lk2_ref, ng_ref,
                      o_ref):
    S = kr_ref.shape[0]
    QS, DH = ATT_QSUB, DA_QK_DIM
    diag_mask = _iota((QS, QS), 1) <= _iota((QS, QS), 0)

    def scores(q0, m):
        q = qr_ref[q0:q0 + QS, m * DH:(m + 1) * DH]
        kv_len = q0 + QS
        s = lax.dot_general(q, kr_ref[0:kv_len, m * DH:(m + 1) * DH], _NT, preferred_element_type=F32)
        s_diag = jnp.where(diag_mask, s[:, q0:], -jnp.inf)
        return s_diag if q0 == 0 else jnp.concatenate([s[:, :q0], s_diag], axis=1)

    def softmax_pv(q0, s):
        p = jnp.exp2(s - jnp.max(s, axis=1, keepdims=True))
        acc = jnp.dot(p.astype(BF16), v_ref[0:q0 + QS, :], preferred_element_type=F32)
        return acc, jnp.sum(p, axis=1, keepdims=True)

    lam = (jnp.exp(jnp.sum(lq1_ref[...] * lk1_ref[...], axis=1, keepdims=True))
           - jnp.exp(jnp.sum(lq2_ref[...] * lk2_ref[...], axis=1, keepdims=True)) + lam_init)
    norm_g = ng_ref[0] * (1.0 - lam_init)

    order = list(reversed(range(0, S, QS)))
    both_maps = lambda q0: (scores(q0, 0), scores(q0, 1))
    ahead = [both_maps(q0) for q0 in order[:ATT_SCORES_AHEAD]]
    for idx, q0 in enumerate(order):
        s1, s2 = ahead.pop(0)
        if idx + ATT_SCORES_AHEAD < len(order):
            ahead.append(both_maps(order[idx + ATT_SCORES_AHEAD]))
        acc1, l1 = softmax_pv(q0, s1)
        acc2, l2 = softmax_pv(q0, s2)
        o = acc1 / l1 - lam * (acc2 / l2)
        hn = o * lax.rsqrt(jnp.mean(o * o, axis=-1, keepdims=True) + EPS) * norm_g
        o_ref[q0:q0 + QS, :] = (hn * _silu(z_ref[q0:q0 + QS, :].astype(F32))).astype(o_ref.dtype)


def _diff_attn(proj, lq1, lk1, lq2, lk2, norm_g, lam_init, B, S):
    M = B * S
    H = DA_HEADS
    qb, kb, vb, zb = (DA_Q_OFF // DA_V_DIM, DA_K_OFF // DA_V_DIM, DA_V_OFF // DA_V_DIM, DA_Z_OFF // DA_V_DIM)
    vec = lambda a: a.reshape(1, DA_QK_DIM)
    vec_spec = pl.BlockSpec((1, DA_QK_DIM), lambda b, h: (0, 0))
    return pl.pallas_call(
        functools.partial(_diff_attn_kernel, lam_init),
        grid=(B, H),
        in_specs=[
            pl.BlockSpec((S, DA_V_DIM), lambda b, h: (b, qb + h)),
            pl.BlockSpec((S, DA_V_DIM), lambda b, h: (b, kb + h)),
            pl.BlockSpec((S, DA_V_DIM), lambda b, h: (b, vb + h)),
            pl.BlockSpec((S, DA_V_DIM), lambda b, h: (b, zb + h)),
            vec_spec, vec_spec, vec_spec, vec_spec,
            pl.BlockSpec((1, 1, DA_V_DIM), lambda b, h: (h, 0, 0)),
        ],
        out_specs=pl.BlockSpec((S, DA_V_DIM), lambda b, h: (b, h)),
        out_shape=jax.ShapeDtypeStruct((M, H * DA_V_DIM), BF16),
        compiler_params=_params(("arbitrary", "arbitrary")),
        name="diff_attn",
    )(proj, proj, proj, proj, vec(lq1), vec(lk1), vec(lq2), vec(lk2), norm_g.reshape(H, 1, DA_V_DIM))


def _ple_tail(x1_ref, ssq, p_ref, gw_ref, pw_ref, xg_ref, emit):
    D = x1_ref.shape[1]
    inv = lax.rsqrt(ssq * (1.0 / D) + EPS)
    p_bf = p_ref[...].astype(BF16)
    for n0 in range(0, D, TAIL_NC):
        cols = slice(n0, n0 + TAIL_NC)
        gate = _sigmoid(jnp.dot(xg_ref[...], gw_ref[:, cols].astype(BF16), preferred_element_type=F32) * inv)
        pp = jnp.dot(p_bf, pw_ref[:, cols].astype(BF16), preferred_element_type=F32)
        emit(n0, x1_ref[:, cols] + gate * pp)


def _out_ple_kernel(n_lhs, has_final_norm, *refs):
    lhs_refs = refs[:n_lhs]
    x_ref, p_ref, wo_ref, png_ref, gw_ref, pw_ref = refs[n_lhs:n_lhs + 6]
    if has_final_norm:
        fng_ref, o_ref, x1_ref, xn_ref = refs[n_lhs + 6:]
    else:
        o_ref, x1_ref, xn_ref = refs[n_lhs + 6:]
    TM, D = x_ref.shape

    ssq = jnp.zeros((TM, 1), F32)
    for n0 in range(0, D, TAIL_NC):
        cols = slice(n0, n0 + TAIL_NC)
        y = x_ref[:, cols]
        k0 = 0
        for lhs_ref in lhs_refs:
            k1 = k0 + lhs_ref.shape[1]
            y = y + jnp.dot(lhs_ref[...], wo_ref[k0:k1, cols].astype(BF16), preferred_element_type=F32)
            k0 = k1
        x1_ref[:, cols] = y
        xn_ref[:, cols] = (y * png_ref[:, cols]).astype(BF16)
        ssq = ssq + jnp.sum(y * y, axis=1, keepdims=True)

    ssq_out = [jnp.zeros((TM, 1), F32)]

    def emit(n0, val):
        o_ref[:, n0:n0 + TAIL_NC] = val
        if has_final_norm:
            ssq_out[0] = ssq_out[0] + jnp.sum(val * val, axis=1, keepdims=True)

    _ple_tail(x1_ref, ssq, p_ref, gw_ref, pw_ref, xn_ref, emit)

    if has_final_norm:
        inv = lax.rsqrt(ssq_out[0] * (1.0 / D) + EPS)
        for n0 in range(0, D, TAIL_NC):
            cols = slice(n0, n0 + TAIL_NC)
            o_ref[:, cols] = o_ref[:, cols] * inv * fng_ref[:, cols]


def _out_ple(lhs_list, x2d, p3d, layer, w_out, ple_g, gate_w3d, proj_w3d, final_g=None, name="out_ple"):
    M, D = x2d.shape
    P = p3d.shape[2]
    TM = TAIL_TM
    row = lambda i: (i, 0)
    fixed = lambda i: (0, 0)
    layer_block = lambda i: (layer, 0, 0)
    in_specs = [pl.BlockSpec((TM, lhs.shape[1]), row) for lhs in lhs_list]
    in_specs += [pl.BlockSpec((TM, D), row),
                 pl.BlockSpec((None, TM, P), lambda i: (layer, i, 0)),
                 _resident(w_out.shape, fixed),
                 _resident((1, D), fixed),
                 pl.BlockSpec((None, D, D), layer_block, pipeline_mode=pl.Buffered(1)),
                 pl.BlockSpec((None, P, D), layer_block, pipeline_mode=pl.Buffered(1))]
    args = [*lhs_list, x2d, p3d, w_out, ple_g.reshape(1, D), gate_w3d, proj_w3d]
    if final_g is not None:
        in_specs.append(_resident((1, D), fixed))
        args.append(final_g.reshape(1, D))
    return pl.pallas_call(
        functools.partial(_out_ple_kernel, len(lhs_list), final_g is not None),
        grid=(M // TM,),
        in_specs=in_specs,
        out_specs=pl.BlockSpec((TM, D), row),
        out_shape=jax.ShapeDtypeStruct((M, D), F32),
        scratch_shapes=[pltpu.VMEM((TM, D), F32), pltpu.VMEM((TM, D), BF16)],
        compiler_params=_params(("arbitrary",)),
        name=name,
    )(*args)


def kernel(x, p, positions, ab_norm_g, ab_w_in, ab_conv_w, ab_conv_b, ab_i_bias, ab_f_bias, ab_ml_norm_g,
           ab_lam_q1, ab_lam_k1, ab_lam_q2, ab_lam_k2, ab_da_norm_g, ab_w_out, c_norm_g, c_w_in, c_ln_g,
           c_ln_b, c_sgu_w, c_sgu_b, c_w_out, ple_norm_g, ple_gate_w, ple_proj_w, final_norm_g):
    B, S, D = x.shape
    M = B * S
    assert ab_norm_g.shape[0] == 1 and c_norm_g.shape[0] == 1 and p.shape[0] == 2
    x2d = x.reshape(M, D)
    p3d = p.reshape(2, M, p.shape[-1])

    cos2, sin2 = _rope_tables(positions)
    proj, gates = _in_proj_ab(x2d, ab_norm_g[0], ab_w_in[0].T, cos2, sin2)

    a_out = _mlstm(proj, gates, ab_conv_w[0], ab_conv_b[0], ab_i_bias[0], ab_f_bias[0], ab_ml_norm_g[0], B, S)

    lam_init = 0.8 - 0.6 * math.exp(-0.3 * 0)
    b_out = _diff_attn(proj, ab_lam_q1[0], ab_lam_k1[0], ab_lam_q2[0], ab_lam_k2[0], ab_da_norm_g[0],
                       lam_init, B, S)

    x2 = _out_ple([a_out, b_out], x2d, p3d, 0, ab_w_out[0], ple_norm_g[0], ple_gate_w, ple_proj_w,
                  name="out_ple_ab")

    t = _in_proj_sgu(x2, c_norm_g[0], c_w_in[0], c_ln_g[0], c_ln_b[0], c_sgu_w[0], c_sgu_b[0])
    out = _out_ple([t], x2, p3d, 1, c_w_out[0], ple_norm_g[1], ple_gate_w, ple_proj_w,
                   final_g=final_norm_g, name="out_ple_c")
    return out.reshape(B, S, D)
```

```python
import functools
import math

import jax
import jax.numpy as jnp
from jax import lax
from jax.experimental import pallas as pl
from jax.experimental.pallas import tpu as pltpu

F32 = jnp.float32
BF16 = jnp.bfloat16
HIGHEST = lax.Precision.HIGHEST

EPS = 1e-6
ROPE_THETA = 10000.0
CONV_WIDTH = 4

ML_HEADS = 4
ML_QK_DIM = 128
ML_V_DIM = 256
DA_HEADS = 4
DA_QK_DIM = 128
DA_V_DIM = 256
SGU_CHUNK = 128
SGU_GROUPS = 8
SGU_GROUP_DIM = 256

ML_Q_OFF, ML_K_OFF, ML_V_OFF, ML_O_OFF, ML_Z_OFF = 0, 512, 1024, 2048, 3072
DA_Q_OFF, DA_K_OFF, DA_V_OFF, DA_Z_OFF = 4096, 5120, 6144, 7168
GATE_COLS = 128
GATE_START = ML_V_OFF + ML_HEADS * ML_V_DIM

VMEM_LIMIT_BYTES = 56 * 1024 * 1024

PROJ_TM = 1024
PROJ_TN = 1024
NORM_ROWS = 128
TAIL_TM = 256
TAIL_NC = 512
PROJ_SLAB = 256
CONV_ROWS = 128
ML_BLOCK = 256
ATT_QSUB = 256
ATT_SCORES_AHEAD = 3


def _params(semantics):
    return pltpu.CompilerParams(dimension_semantics=semantics, vmem_limit_bytes=VMEM_LIMIT_BYTES)


def _resident(shape, index_map):
    return pl.BlockSpec(shape, index_map, pipeline_mode=pl.Buffered(1))


def _sigmoid(x):
    return 1.0 / (1.0 + jnp.exp(-x))


def _silu(x):
    return x * _sigmoid(x)


def _log_sigmoid(x):
    return jnp.minimum(x, 0.0) - jnp.log(1.0 + jnp.exp(-jnp.abs(x)))


def _iota(shape, axis):
    return lax.broadcasted_iota(jnp.int32, shape, axis)


def _rope_table_kernel(pos_ref, cos_ref, sin_ref):
    pos = pos_ref[0]
    half_rows = pos.shape[0]
    lane = _iota(pos.shape, 1)
    lo = lane < 64
    j = jnp.where(lo, lane, lane - 64).astype(F32)
    freq = jnp.exp(j * (-math.log(ROPE_THETA) / 64.0))
    ang = pos * freq
    c = jnp.cos(ang)
    s = jnp.sin(ang)
    cr = pltpu.roll(c, 64, 1)
    sr = pltpu.roll(s, 64, 1)
    cos_ref[0, 0:half_rows, :] = jnp.where(lo, c, cr)
    cos_ref[0, half_rows:2 * half_rows, :] = jnp.where(lo, cr, c)
    sin_ref[0, 0:half_rows, :] = jnp.where(lo, -s, sr)
    sin_ref[0, half_rows:2 * half_rows, :] = jnp.where(lo, -sr, s)


def _rope_tables(positions):
    B, S = positions.shape
    posf = positions.astype(F32)
    lo = jnp.broadcast_to(posf[:, :S // 2, None], (B, S // 2, 64))
    hi = jnp.broadcast_to(posf[:, S // 2:, None], (B, S // 2, 64))
    pos2 = jnp.concatenate([lo, hi], axis=-1)
    return pl.pallas_call(
        _rope_table_kernel,
        grid=(B,),
        in_specs=[pl.BlockSpec((1, S // 2, 128), lambda b: (b, 0, 0))],
        out_specs=[pl.BlockSpec((1, S, 128), lambda b: (b, 0, 0)),
                   pl.BlockSpec((1, S, 128), lambda b: (b, 0, 0))],
        out_shape=[jax.ShapeDtypeStruct((B, S, 128), F32)] * 2,
        compiler_params=_params(("arbitrary",)),
        name="rope_tables",
    )(pos2)


def _normalize_rows(x_ref, g_ref, xn_ref):
    g = g_ref[...]
    for r0 in range(0, x_ref.shape[0], NORM_ROWS):
        rows = slice(r0, r0 + NORM_ROWS)
        x = x_ref[rows, :]
        ms = jnp.mean(x * x, axis=-1, keepdims=True)
        xn_ref[rows, :] = (x * lax.rsqrt(ms + EPS) * g).astype(BF16)


def _x_tile_index(n_tiles, i, j):
    return jnp.minimum(i + (j >= 1), n_tiles - 1), 0


_NT = (((1,), (1,)), ((), ()))


def _in_proj_ab_kernel(n_gate, x_ref, g_ref, wt_ref, edge_ref, wgt_ref, cos_ref, sin_ref, o_ref, og_ref,
                       xn_ref):
    j = pl.program_id(1)
    TN, D = wt_ref.shape
    rope_q_tile, rope_k_tile = DA_Q_OFF // TN, DA_K_OFF // TN
    first_shifted = GATE_START // TN

    @pl.when(j == 0)
    def _():
        _normalize_rows(x_ref, g_ref, xn_ref)
        wg = jnp.concatenate([wgt_ref[...], jnp.zeros((16 - n_gate, D), F32)], axis=0).astype(BF16)
        og_ref[...] = lax.dot_general(wg, xn_ref[...], _NT, preferred_element_type=F32)[0:n_gate, :]

    def slabs(shift):
        for r0 in range(0, TN, PROJ_SLAB):
            lo, hi = r0 + shift, r0 + PROJ_SLAB + shift
            if hi <= TN:
                w = wt_ref[lo:hi, :]
            else:
                w = jnp.concatenate([wt_ref[lo:TN, :], edge_ref[...]], axis=0)
            yield r0, lax.dot_general(xn_ref[...], w.astype(BF16), _NT, preferred_element_type=F32)

    def plain(shift):
        for r0, y in slabs(shift):
            o_ref[r0 // PROJ_SLAB] = y.astype(o_ref.dtype)

    def rope(shift, scale):
        cs = cos_ref[...]
        sn = sin_ref[...]
        for r0, y in slabs(shift):
            for c0 in range(0, PROJ_SLAB, DA_QK_DIM):
                x = y[:, c0:c0 + DA_QK_DIM]
                r = x * cs + pltpu.roll(x, DA_QK_DIM // 2, 1) * sn
                if scale is not None:
                    r = r * scale
                o_ref[r0 // PROJ_SLAB, :, c0:c0 + DA_QK_DIM] = r.astype(o_ref.dtype)

    @pl.when(j < first_shifted)
    def _():
        plain(0)

    @pl.when((j >= first_shifted) & (j != rope_q_tile) & (j != rope_k_tile))
    def _():
        plain(n_gate)

    @pl.when(j == rope_q_tile)
    def _():
        rope(n_gate, DA_QK_DIM ** -0.5 * math.log2(math.e))

    @pl.when(j == rope_k_tile)
    def _():
        rope(n_gate, None)


def _in_proj_ab(x2d, g, w_t, cos2, sin2):
    M, D = x2d.shape
    n_gate = 2 * ML_HEADS
    N = w_t.shape[0] - n_gate
    TM, TN = PROJ_TM, PROJ_TN
    assert n_gate == 8 and GATE_START % TN == 0 and N % TN == 0
    assert DA_Q_OFF % TN == 0 and DA_K_OFF - DA_Q_OFF == TN and DA_V_OFF - DA_K_OFF == TN
    fixed = lambda i, j: (0, 0)
    return pl.pallas_call(
        functools.partial(_in_proj_ab_kernel, n_gate),
        grid=(M // TM, N // TN),
        in_specs=[pl.BlockSpec((TM, D), functools.partial(_x_tile_index, M // TM)),
                  pl.BlockSpec((1, D), fixed),
                  pl.BlockSpec((TN, D), lambda i, j: (j, 0)),
                  pl.BlockSpec((n_gate, D), lambda i, j: ((j + 1) * (TN // n_gate), 0)),
                  pl.BlockSpec((n_gate, D), lambda i, j: (GATE_START // n_gate, 0)),
                  pl.BlockSpec((TM, DA_QK_DIM), lambda i, j: (i, 0)),
                  pl.BlockSpec((TM, DA_QK_DIM), lambda i, j: (i, 0))],
        out_specs=[pl.BlockSpec((TN // PROJ_SLAB, TM, PROJ_SLAB), lambda i, j: (j, i, 0)),
                   pl.BlockSpec((n_gate, TM), lambda i, j: (0, i))],
        out_shape=[jax.ShapeDtypeStruct((N // PROJ_SLAB, M, PROJ_SLAB), BF16),
                   jax.ShapeDtypeStruct((n_gate, M), F32)],
        scratch_shapes=[pltpu.VMEM((TM, D), BF16)],
        compiler_params=_params(("arbitrary", "arbitrary")),
        name="in_proj_ab",
    )(x2d, g.reshape(1, D), w_t, w_t, w_t, cos2.reshape(M, DA_QK_DIM), sin2.reshape(M, DA_QK_DIM))


_SGU_STEP_BLOCKS = (2, 3, 0, 4, 1, 5)


def _sgu_step_block(j):
    blk = jnp.int32(_SGU_STEP_BLOCKS[-1])
    for step in range(len(_SGU_STEP_BLOCKS) - 2, -1, -1):
        blk = jnp.where(j == step, _SGU_STEP_BLOCKS[step], blk)
    return blk


def _in_proj_sgu_kernel(x_ref, g_ref, w_ref, lng_ref, lnb_ref, sw_ref, sbt_ref, t_ref, xn_ref, v_ref, u_ref,
                        st_ref):
    j = pl.program_id(1)
    TM = x_ref.shape[0]
    TN = w_ref.shape[1]
    W = v_ref.shape[1]
    L = SGU_CHUNK
    GD = SGU_GROUP_DIM
    groups_per_step = TN // GD

    def proj_cols(cols):
        return jnp.dot(xn_ref[...], w_ref[:, cols].astype(BF16), preferred_element_type=F32)

    def project_v(half):
        for gl in range(groups_per_step):
            cols = slice(gl * GD, (gl + 1) * GD)
            slab = proj_cols(cols)
            v_ref[:, half * TN + gl * GD:half * TN + (gl + 1) * GD] = slab.astype(BF16)
            if half == 0 and gl == 0:
                shift = jnp.mean(slab, axis=1, keepdims=True)
                s1 = jnp.zeros((TM, 1), F32)
                s2 = jnp.zeros((TM, 1), F32)
            elif gl == 0:
                shift, s1, s2 = st_ref[0], st_ref[1], st_ref[2]
            d = slab - shift
            s1 = s1 + jnp.sum(d, axis=1, keepdims=True)
            s2 = s2 + jnp.sum(d * d, axis=1, keepdims=True)
        if half == 0:
            st_ref[0], st_ref[1], st_ref[2] = shift, s1, s2
        else:
            m1 = s1 * (1.0 / W)
            st_ref[0] = shift + m1
            st_ref[1] = lax.rsqrt(jnp.maximum(s2 * (1.0 / W) - m1 * m1, 0.0) + EPS)

    def gate(half):
        causal = _iota((L, L), 1) <= _iota((L, L), 0)
        for gl in range(groups_per_step):
            g = half * groups_per_step + gl
            cols = slice(gl * GD, (gl + 1) * GD)
            vcols = slice(g * GD, (g + 1) * GD)
            z = proj_cols(cols)
            w_causal = jnp.where(causal, sw_ref[g], 0.0).astype(BF16)
            bias = sbt_ref[:, g:g + 1]
            lng = lng_ref[:, vcols]
            lnb = lnb_ref[:, vcols]
            for c in range(TM // L):
                rows = slice(c * L, (c + 1) * L)
                vn = ((v_ref[rows, vcols].astype(F32) - st_ref[0, rows, :]) * st_ref[1, rows, :] * lng
                      + lnb).astype(BF16)
                sv = jnp.dot(w_causal, vn, preferred_element_type=F32) + bias
                t = u_ref[rows, cols].astype(F32) * sv * _silu(z[rows, :])
                t_ref[rows, cols] = t.astype(BF16)

    @pl.when(j == 0)
    def _():
        _normalize_rows(x_ref, g_ref, xn_ref)
        project_v(0)

    @pl.when(j == 1)
    def _():
        project_v(1)

    @pl.when((j == 2) | (j == 4))
    def _():
        for gl in range(groups_per_step):
            cols = slice(gl * GD, (gl + 1) * GD)
            u_ref[:, cols] = proj_cols(cols).astype(BF16)

    @pl.when(j == 3)
    def _():
        gate(0)

    @pl.when(j == 5)
    def _():
        gate(1)


def _in_proj_sgu(x2d, g, w, ln_g, ln_b, sgu_w, sgu_b):
    M, D = x2d.shape
    W = w.shape[1] // 3
    assert W == 2 * PROJ_TN and len(_SGU_STEP_BLOCKS) == 3 * W // PROJ_TN
    fixed = lambda i, j: (0, 0)
    return pl.pallas_call(
        _in_proj_sgu_kernel,
        grid=(M // PROJ_TM, len(_SGU_STEP_BLOCKS)),
        in_specs=[pl.BlockSpec((PROJ_TM, D), lambda i, j: (i, 0)),
                  pl.BlockSpec((1, D), fixed),
                  pl.BlockSpec((D, PROJ_TN), lambda i, j: (0, _sgu_step_block(j))),
                  pl.BlockSpec((1, W), fixed),
                  pl.BlockSpec((1, W), fixed),
                  pl.BlockSpec(sgu_w.shape, lambda i, j: (0, 0, 0)),
                  pl.BlockSpec((SGU_CHUNK, SGU_GROUPS), fixed)],
        out_specs=pl.BlockSpec((PROJ_TM, PROJ_TN), lambda i, j: (i, jnp.where(j <= 3, 0, 1))),
        out_shape=jax.ShapeDtypeStruct((M, W), BF16),
        scratch_shapes=[pltpu.VMEM((PROJ_TM, D), BF16),
                        pltpu.VMEM((PROJ_TM, W), BF16),
                        pltpu.VMEM((PROJ_TM, PROJ_TN), BF16),
                        pltpu.VMEM((3, PROJ_TM, 1), F32)],
        compiler_params=_params(("arbitrary", "arbitrary")),
        name="in_proj_sgu",
    )(x2d, g.reshape(1, D), w, ln_g.reshape(1, W), ln_b.reshape(1, W), sgu_w, sgu_b.T)


def _mlstm_kernel(qp_ref, kp_ref, v_ref, gt_ref, og_ref, z_ref, cwq_ref, cwk_ref, cbq_ref, cbk_ref,
                  gb_ref, ng_ref, out_ref, xs_ref, qc_ref, kc_ref):
    h = pl.program_id(1)
    S = qp_ref.shape[0]
    LB = ML_BLOCK
    nb = S // LB
    DK, DV = ML_QK_DIM, ML_V_DIM

    for src, cw_ref, cb_ref, dst, scale in ((kp_ref, cwk_ref, cbk_ref, kc_ref, DK ** -0.5),
                                            (qp_ref, cwq_ref, cbq_ref, qc_ref, 1.0)):
        xs_ref[0:8, :] = jnp.zeros((8, DK), F32)
        xs_ref[8:8 + S, :] = src[...].astype(F32)
        cw = cw_ref[...]
        cb = cb_ref[...]
        for c in range(S // CONV_ROWS):
            acc = cb
            for j in range(CONV_WIDTH):
                start = 8 + c * CONV_ROWS - (CONV_WIDTH - 1) + j
                acc = acc + cw[j:j + 1, :] * xs_ref[start:start + CONV_ROWS, :]
            dst[c * CONV_ROWS:(c + 1) * CONV_ROWS, :] = (_silu(acc) * scale).astype(BF16)

    gate_rows = gt_ref[...] + jnp.concatenate([gb_ref[...]] * (S // GATE_COLS), axis=1)
    sub8 = _iota((8, S), 0)
    i_all = jnp.sum(jnp.where(sub8 == h, gate_rows, 0.0), axis=0, keepdims=True)
    logf_all = _log_sigmoid(jnp.sum(jnp.where(sub8 == h + ML_HEADS, gate_rows, 0.0), axis=0, keepdims=True))

    blk = _iota((nb, LB), 0)

    def to_blocks(row):
        tile = jnp.zeros((nb, LB), F32)
        for c in range(nb):
            tile = jnp.where(blk == c, jnp.broadcast_to(row[:, c * LB:(c + 1) * LB], (nb, LB)), tile)
        return tile

    sub = _iota((LB, LB), 0)
    lane = _iota((LB, LB), 1)
    causal = lane <= sub
    i_b = to_blocks(i_all)
    b_b = jnp.dot(to_blocks(logf_all), (sub <= lane).astype(F32), preferred_element_type=F32,
                  precision=HIGHEST)
    g_tot = b_b[:, LB - 1:LB]
    w_b = g_tot - b_b + i_b
    m_loc = jnp.max(w_b, axis=1, keepdims=True)
    e_b = jnp.exp(w_b - m_loc)
    b_cols = jnp.concatenate([b_b, jnp.zeros((GATE_COLS - nb, LB), F32)], axis=0).T

    norm_g = ng_ref[0]
    ct = jnp.zeros((DK, DV), F32)
    n = jnp.zeros((DK, 1), F32)
    m = jnp.zeros((1, 1), F32)
    for c in range(nb):
        rows = slice(c * LB, (c + 1) * LB)
        b_row, i_row, e_row = b_b[c:c + 1, :], i_b[c:c + 1, :], e_b[c:c + 1, :]
        b_col = b_cols[:, c:c + 1]
        q = qc_ref[rows, :]
        v = v_ref[rows, :]
        k_t = kc_ref[rows, :].astype(F32).T

        ct_in, n_in, m_in = ct, n, m
        if c + 1 < nb:
            ke = k_t * e_row
            ct_loc = jnp.dot(ke.astype(BF16), v, preferred_element_type=F32)
            n_loc = jnp.sum(ke, axis=1, keepdims=True)
            g_c, ml_c = g_tot[c:c + 1, :], m_loc[c:c + 1, :]
            m = jnp.maximum(g_c + m_in, ml_c)
            a = jnp.exp(g_c + m_in - m)
            cc = jnp.exp(ml_c - m)
            ct = a * ct_in + cc * ct_loc
            n = a * n_in + cc * n_loc

        rhs = jnp.concatenate([k_t.astype(BF16), ct_in.astype(BF16),
                               jnp.broadcast_to(n_in, (DK, 128)).astype(BF16)], axis=1)
        big = jnp.dot(q, rhs, preferred_element_type=F32)
        qk, inter, qn = big[:, :LB], big[:, LB:LB + DV], big[:, LB + DV:LB + DV + 1]

        log_d = jnp.where(causal, b_col - b_row + i_row, -jnp.inf)
        inter_log = b_col + m_in
        m_t = jnp.maximum(inter_log, jnp.max(log_d, axis=1, keepdims=True))
        d_mat = jnp.exp(log_d - m_t)
        inter_w = jnp.exp(inter_log - m_t)
        s_mat = qk * d_mat
        num = jnp.dot(s_mat.astype(BF16), v, preferred_element_type=F32) + inter_w * inter
        den = jnp.sum(s_mat, axis=1, keepdims=True) + inter_w * qn
        hm = num / jnp.maximum(jnp.abs(den), jnp.exp(-m_t))

        hn = hm * lax.rsqrt(jnp.mean(hm * hm, axis=-1, keepdims=True) + EPS) * norm_g
        og = og_ref[rows, :].astype(F32)
        z = z_ref[rows, :].astype(F32)
        out_ref[rows, :] = (_sigmoid(og) * hn * _silu(z)).astype(out_ref.dtype)


def _mlstm(proj, gates, conv_w, conv_b, i_bias, f_bias, norm_g, B, S):
    M = B * S
    H = ML_HEADS
    assert S // ML_BLOCK == 8, "block rows are packed into one 8-sublane tile"
    gate_bias = jnp.broadcast_to(jnp.concatenate([i_bias, f_bias])[:, None], (2 * H, GATE_COLS))
    conv_b2 = conv_b.reshape(1, -1)
    norm_g3 = norm_g.reshape(H, 1, ML_V_DIM)
    assert proj.shape[2] == ML_V_DIM == 2 * ML_QK_DIM
    qb, kb = ML_Q_OFF // ML_V_DIM, ML_K_OFF // ML_V_DIM
    vb, ob, zb = ML_V_OFF // ML_V_DIM, ML_O_OFF // ML_V_DIM, ML_Z_OFF // ML_V_DIM
    return pl.pallas_call(
        _mlstm_kernel,
        grid=(B, H),
        in_specs=[
            pl.BlockSpec((None, S, ML_QK_DIM), lambda b, h: (qb + h // 2, b, h % 2)),
            pl.BlockSpec((None, S, ML_QK_DIM), lambda b, h: (kb + h // 2, b, h % 2)),
            pl.BlockSpec((None, S, ML_V_DIM), lambda b, h: (vb + h, b, 0)),
            pl.BlockSpec((2 * H, S), lambda b, h: (0, b)),
            pl.BlockSpec((None, S, ML_V_DIM), lambda b, h: (ob + h, b, 0)),
            pl.BlockSpec((None, S, ML_V_DIM), lambda b, h: (zb + h, b, 0)),
            pl.BlockSpec((CONV_WIDTH, ML_QK_DIM), lambda b, h: (0, h)),
            pl.BlockSpec((CONV_WIDTH, ML_QK_DIM), lambda b, h: (0, H + h)),
            pl.BlockSpec((1, ML_QK_DIM), lambda b, h: (0, h)),
            pl.BlockSpec((1, ML_QK_DIM), lambda b, h: (0, H + h)),
            pl.BlockSpec((2 * H, GATE_COLS), lambda b, h: (0, 0)),
            pl.BlockSpec((1, 1, ML_V_DIM), lambda b, h: (h, 0, 0)),
        ],
        out_specs=pl.BlockSpec((S, ML_V_DIM), lambda b, h: (b, h)),
        out_shape=jax.ShapeDtypeStruct((M, H * ML_V_DIM), BF16),
        scratch_shapes=[pltpu.VMEM((8 + S, ML_QK_DIM), F32),
                        pltpu.VMEM((S, ML_QK_DIM), BF16),
                        pltpu.VMEM((S, ML_QK_DIM), BF16)],
        compiler_params=_params(("arbitrary", "arbitrary")),
        name="mlstm",
    )(proj, proj, proj, gates, proj, proj, conv_w, conv_w, conv_b2, conv_b2, gate_bias, norm_g3)


def _diff_attn_kernel(lam_init, qr_ref, kr_ref, v_ref, z_ref, lq1_ref, lk1_ref, lq2_ref, lk2_ref, ng_ref,
                      o_ref):
    S = kr_ref.shape[0]
    QS, DH = ATT_QSUB, DA_QK_DIM
    diag_mask = _iota((QS, QS), 1) <= _iota((QS, QS), 0)

    def scores(q0, m):
        q = qr_ref[q0:q0 + QS, m * DH:(m + 1) * DH]
        kv_len = q0 + QS
        s = lax.dot_general(q, kr_ref[0:kv_len, m * DH:(m + 1) * DH], _NT, preferred_element_type=F32)
        s_diag = jnp.where(diag_mask, s[:, q0:], -jnp.inf)
        return s_diag if q0 == 0 else jnp.concatenate([s[:, :q0], s_diag], axis=1)

    def softmax_pv(q0, s):
        p = jnp.exp2(s - jnp.max(s, axis=1, keepdims=True))
        acc = jnp.dot(p.astype(BF16), v_ref[0:q0 + QS, :], preferred_element_type=F32)
        return acc, jnp.sum(p, axis=1, keepdims=True)

    lam = (jnp.exp(jnp.sum(lq1_ref[...] * lk1_ref[...], axis=1, keepdims=True))
           - jnp.exp(jnp.sum(lq2_ref[...] * lk2_ref[...], axis=1, keepdims=True)) + lam_init)
    norm_g = ng_ref[0] * (1.0 - lam_init)

    order = list(reversed(range(0, S, QS)))
    both_maps = lambda q0: (scores(q0, 0), scores(q0, 1))
    ahead = [both_maps(q0) for q0 in order[:ATT_SCORES_AHEAD]]
    for idx, q0 in enumerate(order):
        s1, s2 = ahead.pop(0)
        if idx + ATT_SCORES_AHEAD < len(order):
            ahead.append(both_maps(order[idx + ATT_SCORES_AHEAD]))
        acc1, l1 = softmax_pv(q0, s1)
        acc2, l2 = softmax_pv(q0, s2)
        o = acc1 / l1 - lam * (acc2 / l2)
        hn = o * lax.rsqrt(jnp.mean(o * o, axis=-1, keepdims=True) + EPS) * norm_g
        o_ref[q0:q0 + QS, :] = (hn * _silu(z_ref[q0:q0 + QS, :].astype(F32))).astype(o_ref.dtype)


def _diff_attn(proj, lq1, lk1, lq2, lk2, norm_g, lam_init, B, S):
    M = B * S
    H = DA_HEADS
    assert proj.shape[2] == DA_V_DIM
    qb, kb, vb, zb = (DA_Q_OFF // DA_V_DIM, DA_K_OFF // DA_V_DIM, DA_V_OFF // DA_V_DIM, DA_Z_OFF // DA_V_DIM)
    vec = lambda a: a.reshape(1, DA_QK_DIM)
    vec_spec = pl.BlockSpec((1, DA_QK_DIM), lambda b, h: (0, 0))
    return pl.pallas_call(
        functools.partial(_diff_attn_kernel, lam_init),
        grid=(B, H),
        in_specs=[
            pl.BlockSpec((None, S, DA_V_DIM), lambda b, h: (qb + h, b, 0)),
            pl.BlockSpec((None, S, DA_V_DIM), lambda b, h: (kb + h, b, 0)),
            pl.BlockSpec((None, S, DA_V_DIM), lambda b, h: (vb + h, b, 0)),
            pl.BlockSpec((None, S, DA_V_DIM), lambda b, h: (zb + h, b, 0)),
            vec_spec, vec_spec, vec_spec, vec_spec,
            pl.BlockSpec((1, 1, DA_V_DIM), lambda b, h: (h, 0, 0)),
        ],
        out_specs=pl.BlockSpec((S, DA_V_DIM), lambda b, h: (b, h)),
        out_shape=jax.ShapeDtypeStruct((M, H * DA_V_DIM), BF16),
        compiler_params=_params(("arbitrary", "arbitrary")),
        name="diff_attn",
    )(proj, proj, proj, proj, vec(lq1), vec(lk1), vec(lq2), vec(lk2), norm_g.reshape(H, 1, DA_V_DIM))


def _ple_tail(x1_ref, ssq, p_ref, gw_ref, pw_ref, xg_ref, emit):
    D = x1_ref.shape[1]
    inv = lax.rsqrt(ssq * (1.0 / D) + EPS)
    p_bf = p_ref[...].astype(BF16)
    for n0 in range(0, D, TAIL_NC):
        cols = slice(n0, n0 + TAIL_NC)
        gate = _sigmoid(jnp.dot(xg_ref[...], gw_ref[:, cols].astype(BF16), preferred_element_type=F32) * inv)
        pp = jnp.dot(p_bf, pw_ref[:, cols].astype(BF16), preferred_element_type=F32)
        emit(n0, x1_ref[:, cols] + gate * pp)


def _out_ple_kernel(n_lhs, has_final_norm, *refs):
    lhs_refs = refs[:n_lhs]
    x_ref, p_ref, wo_ref, png_ref, gw_ref, pw_ref = refs[n_lhs:n_lhs + 6]
    if has_final_norm:
        fng_ref, o_ref, x1_ref, xn_ref = refs[n_lhs + 6:]
    else:
        o_ref, x1_ref, xn_ref = refs[n_lhs + 6:]
    TM, D = x_ref.shape

    ssq = jnp.zeros((TM, 1), F32)
    for n0 in range(0, D, TAIL_NC):
        cols = slice(n0, n0 + TAIL_NC)
        y = x_ref[:, cols]
        k0 = 0
        for lhs_ref in lhs_refs:
            k1 = k0 + lhs_ref.shape[1]
            y = y + jnp.dot(lhs_ref[...], wo_ref[k0:k1, cols].astype(BF16), preferred_element_type=F32)
            k0 = k1
        x1_ref[:, cols] = y
        xn_ref[:, cols] = (y * png_ref[:, cols]).astype(BF16)
        ssq = ssq + jnp.sum(y * y, axis=1, keepdims=True)

    ssq_out = [jnp.zeros((TM, 1), F32)]

    def emit(n0, val):
        o_ref[:, n0:n0 + TAIL_NC] = val
        if has_final_norm:
            ssq_out[0] = ssq_out[0] + jnp.sum(val * val, axis=1, keepdims=True)

    _ple_tail(x1_ref, ssq, p_ref, gw_ref, pw_ref, xn_ref, emit)

    if has_final_norm:
        inv = lax.rsqrt(ssq_out[0] * (1.0 / D) + EPS)
        for n0 in range(0, D, TAIL_NC):
            cols = slice(n0, n0 + TAIL_NC)
            o_ref[:, cols] = o_ref[:, cols] * inv * fng_ref[:, cols]


def _out_ple(lhs_list, x2d, p3d, layer, w_out, ple_g, gate_w3d, proj_w3d, final_g=None, name="out_ple"):
    M, D = x2d.shape
    P = p3d.shape[2]
    TM = TAIL_TM
    row = lambda i: (i, 0)
    fixed = lambda i: (0, 0)
    layer_block = lambda i: (layer, 0, 0)
    in_specs = [pl.BlockSpec((TM, lhs.shape[1]), row) for lhs in lhs_list]
    in_specs += [pl.BlockSpec((TM, D), row),
                 pl.BlockSpec((None, TM, P), lambda i: (layer, i, 0)),
                 _resident(w_out.shape, fixed),
                 _resident((1, D), fixed),
                 pl.BlockSpec((None, D, D), layer_block, pipeline_mode=pl.Buffered(1)),
                 pl.BlockSpec((None, P, D), layer_block, pipeline_mode=pl.Buffered(1))]
    args = [*lhs_list, x2d, p3d, w_out, ple_g.reshape(1, D), gate_w3d, proj_w3d]
    if final_g is not None:
        in_specs.append(_resident((1, D), fixed))
        args.append(final_g.reshape(1, D))
    return pl.pallas_call(
        functools.partial(_out_ple_kernel, len(lhs_list), final_g is not None),
        grid=(M // TM,),
        in_specs=in_specs,
        out_specs=pl.BlockSpec((TM, D), row),
        out_shape=jax.ShapeDtypeStruct((M, D), F32),
        scratch_shapes=[pltpu.VMEM((TM, D), F32), pltpu.VMEM((TM, D), BF16)],
        compiler_params=_params(("arbitrary",)),
        name=name,
    )(*args)


def kernel(x, p, positions, ab_norm_g, ab_w_in, ab_conv_w, ab_conv_b, ab_i_bias, ab_f_bias, ab_ml_norm_g,
           ab_lam_q1, ab_lam_k1, ab_lam_q2, ab_lam_k2, ab_da_norm_g, ab_w_out, c_norm_g, c_w_in, c_ln_g,
           c_ln_b, c_sgu_w, c_sgu_b, c_w_out, ple_norm_g, ple_gate_w, ple_proj_w, final_norm_g):
    B, S, D = x.shape
    M = B * S
    assert ab_norm_g.shape[0] == 1 and c_norm_g.shape[0] == 1 and p.shape[0] == 2
    x2d = x.reshape(M, D)
    p3d = p.reshape(2, M, p.shape[-1])

    cos2, sin2 = _rope_tables(positions)
    proj, gates = _in_proj_ab(x2d, ab_norm_g[0], ab_w_in[0].T, cos2, sin2)

    a_out = _mlstm(proj, gates, ab_conv_w[0], ab_conv_b[0], ab_i_bias[0], ab_f_bias[0], ab_ml_norm_g[0], B, S)

    lam_init = 0.8 - 0.6 * math.exp(-0.3 * 0)
    b_out = _diff_attn(proj, ab_lam_q1[0], ab_lam_k1[0], ab_lam_q2[0], ab_lam_k2[0], ab_da_norm_g[0],
                       lam_init, B, S)

    x2 = _out_ple([a_out, b_out], x2d, p3d, 0, ab_w_out[0], ple_norm_g[0], ple_gate_w, ple_proj_w,
                  name="out_ple_ab")

    t = _in_proj_sgu(x2, c_norm_g[0], c_w_in[0], c_ln_g[0], c_ln_b[0], c_sgu_w[0], c_sgu_b[0])
    out = _out_ple([t], x2, p3d, 1, c_w_out[0], ple_norm_g[1], ple_gate_w, ple_proj_w,
                   final_g=final_norm_g, name="out_ple_c")
    return out.reshape(B, S, D)
```

```python
import functools
import math

import jax
import jax.numpy as jnp
from jax import lax
from jax.experimental import pallas as pl
from jax.experimental.pallas import tpu as pltpu

F32 = jnp.float32
BF16 = jnp.bfloat16
HIGHEST = lax.Precision.HIGHEST

EPS = 1e-6
ROPE_THETA = 10000.0
CONV_WIDTH = 4

ML_HEADS = 4
ML_QK_DIM = 128
ML_V_DIM = 256
DA_HEADS = 4
DA_QK_DIM = 128
DA_V_DIM = 256
SGU_CHUNK = 128
SGU_GROUPS = 8
SGU_GROUP_DIM = 256

ML_Q_OFF, ML_K_OFF, ML_V_OFF, ML_O_OFF, ML_Z_OFF = 0, 512, 1024, 2048, 3072
DA_Q_OFF, DA_K_OFF, DA_V_OFF, DA_Z_OFF = 4096, 5120, 6144, 7168
GATE_COLS = 128
GATE_START = ML_V_OFF + ML_HEADS * ML_V_DIM

VMEM_LIMIT_BYTES = 56 * 1024 * 1024

PROJ_TM = 1024
PROJ_TN = 1024
NORM_ROWS = 128
TAIL_TM = 256
TAIL_NC = 512
PROJ_SLAB = 256
CONV_ROWS = 128
ML_BLOCK = 256
ATT_QSUB = 128
ATT_SCORES_AHEAD = 6


def _params(semantics):
    return pltpu.CompilerParams(dimension_semantics=semantics, vmem_limit_bytes=VMEM_LIMIT_BYTES)


def _resident(shape, index_map):
    return pl.BlockSpec(shape, index_map, pipeline_mode=pl.Buffered(1))


def _sigmoid(x):
    return 1.0 / (1.0 + jnp.exp(-x))


def _silu(x):
    return x * _sigmoid(x)


def _log_sigmoid(x):
    return jnp.minimum(x, 0.0) - jnp.log(1.0 + jnp.exp(-jnp.abs(x)))


def _iota(shape, axis):
    return lax.broadcasted_iota(jnp.int32, shape, axis)


def _rope_table_kernel(pos_ref, cos_ref, sin_ref):
    pos = pos_ref[0]
    half_rows = pos.shape[0]
    lane = _iota(pos.shape, 1)
    lo = lane < 64
    j = jnp.where(lo, lane, lane - 64).astype(F32)
    freq = jnp.exp(j * (-math.log(ROPE_THETA) / 64.0))
    ang = pos * freq
    c = jnp.cos(ang)
    s = jnp.sin(ang)
    cr = pltpu.roll(c, 64, 1)
    sr = pltpu.roll(s, 64, 1)
    cos_ref[0, 0:half_rows, :] = jnp.where(lo, c, cr)
    cos_ref[0, half_rows:2 * half_rows, :] = jnp.where(lo, cr, c)
    sin_ref[0, 0:half_rows, :] = jnp.where(lo, -s, sr)
    sin_ref[0, half_rows:2 * half_rows, :] = jnp.where(lo, -sr, s)


def _rope_tables(positions):
    B, S = positions.shape
    posf = positions.astype(F32)
    lo = jnp.broadcast_to(posf[:, :S // 2, None], (B, S // 2, 64))
    hi = jnp.broadcast_to(posf[:, S // 2:, None], (B, S // 2, 64))
    pos2 = jnp.concatenate([lo, hi], axis=-1)
    return pl.pallas_call(
        _rope_table_kernel,
        grid=(B,),
        in_specs=[pl.BlockSpec((1, S // 2, 128), lambda b: (b, 0, 0))],
        out_specs=[pl.BlockSpec((1, S, 128), lambda b: (b, 0, 0)),
                   pl.BlockSpec((1, S, 128), lambda b: (b, 0, 0))],
        out_shape=[jax.ShapeDtypeStruct((B, S, 128), F32)] * 2,
        compiler_params=_params(("arbitrary",)),
        name="rope_tables",
    )(pos2)


def _normalize_rows(x_ref, g_ref, xn_ref):
    g = g_ref[...]
    for r0 in range(0, x_ref.shape[0], NORM_ROWS):
        rows = slice(r0, r0 + NORM_ROWS)
        x = x_ref[rows, :]
        ms = jnp.mean(x * x, axis=-1, keepdims=True)
        xn_ref[rows, :] = (x * lax.rsqrt(ms + EPS) * g).astype(BF16)


def _x_tile_index(n_tiles, i, j):
    return jnp.minimum(i + (j >= 1), n_tiles - 1), 0


_NT = (((1,), (1,)), ((), ()))


def _in_proj_ab_kernel(n_gate, x_ref, g_ref, wt_ref, edge_ref, wgt_ref, cos_ref, sin_ref, o_ref, og_ref,
                       xn_ref):
    j = pl.program_id(1)
    TN, D = wt_ref.shape
    rope_q_tile, rope_k_tile = DA_Q_OFF // TN, DA_K_OFF // TN
    first_shifted = GATE_START // TN

    @pl.when(j == 0)
    def _():
        _normalize_rows(x_ref, g_ref, xn_ref)
        wg = jnp.concatenate([wgt_ref[...], jnp.zeros((16 - n_gate, D), F32)], axis=0).astype(BF16)
        og_ref[...] = lax.dot_general(wg, xn_ref[...], _NT, preferred_element_type=F32)[0:n_gate, :]

    def slabs(shift):
        for r0 in range(0, TN, PROJ_SLAB):
            lo, hi = r0 + shift, r0 + PROJ_SLAB + shift
            if hi <= TN:
                w = wt_ref[lo:hi, :]
            else:
                w = jnp.concatenate([wt_ref[lo:TN, :], edge_ref[...]], axis=0)
            yield r0, lax.dot_general(xn_ref[...], w.astype(BF16), _NT, preferred_element_type=F32)

    def plain(shift):
        for r0, y in slabs(shift):
            o_ref[r0 // PROJ_SLAB] = y.astype(o_ref.dtype)

    def rope(shift, scale):
        cs = cos_ref[...]
        sn = sin_ref[...]
        for r0, y in slabs(shift):
            for c0 in range(0, PROJ_SLAB, DA_QK_DIM):
                x = y[:, c0:c0 + DA_QK_DIM]
                r = x * cs + pltpu.roll(x, DA_QK_DIM // 2, 1) * sn
                if scale is not None:
                    r = r * scale
                o_ref[r0 // PROJ_SLAB, :, c0:c0 + DA_QK_DIM] = r.astype(o_ref.dtype)

    @pl.when(j < first_shifted)
    def _():
        plain(0)

    @pl.when((j >= first_shifted) & (j != rope_q_tile) & (j != rope_k_tile))
    def _():
        plain(n_gate)

    @pl.when(j == rope_q_tile)
    def _():
        rope(n_gate, DA_QK_DIM ** -0.5 * math.log2(math.e))

    @pl.when(j == rope_k_tile)
    def _():
        rope(n_gate, None)


def _in_proj_ab(x2d, g, w_t, cos2, sin2):
    M, D = x2d.shape
    n_gate = 2 * ML_HEADS
    N = w_t.shape[0] - n_gate
    TM, TN = PROJ_TM, PROJ_TN
    assert n_gate == 8 and GATE_START % TN == 0 and N % TN == 0
    assert DA_Q_OFF % TN == 0 and DA_K_OFF - DA_Q_OFF == TN and DA_V_OFF - DA_K_OFF == TN
    fixed = lambda i, j: (0, 0)
    return pl.pallas_call(
        functools.partial(_in_proj_ab_kernel, n_gate),
        grid=(M // TM, N // TN),
        in_specs=[pl.BlockSpec((TM, D), functools.partial(_x_tile_index, M // TM)),
                  pl.BlockSpec((1, D), fixed),
                  pl.BlockSpec((TN, D), lambda i, j: (j, 0)),
                  pl.BlockSpec((n_gate, D), lambda i, j: ((j + 1) * (TN // n_gate), 0)),
                  pl.BlockSpec((n_gate, D), lambda i, j: (GATE_START // n_gate, 0)),
                  pl.BlockSpec((TM, DA_QK_DIM), lambda i, j: (i, 0)),
                  pl.BlockSpec((TM, DA_QK_DIM), lambda i, j: (i, 0))],
        out_specs=[pl.BlockSpec((TN // PROJ_SLAB, TM, PROJ_SLAB), lambda i, j: (j, i, 0)),
                   pl.BlockSpec((n_gate, TM), lambda i, j: (0, i))],
        out_shape=[jax.ShapeDtypeStruct((N // PROJ_SLAB, M, PROJ_SLAB), BF16),
                   jax.ShapeDtypeStruct((n_gate, M), F32)],
        scratch_shapes=[pltpu.VMEM((TM, D), BF16)],
        compiler_params=_params(("arbitrary", "arbitrary")),
        name="in_proj_ab",
    )(x2d, g.reshape(1, D), w_t, w_t, w_t, cos2.reshape(M, DA_QK_DIM), sin2.reshape(M, DA_QK_DIM))


_SGU_STEP_BLOCKS = (2, 3, 0, 4, 1, 5)


def _sgu_step_block(j):
    blk = jnp.int32(_SGU_STEP_BLOCKS[-1])
    for step in range(len(_SGU_STEP_BLOCKS) - 2, -1, -1):
        blk = jnp.where(j == step, _SGU_STEP_BLOCKS[step], blk)
    return blk


def _in_proj_sgu_kernel(x_ref, g_ref, w_ref, lng_ref, lnb_ref, sw_ref, sbt_ref, t_ref, xn_ref, v_ref, u_ref,
                        st_ref):
    j = pl.program_id(1)
    TM = x_ref.shape[0]
    TN = w_ref.shape[1]
    W = v_ref.shape[1]
    L = SGU_CHUNK
    GD = SGU_GROUP_DIM
    groups_per_step = TN // GD

    def proj_cols(cols):
        return jnp.dot(xn_ref[...], w_ref[:, cols].astype(BF16), preferred_element_type=F32)

    def project_v(half):
        for gl in range(groups_per_step):
            cols = slice(gl * GD, (gl + 1) * GD)
            slab = proj_cols(cols)
            v_ref[:, half * TN + gl * GD:half * TN + (gl + 1) * GD] = slab.astype(BF16)
            if half == 0 and gl == 0:
                shift = jnp.mean(slab, axis=1, keepdims=True)
                s1 = jnp.zeros((TM, 1), F32)
                s2 = jnp.zeros((TM, 1), F32)
            elif gl == 0:
                shift, s1, s2 = st_ref[0], st_ref[1], st_ref[2]
            d = slab - shift
            s1 = s1 + jnp.sum(d, axis=1, keepdims=True)
            s2 = s2 + jnp.sum(d * d, axis=1, keepdims=True)
        if half == 0:
            st_ref[0], st_ref[1], st_ref[2] = shift, s1, s2
        else:
            m1 = s1 * (1.0 / W)
            st_ref[0] = shift + m1
            st_ref[1] = lax.rsqrt(jnp.maximum(s2 * (1.0 / W) - m1 * m1, 0.0) + EPS)

    def gate(half):
        causal = _iota((L, L), 1) <= _iota((L, L), 0)
        for gl in range(groups_per_step):
            g = half * groups_per_step + gl
            cols = slice(gl * GD, (gl + 1) * GD)
            vcols = slice(g * GD, (g + 1) * GD)
            z = proj_cols(cols)
            w_causal = jnp.where(causal, sw_ref[g], 0.0).astype(BF16)
            bias = sbt_ref[:, g:g + 1]
            lng = lng_ref[:, vcols]
            lnb = lnb_ref[:, vcols]
            for c in range(TM // L):
                rows = slice(c * L, (c + 1) * L)
                vn = ((v_ref[rows, vcols].astype(F32) - st_ref[0, rows, :]) * st_ref[1, rows, :] * lng
                      + lnb).astype(BF16)
                sv = jnp.dot(w_causal, vn, preferred_element_type=F32) + bias
                t = u_ref[rows, cols].astype(F32) * sv * _silu(z[rows, :])
                t_ref[rows, cols] = t.astype(BF16)

    @pl.when(j == 0)
    def _():
        _normalize_rows(x_ref, g_ref, xn_ref)
        project_v(0)

    @pl.when(j == 1)
    def _():
        project_v(1)

    @pl.when((j == 2) | (j == 4))
    def _():
        for gl in range(groups_per_step):
            cols = slice(gl * GD, (gl + 1) * GD)
            u_ref[:, cols] = proj_cols(cols).astype(BF16)

    @pl.when(j == 3)
    def _():
        gate(0)

    @pl.when(j == 5)
    def _():
        gate(1)


def _in_proj_sgu(x2d, g, w, ln_g, ln_b, sgu_w, sgu_b):
    M, D = x2d.shape
    W = w.shape[1] // 3
    assert W == 2 * PROJ_TN and len(_SGU_STEP_BLOCKS) == 3 * W // PROJ_TN
    fixed = lambda i, j: (0, 0)
    return pl.pallas_call(
        _in_proj_sgu_kernel,
        grid=(M // PROJ_TM, len(_SGU_STEP_BLOCKS)),
        in_specs=[pl.BlockSpec((PROJ_TM, D), lambda i, j: (i, 0)),
                  pl.BlockSpec((1, D), fixed),
                  pl.BlockSpec((D, PROJ_TN), lambda i, j: (0, _sgu_step_block(j))),
                  pl.BlockSpec((1, W), fixed),
                  pl.BlockSpec((1, W), fixed),
                  pl.BlockSpec(sgu_w.shape, lambda i, j: (0, 0, 0)),
                  pl.BlockSpec((SGU_CHUNK, SGU_GROUPS), fixed)],
        out_specs=pl.BlockSpec((PROJ_TM, PROJ_TN), lambda i, j: (i, jnp.where(j <= 3, 0, 1))),
        out_shape=jax.ShapeDtypeStruct((M, W), BF16),
        scratch_shapes=[pltpu.VMEM((PROJ_TM, D), BF16),
                        pltpu.VMEM((PROJ_TM, W), BF16),
                        pltpu.VMEM((PROJ_TM, PROJ_TN), BF16),
                        pltpu.VMEM((3, PROJ_TM, 1), F32)],
        compiler_params=_params(("arbitrary", "arbitrary")),
        name="in_proj_sgu",
    )(x2d, g.reshape(1, D), w, ln_g.reshape(1, W), ln_b.reshape(1, W), sgu_w, sgu_b.T)


def _mlstm_kernel(qp_ref, kp_ref, v_ref, gt_ref, og_ref, z_ref, cwq_ref, cwk_ref, cbq_ref, cbk_ref,
                  gb_ref, ng_ref, out_ref, xs_ref, qc_ref, kc_ref):
    h = pl.program_id(1)
    S = qp_ref.shape[0]
    LB = ML_BLOCK
    nb = S // LB
    DK, DV = ML_QK_DIM, ML_V_DIM

    for src, cw_ref, cb_ref, dst, scale in ((kp_ref, cwk_ref, cbk_ref, kc_ref, DK ** -0.5),
                                            (qp_ref, cwq_ref, cbq_ref, qc_ref, 1.0)):
        xs_ref[0:8, :] = jnp.zeros((8, DK), F32)
        xs_ref[8:8 + S, :] = src[...].astype(F32)
        cw = cw_ref[...]
        cb = cb_ref[...]
        for c in range(S // CONV_ROWS):
            acc = cb
            for j in range(CONV_WIDTH):
                start = 8 + c * CONV_ROWS - (CONV_WIDTH - 1) + j
                acc = acc + cw[j:j + 1, :] * xs_ref[start:start + CONV_ROWS, :]
            dst[c * CONV_ROWS:(c + 1) * CONV_ROWS, :] = (_silu(acc) * scale).astype(BF16)

    gate_rows = gt_ref[...] + jnp.concatenate([gb_ref[...]] * (S // GATE_COLS), axis=1)
    sub8 = _iota((8, S), 0)
    i_all = jnp.sum(jnp.where(sub8 == h, gate_rows, 0.0), axis=0, keepdims=True)
    logf_all = _log_sigmoid(jnp.sum(jnp.where(sub8 == h + ML_HEADS, gate_rows, 0.0), axis=0, keepdims=True))

    blk = _iota((nb, LB), 0)

    def to_blocks(row):
        tile = jnp.zeros((nb, LB), F32)
        for c in range(nb):
            tile = jnp.where(blk == c, jnp.broadcast_to(row[:, c * LB:(c + 1) * LB], (nb, LB)), tile)
        return tile

    sub = _iota((LB, LB), 0)
    lane = _iota((LB, LB), 1)
    causal = lane <= sub
    i_b = to_blocks(i_all)
    b_b = jnp.dot(to_blocks(logf_all), (sub <= lane).astype(F32), preferred_element_type=F32,
                  precision=HIGHEST)
    g_tot = b_b[:, LB - 1:LB]
    w_b = g_tot - b_b + i_b
    m_loc = jnp.max(w_b, axis=1, keepdims=True)
    e_b = jnp.exp(w_b - m_loc)
    b_cols = jnp.concatenate([b_b, jnp.zeros((GATE_COLS - nb, LB), F32)], axis=0).T

    norm_g = ng_ref[0]
    ct = jnp.zeros((DK, DV), F32)
    n = jnp.zeros((DK, 1), F32)
    m = jnp.zeros((1, 1), F32)
    for c in range(nb):
        rows = slice(c * LB, (c + 1) * LB)
        b_row, i_row, e_row = b_b[c:c + 1, :], i_b[c:c + 1, :], e_b[c:c + 1, :]
        b_col = b_cols[:, c:c + 1]
        q = qc_ref[rows, :]
        v = v_ref[rows, :]
        k_t = kc_ref[rows, :].astype(F32).T

        ct_in, n_in, m_in = ct, n, m
        if c + 1 < nb:
            ke = k_t * e_row
            ct_loc = jnp.dot(ke.astype(BF16), v, preferred_element_type=F32)
            n_loc = jnp.sum(ke, axis=1, keepdims=True)
            g_c, ml_c = g_tot[c:c + 1, :], m_loc[c:c + 1, :]
            m = jnp.maximum(g_c + m_in, ml_c)
            a = jnp.exp(g_c + m_in - m)
            cc = jnp.exp(ml_c - m)
            ct = a * ct_in + cc * ct_loc
            n = a * n_in + cc * n_loc

        rhs = jnp.concatenate([k_t.astype(BF16), ct_in.astype(BF16),
                               jnp.broadcast_to(n_in, (DK, 128)).astype(BF16)], axis=1)
        big = jnp.dot(q, rhs, preferred_element_type=F32)
        qk, inter, qn = big[:, :LB], big[:, LB:LB + DV], big[:, LB + DV:LB + DV + 1]

        log_d = jnp.where(causal, b_col - b_row + i_row, -jnp.inf)
        inter_log = b_col + m_in
        m_t = jnp.maximum(inter_log, jnp.max(log_d, axis=1, keepdims=True))
        d_mat = jnp.exp(log_d - m_t)
        inter_w = jnp.exp(inter_log - m_t)
        s_mat = qk * d_mat
        num = jnp.dot(s_mat.astype(BF16), v, preferred_element_type=F32) + inter_w * inter
        den = jnp.sum(s_mat, axis=1, keepdims=True) + inter_w * qn
        hm = num / jnp.maximum(jnp.abs(den), jnp.exp(-m_t))

        hn = hm * lax.rsqrt(jnp.mean(hm * hm, axis=-1, keepdims=True) + EPS) * norm_g
        og = og_ref[rows, :].astype(F32)
        z = z_ref[rows, :].astype(F32)
        out_ref[rows, :] = (_sigmoid(og) * hn * _silu(z)).astype(out_ref.dtype)


def _mlstm(proj, gates, conv_w, conv_b, i_bias, f_bias, norm_g, B, S):
    M = B * S
    H = ML_HEADS
    assert S // ML_BLOCK == 8, "block rows are packed into one 8-sublane tile"
    gate_bias = jnp.broadcast_to(jnp.concatenate([i_bias, f_bias])[:, None], (2 * H, GATE_COLS))
    conv_b2 = conv_b.reshape(1, -1)
    norm_g3 = norm_g.reshape(H, 1, ML_V_DIM)
    assert proj.shape[2] == ML_V_DIM == 2 * ML_QK_DIM
    qb, kb = ML_Q_OFF // ML_V_DIM, ML_K_OFF // ML_V_DIM
    vb, ob, zb = ML_V_OFF // ML_V_DIM, ML_O_OFF // ML_V_DIM, ML_Z_OFF // ML_V_DIM
    return pl.pallas_call(
        _mlstm_kernel,
        grid=(B, H),
        in_specs=[
            pl.BlockSpec((None, S, ML_QK_DIM), lambda b, h: (qb + h // 2, b, h % 2)),
            pl.BlockSpec((None, S, ML_QK_DIM), lambda b, h: (kb + h // 2, b, h % 2)),
            pl.BlockSpec((None, S, ML_V_DIM), lambda b, h: (vb + h, b, 0)),
            pl.BlockSpec((2 * H, S), lambda b, h: (0, b)),
            pl.BlockSpec((None, S, ML_V_DIM), lambda b, h: (ob + h, b, 0)),
            pl.BlockSpec((None, S, ML_V_DIM), lambda b, h: (zb + h, b, 0)),
            pl.BlockSpec((CONV_WIDTH, ML_QK_DIM), lambda b, h: (0, h)),
            pl.BlockSpec((CONV_WIDTH, ML_QK_DIM), lambda b, h: (0, H + h)),
            pl.BlockSpec((1, ML_QK_DIM), lambda b, h: (0, h)),
            pl.BlockSpec((1, ML_QK_DIM), lambda b, h: (0, H + h)),
            pl.BlockSpec((2 * H, GATE_COLS), lambda b, h: (0, 0)),
            pl.BlockSpec((1, 1, ML_V_DIM), lambda b, h: (h, 0, 0)),
        ],
        out_specs=pl.BlockSpec((S, ML_V_DIM), lambda b, h: (b, h)),
        out_shape=jax.ShapeDtypeStruct((M, H * ML_V_DIM), BF16),
        scratch_shapes=[pltpu.VMEM((8 + S, ML_QK_DIM), F32),
                        pltpu.VMEM((S, ML_QK_DIM), BF16),
                        pltpu.VMEM((S, ML_QK_DIM), BF16)],
        compiler_params=_params(("arbitrary", "arbitrary")),
        name="mlstm",
    )(proj, proj, proj, gates, proj, proj, conv_w, conv_w, conv_b2, conv_b2, gate_bias, norm_g3)


def _diff_attn_kernel(lam_init, qr_ref, kr_ref, v_ref, z_ref, lq1_ref, lk1_ref, lq2_ref, lk2_ref, ng_ref,
                      o_ref):
    S = kr_ref.shape[0]
    QS, DH = ATT_QSUB, DA_QK_DIM
    diag_mask = _iota((QS, QS), 1) <= _iota((QS, QS), 0)

    def scores(q0, m):
        q = qr_ref[q0:q0 + QS, m * DH:(m + 1) * DH]
        kv_len = q0 + QS
        s = lax.dot_general(q, kr_ref[0:kv_len, m * DH:(m + 1) * DH], _NT, preferred_element_type=F32)
        s_diag = jnp.where(diag_mask, s[:, q0:], -jnp.inf)
        return s_diag if q0 == 0 else jnp.concatenate([s[:, :q0], s_diag], axis=1)

    def softmax_pv(q0, s):
        p = jnp.exp2(s - jnp.max(s, axis=1, keepdims=True))
        acc = jnp.dot(p.astype(BF16), v_ref[0:q0 + QS, :], preferred_element_type=F32)
        return acc, jnp.sum(p, axis=1, keepdims=True)

    lam = (jnp.exp(jnp.sum(lq1_ref[...] * lk1_ref[...], axis=1, keepdims=True))
           - jnp.exp(jnp.sum(lq2_ref[...] * lk2_ref[...], axis=1, keepdims=True)) + lam_init)
    norm_g = ng_ref[0] * (1.0 - lam_init)

    order = list(reversed(range(0, S, QS)))
    both_maps = lambda q0: (scores(q0, 0), scores(q0, 1))
    ahead = [both_maps(q0) for q0 in order[:ATT_SCORES_AHEAD]]
    for idx, q0 in enumerate(order):
        s1, s2 = ahead.pop(0)
        if idx + ATT_SCORES_AHEAD < len(order):
            ahead.append(both_maps(order[idx + ATT_SCORES_AHEAD]))
        acc1, l1 = softmax_pv(q0, s1)
        acc2, l2 = softmax_pv(q0, s2)
        o = acc1 / l1 - lam * (acc2 / l2)
        hn = o * lax.rsqrt(jnp.mean(o * o, axis=-1, keepdims=True) + EPS) * norm_g
        o_ref[q0:q0 + QS, :] = (hn * _silu(z_ref[q0:q0 + QS, :].astype(F32))).astype(o_ref.dtype)


def _diff_attn(proj, lq1, lk1, lq2, lk2, norm_g, lam_init, B, S):
    M = B * S
    H = DA_HEADS
    assert proj.shape[2] == DA_V_DIM
    qb, kb, vb, zb = (DA_Q_OFF // DA_V_DIM, DA_K_OFF // DA_V_DIM, DA_V_OFF // DA_V_DIM, DA_Z_OFF // DA_V_DIM)
    vec = lambda a: a.reshape(1, DA_QK_DIM)
    vec_spec = pl.BlockSpec((1, DA_QK_DIM), lambda b, h: (0, 0))
    return pl.pallas_call(
        functools.partial(_diff_attn_kernel, lam_init),
        grid=(B, H),
        in_specs=[
            pl.BlockSpec((None, S, DA_V_DIM), lambda b, h: (qb + h, b, 0)),
            pl.BlockSpec((None, S, DA_V_DIM), lambda b, h: (kb + h, b, 0)),
            pl.BlockSpec((None, S, DA_V_DIM), lambda b, h: (vb + h, b, 0)),
            pl.BlockSpec((None, S, DA_V_DIM), lambda b, h: (zb + h, b, 0)),
            vec_spec, vec_spec, vec_spec, vec_spec,
            pl.BlockSpec((1, 1, DA_V_DIM), lambda b, h: (h, 0, 0)),
        ],
        out_specs=pl.BlockSpec((S, DA_V_DIM), lambda b, h: (b, h)),
        out_shape=jax.ShapeDtypeStruct((M, H * DA_V_DIM), BF16),
        compiler_params=_params(("arbitrary", "arbitrary")),
        name="diff_attn",
    )(proj, proj, proj, proj, vec(lq1), vec(lk1), vec(lq2), vec(lk2), norm_g.reshape(H, 1, DA_V_DIM))


def _ple_tail(x1_ref, ssq, p_ref, gw_ref, pw_ref, xg_ref, emit):
    D = x1_ref.shape[1]
    inv = lax.rsqrt(ssq * (1.0 / D) + EPS)
    p_bf = p_ref[...].astype(BF16)
    for n0 in range(0, D, TAIL_NC):
        cols = slice(n0, n0 + TAIL_NC)
        gate = _sigmoid(jnp.dot(xg_ref[...], gw_ref[:, cols].astype(BF16), preferred_element_type=F32) * inv)
        pp = jnp.dot(p_bf, pw_ref[:, cols].astype(BF16), preferred_element_type=F32)
        emit(n0, x1_ref[:, cols] + gate * pp)


def _out_ple_kernel(n_lhs, has_final_norm, *refs):
    lhs_refs = refs[:n_lhs]
    x_ref, p_ref, wo_ref, png_ref, gw_ref, pw_ref = refs[n_lhs:n_lhs + 6]
    if has_final_norm:
        fng_ref, o_ref, x1_ref, xn_ref = refs[n_lhs + 6:]
    else:
        o_ref, x1_ref, xn_ref = refs[n_lhs + 6:]
    TM, D = x_ref.shape

    ssq = jnp.zeros((TM, 1), F32)
    for n0 in range(0, D, TAIL_NC):
        cols = slice(n0, n0 + TAIL_NC)
        y = x_ref[:, cols]
        k0 = 0
        for lhs_ref in lhs_refs:
            k1 = k0 + lhs_ref.shape[1]
            y = y + jnp.dot(lhs_ref[...], wo_ref[k0:k1, cols].astype(BF16), preferred_element_type=F32)
            k0 = k1
        x1_ref[:, cols] = y
        xn_ref[:, cols] = (y * png_ref[:, cols]).astype(BF16)
        ssq = ssq + jnp.sum(y * y, axis=1, keepdims=True)

    ssq_out = [jnp.zeros((TM, 1), F32)]

    def emit(n0, val):
        o_ref[:, n0:n0 + TAIL_NC] = val
        if has_final_norm:
            ssq_out[0] = ssq_out[0] + jnp.sum(val * val, axis=1, keepdims=True)

    _ple_tail(x1_ref, ssq, p_ref, gw_ref, pw_ref, xn_ref, emit)

    if has_final_norm:
        inv = lax.rsqrt(ssq_out[0] * (1.0 / D) + EPS)
        for n0 in range(0, D, TAIL_NC):
            cols = slice(n0, n0 + TAIL_NC)
            o_ref[:, cols] = o_ref[:, cols] * inv * fng_ref[:, cols]


def _out_ple(lhs_list, x2d, p3d, layer, w_out, ple_g, gate_w3d, proj_w3d, final_g=None, name="out_ple"):
    M, D = x2d.shape
    P = p3d.shape[2]
    TM = TAIL_TM
    row = lambda i: (i, 0)
    fixed = lambda i: (0, 0)
    layer_block = lambda i: (layer, 0, 0)
    in_specs = [pl.BlockSpec((TM, lhs.shape[1]), row) for lhs in lhs_list]
    in_specs += [pl.BlockSpec((TM, D), row),
                 pl.BlockSpec((None, TM, P), lambda i: (layer, i, 0)),
                 _resident(w_out.shape, fixed),
                 _resident((1, D), fixed),
                 pl.BlockSpec((None, D, D), layer_block, pipeline_mode=pl.Buffered(1)),
                 pl.BlockSpec((None, P, D), layer_block, pipeline_mode=pl.Buffered(1))]
    args = [*lhs_list, x2d, p3d, w_out, ple_g.reshape(1, D), gate_w3d, proj_w3d]
    if final_g is not None:
        in_specs.append(_resident((1, D), fixed))
        args.append(final_g.reshape(1, D))
    return pl.pallas_call(
        functools.partial(_out_ple_kernel, len(lhs_list), final_g is not None),
        grid=(M // TM,),
        in_specs=in_specs,
        out_specs=pl.BlockSpec((TM, D), row),
        out_shape=jax.ShapeDtypeStruct((M, D), F32),
        scratch_shapes=[pltpu.VMEM((TM, D), F32), pltpu.VMEM((TM, D), BF16)],
        compiler_params=_params(("arbitrary",)),
        name=name,
    )(*args)


def kernel(x, p, positions, ab_norm_g, ab_w_in, ab_conv_w, ab_conv_b, ab_i_bias, ab_f_bias, ab_ml_norm_g,
           ab_lam_q1, ab_lam_k1, ab_lam_q2, ab_lam_k2, ab_da_norm_g, ab_w_out, c_norm_g, c_w_in, c_ln_g,
           c_ln_b, c_sgu_w, c_sgu_b, c_w_out, ple_norm_g, ple_gate_w, ple_proj_w, final_norm_g):
    B, S, D = x.shape
    M = B * S
    assert ab_norm_g.shape[0] == 1 and c_norm_g.shape[0] == 1 and p.shape[0] == 2
    x2d = x.reshape(M, D)
    p3d = p.reshape(2, M, p.shape[-1])

    cos2, sin2 = _rope_tables(positions)
    proj, gates = _in_proj_ab(x2d, ab_norm_g[0], ab_w_in[0].T, cos2, sin2)

    a_out = _mlstm(proj, gates, ab_conv_w[0], ab_conv_b[0], ab_i_bias[0], ab_f_bias[0], ab_ml_norm_g[0], B, S)

    lam_init = 0.8 - 0.6 * math.exp(-0.3 * 0)
    b_out = _diff_attn(proj, ab_lam_q1[0], ab_lam_k1[0], ab_lam_q2[0], ab_lam_k2[0], ab_da_norm_g[0],
                       lam_init, B, S)

    x2 = _out_ple([a_out, b_out], x2d, p3d, 0, ab_w_out[0], ple_norm_g[0], ple_gate_w, ple_proj_w,
                  name="out_ple_ab")

    t = _in_proj_sgu(x2, c_norm_g[0], c_w_in[0], c_ln_g[0], c_ln_b[0], c_sgu_w[0], c_sgu_b[0])
    out = _out_ple([t], x2, p3d, 1, c_w_out[0], ple_norm_g[1], ple_gate_w, ple_proj_w,
                   final_g=final_norm_g, name="out_ple_c")
    return out.reshape(B, S, D)
```

```python
import functools
import math

import jax
import jax.numpy as jnp
from jax import lax
from jax.experimental import pallas as pl
from jax.experimental.pallas import tpu as pltpu

F32 = jnp.float32
BF16 = jnp.bfloat16
HIGHEST = lax.Precision.HIGHEST

EPS = 1e-6
ROPE_THETA = 10000.0
CONV_WIDTH = 4

ML_HEADS = 4
ML_QK_DIM = 128
ML_V_DIM = 256
DA_HEADS = 4
DA_QK_DIM = 128
DA_V_DIM = 256
SGU_CHUNK = 128
SGU_GROUPS = 8
SGU_GROUP_DIM = 256

ML_Q_OFF, ML_K_OFF, ML_V_OFF, ML_O_OFF, ML_Z_OFF = 0, 512, 1024, 2048, 3072
DA_Q_OFF, DA_K_OFF, DA_V_OFF, DA_Z_OFF = 4096, 5120, 6144, 7168
GATE_COLS = 128
GATE_START = ML_V_OFF + ML_HEADS * ML_V_DIM

VMEM_LIMIT_BYTES = 56 * 1024 * 1024

PROJ_TM = 1024
PROJ_TN = 1024
NORM_ROWS = 128
TAIL_TM = 256
TAIL_NC = 512
PROJ_SLAB = 256
CONV_ROWS = 128
ML_BLOCK = 256
ATT_QSUB = 128
ATT_SCORES_AHEAD = 6


def _params(semantics):
    return pltpu.CompilerParams(dimension_semantics=semantics, vmem_limit_bytes=VMEM_LIMIT_BYTES)


def _resident(shape, index_map):
    return pl.BlockSpec(shape, index_map, pipeline_mode=pl.Buffered(1))


def _sigmoid(x):
    return 1.0 / (1.0 + jnp.exp(-x))


def _silu(x):
    return x * _sigmoid(x)


def _log_sigmoid(x):
    return jnp.minimum(x, 0.0) - jnp.log(1.0 + jnp.exp(-jnp.abs(x)))


def _iota(shape, axis):
    return lax.broadcasted_iota(jnp.int32, shape, axis)


def _rope_tables_into(pos, cos_ref, sin_ref):
    half_rows = pos.shape[0]
    lane = _iota(pos.shape, 1)
    lo = lane < 64
    j = jnp.where(lo, lane, lane - 64).astype(F32)
    freq = jnp.exp(j * (-math.log(ROPE_THETA) / 64.0))
    ang = pos * freq
    c = jnp.cos(ang)
    s = jnp.sin(ang)
    cr = pltpu.roll(c, 64, 1)
    sr = pltpu.roll(s, 64, 1)
    cos_ref[0:half_rows, :] = jnp.where(lo, c, cr)
    cos_ref[half_rows:2 * half_rows, :] = jnp.where(lo, cr, c)
    sin_ref[0:half_rows, :] = jnp.where(lo, -s, sr)
    sin_ref[half_rows:2 * half_rows, :] = jnp.where(lo, -sr, s)


def _rope_position_pairs(positions, tile_rows):
    pos = positions.astype(F32).reshape(-1, 2, tile_rows // 2, 1)
    shape = (pos.shape[0], tile_rows // 2, 64)
    return jnp.concatenate([jnp.broadcast_to(pos[:, 0], shape), jnp.broadcast_to(pos[:, 1], shape)], axis=-1)


def _normalize_rows(x_ref, g_ref, xn_ref):
    g = g_ref[...]
    for r0 in range(0, x_ref.shape[0], NORM_ROWS):
        rows = slice(r0, r0 + NORM_ROWS)
        x = x_ref[rows, :]
        ms = jnp.mean(x * x, axis=-1, keepdims=True)
        xn_ref[rows, :] = (x * lax.rsqrt(ms + EPS) * g).astype(BF16)


def _x_tile_index(n_tiles, i, j):
    return jnp.minimum(i + (j >= 1), n_tiles - 1), 0


_NT = (((1,), (1,)), ((), ()))


def _in_proj_ab_kernel(n_gate, x_ref, g_ref, wt_ref, edge_ref, wgt_ref, pos_ref, o_ref, og_ref,
                       xn_ref, cos_ref, sin_ref):
    j = pl.program_id(1)
    TN, D = wt_ref.shape
    rope_q_tile, rope_k_tile = DA_Q_OFF // TN, DA_K_OFF // TN
    first_shifted = GATE_START // TN

    @pl.when(j == 0)
    def _():
        _normalize_rows(x_ref, g_ref, xn_ref)
        wg = jnp.concatenate([wgt_ref[...], jnp.zeros((16 - n_gate, D), F32)], axis=0).astype(BF16)
        og_ref[...] = lax.dot_general(wg, xn_ref[...], _NT, preferred_element_type=F32)[0:n_gate, :]

    def slabs(shift):
        for r0 in range(0, TN, PROJ_SLAB):
            lo, hi = r0 + shift, r0 + PROJ_SLAB + shift
            if hi <= TN:
                w = wt_ref[lo:hi, :]
            else:
                w = jnp.concatenate([wt_ref[lo:TN, :], edge_ref[...]], axis=0)
            yield r0, lax.dot_general(xn_ref[...], w.astype(BF16), _NT, preferred_element_type=F32)

    def plain(shift):
        for r0, y in slabs(shift):
            o_ref[r0 // PROJ_SLAB] = y.astype(o_ref.dtype)

    def rope(shift, scale, fill_tables):
        if fill_tables:
            _rope_tables_into(pos_ref[0], cos_ref, sin_ref)
        cs = cos_ref[...]
        sn = sin_ref[...]
        for r0, y in slabs(shift):
            for c0 in range(0, PROJ_SLAB, DA_QK_DIM):
                x = y[:, c0:c0 + DA_QK_DIM]
                r = x * cs + pltpu.roll(x, DA_QK_DIM // 2, 1) * sn
                if scale is not None:
                    r = r * scale
                o_ref[r0 // PROJ_SLAB, :, c0:c0 + DA_QK_DIM] = r.astype(o_ref.dtype)

    @pl.when(j < first_shifted)
    def _():
        plain(0)

    @pl.when((j >= first_shifted) & (j != rope_q_tile) & (j != rope_k_tile))
    def _():
        plain(n_gate)

    @pl.when(j == rope_q_tile)
    def _():
        rope(n_gate, DA_QK_DIM ** -0.5 * math.log2(math.e), True)

    @pl.when(j == rope_k_tile)
    def _():
        rope(n_gate, None, False)


def _in_proj_ab(x2d, g, w_t, positions):
    M, D = x2d.shape
    n_gate = 2 * ML_HEADS
    N = w_t.shape[0] - n_gate
    TM, TN = PROJ_TM, PROJ_TN
    assert n_gate == 8 and GATE_START % TN == 0 and N % TN == 0
    assert DA_Q_OFF % TN == 0 and DA_K_OFF - DA_Q_OFF == TN and DA_V_OFF - DA_K_OFF == TN
    fixed = lambda i, j: (0, 0)
    return pl.pallas_call(
        functools.partial(_in_proj_ab_kernel, n_gate),
        grid=(M // TM, N // TN),
        in_specs=[pl.BlockSpec((TM, D), functools.partial(_x_tile_index, M // TM)),
                  pl.BlockSpec((1, D), fixed),
                  pl.BlockSpec((TN, D), lambda i, j: (j, 0)),
                  pl.BlockSpec((n_gate, D), lambda i, j: ((j + 1) * (TN // n_gate), 0)),
                  pl.BlockSpec((n_gate, D), lambda i, j: (GATE_START // n_gate, 0)),
                  pl.BlockSpec((1, TM // 2, DA_QK_DIM), lambda i, j: (i, 0, 0))],
        out_specs=[pl.BlockSpec((TN // PROJ_SLAB, TM, PROJ_SLAB), lambda i, j: (j, i, 0)),
                   pl.BlockSpec((n_gate, TM), lambda i, j: (0, i))],
        out_shape=[jax.ShapeDtypeStruct((N // PROJ_SLAB, M, PROJ_SLAB), BF16),
                   jax.ShapeDtypeStruct((n_gate, M), F32)],
        scratch_shapes=[pltpu.VMEM((TM, D), BF16),
                        pltpu.VMEM((TM, DA_QK_DIM), F32),
                        pltpu.VMEM((TM, DA_QK_DIM), F32)],
        compiler_params=_params(("arbitrary", "arbitrary")),
        name="in_proj_ab",
    )(x2d, g.reshape(1, D), w_t, w_t, w_t, _rope_position_pairs(positions, TM))


_SGU_STEP_BLOCKS = (2, 3, 0, 4, 1, 5)


def _sgu_step_block(j):
    blk = jnp.int32(_SGU_STEP_BLOCKS[-1])
    for step in range(len(_SGU_STEP_BLOCKS) - 2, -1, -1):
        blk = jnp.where(j == step, _SGU_STEP_BLOCKS[step], blk)
    return blk


def _in_proj_sgu_kernel(x_ref, g_ref, w_ref, lng_ref, lnb_ref, sw_ref, sbt_ref, t_ref, xn_ref, v_ref, u_ref,
                        st_ref):
    j = pl.program_id(1)
    TM = x_ref.shape[0]
    TN = w_ref.shape[1]
    W = v_ref.shape[1]
    L = SGU_CHUNK
    GD = SGU_GROUP_DIM
    groups_per_step = TN // GD

    def proj_cols(cols):
        return jnp.dot(xn_ref[...], w_ref[:, cols].astype(BF16), preferred_element_type=F32)

    def project_v(half):
        for gl in range(groups_per_step):
            cols = slice(gl * GD, (gl + 1) * GD)
            slab = proj_cols(cols)
            v_ref[:, half * TN + gl * GD:half * TN + (gl + 1) * GD] = slab.astype(BF16)
            if half == 0 and gl == 0:
                shift = jnp.mean(slab, axis=1, keepdims=True)
                s1 = jnp.zeros((TM, 1), F32)
                s2 = jnp.zeros((TM, 1), F32)
            elif gl == 0:
                shift, s1, s2 = st_ref[0], st_ref[1], st_ref[2]
            d = slab - shift
            s1 = s1 + jnp.sum(d, axis=1, keepdims=True)
            s2 = s2 + jnp.sum(d * d, axis=1, keepdims=True)
        if half == 0:
            st_ref[0], st_ref[1], st_ref[2] = shift, s1, s2
        else:
            m1 = s1 * (1.0 / W)
            st_ref[0] = shift + m1
            st_ref[1] = lax.rsqrt(jnp.maximum(s2 * (1.0 / W) - m1 * m1, 0.0) + EPS)

    def gate(half):
        causal = _iota((L, L), 1) <= _iota((L, L), 0)
        for gl in range(groups_per_step):
            g = half * groups_per_step + gl
            cols = slice(gl * GD, (gl + 1) * GD)
            vcols = slice(g * GD, (g + 1) * GD)
            z = proj_cols(cols)
            w_causal = jnp.where(causal, sw_ref[g], 0.0).astype(BF16)
            bias = sbt_ref[:, g:g + 1]
            lng = lng_ref[:, vcols]
            lnb = lnb_ref[:, vcols]
            for c in range(TM // L):
                rows = slice(c * L, (c + 1) * L)
                vn = ((v_ref[rows, vcols].astype(F32) - st_ref[0, rows, :]) * st_ref[1, rows, :] * lng
                      + lnb).astype(BF16)
                sv = jnp.dot(w_causal, vn, preferred_element_type=F32) + bias
                t = u_ref[rows, cols].astype(F32) * sv * _silu(z[rows, :])
                t_ref[rows, cols] = t.astype(BF16)

    @pl.when(j == 0)
    def _():
        _normalize_rows(x_ref, g_ref, xn_ref)
        project_v(0)

    @pl.when(j == 1)
    def _():
        project_v(1)

    @pl.when((j == 2) | (j == 4))
    def _():
        for gl in range(groups_per_step):
            cols = slice(gl * GD, (gl + 1) * GD)
            u_ref[:, cols] = proj_cols(cols).astype(BF16)

    @pl.when(j == 3)
    def _():
        gate(0)

    @pl.when(j == 5)
    def _():
        gate(1)


def _in_proj_sgu(x2d, g, w, ln_g, ln_b, sgu_w, sgu_b):
    M, D = x2d.shape
    W = w.shape[1] // 3
    assert W == 2 * PROJ_TN and len(_SGU_STEP_BLOCKS) == 3 * W // PROJ_TN
    fixed = lambda i, j: (0, 0)
    return pl.pallas_call(
        _in_proj_sgu_kernel,
        grid=(M // PROJ_TM, len(_SGU_STEP_BLOCKS)),
        in_specs=[pl.BlockSpec((PROJ_TM, D), lambda i, j: (i, 0)),
                  pl.BlockSpec((1, D), fixed),
                  pl.BlockSpec((D, PROJ_TN), lambda i, j: (0, _sgu_step_block(j))),
                  pl.BlockSpec((1, W), fixed),
                  pl.BlockSpec((1, W), fixed),
                  pl.BlockSpec(sgu_w.shape, lambda i, j: (0, 0, 0)),
                  pl.BlockSpec((SGU_CHUNK, SGU_GROUPS), fixed)],
        out_specs=pl.BlockSpec((PROJ_TM, PROJ_TN), lambda i, j: (i, jnp.where(j <= 3, 0, 1))),
        out_shape=jax.ShapeDtypeStruct((M, W), BF16),
        scratch_shapes=[pltpu.VMEM((PROJ_TM, D), BF16),
                        pltpu.VMEM((PROJ_TM, W), BF16),
                        pltpu.VMEM((PROJ_TM, PROJ_TN), BF16),
                        pltpu.VMEM((3, PROJ_TM, 1), F32)],
        compiler_params=_params(("arbitrary", "arbitrary")),
        name="in_proj_sgu",
    )(x2d, g.reshape(1, D), w, ln_g.reshape(1, W), ln_b.reshape(1, W), sgu_w, sgu_b.T)


def _mlstm_kernel(qp_ref, kp_ref, v_ref, gt_ref, og_ref, z_ref, cwq_ref, cwk_ref, cbq_ref, cbk_ref,
                  gb_ref, ng_ref, out_ref, xs_ref, qc_ref, kc_ref):
    h = pl.program_id(1)
    S = qp_ref.shape[0]
    LB = ML_BLOCK
    nb = S // LB
    DK, DV = ML_QK_DIM, ML_V_DIM

    for src, cw_ref, cb_ref, dst, scale in ((kp_ref, cwk_ref, cbk_ref, kc_ref, DK ** -0.5),
                                            (qp_ref, cwq_ref, cbq_ref, qc_ref, 1.0)):
        xs_ref[0:8, :] = jnp.zeros((8, DK), F32)
        xs_ref[8:8 + S, :] = src[...].astype(F32)
        cw = cw_ref[...]
        cb = cb_ref[...]
        for c in range(S // CONV_ROWS):
            acc = cb
            for j in range(CONV_WIDTH):
                start = 8 + c * CONV_ROWS - (CONV_WIDTH - 1) + j
                acc = acc + cw[j:j + 1, :] * xs_ref[start:start + CONV_ROWS, :]
            dst[c * CONV_ROWS:(c + 1) * CONV_ROWS, :] = (_silu(acc) * scale).astype(BF16)

    gate_rows = gt_ref[...] + jnp.concatenate([gb_ref[...]] * (S // GATE_COLS), axis=1)
    sub8 = _iota((8, S), 0)
    i_all = jnp.sum(jnp.where(sub8 == h, gate_rows, 0.0), axis=0, keepdims=True)
    logf_all = _log_sigmoid(jnp.sum(jnp.where(sub8 == h + ML_HEADS, gate_rows, 0.0), axis=0, keepdims=True))

    blk = _iota((nb, LB), 0)

    def to_blocks(row):
        tile = jnp.zeros((nb, LB), F32)
        for c in range(nb):
            tile = jnp.where(blk == c, jnp.broadcast_to(row[:, c * LB:(c + 1) * LB], (nb, LB)), tile)
        return tile

    sub = _iota((LB, LB), 0)
    lane = _iota((LB, LB), 1)
    causal = lane <= sub
    i_b = to_blocks(i_all)
    b_b = jnp.dot(to_blocks(logf_all), (sub <= lane).astype(F32), preferred_element_type=F32,
                  precision=HIGHEST)
    g_tot = b_b[:, LB - 1:LB]
    w_b = g_tot - b_b + i_b
    m_loc = jnp.max(w_b, axis=1, keepdims=True)
    e_b = jnp.exp(w_b - m_loc)
    b_cols = jnp.concatenate([b_b, jnp.zeros((GATE_COLS - nb, LB), F32)], axis=0).T

    norm_g = ng_ref[0]
    ct = jnp.zeros((DK, DV), F32)
    n = jnp.zeros((DK, 1), F32)
    m = jnp.zeros((1, 1), F32)
    for c in range(nb):
        rows = slice(c * LB, (c + 1) * LB)
        b_row, i_row, e_row = b_b[c:c + 1, :], i_b[c:c + 1, :], e_b[c:c + 1, :]
        b_col = b_cols[:, c:c + 1]
        q = qc_ref[rows, :]
        v = v_ref[rows, :]
        k_t = kc_ref[rows, :].astype(F32).T

        ct_in, n_in, m_in = ct, n, m
        if c + 1 < nb:
            ke = k_t * e_row
            ct_loc = jnp.dot(ke.astype(BF16), v, preferred_element_type=F32)
            n_loc = jnp.sum(ke, axis=1, keepdims=True)
            g_c, ml_c = g_tot[c:c + 1, :], m_loc[c:c + 1, :]
            m = jnp.maximum(g_c + m_in, ml_c)
            a = jnp.exp(g_c + m_in - m)
            cc = jnp.exp(ml_c - m)
            ct = a * ct_in + cc * ct_loc
            n = a * n_in + cc * n_loc

        rhs = jnp.concatenate([k_t.astype(BF16), ct_in.astype(BF16),
                               jnp.broadcast_to(n_in, (DK, 128)).astype(BF16)], axis=1)
        big = jnp.dot(q, rhs, preferred_element_type=F32)
        qk, inter, qn = big[:, :LB], big[:, LB:LB + DV], big[:, LB + DV:LB + DV + 1]

        log_d = jnp.where(causal, b_col - b_row + i_row, -jnp.inf)
        inter_log = b_col + m_in
        m_t = jnp.maximum(inter_log, jnp.max(log_d, axis=1, keepdims=True))
        d_mat = jnp.exp(log_d - m_t)
        inter_w = jnp.exp(inter_log - m_t)
        s_mat = qk * d_mat
        num = jnp.dot(s_mat.astype(BF16), v, preferred_element_type=F32) + inter_w * inter
        den = jnp.sum(s_mat, axis=1, keepdims=True) + inter_w * qn
        hm = num / jnp.maximum(jnp.abs(den), jnp.exp(-m_t))

        hn = hm * lax.rsqrt(jnp.mean(hm * hm, axis=-1, keepdims=True) + EPS) * norm_g
        og = og_ref[rows, :].astype(F32)
        z = z_ref[rows, :].astype(F32)
        out_ref[rows, :] = (_sigmoid(og) * hn * _silu(z)).astype(out_ref.dtype)


def _mlstm(proj, gates, conv_w, conv_b, i_bias, f_bias, norm_g, B, S):
    M = B * S
    H = ML_HEADS
    assert S // ML_BLOCK == 8, "block rows are packed into one 8-sublane tile"
    gate_bias = jnp.broadcast_to(jnp.concatenate([i_bias, f_bias])[:, None], (2 * H, GATE_COLS))
    conv_b2 = conv_b.reshape(1, -1)
    norm_g3 = norm_g.reshape(H, 1, ML_V_DIM)
    assert proj.shape[2] == ML_V_DIM == 2 * ML_QK_DIM
    qb, kb = ML_Q_OFF // ML_V_DIM, ML_K_OFF // ML_V_DIM
    vb, ob, zb = ML_V_OFF // ML_V_DIM, ML_O_OFF // ML_V_DIM, ML_Z_OFF // ML_V_DIM
    return pl.pallas_call(
        _mlstm_kernel,
        grid=(B, H),
        in_specs=[
            pl.BlockSpec((None, S, ML_QK_DIM), lambda b, h: (qb + h // 2, b, h % 2)),
            pl.BlockSpec((None, S, ML_QK_DIM), lambda b, h: (kb + h // 2, b, h % 2)),
            pl.BlockSpec((None, S, ML_V_DIM), lambda b, h: (vb + h, b, 0)),
            pl.BlockSpec((2 * H, S), lambda b, h: (0, b)),
            pl.BlockSpec((None, S, ML_V_DIM), lambda b, h: (ob + h, b, 0)),
            pl.BlockSpec((None, S, ML_V_DIM), lambda b, h: (zb + h, b, 0)),
            pl.BlockSpec((CONV_WIDTH, ML_QK_DIM), lambda b, h: (0, h)),
            pl.BlockSpec((CONV_WIDTH, ML_QK_DIM), lambda b, h: (0, H + h)),
            pl.BlockSpec((1, ML_QK_DIM), lambda b, h: (0, h)),
            pl.BlockSpec((1, ML_QK_DIM), lambda b, h: (0, H + h)),
            pl.BlockSpec((2 * H, GATE_COLS), lambda b, h: (0, 0)),
            pl.BlockSpec((1, 1, ML_V_DIM), lambda b, h: (h, 0, 0)),
        ],
        out_specs=pl.BlockSpec((S, ML_V_DIM), lambda b, h: (b, h)),
        out_shape=jax.ShapeDtypeStruct((M, H * ML_V_DIM), BF16),
        scratch_shapes=[pltpu.VMEM((8 + S, ML_QK_DIM), F32),
                        pltpu.VMEM((S, ML_QK_DIM), BF16),
                        pltpu.VMEM((S, ML_QK_DIM), BF16)],
        compiler_params=_params(("arbitrary", "arbitrary")),
        name="mlstm",
    )(proj, proj, proj, gates, proj, proj, conv_w, conv_w, conv_b2, conv_b2, gate_bias, norm_g3)


def _diff_attn_kernel(lam_init, qr_ref, kr_ref, v_ref, z_ref, lq1_ref, lk1_ref, lq2_ref, lk2_ref, ng_ref,
                      o_ref):
    S = kr_ref.shape[0]
    QS, DH = ATT_QSUB, DA_QK_DIM
    diag_mask = _iota((QS, QS), 1) <= _iota((QS, QS), 0)

    def scores(q0, m):
        q = qr_ref[q0:q0 + QS, m * DH:(m + 1) * DH]
        kv_len = q0 + QS
        s = lax.dot_general(q, kr_ref[0:kv_len, m * DH:(m + 1) * DH], _NT, preferred_element_type=F32)
        s_diag = jnp.where(diag_mask, s[:, q0:], -jnp.inf)
        return s_diag if q0 == 0 else jnp.concatenate([s[:, :q0], s_diag], axis=1)

    def softmax_pv(q0, s):
        p = jnp.exp2(s - jnp.max(s, axis=1, keepdims=True))
        acc = jnp.dot(p.astype(BF16), v_ref[0:q0 + QS, :], preferred_element_type=F32)
        return acc, jnp.sum(p, axis=1, keepdims=True)

    lam = (jnp.exp(jnp.sum(lq1_ref[...] * lk1_ref[...], axis=1, keepdims=True))
           - jnp.exp(jnp.sum(lq2_ref[...] * lk2_ref[...], axis=1, keepdims=True)) + lam_init)
    norm_g = ng_ref[0] * (1.0 - lam_init)

    order = list(reversed(range(0, S, QS)))
    both_maps = lambda q0: (scores(q0, 0), scores(q0, 1))
    ahead = [both_maps(q0) for q0 in order[:ATT_SCORES_AHEAD]]
    for idx, q0 in enumerate(order):
        s1, s2 = ahead.pop(0)
        if idx + ATT_SCORES_AHEAD < len(order):
            ahead.append(both_maps(order[idx + ATT_SCORES_AHEAD]))
        acc1, l1 = softmax_pv(q0, s1)
        acc2, l2 = softmax_pv(q0, s2)
        o = acc1 / l1 - lam * (acc2 / l2)
        hn = o * lax.rsqrt(jnp.mean(o * o, axis=-1, keepdims=True) + EPS) * norm_g
        o_ref[q0:q0 + QS, :] = (hn * _silu(z_ref[q0:q0 + QS, :].astype(F32))).astype(o_ref.dtype)


def _diff_attn(proj, lq1, lk1, lq2, lk2, norm_g, lam_init, B, S):
    M = B * S
    H = DA_HEADS
    assert proj.shape[2] == DA_V_DIM
    qb, kb, vb, zb = (DA_Q_OFF // DA_V_DIM, DA_K_OFF // DA_V_DIM, DA_V_OFF // DA_V_DIM, DA_Z_OFF // DA_V_DIM)
    vec = lambda a: a.reshape(1, DA_QK_DIM)
    vec_spec = pl.BlockSpec((1, DA_QK_DIM), lambda b, h: (0, 0))
    return pl.pallas_call(
        functools.partial(_diff_attn_kernel, lam_init),
        grid=(B, H),
        in_specs=[
            pl.BlockSpec((None, S, DA_V_DIM), lambda b, h: (qb + h, b, 0)),
            pl.BlockSpec((None, S, DA_V_DIM), lambda b, h: (kb + h, b, 0)),
            pl.BlockSpec((None, S, DA_V_DIM), lambda b, h: (vb + h, b, 0)),
            pl.BlockSpec((None, S, DA_V_DIM), lambda b, h: (zb + h, b, 0)),
            vec_spec, vec_spec, vec_spec, vec_spec,
            pl.BlockSpec((1, 1, DA_V_DIM), lambda b, h: (h, 0, 0)),
        ],
        out_specs=pl.BlockSpec((S, DA_V_DIM), lambda b, h: (b, h)),
        out_shape=jax.ShapeDtypeStruct((M, H * DA_V_DIM), BF16),
        compiler_params=_params(("arbitrary", "arbitrary")),
        name="diff_attn",
    )(proj, proj, proj, proj, vec(lq1), vec(lk1), vec(lq2), vec(lk2), norm_g.reshape(H, 1, DA_V_DIM))


def _ple_tail(x1_ref, ssq, p_ref, gw_ref, pw_ref, xg_ref, emit):
    D = x1_ref.shape[1]
    inv = lax.rsqrt(ssq * (1.0 / D) + EPS)
    p_bf = p_ref[...].astype(BF16)
    for n0 in range(0, D, TAIL_NC):
        cols = slice(n0, n0 + TAIL_NC)
        gate = _sigmoid(jnp.dot(xg_ref[...], gw_ref[:, cols].astype(BF16), preferred_element_type=F32) * inv)
        pp = jnp.dot(p_bf, pw_ref[:, cols].astype(BF16), preferred_element_type=F32)
        emit(n0, x1_ref[:, cols] + gate * pp)


def _out_ple_kernel(n_lhs, has_final_norm, *refs):
    lhs_refs = refs[:n_lhs]
    x_ref, p_ref, wo_ref, png_ref, gw_ref, pw_ref = refs[n_lhs:n_lhs + 6]
    if has_final_norm:
        fng_ref, o_ref, x1_ref, xn_ref = refs[n_lhs + 6:]
    else:
        o_ref, x1_ref, xn_ref = refs[n_lhs + 6:]
    TM, D = x_ref.shape

    ssq = jnp.zeros((TM, 1), F32)
    for n0 in range(0, D, TAIL_NC):
        cols = slice(n0, n0 + TAIL_NC)
        y = x_ref[:, cols]
        k0 = 0
        for lhs_ref in lhs_refs:
            k1 = k0 + lhs_ref.shape[1]
            y = y + jnp.dot(lhs_ref[...], wo_ref[k0:k1, cols].astype(BF16), preferred_element_type=F32)
            k0 = k1
        x1_ref[:, cols] = y
        xn_ref[:, cols] = (y * png_ref[:, cols]).astype(BF16)
        ssq = ssq + jnp.sum(y * y, axis=1, keepdims=True)

    ssq_out = [jnp.zeros((TM, 1), F32)]

    def emit(n0, val):
        o_ref[:, n0:n0 + TAIL_NC] = val
        if has_final_norm:
            ssq_out[0] = ssq_out[0] + jnp.sum(val * val, axis=1, keepdims=True)

    _ple_tail(x1_ref, ssq, p_ref, gw_ref, pw_ref, xn_ref, emit)

    if has_final_norm:
        inv = lax.rsqrt(ssq_out[0] * (1.0 / D) + EPS)
        for n0 in range(0, D, TAIL_NC):
            cols = slice(n0, n0 + TAIL_NC)
            o_ref[:, cols] = o_ref[:, cols] * inv * fng_ref[:, cols]


def _out_ple(lhs_list, x2d, p3d, layer, w_out, ple_g, gate_w3d, proj_w3d, final_g=None, name="out_ple"):
    M, D = x2d.shape
    P = p3d.shape[2]
    TM = TAIL_TM
    row = lambda i: (i, 0)
    fixed = lambda i: (0, 0)
    layer_block = lambda i: (layer, 0, 0)
    in_specs = [pl.BlockSpec((TM, lhs.shape[1]), row) for lhs in lhs_list]
    in_specs += [pl.BlockSpec((TM, D), row),
                 pl.BlockSpec((None, TM, P), lambda i: (layer, i, 0)),
                 _resident(w_out.shape, fixed),
                 _resident((1, D), fixed),
                 pl.BlockSpec((None, D, D), layer_block, pipeline_mode=pl.Buffered(1)),
                 pl.BlockSpec((None, P, D), layer_block, pipeline_mode=pl.Buffered(1))]
    args = [*lhs_list, x2d, p3d, w_out, ple_g.reshape(1, D), gate_w3d, proj_w3d]
    if final_g is not None:
        in_specs.append(_resident((1, D), fixed))
        args.append(final_g.reshape(1, D))
    return pl.pallas_call(
        functools.partial(_out_ple_kernel, len(lhs_list), final_g is not None),
        grid=(M // TM,),
        in_specs=in_specs,
        out_specs=pl.BlockSpec((TM, D), row),
        out_shape=jax.ShapeDtypeStruct((M, D), F32),
        scratch_shapes=[pltpu.VMEM((TM, D), F32), pltpu.VMEM((TM, D), BF16)],
        compiler_params=_params(("arbitrary",)),
        name=name,
    )(*args)


def kernel(x, p, positions, ab_norm_g, ab_w_in, ab_conv_w, ab_conv_b, ab_i_bias, ab_f_bias, ab_ml_norm_g,
           ab_lam_q1, ab_lam_k1, ab_lam_q2, ab_lam_k2, ab_da_norm_g, ab_w_out, c_norm_g, c_w_in, c_ln_g,
           c_ln_b, c_sgu_w, c_sgu_b, c_w_out, ple_norm_g, ple_gate_w, ple_proj_w, final_norm_g):
    B, S, D = x.shape
    M = B * S
    assert ab_norm_g.shape[0] == 1 and c_norm_g.shape[0] == 1 and p.shape[0] == 2
    x2d = x.reshape(M, D)
    p3d = p.reshape(2, M, p.shape[-1])

    proj, gates = _in_proj_ab(x2d, ab_norm_g[0], ab_w_in[0].T, positions)

    a_out = _mlstm(proj, gates, ab_conv_w[0], ab_conv_b[0], ab_i_bias[0], ab_f_bias[0], ab_ml_norm_g[0], B, S)

    lam_init = 0.8 - 0.6 * math.exp(-0.3 * 0)
    b_out = _diff_attn(proj, ab_lam_q1[0], ab_lam_k1[0], ab_lam_q2[0], ab_lam_k2[0], ab_da_norm_g[0],
                       lam_init, B, S)

    x2 = _out_ple([a_out, b_out], x2d, p3d, 0, ab_w_out[0], ple_norm_g[0], ple_gate_w, ple_proj_w,
                  name="out_ple_ab")

    t = _in_proj_sgu(x2, c_norm_g[0], c_w_in[0], c_ln_g[0], c_ln_b[0], c_sgu_w[0], c_sgu_b[0])
    out = _out_ple([t], x2, p3d, 1, c_w_out[0], ple_norm_g[1], ple_gate_w, ple_proj_w,
                   final_g=final_norm_g, name="out_ple_c")
    return out.reshape(B, S, D)
```
